```python
import jax, jax.numpy as jnp
from jax import lax
import numpy as np

D_MODEL = 1024
BATCH = 8
SEQ = 2048
DEPTH = 2
DEC_BATCH = 128
DEC_SEQ = 4
PAST_LEN = 16384
PAGE_SIZE = 128

N_META = 16
CHUNK = 64
N_MIXERS = 4
GROUP_WIDTH = D_MODEL // N_MIXERS
N_HEADS = 4
HEAD_DIM = GROUP_WIDTH // N_HEADS
GLA_DK = HEAD_DIM // 2
GLA_RANK = 16
GLA_GATE_NORM = 16.0
CONV_W = 4
D_FF = -((-8 * D_MODEL) // (3 * 256)) * 256
ALPHA = (2 * DEPTH) ** 0.25
BETA = (8 * DEPTH) ** -0.25
RET_THETA_BASE = 10000.0
LN_EPS = 1e-5
NORM_EPS = 1e-6
GATE_CLAMP = 1e-6
IN_SPLITS = (GROUP_WIDTH, GROUP_WIDTH, GROUP_WIDTH, N_HEADS, N_HEADS, GROUP_WIDTH,
             GROUP_WIDTH, GROUP_WIDTH, GROUP_WIDTH, GROUP_WIDTH,
             N_HEADS * GLA_DK, N_HEADS * GLA_DK, GROUP_WIDTH, GLA_RANK, GROUP_WIDTH,
             GROUP_WIDTH, GROUP_WIDTH, GROUP_WIDTH, GROUP_WIDTH)
VALUE_SPLITS = (2, 8, 12, 17)
D_IN = sum(IN_SPLITS)

kernel_name = "hymba_delta_hgrn2_gla_retnet_step"


def _layer_norm(x, g, b):
    xf = x.astype(jnp.float32)
    mu = jnp.mean(xf, -1, keepdims=True)
    var = jnp.mean(jnp.square(xf - mu), -1, keepdims=True)
    return ((xf - mu) * lax.rsqrt(var + LN_EPS) * g + b).astype(x.dtype)


def _head_rms(o, g):
    o = o * lax.rsqrt(jnp.mean(jnp.square(o), -1, keepdims=True) + NORM_EPS)
    return o.reshape(o.shape[:2] + (-1,)) * g


def _head_group_norm(o, g):
    mu = jnp.mean(o, -1, keepdims=True)
    var = jnp.mean(jnp.square(o - mu), -1, keepdims=True)
    o = (o - mu) * lax.rsqrt(var + LN_EPS)
    return o.reshape(o.shape[:2] + (-1,)) * g


def _l2norm(x):
    x = x.astype(jnp.float32)
    return x * lax.rsqrt(jnp.sum(jnp.square(x), -1, keepdims=True) + NORM_EPS)


def _masked_exp(diff, mask):
    return jnp.where(mask, jnp.exp(jnp.where(mask, diff, 0.0)), 0.0)


def _rope(x, pos):
    half = x.shape[-1] // 2
    inv = 1.0 / (RET_THETA_BASE ** jnp.linspace(0.0, 1.0, half, dtype=jnp.float32))
    ang = pos[:, None] * inv[None, :]
    cos = jnp.cos(ang)[None, :, None, :]
    sin = jnp.sin(ang)[None, :, None, :]
    x1, x2 = x[..., :half], x[..., half:]
    return jnp.concatenate([x1 * cos - x2 * sin, x1 * sin + x2 * cos], -1)


def _gla_chunk(s, inp):
    q, k, v, g = (a.astype(jnp.float32) for a in inp)
    c = q.shape[1]
    b = jnp.cumsum(g, axis=1)
    incl = jnp.tril(jnp.ones((c, c), bool))[None, :, :, None, None]
    diff = b[:, :, None] - b[:, None, :]
    dec = _masked_exp(diff, incl)
    att = jnp.einsum('bthk,bshk,btshk->bhts', q, k, dec)
    o = (jnp.einsum('bthk,bhkv->bthv', q * jnp.exp(b), s)
         + jnp.einsum('bhts,bshv->bthv', att, v))
    b_last = b[:, -1]
    s_new = (jnp.exp(b_last)[..., None] * s
             + jnp.einsum('bshk,bshv->bhkv', k * jnp.exp(b_last[:, None] - b), v))
    return s_new, o


def _delta_chunk(s, inp):
    q, k, v, beta, g = (a.astype(jnp.float32) for a in inp)
    c = q.shape[1]
    b = jnp.cumsum(g, axis=1)
    incl = jnp.tril(jnp.ones((c, c), bool))
    strict = jnp.tril(jnp.ones((c, c), bool), -1)
    diff = (b[:, :, None] - b[:, None, :]).transpose(0, 3, 1, 2)
    dec = _masked_exp(diff, incl[None, None])
    kk = jnp.einsum('bthk,bshk->bhts', k, k)
    qk = jnp.einsum('bthk,bshk->bhts', q, k)
    a_mat = jnp.where(strict, beta.transpose(0, 2, 1)[..., None] * dec * kk, 0.0)
    gam = jnp.exp(b)
    rhs = (beta[..., None] * (v - gam[..., None] * jnp.einsum('bthk,bhkv->bthv', k, s))).transpose(0, 2, 1, 3)
    u = lax.linalg.triangular_solve(a_mat + jnp.eye(c, dtype=jnp.float32), rhs,
                                    left_side=True, lower=True, unit_diagonal=True)
    o = (gam[..., None] * jnp.einsum('bthk,bhkv->bthv', q, s)
         + jnp.einsum('bhts,bhsv->bthv', dec * qk, u))
    b_last = b[:, -1]
    s_new = (jnp.exp(b_last)[..., None, None] * s
             + jnp.einsum('bshk,bhsv->bhkv', k * jnp.exp(b_last[:, None] - b)[..., None], u))
    return s_new, o


def _segmented(step, s0, xs, lead, chunk):
    s = s0.astype(jnp.float32)
    outs = []
    if lead > 0:
        s, o = step(s, tuple(a[:, :lead] for a in xs))
        outs.append(o)
        xs = tuple(a[:, lead:] for a in xs)
    bsz, t = xs[0].shape[:2]
    n = t // chunk
    blocks = tuple(jnp.moveaxis(a.reshape((bsz, n, chunk) + a.shape[2:]), 1, 0) for a in xs)
    s, o = lax.scan(step, s, blocks)
    outs.append(jnp.moveaxis(o, 0, 1).reshape((bsz, t) + o.shape[3:]))
    return s.astype(s0.dtype), jnp.concatenate(outs, axis=1)


def _mixers(h, conv_buf, s_delta, s_hgrn, s_gla, s_ret, pos, lead, chunk, lb,
            w_in, conv_w, a_log, dt_bias, delta_g, hgrn_g, gla_wg, gla_bg, gla_g, ret_g, w_out):
    bsz, t, _ = h.shape
    f32 = jnp.float32
    heads = lambda a: a.reshape(bsz, t, N_HEADS, -1)
    silu = jax.nn.silu
    proj = h @ w_in
    (a_q, a_k, a_v, a_beta, a_alpha, a_gate, b_q, b_f, b_i, b_gate,
     c_q, c_k, c_v, c_lr, c_gate, d_q, d_k, d_v, d_gate) = jnp.split(
        proj, np.cumsum(IN_SPLITS)[:-1].tolist(), axis=-1)

    conv_in = jnp.concatenate([conv_buf.astype(proj.dtype),
                               jnp.concatenate([a_q, a_k, a_v], -1)], axis=1)
    new_conv = conv_in[:, conv_in.shape[1] - (CONV_W - 1):]
    conv_out = silu(sum(conv_in[:, j:j + t] * conv_w[j] for j in range(CONV_W)))
    aq, ak, av = jnp.split(conv_out, 3, axis=-1)
    aq = _l2norm(heads(aq)) * HEAD_DIM ** -0.5
    ak = _l2norm(heads(ak))
    beta = jax.nn.sigmoid(a_beta.astype(f32))
    log_alpha = -jnp.exp(a_log.astype(f32)) * jax.nn.softplus(a_alpha.astype(f32) + dt_bias)
    s_a, o_a = _segmented(_delta_chunk, s_delta, (aq, ak, heads(av), beta, log_alpha), lead, chunk)
    o_a = _head_rms(o_a, delta_g) * silu(a_gate.astype(f32))

    z = heads(b_f).astype(f32)
    lbh = lb.reshape(N_HEADS, HEAD_DIM).astype(f32)
    b_k = (1.0 - lbh) * jax.nn.sigmoid(-z)
    log_f = jnp.log1p(-jnp.clip(b_k, 0.0, 1.0 - GATE_CLAMP))
    s_b, o_b = _segmented(_gla_chunk, s_hgrn, (silu(heads(b_q)), b_k, heads(b_i), log_f), lead, chunk)
    o_b = _head_rms(o_b, hgrn_g) * silu(b_gate.astype(f32))

    c_logg = jax.nn.log_sigmoid((c_lr @ gla_wg + gla_bg).astype(f32)) / GLA_GATE_NORM
    s_c, o_c = _segmented(_gla_chunk, s_gla,
                          (heads(c_q) * GLA_DK ** -0.5, heads(c_k), heads(c_v), heads(c_logg)),
                          lead, chunk)
    o_c = _head_rms(o_c, gla_g) * silu(c_gate.astype(f32))

    dq = _rope(heads(d_q).astype(f32), pos)
    dk = _rope(heads(d_k).astype(f32), pos) * HEAD_DIM ** -0.5
    log_gamma = jnp.log1p(-jnp.exp2(-5.0 - jnp.arange(N_HEADS, dtype=f32)))
    d_logg = jnp.broadcast_to(log_gamma[:, None], dq.shape)
    s_d, o_d = _segmented(_gla_chunk, s_ret, (dq, dk, heads(d_v), d_logg), lead, chunk)
    o_d = _head_group_norm(o_d, ret_g) * silu(d_gate.astype(f32))

    o = jnp.concatenate([o_a, o_b, o_c, o_d], -1).astype(h.dtype) @ w_out
    return o, new_conv, s_a, s_b, s_c, s_d


def _trunk(x, conv_bufs, s_delta, s_hgrn, s_gla, s_ret, pos, lead, chunk, lb,
           w_in, conv_w, delta_a_log, delta_dt_bias, delta_norm_g, hgrn_norm_g,
           gla_w_gate, gla_b_gate, gla_norm_g, ret_norm_g, w_out, ln1_g, ln1_b,
           w_ffn_gate, w_ffn_up, w_ffn_down, ln2_g, ln2_b):
    new = ([], [], [], [], [])
    for l in range(DEPTH):
        m, *st = _mixers(x, conv_bufs[l], s_delta[l], s_hgrn[l], s_gla[l], s_ret[l], pos, lead, chunk,
                         lb[l], w_in[l], conv_w[l], delta_a_log[l], delta_dt_bias[l], delta_norm_g[l],
                         hgrn_norm_g[l], gla_w_gate[l], gla_b_gate[l], gla_norm_g[l], ret_norm_g[l],
                         w_out[l])
        for lst, s in zip(new, st):
            lst.append(s)
        x = _layer_norm(ALPHA * x + m, ln1_g[l], ln1_b[l])
        f = (jax.nn.silu(x @ w_ffn_gate[l]) * (x @ w_ffn_up[l])) @ w_ffn_down[l]
        x = _layer_norm(ALPHA * x + f, ln2_g[l], ln2_b[l])
    return (x,) + tuple(jnp.stack(lst) for lst in new)


def setup_inputs(seed: int = 0) -> dict:
    key = jax.random.key(seed)
    ks = jax.random.split(key, 32)
    f32 = jnp.float32
    nrm = lambda k, shape, scale: jax.random.normal(k, shape, f32) * scale
    col_scale = jnp.concatenate([jnp.full((n,), BETA if i in VALUE_SPLITS else 1.0, f32)
                                 for i, n in enumerate(IN_SPLITS)])
    dt = jnp.exp(jax.random.uniform(ks[13], (DEPTH, N_HEADS), f32, np.log(1e-3), np.log(1e-1)))
    return {
        "x_prompt": nrm(ks[0], (BATCH, SEQ, D_MODEL), 1.0),
        "x_sample": nrm(ks[1], (DEC_BATCH, DEC_SEQ, D_MODEL), 1.0),
        "state_delta_conv": nrm(ks[2], (DEPTH, DEC_BATCH, CONV_W - 1, 3 * GROUP_WIDTH), 1.0),
        "state_delta": nrm(ks[3], (DEPTH, DEC_BATCH, N_HEADS, HEAD_DIM, HEAD_DIM), 0.1),
        "state_hgrn": nrm(ks[4], (DEPTH, DEC_BATCH, N_HEADS, HEAD_DIM, HEAD_DIM), 0.5),
        "state_gla": nrm(ks[5], (DEPTH, DEC_BATCH, N_HEADS, GLA_DK, HEAD_DIM), 0.5),
        "state_ret": nrm(ks[6], (DEPTH, DEC_BATCH, N_HEADS, HEAD_DIM, HEAD_DIM), 0.5),
        "meta_tokens": nrm(ks[7], (N_META, D_MODEL), 1.0),
        "emb_ln_g": 1.0 + nrm(ks[8], (D_MODEL,), 0.02),
        "emb_ln_b": nrm(ks[9], (D_MODEL,), 0.02),
        "w_in": nrm(ks[10], (DEPTH, D_MODEL, D_IN), D_MODEL ** -0.5) * col_scale,
        "conv_w": nrm(ks[11], (DEPTH, CONV_W, 3 * GROUP_WIDTH), CONV_W ** -0.5),
        "delta_a_log": jnp.log(jax.random.uniform(ks[12], (DEPTH, N_HEADS), f32, 1.0, 16.0)),
        "delta_dt_bias": dt + jnp.log(-jnp.expm1(-dt)),
        "delta_norm_g": 1.0 + nrm(ks[14], (DEPTH, GROUP_WIDTH), 0.02),
        "hgrn_lb_logits": nrm(ks[15], (DEPTH, GROUP_WIDTH), 1.0),
        "hgrn_norm_g": 1.0 + nrm(ks[16], (DEPTH, GROUP_WIDTH), 0.02),
        "gla_w_gate": nrm(ks[17], (DEPTH, GLA_RANK, N_HEADS * GLA_DK), GLA_RANK ** -0.5),
        "gla_b_gate": nrm(ks[18], (DEPTH, N_HEADS * GLA_DK), 0.1),
        "gla_norm_g": 1.0 + nrm(ks[19], (DEPTH, GROUP_WIDTH), 0.02),
        "ret_norm_g": 1.0 + nrm(ks[20], (DEPTH, GROUP_WIDTH), 0.02),
        "w_out": nrm(ks[21], (DEPTH, N_MIXERS * GROUP_WIDTH, D_MODEL), D_MODEL ** -0.5) * BETA,
        "ln1_g": 1.0 + nrm(ks[22], (DEPTH, D_MODEL), 0.02),
        "ln1_b": nrm(ks[23], (DEPTH, D_MODEL), 0.02),
        "w_ffn_gate": nrm(ks[24], (DEPTH, D_MODEL, D_FF), D_MODEL ** -0.5) * BETA,
        "w_ffn_up": nrm(ks[25], (DEPTH, D_MODEL, D_FF), D_MODEL ** -0.5) * BETA,
        "w_ffn_down": nrm(ks[26], (DEPTH, D_FF, D_MODEL), D_FF ** -0.5) * BETA,
        "ln2_g": 1.0 + nrm(ks[27], (DEPTH, D_MODEL), 0.02),
        "ln2_b": nrm(ks[28], (DEPTH, D_MODEL), 0.02),
    }


def reference(x_prompt, x_sample, state_delta_conv, state_delta, state_hgrn, state_gla, state_ret,
              meta_tokens, emb_ln_g, emb_ln_b, w_in, conv_w, delta_a_log, delta_dt_bias,
              delta_norm_g, hgrn_lb_logits, hgrn_norm_g, gla_w_gate, gla_b_gate, gla_norm_g,
              ret_norm_g, w_out, ln1_g, ln1_b, w_ffn_gate, w_ffn_up, w_ffn_down, ln2_g, ln2_b):
    f32 = jnp.float32
    p = jax.nn.softmax(hgrn_lb_logits.astype(f32), axis=0)
    lb = jnp.cumsum(p, axis=0) - p[0]
    weights = (lb, w_in, conv_w, delta_a_log, delta_dt_bias, delta_norm_g, hgrn_norm_g,
               gla_w_gate, gla_b_gate, gla_norm_g, ret_norm_g, w_out, ln1_g, ln1_b,
               w_ffn_gate, w_ffn_up, w_ffn_down, ln2_g, ln2_b)

    bsz = x_prompt.shape[0]
    meta = jnp.broadcast_to(meta_tokens.astype(x_prompt.dtype)[None], (bsz, N_META, D_MODEL))
    xp = _layer_norm(jnp.concatenate([meta, x_prompt], axis=1), emb_ln_g, emb_ln_b)
    pos_p = jnp.arange(N_META + x_prompt.shape[1], dtype=f32)
    conv0 = jnp.zeros((DEPTH, bsz, CONV_W - 1, 3 * GROUP_WIDTH), x_prompt.dtype)
    sq0 = jnp.zeros((DEPTH, bsz, N_HEADS, HEAD_DIM, HEAD_DIM), f32)
    sg0 = jnp.zeros((DEPTH, bsz, N_HEADS, GLA_DK, HEAD_DIM), f32)
    yp, conv_p, delta_p, hgrn_p, gla_p, ret_p = _trunk(
        xp, conv0, sq0, sq0, sg0, sq0, pos_p, N_META, CHUNK, *weights)
    y_prompt = yp[:, N_META:]

    xs = _layer_norm(x_sample, emb_ln_g, emb_ln_b)
    pos_s = PAST_LEN + jnp.arange(x_sample.shape[1], dtype=f32)
    y_sample, conv_s, delta_s, hgrn_s, gla_s, ret_s = _trunk(
        xs, state_delta_conv, state_delta, state_hgrn, state_gla, state_ret,
        pos_s, 0, x_sample.shape[1], *weights)

    return (y_prompt, y_sample, conv_p, conv_s, delta_p, delta_s, hgrn_p, hgrn_s,
            gla_p, gla_s, ret_p, ret_s)
```

```python
import functools
import math

import numpy as np
import jax
import jax.numpy as jnp
from jax import lax
from jax.experimental import pallas as pl
from jax.experimental.pallas import tpu as pltpu

f32 = jnp.float32
bf16 = jnp.bfloat16

D_MODEL = 1024
N_META = 16
CHUNK = 64
GROUP = 256
N_HEADS = 4
HEAD_DIM = 64
GLA_DK = 32
GLA_RANK = 16
GLA_GATE_NORM = 16.0
CONV_W = 4
D_FF = 2816
DEPTH = 2
ALPHA = (2 * DEPTH) ** 0.25
PAST_LEN = 16384
RET_THETA_BASE = 10000.0
LN_EPS = 1e-5
NORM_EPS = 1e-6
GATE_CLAMP = 1e-6
IN_SPLITS = (GROUP, GROUP, GROUP, N_HEADS, N_HEADS, GROUP,
             GROUP, GROUP, GROUP, GROUP,
             N_HEADS * GLA_DK, N_HEADS * GLA_DK, GROUP, GLA_RANK, GROUP,
             GROUP, GROUP, GROUP, GROUP)

LANE = 128
SUBLANE = 8
VMEM_LIMIT = 56 * 1024 * 1024

COL_ORDER = (0, 1, 2, 5, 6, 7, 8, 9, 10, 11, 12, 14, 15, 16, 17, 18)
A_QKV, A_GATE = 0, 768
B_Q, B_F, B_I, B_GATE = 1024, 1280, 1536, 1792
C_Q, C_K, C_V, C_GATE = 2048, 2176, 2304, 2560
D_Q, D_K, D_V, D_GATE = 2816, 3072, 3328, 3584
SMALL = 3840
SM_BETA, SM_ALPHA, SM_LR = 0, N_HEADS, 2 * N_HEADS
D_IN_AL = SMALL + LANE
SAMPLE_ROWS = SUBLANE
SAMPLE_GROUP = 2
CONV_HIST = SUBLANE
ROW_TILE = 512


def _layer_norm(x, g, b):
    mu = jnp.mean(x, -1, keepdims=True)
    xc = x - mu
    var = jnp.mean(xc * xc, -1, keepdims=True)
    return xc * lax.rsqrt(var + LN_EPS) * g + b


def _dot(a, b):
    return jnp.dot(a.astype(bf16), b.astype(bf16), preferred_element_type=f32)


def _dot_nt(a, b):
    return lax.dot_general(a.astype(bf16), b.astype(bf16), (((1,), (1,)), ((), ())),
                           preferred_element_type=f32)


def _dot_tn(a, b):
    return lax.dot_general(a.astype(bf16), b.astype(bf16), (((0,), (0,)), ((), ())),
                           preferred_element_type=f32)


def _dot01(m01, x):
    hi = x.astype(bf16)
    r1 = x - hi.astype(f32)
    mid = r1.astype(bf16)
    lo = (r1 - mid.astype(f32)).astype(bf16)
    m01 = m01.astype(bf16)
    d = lambda y: jnp.dot(m01, y, preferred_element_type=f32)
    return d(hi) + d(mid) + d(lo)


def _levels(c):
    m, out = c // 2, []
    while m >= 1:
        out.append(m)
        m //= 2
    return tuple(out)


def _chunk_constants(c):
    lv = _levels(c)
    t = np.arange(c)
    rows, masks = [], []
    for m in lv:
        blk = t // m
        upper = (blk % 2) == 1
        lo = np.where(upper, blk * m, t + 1)
        hi = np.where(upper, t, (blk + 1) * m - 1)
        rows.append(((t[None, :] >= lo[:, None]) & (t[None, :] <= hi[:, None])).astype(np.float32))
        same = (t[:, None] // (2 * m)) == (t[None, :] // (2 * m))
        masks.append((same & upper[:, None] & (~upper)[None, :]).astype(np.float32))
    rows.append((t[None, :] <= t[:, None]).astype(np.float32))
    rows.append((t[None, :] > t[:, None]).astype(np.float32))
    masks.append(np.eye(c, dtype=np.float32))
    return np.concatenate(rows, 0), np.stack(masks, 0)


def _proj_body(with_ln, x_ref, g_ref, b_ref, w_ref, o_ref, *h_ref):
    x = x_ref[...]
    if with_ln:
        x = _layer_norm(x, g_ref[...], b_ref[...])
        h_ref[0][...] = x
    h = x.astype(bf16)
    n = w_ref.shape[1]
    step = 4 * LANE
    for n0 in range(0, n, step):
        n1 = min(n0 + step, n)
        o_ref[:, n0:n1] = jnp.dot(h, w_ref[:, n0:n1], preferred_element_type=f32)


def _proj(x, g, b, w, with_ln):
    rows = x.shape[0]
    n = w.shape[1]
    out_shape = [jax.ShapeDtypeStruct((rows, n), f32)]
    out_specs = [pl.BlockSpec((ROW_TILE, n), lambda i: (i, 0))]
    if with_ln:
        out_shape.append(jax.ShapeDtypeStruct((rows, D_MODEL), f32))
        out_specs.append(pl.BlockSpec((ROW_TILE, D_MODEL), lambda i: (i, 0)))
    return pl.pallas_call(
        functools.partial(_proj_body, with_ln),
        out_shape=out_shape,
        grid=(rows // ROW_TILE,),
        in_specs=[pl.BlockSpec((ROW_TILE, D_MODEL), lambda i: (i, 0)),
                  pl.BlockSpec((1, D_MODEL), lambda i: (0, 0)),
                  pl.BlockSpec((1, D_MODEL), lambda i: (0, 0)),
                  pl.BlockSpec((D_MODEL, n), lambda i: (0, 0), pipeline_mode=pl.Buffered(1))],
        out_specs=out_specs,
        compiler_params=pltpu.CompilerParams(dimension_semantics=("arbitrary",),
                                             vmem_limit_bytes=VMEM_LIMIT),
        name="in_proj",
    )(x, g, b, w)


def _post_body(x_ref, o_ref, wo_ref, g1_ref, b1_ref, wg_ref, wu_ref, wd_ref, g2_ref, b2_ref, y_ref):
    m = jnp.dot(o_ref[...], wo_ref[...], preferred_element_type=f32)
    y1 = _layer_norm(ALPHA * x_ref[...] + m, g1_ref[...], b1_ref[...])
    y1b = y1.astype(bf16)
    dff = wg_ref.shape[1]
    step = 4 * LANE
    acc = jnp.zeros(y1.shape, f32)
    for n0 in range(0, dff, step):
        n1 = min(n0 + step, dff)
        gate = jnp.dot(y1b, wg_ref[:, n0:n1], preferred_element_type=f32)
        up = jnp.dot(y1b, wu_ref[:, n0:n1], preferred_element_type=f32)
        act = (jax.nn.silu(gate) * up).astype(bf16)
        acc = acc + jnp.dot(act, wd_ref[n0:n1, :], preferred_element_type=f32)
    y_ref[...] = _layer_norm(ALPHA * y1 + acc, g2_ref[...], b2_ref[...])


def _post(x, o, wo, g1, b1, wg, wu, wd, g2, b2):
    rows = x.shape[0]
    row_spec = lambda: pl.BlockSpec((ROW_TILE, D_MODEL), lambda i: (i, 0))
    vec_spec = lambda: pl.BlockSpec((1, D_MODEL), lambda i: (0, 0))
    wspec = lambda s: pl.BlockSpec(s, lambda i: (0, 0), pipeline_mode=pl.Buffered(1))
    return pl.pallas_call(
        _post_body,
        out_shape=jax.ShapeDtypeStruct((rows, D_MODEL), f32),
        grid=(rows // ROW_TILE,),
        in_specs=[row_spec(), row_spec(), wspec((D_MODEL, D_MODEL)), vec_spec(), vec_spec(),
                  wspec((D_MODEL, D_FF)), wspec((D_MODEL, D_FF)), wspec((D_FF, D_MODEL)),
                  vec_spec(), vec_spec()],
        out_specs=row_spec(),
        compiler_params=pltpu.CompilerParams(dimension_semantics=("arbitrary",),
                                             vmem_limit_bytes=VMEM_LIMIT),
        name="post_ffn",
    )(x, o, wo, g1, b1, wg, wu, wd, g2, b2)


def _head_rms(o):
    return o * lax.rsqrt(jnp.mean(o * o, -1, keepdims=True) + NORM_EPS)


def _head_group_norm(o):
    mu = jnp.mean(o, -1, keepdims=True)
    oc = o - mu
    return oc * lax.rsqrt(jnp.mean(oc * oc, -1, keepdims=True) + LN_EPS)


def _channel_decay_mixer(q_all, k_all, v_all, g_all, kdim, s_in, s_out, sums_ref, masks_ref, oscr, base, c):
    lv = _levels(c)
    nl = len(lv)
    sums = _dot01(sums_ref[...], g_all)
    b = sums[nl * c:(nl + 1) * c]
    rev = sums[(nl + 1) * c:(nl + 2) * c]
    row = lax.broadcasted_iota(jnp.int32, (c, 1), 0)
    q_in = q_all * jnp.exp(b)
    k_out = k_all * jnp.exp(rev)
    side = []
    for li, m in enumerate(lv):
        upper = ((row // m) % 2) == 1
        side.append(jnp.where(upper, q_all, k_all) * jnp.exp(sums[li * c:(li + 1) * c]))
    b_last_col = jnp.exp(b[c - SUBLANE:c, :].T[:, SUBLANE - 1:SUBLANE])
    for h in range(N_HEADS):
        sl = slice(h * kdim, (h + 1) * kdim)
        vh = v_all[:, h * HEAD_DIM:(h + 1) * HEAD_DIM]
        att = masks_ref[nl] * _dot_nt(q_all[:, sl], k_all[:, sl])
        for li in range(nl):
            w = side[li][:, sl]
            att = att + masks_ref[li] * _dot_nt(w, w)
        s = s_in(h)
        o = _dot(q_in[:, sl], s) + _dot(att, vh)
        s_out(h, b_last_col[sl, :] * s + _dot_tn(k_out[:, sl], vh))
        oscr[:, base + h * HEAD_DIM: base + (h + 1) * HEAD_DIM] = o


def _mixer_chunk(c, n_solve, r0, r1, layer, p_ref, cbuf, oscr, o_ref, st_in, st_out,
                 cw_ref, alog_ref, dtb_ref, lbl_ref, wg_ref, bg_ref, gains_ref,
                 cos_ref, sina_ref, sinb_ref, sums_ref, masks_ref):
    row = lax.broadcasted_iota(jnp.int32, (c, 1), 0)
    col = lax.broadcasted_iota(jnp.int32, (1, c), 1)
    vm = ((row >= r0) & (row < r1)).astype(f32)
    incl = col <= row
    strict = col < row
    sm = p_ref[:, SMALL:SMALL + LANE]

    cbuf[CONV_HIST:CONV_HIST + c, :] = p_ref[:, A_QKV:A_QKV + 3 * GROUP] * vm
    acc = cbuf[pl.ds(CONV_HIST - 3, c), :] * cw_ref[0:1, :]
    for j in range(1, CONV_W):
        acc = acc + cbuf[pl.ds(CONV_HIST - 3 + j, c), :] * cw_ref[j:j + 1, :]
    conv = jax.nn.silu(acc)
    beta = jax.nn.sigmoid(sm) * vm
    g_a = -jnp.exp(alog_ref[...]) * jax.nn.softplus(sm + dtb_ref[...]) * vm
    nl = len(_levels(c))
    b_a = _dot01(sums_ref[nl * c:(nl + 1) * c, :], g_a)
    b_a_t = b_a.T
    for h in range(N_HEADS):
        hs = slice(h * HEAD_DIM, (h + 1) * HEAD_DIM)
        q = conv[:, hs]
        k = conv[:, GROUP + h * HEAD_DIM: GROUP + (h + 1) * HEAD_DIM]
        v = conv[:, 2 * GROUP + h * HEAD_DIM: 2 * GROUP + (h + 1) * HEAD_DIM] * vm
        q = q * lax.rsqrt(jnp.sum(q * q, -1, keepdims=True) + NORM_EPS) * HEAD_DIM ** -0.5
        k = k * lax.rsqrt(jnp.sum(k * k, -1, keepdims=True) + NORM_EPS) * vm
        bc = b_a[:, SM_ALPHA + h:SM_ALPHA + h + 1]
        br = b_a_t[SM_ALPHA + h:SM_ALPHA + h + 1, :]
        dec = jnp.where(incl, jnp.exp(jnp.where(incl, bc - br, 0.0)), 0.0)
        bet = beta[:, SM_BETA + h:SM_BETA + h + 1]
        kk = _dot_nt(k, k)
        qk = _dot_nt(q, k)
        a_mat = jnp.where(strict, bet * dec * kk, 0.0)
        gam = jnp.exp(bc)
        s = st_in[0](h)
        u = bet * (v - gam * _dot(k, s))
        p = -a_mat
        for i in range(n_solve):
            u = u + _dot(p, u)
            if i + 1 < n_solve:
                p = _dot(p, p)
        o = gam * _dot(q, s) + _dot(dec * qk, u)
        bl = bc[c - 1:c, :]
        st_out[0](h, jnp.exp(bl) * s + _dot_tn(k * jnp.exp(bl - bc), u))
        oscr[:, hs] = _head_rms(o)

    lbl = lbl_ref[...]
    e = jnp.exp(lbl - jnp.max(lbl, axis=0, keepdims=True))
    prob = e / jnp.sum(e, axis=0, keepdims=True)
    lb = jnp.sum(prob[1:layer + 1], axis=0, keepdims=True) if layer > 0 else jnp.zeros((1, GROUP), f32)
    z = p_ref[:, B_F:B_F + GROUP]
    kb = (1.0 - lb) * jax.nn.sigmoid(-z)
    g_b = jnp.log1p(-jnp.clip(kb, 0.0, 1.0 - GATE_CLAMP)) * vm
    _channel_decay_mixer(jax.nn.silu(p_ref[:, B_Q:B_Q + GROUP]), kb * vm, p_ref[:, B_I:B_I + GROUP] * vm, g_b,
                         HEAD_DIM, st_in[1], st_out[1], sums_ref, masks_ref, oscr, GROUP, c)

    g_c = jax.nn.log_sigmoid(_dot(sm, wg_ref[...]) + bg_ref[...]) * (vm / GLA_GATE_NORM)
    _channel_decay_mixer(p_ref[:, C_Q:C_Q + N_HEADS * GLA_DK] * GLA_DK ** -0.5,
                         p_ref[:, C_K:C_K + N_HEADS * GLA_DK] * vm, p_ref[:, C_V:C_V + GROUP] * vm, g_c,
                         GLA_DK, st_in[2], st_out[2], sums_ref, masks_ref, oscr, 2 * GROUP, c)

    def rope(x):
        return (x * cos_ref[...] + pltpu.roll(x, GROUP - HEAD_DIM // 2, 1) * sina_ref[...]
                + pltpu.roll(x, HEAD_DIM // 2, 1) * sinb_ref[...])
    dq = rope(p_ref[:, D_Q:D_Q + GROUP])
    dk = rope(p_ref[:, D_K:D_K + GROUP]) * (vm * HEAD_DIM ** -0.5)
    dv = p_ref[:, D_V:D_V + GROUP] * vm
    n_valid = jnp.asarray(r1 - r0, f32)
    cnt_c = jnp.clip(row + 1 - r0, 0, r1 - r0).astype(f32)
    cnt_r = jnp.clip(col + 1 - r0, 0, r1 - r0).astype(f32)
    for h in range(N_HEADS):
        hs = slice(h * HEAD_DIM, (h + 1) * HEAD_DIM)
        lg = math.log1p(-2.0 ** (-5.0 - h))
        dec = jnp.where(incl, jnp.exp(jnp.where(incl, (cnt_c - cnt_r) * lg, 0.0)), 0.0)
        q, k, v = dq[:, hs], dk[:, hs], dv[:, hs]
        s = st_in[3](h)
        o = _dot(q * jnp.exp(cnt_c * lg), s) + _dot(dec * _dot_nt(q, k), v)
        st_out[3](h, jnp.exp(n_valid * lg) * s + _dot_tn(k * jnp.exp((n_valid - cnt_c) * lg), v))
        oscr[:, 3 * GROUP + h * HEAD_DIM: 3 * GROUP + (h + 1) * HEAD_DIM] = _head_group_norm(o)
    for hb in range(N_HEADS):
        oscr[:, GROUP + hb * HEAD_DIM: GROUP + (hb + 1) * HEAD_DIM] = _head_rms(
            oscr[:, GROUP + hb * HEAD_DIM: GROUP + (hb + 1) * HEAD_DIM])
        oscr[:, 2 * GROUP + hb * HEAD_DIM: 2 * GROUP + (hb + 1) * HEAD_DIM] = _head_rms(
            oscr[:, 2 * GROUP + hb * HEAD_DIM: 2 * GROUP + (hb + 1) * HEAD_DIM])

    for i, gate_col in enumerate((A_GATE, B_GATE, C_GATE, D_GATE)):
        sl = slice(i * GROUP, (i + 1) * GROUP)
        o_ref[:, sl] = (oscr[:, sl] * gains_ref[i:i + 1, :]
                        * jax.nn.silu(p_ref[:, gate_col:gate_col + GROUP])).astype(o_ref.dtype)


def _state_io(ref, lead):
    idx = (0,) * lead
    return (lambda h: ref[idx + (h,)]), (lambda h, val: ref.__setitem__(idx + (h,), val))


def _prompt_body(c, lead_pad, layer, p_ref, cw_ref, alog_ref, dtb_ref, lbl_ref, wg_ref, bg_ref, gains_ref,
                 cos_ref, sina_ref, sinb_ref, sums_ref, masks_ref,
                 o_ref, conv_ref, sa_ref, sb_ref, sc_ref, sd_ref, cbuf, oscr):
    ci = pl.program_id(1)

    @pl.when(ci == 0)
    def _():
        cbuf[0:CONV_HIST, :] = jnp.zeros((CONV_HIST, 3 * GROUP), f32)
        for r in (sa_ref, sb_ref, sc_ref, sd_ref):
            r[...] = jnp.zeros(r.shape, f32)

    r0 = jnp.where(ci == 0, lead_pad, 0)
    ios = [_state_io(r, 1) for r in (sa_ref, sb_ref, sc_ref, sd_ref)]
    _mixer_chunk(c, int(math.log2(c)), r0, c, layer, p_ref, cbuf, oscr, o_ref,
                 [io[0] for io in ios], [io[1] for io in ios],
                 cw_ref, alog_ref, dtb_ref, lbl_ref, wg_ref, bg_ref, gains_ref,
                 cos_ref, sina_ref, sinb_ref, sums_ref, masks_ref)
    conv_ref[0] = cbuf[pl.ds(CONV_HIST + c - (CONV_W - 1), CONV_W - 1), :]
    cbuf[0:CONV_HIST, :] = cbuf[c:c + CONV_HIST, :]


def _sample_body(c, n_tok, layer, p_ref, cw_ref, alog_ref, dtb_ref, lbl_ref, wg_ref, bg_ref, gains_ref,
                 cos_ref, sina_ref, sinb_ref, sums_ref, masks_ref,
                 convi_ref, sai_ref, sbi_ref, sci_ref, sdi_ref, oprev_ref,
                 o_ref, conv_ref, sa_ref, sb_ref, sc_ref, sd_ref, cbuf, oscr):
    del oprev_ref
    for s in range(SAMPLE_GROUP):
        rows = pl.ds(s * c, c)
        cbuf[0:CONV_HIST, :] = jnp.zeros((CONV_HIST, 3 * GROUP), f32)
        cbuf[CONV_HIST - (CONV_W - 1):CONV_HIST, :] = convi_ref[0, s]
        ins = [(lambda h, r=r, s=s: r[0, s, h]) for r in (sai_ref, sbi_ref, sci_ref, sdi_ref)]
        outs = [(lambda h, val, r=r, s=s: r.__setitem__((s, h), val)) for r in (sa_ref, sb_ref, sc_ref, sd_ref)]
        _mixer_chunk(c, max(1, math.ceil(math.log2(n_tok))), 0, n_tok, layer, p_ref.at[rows, :], cbuf, oscr,
                     o_ref.at[rows, :], ins, outs,
                     cw_ref, alog_ref, dtb_ref, lbl_ref, wg_ref, bg_ref, gains_ref,
                     cos_ref, sina_ref, sinb_ref, sums_ref, masks_ref)
        conv_ref[s] = cbuf[pl.ds(CONV_HIST + n_tok - (CONV_W - 1), CONV_W - 1), :]


def _full(shape):
    nd = len(shape)
    return pl.BlockSpec(shape, lambda *_: (0,) * nd)


def _mixer_weight_specs(c):
    nl = len(_levels(c))
    return [_full((CONV_W, 3 * GROUP)), _full((1, LANE)), _full((1, LANE)), _full((DEPTH, GROUP)),
            _full((LANE, LANE)), _full((1, LANE)), _full((4, GROUP))], [
            _full(((nl + 2) * c, c)), _full((nl + 1, c, c))]


def _mixers_prompt(proj, bsz, n_chunks, lead_pad, layer, wts, tables, consts, n_rows_total):
    c = CHUNK
    wspecs, cspecs = _mixer_weight_specs(c)
    tspec = lambda: pl.BlockSpec((c, GROUP), lambda b, i: (i, 0))
    st = lambda k: pl.BlockSpec((1, N_HEADS, k, HEAD_DIM), lambda b, i: (b, 0, 0, 0))
    out_shape = (jax.ShapeDtypeStruct((n_rows_total, D_MODEL), bf16),
                 jax.ShapeDtypeStruct((bsz, CONV_W - 1, 3 * GROUP), f32),
                 jax.ShapeDtypeStruct((bsz, N_HEADS, HEAD_DIM, HEAD_DIM), f32),
                 jax.ShapeDtypeStruct((bsz, N_HEADS, HEAD_DIM, HEAD_DIM), f32),
                 jax.ShapeDtypeStruct((bsz, N_HEADS, GLA_DK, HEAD_DIM), f32),
                 jax.ShapeDtypeStruct((bsz, N_HEADS, HEAD_DIM, HEAD_DIM), f32))
    return pl.pallas_call(
        functools.partial(_prompt_body, c, lead_pad, layer),
        out_shape=out_shape,
        grid=(bsz, n_chunks),
        in_specs=[pl.BlockSpec((c, D_IN_AL), lambda b, i: (b * n_chunks + i, 0))] + wspecs
                 + [tspec(), tspec(), tspec()] + cspecs,
        out_specs=(pl.BlockSpec((c, D_MODEL), lambda b, i: (b * n_chunks + i, 0)),
                   pl.BlockSpec((1, CONV_W - 1, 3 * GROUP), lambda b, i: (b, 0, 0)),
                   st(HEAD_DIM), st(HEAD_DIM), st(GLA_DK), st(HEAD_DIM)),
        scratch_shapes=[pltpu.VMEM((c + CONV_HIST, 3 * GROUP), f32), pltpu.VMEM((c, D_MODEL), f32)],
        compiler_params=pltpu.CompilerParams(dimension_semantics=("arbitrary", "arbitrary"),
                                             vmem_limit_bytes=VMEM_LIMIT),
        name="mixers_prompt",
    )(proj, *wts, *tables, *consts)


def _mixers_sample(proj, o_prev, n_seq, n_tok, row_block0, layer, wts, tables, consts, states):
    c = SAMPLE_ROWS
    g = SAMPLE_GROUP
    assert n_seq % g == 0 and row_block0 % g == 0
    blk0 = row_block0 // g
    wspecs, cspecs = _mixer_weight_specs(c)
    tspec = lambda: pl.BlockSpec((c, GROUP), lambda i: (0, 0))
    sti = lambda k: pl.BlockSpec((1, g, N_HEADS, k, HEAD_DIM), lambda i: (layer, i, 0, 0, 0))
    sto = lambda k: pl.BlockSpec((g, N_HEADS, k, HEAD_DIM), lambda i: (i, 0, 0, 0))
    out_shape = (jax.ShapeDtypeStruct(o_prev.shape, o_prev.dtype),
                 jax.ShapeDtypeStruct((n_seq, CONV_W - 1, 3 * GROUP), f32),
                 jax.ShapeDtypeStruct((n_seq, N_HEADS, HEAD_DIM, HEAD_DIM), f32),
                 jax.ShapeDtypeStruct((n_seq, N_HEADS, HEAD_DIM, HEAD_DIM), f32),
                 jax.ShapeDtypeStruct((n_seq, N_HEADS, GLA_DK, HEAD_DIM), f32),
                 jax.ShapeDtypeStruct((n_seq, N_HEADS, HEAD_DIM, HEAD_DIM), f32))
    n_in = 1 + len(wspecs) + 3 + len(cspecs) + 5
    return pl.pallas_call(
        functools.partial(_sample_body, c, n_tok, layer),
        out_shape=out_shape,
        grid=(n_seq // g,),
        in_specs=[pl.BlockSpec((g * c, D_IN_AL), lambda i: (blk0 + i, 0))] + wspecs
                 + [tspec(), tspec(), tspec()] + cspecs
                 + [pl.BlockSpec((1, g, CONV_W - 1, 3 * GROUP), lambda i: (layer, i, 0, 0)),
                    sti(HEAD_DIM), sti(HEAD_DIM), sti(GLA_DK), sti(HEAD_DIM),
                    pl.BlockSpec(memory_space=pl.ANY)],
        out_specs=(pl.BlockSpec((g * c, D_MODEL), lambda i: (blk0 + i, 0)),
                   pl.BlockSpec((g, CONV_W - 1, 3 * GROUP), lambda i: (i, 0, 0)),
                   sto(HEAD_DIM), sto(HEAD_DIM), sto(GLA_DK), sto(HEAD_DIM)),
        scratch_shapes=[pltpu.VMEM((c + CONV_HIST, 3 * GROUP), f32), pltpu.VMEM((c, D_MODEL), f32)],
        input_output_aliases={n_in: 0},
        compiler_params=pltpu.CompilerParams(dimension_semantics=("arbitrary",),
                                             vmem_limit_bytes=VMEM_LIMIT),
        name="mixers_sample",
    )(proj, *wts, *tables, *consts, *states, o_prev)


def _rope_tables(pos):
    half = HEAD_DIM // 2
    inv = 1.0 / (RET_THETA_BASE ** jnp.linspace(0.0, 1.0, half, dtype=f32))
    ang = pos[:, None] * inv[None, :]
    cos, sin = jnp.cos(ang), jnp.sin(ang)
    zero = jnp.zeros_like(sin)
    tile = lambda a, b: jnp.tile(jnp.concatenate([a, b], -1), (1, N_HEADS))
    return tile(cos, cos), tile(-sin, zero), tile(zero, sin)


def _align_w_in(w):
    offs = np.concatenate([[0], np.cumsum(IN_SPLITS)])
    seg = lambda i: w[:, offs[i]:offs[i + 1]]
    small = jnp.concatenate([seg(3), seg(4), seg(13)], -1)
    small = jnp.pad(small, ((0, 0), (0, LANE - small.shape[1])))
    return jnp.concatenate([seg(i) for i in COL_ORDER] + [small], -1).astype(bf16)


def kernel(x_prompt, x_sample, state_delta_conv, state_delta, state_hgrn, state_gla, state_ret, meta_tokens, emb_ln_g, emb_ln_b, w_in, conv_w, delta_a_log, delta_dt_bias, delta_norm_g, hgrn_lb_logits, hgrn_norm_g, gla_w_gate, gla_b_gate, gla_norm_g, ret_norm_g, w_out, ln1_g, ln1_b, w_ffn_gate, w_ffn_up, w_ffn_down, ln2_g, ln2_b):
    bsz, seq, _ = x_prompt.shape
    n_seq, n_tok, _ = x_sample.shape
    assert seq % CHUNK == 0 and N_META <= CHUNK and CONV_W - 1 <= n_tok <= SAMPLE_ROWS
    lead_pad = CHUNK - N_META
    n_chunks = seq // CHUNK + 1
    t_pad = n_chunks * CHUNK
    rows_p = bsz * t_pad
    rows_s = n_seq * SAMPLE_ROWS
    rows = -(-(rows_p + rows_s) // ROW_TILE) * ROW_TILE

    head = jnp.concatenate([jnp.zeros((lead_pad, D_MODEL), f32), meta_tokens.astype(f32)], 0)
    xp = jnp.concatenate([jnp.broadcast_to(head[None], (bsz, CHUNK, D_MODEL)), x_prompt], 1)
    xs = jnp.pad(x_sample, ((0, 0), (0, SAMPLE_ROWS - n_tok), (0, 0)))
    x = jnp.concatenate([xp.reshape(rows_p, D_MODEL), xs.reshape(rows_s, D_MODEL),
                         jnp.zeros((rows - rows_p - rows_s, D_MODEL), f32)], 0)

    tab_p = _rope_tables(jnp.arange(t_pad, dtype=f32) - lead_pad)
    tab_s = _rope_tables(PAST_LEN + jnp.arange(SAMPLE_ROWS, dtype=f32))
    sums_p, masks_p = _chunk_constants(CHUNK)
    sums_s, masks_s = _chunk_constants(SAMPLE_ROWS)
    consts_p = (jnp.asarray(sums_p, f32), jnp.asarray(masks_p, f32))
    consts_s = (jnp.asarray(sums_s, f32), jnp.asarray(masks_s, f32))
    row_vec = lambda v: v.reshape(1, -1).astype(f32)
    small_row = lambda v: jnp.zeros((1, LANE), f32).at[0, SM_ALPHA:SM_ALPHA + N_HEADS].set(v)

    new = [[] for _ in range(10)]
    for l in range(DEPTH):
        res = _proj(x, row_vec(emb_ln_g), row_vec(emb_ln_b), _align_w_in(w_in[l]), with_ln=(l == 0))
        proj = res[0]
        if l == 0:
            x = res[1]
        wg = jnp.zeros((LANE, N_HEADS * GLA_DK), f32).at[SM_LR:SM_LR + GLA_RANK].set(gla_w_gate[l])
        gains = jnp.stack([delta_norm_g[l], hgrn_norm_g[l], gla_norm_g[l], ret_norm_g[l]], 0)
        wts = (conv_w[l], small_row(delta_a_log[l]), small_row(delta_dt_bias[l]), hgrn_lb_logits,
               wg, row_vec(gla_b_gate[l]), gains)
        o, conv_p, da_p, hg_p, gl_p, rt_p = _mixers_prompt(
            proj, bsz, n_chunks, lead_pad, l, wts, tab_p, consts_p, rows)
        o, conv_s, da_s, hg_s, gl_s, rt_s = _mixers_sample(
            proj, o, n_seq, n_tok, rows_p // SAMPLE_ROWS, l, wts, tab_s, consts_s,
            (state_delta_conv, state_delta, state_hgrn, state_gla, state_ret))
        for lst, s in zip(new, (conv_p, conv_s, da_p, da_s, hg_p, hg_s, gl_p, gl_s, rt_p, rt_s)):
            lst.append(s)
        x = _post(x, o,
                  w_out[l].astype(bf16), row_vec(ln1_g[l]), row_vec(ln1_b[l]),
                  w_ffn_gate[l].astype(bf16), w_ffn_up[l].astype(bf16), w_ffn_down[l].astype(bf16),
                  row_vec(ln2_g[l]), row_vec(ln2_b[l]))

    y_prompt = x[:rows_p].reshape(bsz, t_pad, D_MODEL)[:, CHUNK:]
    y_sample = x[rows_p:rows_p + rows_s].reshape(n_seq, SAMPLE_ROWS, D_MODEL)[:, :n_tok]
    return (y_prompt, y_sample) + tuple(jnp.stack(lst) for lst in new)
```

```python
import functools
import math

import numpy as np
import jax
import jax.numpy as jnp
from jax import lax
from jax.experimental import pallas as pl
from jax.experimental.pallas import tpu as pltpu

f32 = jnp.float32
bf16 = jnp.bfloat16

D_MODEL = 1024
N_META = 16
CHUNK = 64
GROUP = 256
N_HEADS = 4
HEAD_DIM = 64
GLA_DK = 32
GLA_RANK = 16
GLA_GATE_NORM = 16.0
CONV_W = 4
D_FF = 2816
DEPTH = 2
ALPHA = (2 * DEPTH) ** 0.25
PAST_LEN = 16384
RET_THETA_BASE = 10000.0
LN_EPS = 1e-5
NORM_EPS = 1e-6
GATE_CLAMP = 1e-6
IN_SPLITS = (GROUP, GROUP, GROUP, N_HEADS, N_HEADS, GROUP,
             GROUP, GROUP, GROUP, GROUP,
             N_HEADS * GLA_DK, N_HEADS * GLA_DK, GROUP, GLA_RANK, GROUP,
             GROUP, GROUP, GROUP, GROUP)

LANE = 128
SUBLANE = 8
VMEM_LIMIT = 56 * 1024 * 1024

COL_ORDER = (0, 1, 2, 5, 6, 7, 8, 9, 10, 11, 12, 14, 15, 16, 17, 18)
A_QKV, A_GATE = 0, 768
B_Q, B_F, B_I, B_GATE = 1024, 1280, 1536, 1792
C_Q, C_K, C_V, C_GATE = 2048, 2176, 2304, 2560
D_Q, D_K, D_V, D_GATE = 2816, 3072, 3328, 3584
SMALL = 3840
SM_BETA, SM_ALPHA, SM_LR = 0, N_HEADS, 2 * N_HEADS
D_IN_AL = SMALL + LANE
SAMPLE_ROWS = SUBLANE
SAMPLE_GROUP = 2
CONV_HIST = SUBLANE
ROW_TILE = 512


def _layer_norm(x, g, b):
    mu = jnp.mean(x, -1, keepdims=True)
    xc = x - mu
    var = jnp.mean(xc * xc, -1, keepdims=True)
    return xc * lax.rsqrt(var + LN_EPS) * g + b


def _dot(a, b):
    return jnp.dot(a.astype(bf16), b.astype(bf16), preferred_element_type=f32)


def _dot_nt(a, b):
    return lax.dot_general(a.astype(bf16), b.astype(bf16), (((1,), (1,)), ((), ())),
                           preferred_element_type=f32)


def _dot_tn(a, b):
    return lax.dot_general(a.astype(bf16), b.astype(bf16), (((0,), (0,)), ((), ())),
                           preferred_element_type=f32)


def _dot01(m01, x):
    hi = x.astype(bf16)
    r1 = x - hi.astype(f32)
    mid = r1.astype(bf16)
    lo = (r1 - mid.astype(f32)).astype(bf16)
    m01 = m01.astype(bf16)
    d = lambda y: jnp.dot(m01, y, preferred_element_type=f32)
    return d(hi) + d(mid) + d(lo)


def _levels(c):
    m, out = c // 2, []
    while m >= 1:
        out.append(m)
        m //= 2
    return tuple(out)


def _chunk_constants(c):
    lv = _levels(c)
    t = np.arange(c)
    rows, masks = [], []
    for m in lv:
        blk = t // m
        upper = (blk % 2) == 1
        lo = np.where(upper, blk * m, t + 1)
        hi = np.where(upper, t, (blk + 1) * m - 1)
        rows.append(((t[None, :] >= lo[:, None]) & (t[None, :] <= hi[:, None])).astype(np.float32))
        same = (t[:, None] // (2 * m)) == (t[None, :] // (2 * m))
        masks.append((same & upper[:, None] & (~upper)[None, :]).astype(np.float32))
    rows.append((t[None, :] <= t[:, None]).astype(np.float32))
    rows.append((t[None, :] > t[:, None]).astype(np.float32))
    masks.append(np.eye(c, dtype=np.float32))
    return np.concatenate(rows, 0), np.stack(masks, 0)


def _proj_body(with_ln, x_ref, g_ref, b_ref, w_ref, o_ref, *h_ref):
    x = x_ref[...]
    if with_ln:
        x = _layer_norm(x, g_ref[...], b_ref[...])
        h_ref[0][...] = x
    h = x.astype(bf16)
    n = w_ref.shape[1]
    step = 4 * LANE
    for n0 in range(0, n, step):
        n1 = min(n0 + step, n)
        o_ref[:, n0:n1] = jnp.dot(h, w_ref[:, n0:n1], preferred_element_type=f32)


def _proj(x, g, b, w, with_ln):
    rows = x.shape[0]
    n = w.shape[1]
    out_shape = [jax.ShapeDtypeStruct((rows, n), f32)]
    out_specs = [pl.BlockSpec((ROW_TILE, n), lambda i: (i, 0))]
    if with_ln:
        out_shape.append(jax.ShapeDtypeStruct((rows, D_MODEL), f32))
        out_specs.append(pl.BlockSpec((ROW_TILE, D_MODEL), lambda i: (i, 0)))
    return pl.pallas_call(
        functools.partial(_proj_body, with_ln),
        out_shape=out_shape,
        grid=(rows // ROW_TILE,),
        in_specs=[pl.BlockSpec((ROW_TILE, D_MODEL), lambda i: (i, 0)),
                  pl.BlockSpec((1, D_MODEL), lambda i: (0, 0)),
                  pl.BlockSpec((1, D_MODEL), lambda i: (0, 0)),
                  pl.BlockSpec((D_MODEL, n), lambda i: (0, 0), pipeline_mode=pl.Buffered(1))],
        out_specs=out_specs,
        compiler_params=pltpu.CompilerParams(dimension_semantics=("arbitrary",),
                                             vmem_limit_bytes=VMEM_LIMIT),
        name="in_proj",
    )(x, g, b, w)


def _post_body(x_ref, o_ref, wo_ref, g1_ref, b1_ref, wg_ref, wu_ref, wd_ref, g2_ref, b2_ref, y_ref):
    m = jnp.dot(o_ref[...], wo_ref[...], preferred_element_type=f32)
    y1 = _layer_norm(ALPHA * x_ref[...] + m, g1_ref[...], b1_ref[...])
    y1b = y1.astype(bf16)
    dff = wg_ref.shape[1]
    step = 4 * LANE
    acc = jnp.zeros(y1.shape, f32)
    for n0 in range(0, dff, step):
        n1 = min(n0 + step, dff)
        gate = jnp.dot(y1b, wg_ref[:, n0:n1], preferred_element_type=f32)
        up = jnp.dot(y1b, wu_ref[:, n0:n1], preferred_element_type=f32)
        act = (jax.nn.silu(gate) * up).astype(bf16)
        acc = acc + jnp.dot(act, wd_ref[n0:n1, :], preferred_element_type=f32)
    y_ref[...] = _layer_norm(ALPHA * y1 + acc, g2_ref[...], b2_ref[...])


def _post(x, o, wo, g1, b1, wg, wu, wd, g2, b2):
    rows = x.shape[0]
    row_spec = lambda: pl.BlockSpec((ROW_TILE, D_MODEL), lambda i: (i, 0))
    vec_spec = lambda: pl.BlockSpec((1, D_MODEL), lambda i: (0, 0))
    wspec = lambda s: pl.BlockSpec(s, lambda i: (0, 0), pipeline_mode=pl.Buffered(1))
    return pl.pallas_call(
        _post_body,
        out_shape=jax.ShapeDtypeStruct((rows, D_MODEL), f32),
        grid=(rows // ROW_TILE,),
        in_specs=[row_spec(), row_spec(), wspec((D_MODEL, D_MODEL)), vec_spec(), vec_spec(),
                  wspec((D_MODEL, D_FF)), wspec((D_MODEL, D_FF)), wspec((D_FF, D_MODEL)),
                  vec_spec(), vec_spec()],
        out_specs=row_spec(),
        compiler_params=pltpu.CompilerParams(dimension_semantics=("arbitrary",),
                                             vmem_limit_bytes=VMEM_LIMIT),
        name="post_ffn",
    )(x, o, wo, g1, b1, wg, wu, wd, g2, b2)


def _head_rms(o):
    return o * lax.rsqrt(jnp.mean(o * o, -1, keepdims=True) + NORM_EPS)


def _head_group_norm(o):
    mu = jnp.mean(o, -1, keepdims=True)
    oc = o - mu
    return oc * lax.rsqrt(jnp.mean(oc * oc, -1, keepdims=True) + LN_EPS)


def _heads(x, d):
    return jnp.stack([x[:, h * d:(h + 1) * d] for h in range(N_HEADS)], 0)


def _bdot(a, b):
    return jnp.einsum('hmk,hkn->hmn', a.astype(bf16), b.astype(bf16), preferred_element_type=f32)


def _bdot_nt(a, b):
    return jnp.einsum('hmk,hnk->hmn', a.astype(bf16), b.astype(bf16), preferred_element_type=f32)


def _bdot_tn(a, b):
    return lax.dot_general(a.astype(bf16), b.astype(bf16), (((1,), (1,)), ((0,), (0,))),
                           preferred_element_type=f32)


def _channel_decay_mixer(q_all, k_all, v_all, g_all, kdim, s, sums_ref, masks_ref, c):
    lv = _levels(c)
    nl = len(lv)
    sums = _dot01(sums_ref[...], g_all)
    b = sums[nl * c:(nl + 1) * c]
    rev = sums[(nl + 1) * c:(nl + 2) * c]
    row = lax.broadcasted_iota(jnp.int32, (c, 1), 0)
    vh = _heads(v_all, HEAD_DIM)
    att = masks_ref[nl] * _bdot_nt(_heads(q_all, kdim), _heads(k_all, kdim))
    for li, m in enumerate(lv):
        upper = ((row // m) % 2) == 1
        w = _heads(jnp.where(upper, q_all, k_all) * jnp.exp(sums[li * c:(li + 1) * c]), kdim)
        att = att + masks_ref[li] * _bdot_nt(w, w)
    o = _bdot(_heads(q_all * jnp.exp(b), kdim), s) + _bdot(att, vh)
    b_last_col = jnp.exp(b[c - SUBLANE:c, :].T[:, SUBLANE - 1:SUBLANE])
    s_new = b_last_col.reshape(N_HEADS, kdim, 1) * s + _bdot_tn(_heads(k_all * jnp.exp(rev), kdim), vh)
    return o, s_new


def _mixer_chunk(c, n_solve, r0, r1, layer, p_ref, cbuf, oscr, o_ref, states,
                 cw_ref, alog_ref, dtb_ref, lbl_ref, wg_ref, bg_ref, gains_ref,
                 cos_ref, sina_ref, sinb_ref, sums_ref, masks_ref):
    s_a, s_b, s_c, s_d = states
    row = lax.broadcasted_iota(jnp.int32, (c, 1), 0)
    col = lax.broadcasted_iota(jnp.int32, (1, c), 1)
    vm = ((row >= r0) & (row < r1)).astype(f32)
    incl = col <= row
    strict = col < row
    sm = p_ref[:, SMALL:SMALL + LANE]

    cbuf[CONV_HIST:CONV_HIST + c, :] = p_ref[:, A_QKV:A_QKV + 3 * GROUP] * vm
    acc = cbuf[pl.ds(CONV_HIST - 3, c), :] * cw_ref[0:1, :]
    for j in range(1, CONV_W):
        acc = acc + cbuf[pl.ds(CONV_HIST - 3 + j, c), :] * cw_ref[j:j + 1, :]
    conv = jax.nn.silu(acc)
    beta = jax.nn.sigmoid(sm) * vm
    g_a = -jnp.exp(alog_ref[...]) * jax.nn.softplus(sm + dtb_ref[...]) * vm
    nl = len(_levels(c))
    b_a = _dot01(sums_ref[nl * c:(nl + 1) * c, :], g_a)
    b_a_t = b_a.T
    q = _heads(conv[:, 0:GROUP], HEAD_DIM)
    k = _heads(conv[:, GROUP:2 * GROUP], HEAD_DIM)
    v = _heads(conv[:, 2 * GROUP:3 * GROUP] * vm, HEAD_DIM)
    q = q * lax.rsqrt(jnp.sum(q * q, -1, keepdims=True) + NORM_EPS) * HEAD_DIM ** -0.5
    k = k * lax.rsqrt(jnp.sum(k * k, -1, keepdims=True) + NORM_EPS) * vm
    bc = jnp.stack([b_a[:, SM_ALPHA + h:SM_ALPHA + h + 1] for h in range(N_HEADS)], 0)
    br = jnp.stack([b_a_t[SM_ALPHA + h:SM_ALPHA + h + 1, :] for h in range(N_HEADS)], 0)
    bet = jnp.stack([beta[:, SM_BETA + h:SM_BETA + h + 1] for h in range(N_HEADS)], 0)
    dec = jnp.where(incl, jnp.exp(jnp.where(incl, bc - br, 0.0)), 0.0)
    kq_k = _bdot_nt(jnp.concatenate([k, q], 1), k)
    kq_s = _bdot(jnp.concatenate([k, q], 1), s_a)
    a_mat = jnp.where(strict, bet * dec * kq_k[:, 0:c], 0.0)
    gam = jnp.exp(bc)
    u = bet * (v - gam * kq_s[:, 0:c])
    p = -a_mat
    for i in range(n_solve):
        u = u + _bdot(p, u)
        if i + 1 < n_solve:
            p = _bdot(p, p)
    o_a = gam * kq_s[:, c:2 * c] + _bdot(dec * kq_k[:, c:2 * c], u)
    bl = bc[:, c - 1:c, :]
    s_a_new = jnp.exp(bl) * s_a + _bdot_tn(k * jnp.exp(bl - bc), u)
    o_a = _head_rms(o_a)

    lbl = lbl_ref[...]
    e = jnp.exp(lbl - jnp.max(lbl, axis=0, keepdims=True))
    prob = e / jnp.sum(e, axis=0, keepdims=True)
    lb = jnp.sum(prob[1:layer + 1], axis=0, keepdims=True) if layer > 0 else jnp.zeros((1, GROUP), f32)
    z = p_ref[:, B_F:B_F + GROUP]
    kb = (1.0 - lb) * jax.nn.sigmoid(-z)
    g_b = jnp.log1p(-jnp.clip(kb, 0.0, 1.0 - GATE_CLAMP)) * vm
    o_b, s_b_new = _channel_decay_mixer(
        jax.nn.silu(p_ref[:, B_Q:B_Q + GROUP]), kb * vm, p_ref[:, B_I:B_I + GROUP] * vm, g_b,
        HEAD_DIM, s_b, sums_ref, masks_ref, c)
    o_b = _head_rms(o_b)

    g_c = jax.nn.log_sigmoid(_dot(sm, wg_ref[...]) + bg_ref[...]) * (vm / GLA_GATE_NORM)
    o_c, s_c_new = _channel_decay_mixer(
        p_ref[:, C_Q:C_Q + N_HEADS * GLA_DK] * GLA_DK ** -0.5,
        p_ref[:, C_K:C_K + N_HEADS * GLA_DK] * vm, p_ref[:, C_V:C_V + GROUP] * vm, g_c,
        GLA_DK, s_c, sums_ref, masks_ref, c)
    o_c = _head_rms(o_c)

    def rope(x):
        return (x * cos_ref[...] + pltpu.roll(x, GROUP - HEAD_DIM // 2, 1) * sina_ref[...]
                + pltpu.roll(x, HEAD_DIM // 2, 1) * sinb_ref[...])
    dq = rope(p_ref[:, D_Q:D_Q + GROUP])
    dk = rope(p_ref[:, D_K:D_K + GROUP]) * (vm * HEAD_DIM ** -0.5)
    dv = p_ref[:, D_V:D_V + GROUP] * vm
    n_valid = jnp.asarray(r1 - r0, f32)
    cnt_c = jnp.clip(row + 1 - r0, 0, r1 - r0).astype(f32)
    cnt_r = jnp.clip(col + 1 - r0, 0, r1 - r0).astype(f32)
    hd = lax.broadcasted_iota(jnp.int32, (N_HEADS, 1, 1), 0).astype(f32)
    lg = jnp.log1p(-jnp.exp2(-5.0 - hd))
    dec = jnp.where(incl, jnp.exp(jnp.where(incl, (cnt_c - cnt_r) * lg, 0.0)), 0.0)
    q, k, v = _heads(dq, HEAD_DIM), _heads(dk, HEAD_DIM), _heads(dv, HEAD_DIM)
    o_d = _bdot(q * jnp.exp(cnt_c * lg), s_d) + _bdot(dec * _bdot_nt(q, k), v)
    s_d_new = jnp.exp(n_valid * lg) * s_d + _bdot_tn(k * jnp.exp((n_valid - cnt_c) * lg), v)
    o_d = _head_group_norm(o_d)

    for i, (o_m, gate_col) in enumerate(zip((o_a, o_b, o_c, o_d), (A_GATE, B_GATE, C_GATE, D_GATE))):
        for h in range(N_HEADS):
            oscr[:, i * GROUP + h * HEAD_DIM: i * GROUP + (h + 1) * HEAD_DIM] = o_m[h]
        sl = slice(i * GROUP, (i + 1) * GROUP)
        o_ref[:, sl] = (oscr[:, sl] * gains_ref[i:i + 1, :]
                        * jax.nn.silu(p_ref[:, gate_col:gate_col + GROUP])).astype(o_ref.dtype)
    return s_a_new, s_b_new, s_c_new, s_d_new


def _prompt_body(c, lead_pad, layer, p_ref, cw_ref, alog_ref, dtb_ref, lbl_ref, wg_ref, bg_ref, gains_ref,
                 cos_ref, sina_ref, sinb_ref, sums_ref, masks_ref,
                 o_ref, conv_ref, sa_ref, sb_ref, sc_ref, sd_ref, cbuf, oscr):
    ci = pl.program_id(1)

    @pl.when(ci == 0)
    def _():
        cbuf[0:CONV_HIST, :] = jnp.zeros((CONV_HIST, 3 * GROUP), f32)
        for r in (sa_ref, sb_ref, sc_ref, sd_ref):
            r[...] = jnp.zeros(r.shape, f32)

    r0 = jnp.where(ci == 0, lead_pad, 0)
    st_refs = (sa_ref, sb_ref, sc_ref, sd_ref)
    new = _mixer_chunk(c, int(math.log2(c)), r0, c, layer, p_ref, cbuf, oscr, o_ref,
                       tuple(r[0] for r in st_refs),
                       cw_ref, alog_ref, dtb_ref, lbl_ref, wg_ref, bg_ref, gains_ref,
                       cos_ref, sina_ref, sinb_ref, sums_ref, masks_ref)
    for r, val in zip(st_refs, new):
        r[0] = val
    conv_ref[0] = cbuf[pl.ds(CONV_HIST + c - (CONV_W - 1), CONV_W - 1), :]
    cbuf[0:CONV_HIST, :] = cbuf[c:c + CONV_HIST, :]


def _sample_body(c, n_tok, layer, p_ref, cw_ref, alog_ref, dtb_ref, lbl_ref, wg_ref, bg_ref, gains_ref,
                 cos_ref, sina_ref, sinb_ref, sums_ref, masks_ref,
                 convi_ref, sai_ref, sbi_ref, sci_ref, sdi_ref, oprev_ref,
                 o_ref, conv_ref, sa_ref, sb_ref, sc_ref, sd_ref, cbuf, oscr):
    del oprev_ref
    for s in range(SAMPLE_GROUP):
        rows = pl.ds(s * c, c)
        cbuf[0:CONV_HIST, :] = jnp.zeros((CONV_HIST, 3 * GROUP), f32)
        cbuf[CONV_HIST - (CONV_W - 1):CONV_HIST, :] = convi_ref[0, s]
        new = _mixer_chunk(c, max(1, math.ceil(math.log2(n_tok))), 0, n_tok, layer, p_ref.at[rows, :], cbuf, oscr,
                           o_ref.at[rows, :], tuple(r[0, s] for r in (sai_ref, sbi_ref, sci_ref, sdi_ref)),
                           cw_ref, alog_ref, dtb_ref, lbl_ref, wg_ref, bg_ref, gains_ref,
                           cos_ref, sina_ref, sinb_ref, sums_ref, masks_ref)
        for r, val in zip((sa_ref, sb_ref, sc_ref, sd_ref), new):
            r[s] = val
        conv_ref[s] = cbuf[pl.ds(CONV_HIST + n_tok - (CONV_W - 1), CONV_W - 1), :]


def _full(shape):
    nd = len(shape)
    return pl.BlockSpec(shape, lambda *_: (0,) * nd)


def _mixer_weight_specs(c):
    nl = len(_levels(c))
    return [_full((CONV_W, 3 * GROUP)), _full((1, LANE)), _full((1, LANE)), _full((DEPTH, GROUP)),
            _full((LANE, LANE)), _full((1, LANE)), _full((4, GROUP))], [
            _full(((nl + 2) * c, c)), _full((nl + 1, c, c))]


def _mixers_prompt(proj, bsz, n_chunks, lead_pad, layer, wts, tables, consts, n_rows_total):
    c = CHUNK
    wspecs, cspecs = _mixer_weight_specs(c)
    tspec = lambda: pl.BlockSpec((c, GROUP), lambda b, i: (i, 0))
    st = lambda k: pl.BlockSpec((1, N_HEADS, k, HEAD_DIM), lambda b, i: (b, 0, 0, 0))
    out_shape = (jax.ShapeDtypeStruct((n_rows_total, D_MODEL), bf16),
                 jax.ShapeDtypeStruct((bsz, CONV_W - 1, 3 * GROUP), f32),
                 jax.ShapeDtypeStruct((bsz, N_HEADS, HEAD_DIM, HEAD_DIM), f32),
                 jax.ShapeDtypeStruct((bsz, N_HEADS, HEAD_DIM, HEAD_DIM), f32),
                 jax.ShapeDtypeStruct((bsz, N_HEADS, GLA_DK, HEAD_DIM), f32),
                 jax.ShapeDtypeStruct((bsz, N_HEADS, HEAD_DIM, HEAD_DIM), f32))
    return pl.pallas_call(
        functools.partial(_prompt_body, c, lead_pad, layer),
        out_shape=out_shape,
        grid=(bsz, n_chunks),
        in_specs=[pl.BlockSpec((c, D_IN_AL), lambda b, i: (b * n_chunks + i, 0))] + wspecs
                 + [tspec(), tspec(), tspec()] + cspecs,
        out_specs=(pl.BlockSpec((c, D_MODEL), lambda b, i: (b * n_chunks + i, 0)),
                   pl.BlockSpec((1, CONV_W - 1, 3 * GROUP), lambda b, i: (b, 0, 0)),
                   st(HEAD_DIM), st(HEAD_DIM), st(GLA_DK), st(HEAD_DIM)),
        scratch_shapes=[pltpu.VMEM((c + CONV_HIST, 3 * GROUP), f32), pltpu.VMEM((c, D_MODEL), f32)],
        compiler_params=pltpu.CompilerParams(dimension_semantics=("arbitrary", "arbitrary"),
                                             vmem_limit_bytes=VMEM_LIMIT),
        name="mixers_prompt",
    )(proj, *wts, *tables, *consts)


def _mixers_sample(proj, o_prev, n_seq, n_tok, row_block0, layer, wts, tables, consts, states):
    c = SAMPLE_ROWS
    g = SAMPLE_GROUP
    assert n_seq % g == 0 and row_block0 % g == 0
    blk0 = row_block0 // g
    wspecs, cspecs = _mixer_weight_specs(c)
    tspec = lambda: pl.BlockSpec((c, GROUP), lambda i: (0, 0))
    sti = lambda k: pl.BlockSpec((1, g, N_HEADS, k, HEAD_DIM), lambda i: (layer, i, 0, 0, 0))
    sto = lambda k: pl.BlockSpec((g, N_HEADS, k, HEAD_DIM), lambda i: (i, 0, 0, 0))
    out_shape = (jax.ShapeDtypeStruct(o_prev.shape, o_prev.dtype),
                 jax.ShapeDtypeStruct((n_seq, CONV_W - 1, 3 * GROUP), f32),
                 jax.ShapeDtypeStruct((n_seq, N_HEADS, HEAD_DIM, HEAD_DIM), f32),
                 jax.ShapeDtypeStruct((n_seq, N_HEADS, HEAD_DIM, HEAD_DIM), f32),
                 jax.ShapeDtypeStruct((n_seq, N_HEADS, GLA_DK, HEAD_DIM), f32),
                 jax.ShapeDtypeStruct((n_seq, N_HEADS, HEAD_DIM, HEAD_DIM), f32))
    n_in = 1 + len(wspecs) + 3 + len(cspecs) + 5
    return pl.pallas_call(
        functools.partial(_sample_body, c, n_tok, layer),
        out_shape=out_shape,
        grid=(n_seq // g,),
        in_specs=[pl.BlockSpec((g * c, D_IN_AL), lambda i: (blk0 + i, 0))] + wspecs
                 + [tspec(), tspec(), tspec()] + cspecs
                 + [pl.BlockSpec((1, g, CONV_W - 1, 3 * GROUP), lambda i: (layer, i, 0, 0)),
                    sti(HEAD_DIM), sti(HEAD_DIM), sti(GLA_DK), sti(HEAD_DIM),
                    pl.BlockSpec(memory_space=pl.ANY)],
        out_specs=(pl.BlockSpec((g * c, D_MODEL), lambda i: (blk0 + i, 0)),
                   pl.BlockSpec((g, CONV_W - 1, 3 * GROUP), lambda i: (i, 0, 0)),
                   sto(HEAD_DIM), sto(HEAD_DIM), sto(GLA_DK), sto(HEAD_DIM)),
        scratch_shapes=[pltpu.VMEM((c + CONV_HIST, 3 * GROUP), f32), pltpu.VMEM((c, D_MODEL), f32)],
        input_output_aliases={n_in: 0},
        compiler_params=pltpu.CompilerParams(dimension_semantics=("arbitrary",),
                                             vmem_limit_bytes=VMEM_LIMIT),
        name="mixers_sample",
    )(proj, *wts, *tables, *consts, *states, o_prev)


def _rope_tables(pos):
    half = HEAD_DIM // 2
    inv = 1.0 / (RET_THETA_BASE ** jnp.linspace(0.0, 1.0, half, dtype=f32))
    ang = pos[:, None] * inv[None, :]
    cos, sin = jnp.cos(ang), jnp.sin(ang)
    zero = jnp.zeros_like(sin)
    tile = lambda a, b: jnp.tile(jnp.concatenate([a, b], -1), (1, N_HEADS))
    return tile(cos, cos), tile(-sin, zero), tile(zero, sin)


def _align_w_in(w):
    offs = np.concatenate([[0], np.cumsum(IN_SPLITS)])
    seg = lambda i: w[:, offs[i]:offs[i + 1]]
    small = jnp.concatenate([seg(3), seg(4), seg(13)], -1)
    small = jnp.pad(small, ((0, 0), (0, LANE - small.shape[1])))
    return jnp.concatenate([seg(i) for i in COL_ORDER] + [small], -1).astype(bf16)


def kernel(x_prompt, x_sample, state_delta_conv, state_delta, state_hgrn, state_gla, state_ret, meta_tokens, emb_ln_g, emb_ln_b, w_in, conv_w, delta_a_log, delta_dt_bias, delta_norm_g, hgrn_lb_logits, hgrn_norm_g, gla_w_gate, gla_b_gate, gla_norm_g, ret_norm_g, w_out, ln1_g, ln1_b, w_ffn_gate, w_ffn_up, w_ffn_down, ln2_g, ln2_b):
    bsz, seq, _ = x_prompt.shape
    n_seq, n_tok, _ = x_sample.shape
    assert seq % CHUNK == 0 and N_META <= CHUNK and CONV_W - 1 <= n_tok <= SAMPLE_ROWS
    lead_pad = CHUNK - N_META
    n_chunks = seq // CHUNK + 1
    t_pad = n_chunks * CHUNK
    rows_p = bsz * t_pad
    rows_s = n_seq * SAMPLE_ROWS
    rows = -(-(rows_p + rows_s) // ROW_TILE) * ROW_TILE

    head = jnp.concatenate([jnp.zeros((lead_pad, D_MODEL), f32), meta_tokens.astype(f32)], 0)
    xp = jnp.concatenate([jnp.broadcast_to(head[None], (bsz, CHUNK, D_MODEL)), x_prompt], 1)
    xs = jnp.pad(x_sample, ((0, 0), (0, SAMPLE_ROWS - n_tok), (0, 0)))
    x = jnp.concatenate([xp.reshape(rows_p, D_MODEL), xs.reshape(rows_s, D_MODEL),
                         jnp.zeros((rows - rows_p - rows_s, D_MODEL), f32)], 0)

    tab_p = _rope_tables(jnp.arange(t_pad, dtype=f32) - lead_pad)
    tab_s = _rope_tables(PAST_LEN + jnp.arange(SAMPLE_ROWS, dtype=f32))
    sums_p, masks_p = _chunk_constants(CHUNK)
    sums_s, masks_s = _chunk_constants(SAMPLE_ROWS)
    consts_p = (jnp.asarray(sums_p, f32), jnp.asarray(masks_p, f32))
    consts_s = (jnp.asarray(sums_s, f32), jnp.asarray(masks_s, f32))
    row_vec = lambda v: v.reshape(1, -1).astype(f32)
    small_row = lambda v: jnp.zeros((1, LANE), f32).at[0, SM_ALPHA:SM_ALPHA + N_HEADS].set(v)

    new = [[] for _ in range(10)]
    for l in range(DEPTH):
        res = _proj(x, row_vec(emb_ln_g), row_vec(emb_ln_b), _align_w_in(w_in[l]), with_ln=(l == 0))
        proj = res[0]
        if l == 0:
            x = res[1]
        wg = jnp.zeros((LANE, N_HEADS * GLA_DK), f32).at[SM_LR:SM_LR + GLA_RANK].set(gla_w_gate[l])
        gains = jnp.stack([delta_norm_g[l], hgrn_norm_g[l], gla_norm_g[l], ret_norm_g[l]], 0)
        wts = (conv_w[l], small_row(delta_a_log[l]), small_row(delta_dt_bias[l]), hgrn_lb_logits,
               wg, row_vec(gla_b_gate[l]), gains)
        o, conv_p, da_p, hg_p, gl_p, rt_p = _mixers_prompt(
            proj, bsz, n_chunks, lead_pad, l, wts, tab_p, consts_p, rows)
        o, conv_s, da_s, hg_s, gl_s, rt_s = _mixers_sample(
            proj, o, n_seq, n_tok, rows_p // SAMPLE_ROWS, l, wts, tab_s, consts_s,
            (state_delta_conv, state_delta, state_hgrn, state_gla, state_ret))
        for lst, s in zip(new, (conv_p, conv_s, da_p, da_s, hg_p, hg_s, gl_p, gl_s, rt_p, rt_s)):
            lst.append(s)
        x = _post(x, o,
                  w_out[l].astype(bf16), row_vec(ln1_g[l]), row_vec(ln1_b[l]),
                  w_ffn_gate[l].astype(bf16), w_ffn_up[l].astype(bf16), w_ffn_down[l].astype(bf16),
                  row_vec(ln2_g[l]), row_vec(ln2_b[l]))

    y_prompt = x[:rows_p].reshape(bsz, t_pad, D_MODEL)[:, CHUNK:]
    y_sample = x[rows_p:rows_p + rows_s].reshape(n_seq, SAMPLE_ROWS, D_MODEL)[:, :n_tok]
    return (y_prompt, y_sample) + tuple(jnp.stack(lst) for lst in new)
```

```python
import collections
import functools
import math

import numpy as np
import jax
import jax.numpy as jnp
from jax import lax
from jax.experimental import pallas as pl
from jax.experimental.pallas import tpu as pltpu

f32 = jnp.float32
bf16 = jnp.bfloat16

D_MODEL = 1024
N_META = 16
CHUNK = 64
GROUP = 256
N_HEADS = 4
HEAD_DIM = 64
GLA_DK = 32
GLA_RANK = 16
GLA_GATE_NORM = 16.0
CONV_W = 4
D_FF = 2816
DEPTH = 2
ALPHA = (2 * DEPTH) ** 0.25
PAST_LEN = 16384
RET_THETA_BASE = 10000.0
LN_EPS = 1e-5
NORM_EPS = 1e-6
GATE_CLAMP = 1e-6
IN_SPLITS = (GROUP, GROUP, GROUP, N_HEADS, N_HEADS, GROUP,
             GROUP, GROUP, GROUP, GROUP,
             N_HEADS * GLA_DK, N_HEADS * GLA_DK, GROUP, GLA_RANK, GROUP,
             GROUP, GROUP, GROUP, GROUP)

LANE = 128
SUBLANE = 8
VMEM_LIMIT = 56 * 1024 * 1024

COL_ORDER = (0, 1, 2, 5, 6, 7, 8, 9, 10, 11, 12, 14, 15, 16, 17, 18)
A_QKV, A_GATE = 0, 768
B_Q, B_F, B_I, B_GATE = 1024, 1280, 1536, 1792
C_Q, C_K, C_V, C_GATE = 2048, 2176, 2304, 2560
D_Q, D_K, D_V, D_GATE = 2816, 3072, 3328, 3584
SMALL = 3840
SM_BETA, SM_ALPHA, SM_LR = 0, N_HEADS, 2 * N_HEADS
D_IN_AL = SMALL + LANE
SAMPLE_ROWS = SUBLANE
CONV_HIST = SUBLANE
ROW_TILE = 512
PROJ_ROW_TILE = 256
PROMPT_GROUP = 2
SAMPLE_GROUP = 8


def _layer_norm(x, g, b):
    mu = jnp.mean(x, -1, keepdims=True)
    xc = x - mu
    var = jnp.mean(xc * xc, -1, keepdims=True)
    return xc * lax.rsqrt(var + LN_EPS) * g + b


def _row_specs(width, n_main, n_tail, tile=ROW_TILE):
    return (pl.BlockSpec((tile, width), lambda i: (jnp.minimum(i, n_main - 1), 0)),
            pl.BlockSpec((tile, width), lambda i: (jnp.clip(i - n_main, 0, n_tail - 1), 0)))


def _const_spec(shape, single_buffer=False):
    nd = len(shape)
    kw = dict(pipeline_mode=pl.Buffered(1)) if single_buffer else {}
    return pl.BlockSpec(shape, lambda *_: (0,) * nd, **kw)


def _on_region(n_main, main_fn, tail_fn):
    i = pl.program_id(0)
    pl.when(i < n_main)(main_fn)
    pl.when(i >= n_main)(tail_fn)


def _proj_tile(with_ln, x_ref, g_ref, b_ref, w_ref, p_ref, h_ref):
    x = x_ref[...]
    if with_ln:
        x = _layer_norm(x, g_ref[...], b_ref[...])
        h_ref[...] = x
    h = x.astype(bf16)
    n = w_ref.shape[1]
    step = 4 * LANE
    for n0 in range(0, n, step):
        n1 = min(n0 + step, n)
        p_ref[:, n0:n1] = jnp.dot(h, w_ref[:, n0:n1], preferred_element_type=f32)


def _proj_body(with_ln, n_main, xm_ref, xt_ref, g_ref, b_ref, w_ref, pm_ref, pt_ref, *h_refs):
    hm, ht = h_refs if with_ln else (None, None)
    _on_region(n_main,
               lambda: _proj_tile(with_ln, xm_ref, g_ref, b_ref, w_ref, pm_ref, hm),
               lambda: _proj_tile(with_ln, xt_ref, g_ref, b_ref, w_ref, pt_ref, ht))


def _proj(xm, xt, g, b, w, with_ln):
    tile = PROJ_ROW_TILE
    n_main, n_tail = xm.shape[0] // tile, xt.shape[0] // tile
    n = w.shape[1]
    out_shape = [jax.ShapeDtypeStruct((xm.shape[0], n), f32), jax.ShapeDtypeStruct((xt.shape[0], n), f32)]
    out_specs = list(_row_specs(n, n_main, n_tail, tile))
    if with_ln:
        out_shape += [jax.ShapeDtypeStruct(xm.shape, f32), jax.ShapeDtypeStruct(xt.shape, f32)]
        out_specs += list(_row_specs(D_MODEL, n_main, n_tail, tile))
    return pl.pallas_call(
        functools.partial(_proj_body, with_ln, n_main),
        out_shape=out_shape,
        grid=(n_main + n_tail,),
        in_specs=list(_row_specs(D_MODEL, n_main, n_tail, tile))
                 + [_const_spec((1, D_MODEL)), _const_spec((1, D_MODEL)), _const_spec((D_MODEL, n), True)],
        out_specs=out_specs,
        compiler_params=pltpu.CompilerParams(dimension_semantics=("arbitrary",),
                                             vmem_limit_bytes=VMEM_LIMIT),
        name="in_proj",
    )(xm, xt, g, b, w)


def _post_tile(x_ref, o_ref, wo_ref, g1_ref, b1_ref, wg_ref, wu_ref, wd_ref, g2_ref, b2_ref, y_ref):
    m = jnp.dot(o_ref[...], wo_ref[...], preferred_element_type=f32)
    y1 = _layer_norm(ALPHA * x_ref[...] + m, g1_ref[...], b1_ref[...])
    y1b = y1.astype(bf16)
    dff = wg_ref.shape[1]
    step = 4 * LANE
    acc = jnp.zeros(y1.shape, f32)
    for n0 in range(0, dff, step):
        n1 = min(n0 + step, dff)
        gate = jnp.dot(y1b, wg_ref[:, n0:n1], preferred_element_type=f32)
        up = jnp.dot(y1b, wu_ref[:, n0:n1], preferred_element_type=f32)
        act = (jax.nn.silu(gate) * up).astype(bf16)
        acc = acc + jnp.dot(act, wd_ref[n0:n1, :], preferred_element_type=f32)
    y_ref[...] = _layer_norm(ALPHA * y1 + acc, g2_ref[...], b2_ref[...])


def _post_body(n_main, xm_ref, xt_ref, om_ref, ot_ref, wo_ref, g1_ref, b1_ref, wg_ref, wu_ref, wd_ref,
               g2_ref, b2_ref, ym_ref, yt_ref):
    w = (wo_ref, g1_ref, b1_ref, wg_ref, wu_ref, wd_ref, g2_ref, b2_ref)
    _on_region(n_main, lambda: _post_tile(xm_ref, om_ref, *w, ym_ref), lambda: _post_tile(xt_ref, ot_ref, *w, yt_ref))


def _post(xm, xt, om, ot, wo, g1, b1, wg, wu, wd, g2, b2):
    n_main, n_tail = xm.shape[0] // ROW_TILE, xt.shape[0] // ROW_TILE
    rows = lambda: list(_row_specs(D_MODEL, n_main, n_tail))
    vec = lambda: _const_spec((1, D_MODEL))
    return pl.pallas_call(
        functools.partial(_post_body, n_main),
        out_shape=[jax.ShapeDtypeStruct(xm.shape, f32), jax.ShapeDtypeStruct(xt.shape, f32)],
        grid=(n_main + n_tail,),
        in_specs=rows() + rows() + [_const_spec((D_MODEL, D_MODEL), True), vec(), vec(),
                                    _const_spec((D_MODEL, D_FF), True), _const_spec((D_MODEL, D_FF), True),
                                    _const_spec((D_FF, D_MODEL), True), vec(), vec()],
        out_specs=rows(),
        compiler_params=pltpu.CompilerParams(dimension_semantics=("arbitrary",),
                                             vmem_limit_bytes=VMEM_LIMIT),
        name="post_ffn",
    )(xm, xt, om, ot, wo, g1, b1, wg, wu, wd, g2, b2)


def _levels(c):
    m, out = c // 2, []
    while m >= 1:
        out.append(m)
        m //= 2
    return tuple(out)


def _chunk_constants(c):
    lv = _levels(c)
    t = np.arange(c)
    rows, masks = [], []
    for m in lv:
        blk = t // m
        upper = (blk % 2) == 1
        lo = np.where(upper, blk * m, t + 1)
        hi = np.where(upper, t, (blk + 1) * m - 1)
        rows.append(((t[None, :] >= lo[:, None]) & (t[None, :] <= hi[:, None])).astype(np.float32))
        same = (t[:, None] // (2 * m)) == (t[None, :] // (2 * m))
        masks.append((same & upper[:, None] & (~upper)[None, :]).astype(np.float32))
    rows.append((t[None, :] <= t[:, None]).astype(np.float32))
    rows.append((t[None, :] > t[:, None]).astype(np.float32))
    masks.append(np.eye(c, dtype=np.float32))
    masks = np.stack(masks, 0)
    return np.concatenate(rows, 0), np.tile(masks, (1, 1, 2)), np.tile(masks, (1, 1, N_HEADS))


def _mm(a, b):
    return jnp.einsum('gmk,gkn->gmn', a.astype(bf16), b.astype(bf16), preferred_element_type=f32)


def _mm_nt(a, b):
    return jnp.einsum('gmk,gnk->gmn', a.astype(bf16), b.astype(bf16), preferred_element_type=f32)


def _mm_tn(a, b):
    return lax.dot_general(a.astype(bf16), b.astype(bf16), (((1,), (1,)), ((0,), (0,))),
                           preferred_element_type=f32)


def _iota(shape, dim):
    return lax.broadcasted_iota(jnp.int32, shape, dim)


def _bd(z, nblk):
    z = z.astype(bf16)
    d = z.shape[-1] // nblk
    blk = _iota((1, 1, z.shape[-1]), 2) // d
    return jnp.concatenate([jnp.where(blk == h, z, jnp.zeros_like(z)) for h in range(nblk)], axis=1)


def _hmm(x, z, nblk):
    return _mm(x, _bd(z, nblk))


def _hmm_nt(x, z, nblk):
    return _mm_nt(x, _bd(z, nblk))


def _hmm_split(x, z, nblk):
    m = x.shape[1]
    x_hi, z_hi = x.astype(bf16), z.astype(bf16)
    x_lo = (x - x_hi.astype(f32)).astype(bf16)
    z_lo = (z - z_hi.astype(f32)).astype(bf16)
    both = _mm(jnp.concatenate([x_hi, x_lo], 1), _bd(z_hi, nblk))
    return both[:, 0:m] + (both[:, m:2 * m] + _mm(x_hi, _bd(z_lo, nblk)))


def _block_mask(rows, lanes, dr, dl):
    return (_iota((rows, lanes), 0) // dr) == (_iota((rows, lanes), 1) // dl)


def _dot01(m01, x):
    nb = x.shape[0]
    x2 = jnp.concatenate([x[i] for i in range(nb)], -1) if nb > 1 else x[0]
    hi = x2.astype(bf16)
    r1 = x2 - hi.astype(f32)
    mid = r1.astype(bf16)
    lo = (r1 - mid.astype(f32)).astype(bf16)
    m01 = m01.astype(bf16)
    d = lambda y: jnp.dot(m01, y, preferred_element_type=f32)
    out = d(hi) + d(mid) + d(lo)
    w = x.shape[-1]
    return jnp.stack([out[:, i * w:(i + 1) * w] for i in range(nb)], 0)


def _seg_sum(x, ones_bd):
    nb, c, w = x.shape
    x2 = x.reshape(nb * c, w)
    hi = x2.astype(bf16)
    lo = (x2 - hi.astype(f32)).astype(bf16)
    out = jnp.dot(hi, ones_bd, preferred_element_type=f32) + jnp.dot(lo, ones_bd, preferred_element_type=f32)
    return out.reshape(nb, c, w)


def _last_row_as_col(b, c):
    return jnp.stack([b[i, c - SUBLANE:c, :].T[:, SUBLANE - 1:SUBLANE] for i in range(b.shape[0])], 0)


def _channel_decay_mixer(q, k, v, gl, s, nblk, sums_ref, masks_ref, bdmask, c):
    lv = _levels(c)
    nl = len(lv)
    sums = _dot01(sums_ref[...], gl)
    b = sums[:, nl * c:(nl + 1) * c]
    rev = sums[:, (nl + 1) * c:(nl + 2) * c]
    row = _iota((1, c, 1), 1)
    att = masks_ref[nl] * _hmm_nt(q, k, nblk)
    for li, m in enumerate(lv):
        upper = ((row // m) % 2) == 1
        w = jnp.where(upper, q, k) * jnp.exp(sums[:, li * c:(li + 1) * c])
        att = att + masks_ref[li] * _hmm_nt(w, w, nblk)
    o = _mm(q * jnp.exp(b), s) + _hmm(att, v, nblk)
    s_new = jnp.exp(_last_row_as_col(b, c)) * s + jnp.where(bdmask, _mm_tn(k * jnp.exp(rev), v), 0.0)
    return o, s_new


MixCfg = collections.namedtuple("MixCfg", "c g r0 r1 layer n_chunks init p4d o_rows")


def _pairs_to_bd(s):
    z = jnp.zeros(s.shape[:1] + (HEAD_DIM, HEAD_DIM), f32)
    pair = lambda p: jnp.concatenate([jnp.concatenate([s[:, 2 * p], z], -1),
                                      jnp.concatenate([z, s[:, 2 * p + 1]], -1)], 1)
    return jnp.concatenate([pair(0), pair(1)], 0)


def _quad_to_bd(s):
    z = lambda n: jnp.zeros(s.shape[:1] + (GLA_DK, n * HEAD_DIM), f32)
    rows = []
    for h in range(N_HEADS):
        parts = ([z(h)] if h else []) + [s[:, h]] + ([z(N_HEADS - 1 - h)] if h < N_HEADS - 1 else [])
        rows.append(jnp.concatenate(parts, -1))
    return jnp.concatenate(rows, 1)


def _mixer_body(cfg, *refs):
    c, g = cfg.c, cfg.g
    n_w, n_t, n_c = 7, 3, 3
    p_ref = refs[0]
    cw_ref, alog_ref, dtb_ref, lbl_ref, wg_ref, bg_ref, gains_ref = refs[1:1 + n_w]
    cos_ref, sina_ref, sinb_ref = refs[1 + n_w:1 + n_w + n_t]
    sums_ref, masks2_ref, masks4_ref = refs[1 + n_w + n_t:1 + n_w + n_t + n_c]
    pos = 1 + n_w + n_t + n_c
    if cfg.init:
        convi_ref, sai_ref, sbi_ref, sci_ref, sdi_ref = refs[pos:pos + 5]
        pos += 5
        if cfg.init == "per_seq":
            pos += 1
    o_ref, convo_ref, sao_ref, sbo_ref, sco_ref, sdo_ref = refs[pos:pos + 6]
    cbuf, sa_scr, sb_scr, sc_scr, sd_scr = refs[pos + 6:]
    ci = pl.program_id(1)

    @pl.when(ci == 0)
    def _():
        cbuf[:, 0:CONV_HIST, :] = jnp.zeros((g, CONV_HIST, 3 * GROUP), f32)
        if not cfg.init:
            for r in (sa_scr, sb_scr, sc_scr, sd_scr):
                r[...] = jnp.zeros(r.shape, f32)
        else:
            shared = cfg.init == "shared"
            take = lambda r: (jnp.broadcast_to(r[...], (g,) + r.shape[1:]) if shared else r[0])
            cbuf[:, CONV_HIST - (CONV_W - 1):CONV_HIST, :] = take(convi_ref)
            sa_scr[...] = _pairs_to_bd(take(sai_ref))
            sb_scr[...] = _pairs_to_bd(take(sbi_ref))
            sc_scr[...] = _quad_to_bd(take(sci_ref))
            sd_scr[...] = _pairs_to_bd(take(sdi_ref))

    def load(a, b):
        if cfg.p4d:
            return p_ref[0, :, :, a:b]
        return p_ref[:, a:b].reshape(g, c, b - a)

    r0, r1 = cfg.r0, cfg.r1
    n_valid = float(r1 - r0)
    row = _iota((1, c, 1), 1)
    vm = ((row >= r0) & (row < r1)).astype(f32)
    lane_v = _iota((1, 1, 2 * HEAD_DIM), 2)
    lane_s = _iota((1, 1, 2 * c), 2)
    s_idx = lane_s % c
    incl = s_idx <= row
    strict = s_idx < row
    bd2 = _block_mask(2 * HEAD_DIM, 2 * HEAD_DIM, HEAD_DIM, HEAD_DIM)
    bd4 = _block_mask(N_HEADS * GLA_DK, GROUP, GLA_DK, HEAD_DIM)
    ones2 = bd2.astype(bf16)
    pairs = lambda x: jnp.concatenate([x[..., :LANE], x[..., LANE:]], 0)
    unpairs = lambda x: jnp.concatenate([x[:g], x[g:]], -1)
    n_solve = max(1, math.ceil(math.log2(r1 - r0)))
    sm = load(SMALL, SMALL + LANE)

    cbuf[:, CONV_HIST:CONV_HIST + c, :] = load(A_QKV, A_QKV + 3 * GROUP) * vm
    acc = cbuf[:, pl.ds(CONV_HIST - 3, c), :] * cw_ref[0:1, :]
    for j in range(1, CONV_W):
        acc = acc + cbuf[:, pl.ds(CONV_HIST - 3 + j, c), :] * cw_ref[j:j + 1, :]
    conv = jax.nn.silu(acc)
    beta = jax.nn.sigmoid(sm) * vm
    g_a = -jnp.exp(alog_ref[...]) * jax.nn.softplus(sm + dtb_ref[...]) * vm
    nl = len(_levels(c))
    b_a = _dot01(sums_ref[nl * c:(nl + 1) * c, :], g_a)

    def head_cols(x, lane0):
        col = lambda h: x[..., lane0 + h:lane0 + h + 1]
        return jnp.concatenate([col(0), col(2)], 0), jnp.concatenate([col(1), col(3)], 0)

    def spread(c0, c1, lane, width):
        return jnp.where(lane < width, c0, c1)

    bc0, bc1 = head_cols(b_a, SM_ALPHA)
    bt0, bt1 = head_cols(beta, SM_BETA)
    bc_v, bet_v = spread(bc0, bc1, lane_v, HEAD_DIM), spread(bt0, bt1, lane_v, HEAD_DIM)
    bc_s, bet_s = spread(bc0, bc1, lane_s, c), spread(bt0, bt1, lane_s, c)
    b_rows = [b_a[i].T for i in range(g)]
    b_row = lambda h: jnp.stack([t[SM_ALPHA + h:SM_ALPHA + h + 1, :] for t in b_rows], 0)
    br_s = jnp.concatenate([jnp.concatenate([b_row(0), b_row(1)], -1),
                            jnp.concatenate([b_row(2), b_row(3)], -1)], 0)
    dec = jnp.where(incl, jnp.exp(jnp.where(incl, bc_s - br_s, 0.0)), 0.0)
    q = pairs(conv[..., 0:GROUP])
    k = pairs(conv[..., GROUP:2 * GROUP])
    v = pairs(conv[..., 2 * GROUP:3 * GROUP] * vm)
    q = q * lax.rsqrt(_seg_sum(q * q, ones2) + NORM_EPS) * HEAD_DIM ** -0.5
    k = k * lax.rsqrt(_seg_sum(k * k, ones2) + NORM_EPS) * vm
    s_a = sa_scr[...]
    kq = jnp.concatenate([k, q], 1)
    kq_k = _hmm_nt(kq, k, 2)
    kq_s = _mm(kq, s_a)
    a_mat = jnp.where(strict, bet_s * dec * kq_k[:, 0:c], 0.0)
    gam = jnp.exp(bc_v)
    u = bet_v * (v - gam * kq_s[:, 0:c])
    p = -a_mat
    for i in range(n_solve):
        u = u + _hmm_split(p, u, 2)
        if i + 1 < n_solve:
            p = _hmm_split(p, p, 2)
    o_a = gam * kq_s[:, c:2 * c] + _hmm(dec * kq_k[:, c:2 * c], u, 2)
    bl = bc_v[:, c - 1:c, :]
    sa_scr[...] = jnp.exp(bl) * s_a + jnp.where(bd2, _mm_tn(k * jnp.exp(bl - bc_v), u), 0.0)
    o_a = unpairs(o_a * lax.rsqrt(_seg_sum(o_a * o_a, ones2) * (1.0 / HEAD_DIM) + NORM_EPS))

    lbl = lbl_ref[...]
    e = jnp.exp(lbl - jnp.max(lbl, axis=0, keepdims=True))
    prob = e / jnp.sum(e, axis=0, keepdims=True)
    lb = (jnp.sum(prob[1:cfg.layer + 1], axis=0, keepdims=True) if cfg.layer > 0
          else jnp.zeros((1, GROUP), f32))
    kb = (1.0 - lb) * jax.nn.sigmoid(-load(B_F, B_F + GROUP))
    g_b = jnp.log1p(-jnp.clip(kb, 0.0, 1.0 - GATE_CLAMP)) * vm
    o_b, s_b_new = _channel_decay_mixer(
        pairs(jax.nn.silu(load(B_Q, B_Q + GROUP))), pairs(kb * vm), pairs(load(B_I, B_I + GROUP) * vm),
        pairs(g_b), sb_scr[...], 2, sums_ref, masks2_ref, bd2, c)
    sb_scr[...] = s_b_new
    o_b = unpairs(o_b * lax.rsqrt(_seg_sum(o_b * o_b, ones2) * (1.0 / HEAD_DIM) + NORM_EPS))

    logit = jnp.dot(sm.reshape(g * c, LANE).astype(bf16), wg_ref[...].astype(bf16),
                    preferred_element_type=f32).reshape(g, c, LANE) + bg_ref[...]
    g_c = jax.nn.log_sigmoid(logit) * (vm / GLA_GATE_NORM)
    o_c, s_c_new = _channel_decay_mixer(
        load(C_Q, C_Q + N_HEADS * GLA_DK) * GLA_DK ** -0.5, load(C_K, C_K + N_HEADS * GLA_DK) * vm,
        load(C_V, C_V + GROUP) * vm, g_c, sc_scr[...], N_HEADS, sums_ref, masks4_ref, bd4, c)
    sc_scr[...] = s_c_new
    ones4 = _block_mask(GROUP, GROUP, HEAD_DIM, HEAD_DIM).astype(bf16)
    o_c = o_c * lax.rsqrt(_seg_sum(o_c * o_c, ones4) * (1.0 / HEAD_DIM) + NORM_EPS)

    def rope(x):
        x2 = x.reshape(g * c, GROUP)
        rot = lambda sh: pltpu.roll(x2, sh, 1).reshape(g, c, GROUP)
        return x * cos_ref[...] + rot(GROUP - HEAD_DIM // 2) * sina_ref[...] + rot(HEAD_DIM // 2) * sinb_ref[...]
    q = pairs(rope(load(D_Q, D_Q + GROUP)))
    k = pairs(rope(load(D_K, D_K + GROUP)) * (vm * HEAD_DIM ** -0.5))
    v = pairs(load(D_V, D_V + GROUP) * vm)
    pair_idx = _iota((2 * g, 1, 1), 0) // g
    log_gamma = lambda lane, width: jnp.log1p(-jnp.exp2(-5.0 - (2 * pair_idx + lane // width).astype(f32)))
    lg_v, lg_s = log_gamma(lane_v, HEAD_DIM), log_gamma(lane_s, c)
    cnt_c = jnp.clip(row + 1 - r0, 0, r1 - r0).astype(f32)
    cnt_s = jnp.clip(s_idx + 1 - r0, 0, r1 - r0).astype(f32)
    dec = jnp.where(incl, jnp.exp(jnp.where(incl, (cnt_c - cnt_s) * lg_s, 0.0)), 0.0)
    s_d = sd_scr[...]
    o_d = _mm(q * jnp.exp(cnt_c * lg_v), s_d) + _hmm(dec * _hmm_nt(q, k, 2), v, 2)
    sd_scr[...] = (jnp.exp(n_valid * lg_v) * s_d
                   + jnp.where(bd2, _mm_tn(k * jnp.exp((n_valid - cnt_c) * lg_v), v), 0.0))
    oc = o_d - _seg_sum(o_d, ones2) * (1.0 / HEAD_DIM)
    o_d = unpairs(oc * lax.rsqrt(_seg_sum(oc * oc, ones2) * (1.0 / HEAD_DIM) + LN_EPS))

    for i, (o_m, gate_col) in enumerate(zip((o_a, o_b, o_c, o_d), (A_GATE, B_GATE, C_GATE, D_GATE))):
        val = (o_m * gains_ref[i:i + 1, :] * jax.nn.silu(load(gate_col, gate_col + GROUP))).astype(o_ref.dtype)
        if cfg.p4d:
            o_ref[0, :, :, i * GROUP:(i + 1) * GROUP] = val
        else:
            o_ref[0:g * c, i * GROUP:(i + 1) * GROUP] = val.reshape(g * c, GROUP)
    if cfg.o_rows > g * c:
        o_ref[g * c:cfg.o_rows, :] = jnp.zeros((cfg.o_rows - g * c, D_MODEL), o_ref.dtype)
    cbuf[:, 0:CONV_HIST, :] = cbuf[:, c:c + CONV_HIST, :]

    @pl.when(ci == cfg.n_chunks - 1)
    def _():
        convo_ref[...] = cbuf[:, pl.ds(CONV_HIST + r1 - (CONV_W - 1), CONV_W - 1), :]
        for scr, out in ((sa_scr, sao_ref), (sb_scr, sbo_ref), (sd_scr, sdo_ref)):
            s = scr[...]
            for p in range(2):
                out[:, 2 * p] = s[p * g:(p + 1) * g, 0:HEAD_DIM, 0:HEAD_DIM]
                out[:, 2 * p + 1] = s[p * g:(p + 1) * g, HEAD_DIM:2 * HEAD_DIM, HEAD_DIM:2 * HEAD_DIM]
        s = sc_scr[...]
        for h in range(N_HEADS):
            sco_ref[:, h] = s[:, h * GLA_DK:(h + 1) * GLA_DK, h * HEAD_DIM:(h + 1) * HEAD_DIM]


def _state_shapes(n):
    return [jax.ShapeDtypeStruct((n, CONV_W - 1, 3 * GROUP), f32),
            jax.ShapeDtypeStruct((n, N_HEADS, HEAD_DIM, HEAD_DIM), f32),
            jax.ShapeDtypeStruct((n, N_HEADS, HEAD_DIM, HEAD_DIM), f32),
            jax.ShapeDtypeStruct((n, N_HEADS, GLA_DK, HEAD_DIM), f32),
            jax.ShapeDtypeStruct((n, N_HEADS, HEAD_DIM, HEAD_DIM), f32)]


def _state_out_specs(g):
    st = lambda k: pl.BlockSpec((g, N_HEADS, k, HEAD_DIM), lambda b, i: (b, 0, 0, 0))
    return [pl.BlockSpec((g, CONV_W - 1, 3 * GROUP), lambda b, i: (b, 0, 0)),
            st(HEAD_DIM), st(HEAD_DIM), st(GLA_DK), st(HEAD_DIM)]


def _mixer_call(cfg, name, n_groups, p, p_spec, wts, tables, table_spec, consts, init, init_specs,
                o_shape, o_spec, n_state_rows, o_prev=None):
    c, g = cfg.c, cfg.g
    nl = len(_levels(c))
    wspecs = [_const_spec((CONV_W, 3 * GROUP)), _const_spec((1, LANE)), _const_spec((1, LANE)),
              _const_spec((DEPTH, GROUP)), _const_spec((LANE, LANE)), _const_spec((1, LANE)),
              _const_spec((4, GROUP))]
    cspecs = [_const_spec(((nl + 2) * c, c)), _const_spec((nl + 1, c, 2 * c)), _const_spec((nl + 1, c, N_HEADS * c))]
    in_specs = [p_spec] + wspecs + [table_spec] * 3 + cspecs + list(init_specs)
    args = [p, *wts, *tables, *consts, *init]
    aliases = {}
    if o_prev is not None:
        in_specs.append(pl.BlockSpec(memory_space=pl.ANY))
        aliases = {len(args): 0}
        args.append(o_prev)
    return pl.pallas_call(
        functools.partial(_mixer_body, cfg),
        out_shape=[o_shape] + _state_shapes(n_state_rows),
        grid=(n_groups, cfg.n_chunks),
        in_specs=in_specs,
        out_specs=[o_spec] + _state_out_specs(g),
        scratch_shapes=[pltpu.VMEM((g, c + CONV_HIST, 3 * GROUP), f32),
                        pltpu.VMEM((2 * g, 2 * HEAD_DIM, 2 * HEAD_DIM), f32),
                        pltpu.VMEM((2 * g, 2 * HEAD_DIM, 2 * HEAD_DIM), f32),
                        pltpu.VMEM((g, N_HEADS * GLA_DK, GROUP), f32),
                        pltpu.VMEM((2 * g, 2 * HEAD_DIM, 2 * HEAD_DIM), f32)],
        input_output_aliases=aliases,
        compiler_params=pltpu.CompilerParams(dimension_semantics=("arbitrary", "arbitrary"),
                                             vmem_limit_bytes=VMEM_LIMIT),
        name=name,
    )(*args)


def _rope_tables(pos):
    half = HEAD_DIM // 2
    inv = 1.0 / (RET_THETA_BASE ** jnp.linspace(0.0, 1.0, half, dtype=f32))
    ang = pos[:, None] * inv[None, :]
    cos, sin = jnp.cos(ang), jnp.sin(ang)
    zero = jnp.zeros_like(sin)
    tile = lambda a, b: jnp.tile(jnp.concatenate([a, b], -1), (1, N_HEADS))
    return tile(cos, cos), tile(-sin, zero), tile(zero, sin)


def _align_w_in(w):
    offs = np.concatenate([[0], np.cumsum(IN_SPLITS)])
    seg = lambda i: w[:, offs[i]:offs[i + 1]]
    small = jnp.concatenate([seg(3), seg(4), seg(13)], -1)
    small = jnp.pad(small, ((0, 0), (0, LANE - small.shape[1])))
    return jnp.concatenate([seg(i) for i in COL_ORDER] + [small], -1).astype(bf16)


def kernel(x_prompt, x_sample, state_delta_conv, state_delta, state_hgrn, state_gla, state_ret, meta_tokens, emb_ln_g, emb_ln_b, w_in, conv_w, delta_a_log, delta_dt_bias, delta_norm_g, hgrn_lb_logits, hgrn_norm_g, gla_w_gate, gla_b_gate, gla_norm_g, ret_norm_g, w_out, ln1_g, ln1_b, w_ffn_gate, w_ffn_up, w_ffn_down, ln2_g, ln2_b):
    bsz, seq, _ = x_prompt.shape
    n_seq, n_tok, _ = x_sample.shape
    pg = PROMPT_GROUP if bsz % PROMPT_GROUP == 0 else 1
    sg = SAMPLE_GROUP if n_seq % SAMPLE_GROUP == 0 else 1
    assert seq % CHUNK == 0 and (bsz * seq) % ROW_TILE == 0 and N_META <= CHUNK
    assert CONV_W - 1 <= n_tok <= SAMPLE_ROWS and ROW_TILE % (sg * SAMPLE_ROWS) == 0
    lead_pad = CHUNK - N_META
    rows_main = bsz * seq
    rows_s = n_seq * SAMPLE_ROWS
    rows_tail = ROW_TILE + -(-rows_s // ROW_TILE) * ROW_TILE

    xs = jnp.pad(x_sample, ((0, 0), (0, SAMPLE_ROWS - n_tok), (0, 0))).reshape(rows_s, D_MODEL)
    xt = jnp.concatenate([jnp.zeros((lead_pad, D_MODEL), f32), meta_tokens.astype(f32),
                          jnp.zeros((ROW_TILE - CHUNK, D_MODEL), f32), xs,
                          jnp.zeros((rows_tail - ROW_TILE - rows_s, D_MODEL), f32)], 0)
    xm = x_prompt.reshape(rows_main, D_MODEL)

    tab_m = _rope_tables(jnp.arange(CHUNK, dtype=f32) - lead_pad)
    tab_p = _rope_tables(N_META + jnp.arange(seq, dtype=f32))
    tab_s = _rope_tables(PAST_LEN + jnp.arange(SAMPLE_ROWS, dtype=f32))
    consts_c = tuple(jnp.asarray(a, f32) for a in _chunk_constants(CHUNK))
    consts_s = tuple(jnp.asarray(a, f32) for a in _chunk_constants(SAMPLE_ROWS))
    row_vec = lambda v: v.reshape(1, -1).astype(f32)
    small_row = lambda v: jnp.zeros((1, LANE), f32).at[0, SM_ALPHA:SM_ALPHA + N_HEADS].set(v)
    n_chunks = seq // CHUNK
    blocks_per_tile = ROW_TILE // (sg * SAMPLE_ROWS)

    new = [[] for _ in range(10)]
    for l in range(DEPTH):
        res = _proj(xm, xt, row_vec(emb_ln_g), row_vec(emb_ln_b), _align_w_in(w_in[l]), with_ln=(l == 0))
        pm, pt = res[0], res[1]
        if l == 0:
            xm, xt = res[2], res[3]
        wg = jnp.zeros((LANE, N_HEADS * GLA_DK), f32).at[SM_LR:SM_LR + GLA_RANK].set(gla_w_gate[l])
        gains = jnp.stack([delta_norm_g[l], hgrn_norm_g[l], gla_norm_g[l], ret_norm_g[l]], 0)
        wts = (conv_w[l], small_row(delta_a_log[l]), small_row(delta_dt_bias[l]), hgrn_lb_logits,
               wg, row_vec(gla_b_gate[l]), gains)

        cfg = MixCfg(CHUNK, 1, lead_pad, CHUNK, l, 1, None, False, ROW_TILE)
        ot, *st_meta = _mixer_call(
            cfg, "mixers_meta", 1, pt, pl.BlockSpec((CHUNK, D_IN_AL), lambda b, i: (0, 0)),
            wts, tab_m, pl.BlockSpec((CHUNK, GROUP), lambda b, i: (0, 0)), consts_c, (), (),
            jax.ShapeDtypeStruct((rows_tail, D_MODEL), bf16),
            pl.BlockSpec((ROW_TILE, D_MODEL), lambda b, i: (0, 0)), 1)

        cfg = MixCfg(CHUNK, pg, 0, CHUNK, l, n_chunks, "shared", True, pg * CHUNK)
        st1 = lambda k: pl.BlockSpec((1, N_HEADS, k, HEAD_DIM), lambda b, i: (0, 0, 0, 0))
        om, *st_p = _mixer_call(
            cfg, "mixers_prompt", bsz // pg, pm.reshape(bsz // pg, pg, seq, D_IN_AL),
            pl.BlockSpec((1, pg, CHUNK, D_IN_AL), lambda b, i: (b, 0, i, 0)),
            wts, tab_p, pl.BlockSpec((CHUNK, GROUP), lambda b, i: (i, 0)), consts_c, st_meta,
            [pl.BlockSpec((1, CONV_W - 1, 3 * GROUP), lambda b, i: (0, 0, 0)),
             st1(HEAD_DIM), st1(HEAD_DIM), st1(GLA_DK), st1(HEAD_DIM)],
            jax.ShapeDtypeStruct((bsz // pg, pg, seq, D_MODEL), bf16),
            pl.BlockSpec((1, pg, CHUNK, D_MODEL), lambda b, i: (b, 0, i, 0)), bsz)
        om = om.reshape(rows_main, D_MODEL)

        cfg = MixCfg(SAMPLE_ROWS, sg, 0, n_tok, l, 1, "per_seq", False, sg * SAMPLE_ROWS)
        sti = lambda k: pl.BlockSpec((1, sg, N_HEADS, k, HEAD_DIM), lambda b, i: (l, b, 0, 0, 0))
        ot, *st_s = _mixer_call(
            cfg, "mixers_sample", n_seq // sg, pt,
            pl.BlockSpec((sg * SAMPLE_ROWS, D_IN_AL), lambda b, i: (blocks_per_tile + b, 0)),
            wts, tab_s, pl.BlockSpec((SAMPLE_ROWS, GROUP), lambda b, i: (0, 0)), consts_s,
            (state_delta_conv, state_delta, state_hgrn, state_gla, state_ret),
            [pl.BlockSpec((1, sg, CONV_W - 1, 3 * GROUP), lambda b, i: (l, b, 0, 0)),
             sti(HEAD_DIM), sti(HEAD_DIM), sti(GLA_DK), sti(HEAD_DIM)],
            jax.ShapeDtypeStruct((rows_tail, D_MODEL), bf16),
            pl.BlockSpec((sg * SAMPLE_ROWS, D_MODEL), lambda b, i: (blocks_per_tile + b, 0)), n_seq,
            o_prev=ot)

        for lst, s in zip(new, [v for pair in zip(st_p, st_s) for v in pair]):
            lst.append(s)
        xm, xt = _post(xm, xt, om, ot,
                       w_out[l].astype(bf16), row_vec(ln1_g[l]), row_vec(ln1_b[l]),
                       w_ffn_gate[l].astype(bf16), w_ffn_up[l].astype(bf16), w_ffn_down[l].astype(bf16),
                       row_vec(ln2_g[l]), row_vec(ln2_b[l]))

    y_prompt = xm.reshape(bsz, seq, D_MODEL)
    y_sample = xt[ROW_TILE:ROW_TILE + rows_s].reshape(n_seq, SAMPLE_ROWS, D_MODEL)[:, :n_tok]
    return (y_prompt, y_sample) + tuple(jnp.stack(lst) for lst in new)
```

```python
import collections
import functools
import math

import numpy as np
import jax
import jax.numpy as jnp
from jax import lax
from jax.experimental import pallas as pl
from jax.experimental.pallas import tpu as pltpu

f32 = jnp.float32
bf16 = jnp.bfloat16

D_MODEL = 1024
N_META = 16
CHUNK = 64
GROUP = 256
N_HEADS = 4
HEAD_DIM = 64
GLA_DK = 32
GLA_RANK = 16
GLA_GATE_NORM = 16.0
CONV_W = 4
D_FF = 2816
DEPTH = 2
ALPHA = (2 * DEPTH) ** 0.25
PAST_LEN = 16384
RET_THETA_BASE = 10000.0
LN_EPS = 1e-5
NORM_EPS = 1e-6
GATE_CLAMP = 1e-6
IN_SPLITS = (GROUP, GROUP, GROUP, N_HEADS, N_HEADS, GROUP,
             GROUP, GROUP, GROUP, GROUP,
             N_HEADS * GLA_DK, N_HEADS * GLA_DK, GROUP, GLA_RANK, GROUP,
             GROUP, GROUP, GROUP, GROUP)

LANE = 128
SUBLANE = 8
VMEM_LIMIT = 56 * 1024 * 1024

COL_ORDER = (0, 1, 2, 5, 6, 7, 8, 9, 10, 11, 12, 14, 15, 16, 17, 18)
A_QKV, A_GATE = 0, 768
B_Q, B_F, B_I, B_GATE = 1024, 1280, 1536, 1792
C_Q, C_K, C_V, C_GATE = 2048, 2176, 2304, 2560
D_Q, D_K, D_V, D_GATE = 2816, 3072, 3328, 3584
SMALL = 3840
SM_BETA, SM_ALPHA, SM_LR = 0, N_HEADS, 2 * N_HEADS
D_IN_AL = SMALL + LANE
SAMPLE_ROWS = SUBLANE
CONV_HIST = SUBLANE
ROW_TILE = 512
PROJ_ROW_TILE = 256
PROJ_STAGE_ROWS = 256
POST_STAGE_ROWS_WIDE = 128
POST_STAGE_ROWS_NARROW = 352
PROMPT_GROUP = 8
SAMPLE_GROUP = 8


def _layer_norm(x, g, b):
    mu = jnp.mean(x, -1, keepdims=True)
    xc = x - mu
    var = jnp.mean(xc * xc, -1, keepdims=True)
    return xc * lax.rsqrt(var + LN_EPS) * g + b


def _row_specs(width, n_main, n_tail, tile=ROW_TILE):
    return (pl.BlockSpec((tile, width), lambda i: (jnp.minimum(i, n_main - 1), 0)),
            pl.BlockSpec((tile, width), lambda i: (jnp.clip(i - n_main, 0, n_tail - 1), 0)))


def _const_spec(shape, single_buffer=False):
    nd = len(shape)
    kw = dict(pipeline_mode=pl.Buffered(1)) if single_buffer else {}
    return pl.BlockSpec(shape, lambda *_: (0,) * nd, **kw)


def _on_region(n_main, main_fn, tail_fn):
    i = pl.program_id(0)
    pl.when(i < n_main)(main_fn)
    pl.when(i >= n_main)(tail_fn)


def _stage_rows(w_hbm, layer, n_rows, chunk, stage, sem, store):
    assert n_rows % chunk == 0 and chunk <= stage.shape[1]
    n = n_rows // chunk

    def copy(i):
        return pltpu.make_async_copy(w_hbm.at[layer, pl.ds(i * chunk, chunk), :],
                                     stage.at[i % 2, pl.ds(0, chunk), :], sem.at[i % 2])
    copy(0).start()
    for i in range(n):
        if i + 1 < n:
            copy(i + 1).start()
        copy(i).wait()
        store(i * chunk, stage.at[i % 2, pl.ds(0, chunk), :])


def _w_in_segments():
    offs = np.concatenate([[0], np.cumsum(IN_SPLITS)])
    runs, dst = [], 0
    for i in COL_ORDER + (3, 4, 13):
        if runs and runs[-1][1] == offs[i]:
            runs[-1][1] = int(offs[i + 1])
        else:
            runs.append([int(offs[i]), int(offs[i + 1]), dst])
        dst += IN_SPLITS[i]
    return [tuple(r) for r in runs]


def _proj_tile(with_ln, x_ref, g_ref, b_ref, w_ref, p_ref, h_ref):
    x = x_ref[...]
    if with_ln:
        x = _layer_norm(x, g_ref[...], b_ref[...])
        h_ref[...] = x
    h = x.astype(bf16)
    n = w_ref.shape[1]
    step = 4 * LANE
    for n0 in range(0, n, step):
        n1 = min(n0 + step, n)
        p_ref[:, n0:n1] = jnp.dot(h, w_ref[:, n0:n1], preferred_element_type=f32)


def _proj_body(with_ln, layer, n_main, xm_ref, xt_ref, g_ref, b_ref, w_hbm, pm_ref, pt_ref, *rest):
    (hm, ht), (w_scr, stage, sem) = (rest[:2], rest[2:]) if with_ln else ((None, None), rest)

    @pl.when(pl.program_id(0) == 0)
    def _():
        w_scr[:, SMALL:SMALL + LANE] = jnp.zeros((D_MODEL, LANE), bf16)

        def store(r0, ref):
            for s0, s1, d0 in _w_in_segments():
                w_scr[r0:r0 + ref.shape[0], d0:d0 + s1 - s0] = ref[:, s0:s1].astype(bf16)
        _stage_rows(w_hbm, layer, D_MODEL, PROJ_STAGE_ROWS, stage, sem, store)

    _on_region(n_main,
               lambda: _proj_tile(with_ln, xm_ref, g_ref, b_ref, w_scr, pm_ref, hm),
               lambda: _proj_tile(with_ln, xt_ref, g_ref, b_ref, w_scr, pt_ref, ht))


def _proj(xm, xt, g, b, w_in, layer, with_ln):
    tile = PROJ_ROW_TILE
    n_main, n_tail = xm.shape[0] // tile, xt.shape[0] // tile
    n = D_IN_AL
    out_shape = [jax.ShapeDtypeStruct((xm.shape[0], n), f32), jax.ShapeDtypeStruct((xt.shape[0], n), f32)]
    out_specs = list(_row_specs(n, n_main, n_tail, tile))
    if with_ln:
        out_shape += [jax.ShapeDtypeStruct(xm.shape, f32), jax.ShapeDtypeStruct(xt.shape, f32)]
        out_specs += list(_row_specs(D_MODEL, n_main, n_tail, tile))
    return pl.pallas_call(
        functools.partial(_proj_body, with_ln, layer, n_main),
        out_shape=out_shape,
        grid=(n_main + n_tail,),
        in_specs=list(_row_specs(D_MODEL, n_main, n_tail, tile))
                 + [_const_spec((1, D_MODEL)), _const_spec((1, D_MODEL)), pl.BlockSpec(memory_space=pl.ANY)],
        out_specs=out_specs,
        scratch_shapes=[pltpu.VMEM((D_MODEL, n), bf16),
                        pltpu.VMEM((2, PROJ_STAGE_ROWS, w_in.shape[2]), f32),
                        pltpu.SemaphoreType.DMA((2,))],
        compiler_params=pltpu.CompilerParams(dimension_semantics=("arbitrary",),
                                             vmem_limit_bytes=VMEM_LIMIT),
        name="in_proj",
    )(xm, xt, g, b, w_in)


def _post_tile(x_ref, o_ref, wo_ref, g1_ref, b1_ref, wg_ref, wu_ref, wd_ref, g2_ref, b2_ref, y_ref):
    m = jnp.dot(o_ref[...], wo_ref[...], preferred_element_type=f32)
    y1 = _layer_norm(ALPHA * x_ref[...] + m, g1_ref[...], b1_ref[...])
    y1b = y1.astype(bf16)
    dff = wg_ref.shape[1]
    step = 4 * LANE
    acc = jnp.zeros(y1.shape, f32)
    for n0 in range(0, dff, step):
        n1 = min(n0 + step, dff)
        gate = jnp.dot(y1b, wg_ref[:, n0:n1], preferred_element_type=f32)
        up = jnp.dot(y1b, wu_ref[:, n0:n1], preferred_element_type=f32)
        act = (jax.nn.silu(gate) * up).astype(bf16)
        acc = acc + jnp.dot(act, wd_ref[n0:n1, :], preferred_element_type=f32)
    y_ref[...] = _layer_norm(ALPHA * y1 + acc, g2_ref[...], b2_ref[...])


def _post_body(layer, n_main, xm_ref, xt_ref, om_ref, ot_ref, g1_ref, b1_ref, g2_ref, b2_ref,
               wo_hbm, wg_hbm, wu_hbm, wd_hbm, ym_ref, yt_ref,
               wo_scr, wg_scr, wu_scr, wd_scr, stage_wide, stage_narrow, sem):
    @pl.when(pl.program_id(0) == 0)
    def _():
        def into(dst):
            def store(r0, ref):
                dst[r0:r0 + ref.shape[0], :] = ref[...].astype(bf16)
            return store
        _stage_rows(wo_hbm, layer, D_MODEL, PROJ_STAGE_ROWS, stage_narrow, sem, into(wo_scr))
        _stage_rows(wg_hbm, layer, D_MODEL, POST_STAGE_ROWS_WIDE, stage_wide, sem, into(wg_scr))
        _stage_rows(wu_hbm, layer, D_MODEL, POST_STAGE_ROWS_WIDE, stage_wide, sem, into(wu_scr))
        _stage_rows(wd_hbm, layer, D_FF, POST_STAGE_ROWS_NARROW, stage_narrow, sem, into(wd_scr))

    w = (wo_scr, g1_ref, b1_ref, wg_scr, wu_scr, wd_scr, g2_ref, b2_ref)
    _on_region(n_main, lambda: _post_tile(xm_ref, om_ref, *w, ym_ref), lambda: _post_tile(xt_ref, ot_ref, *w, yt_ref))


def _post(xm, xt, om, ot, g1, b1, g2, b2, w_out, w_gate, w_up, w_down, layer):
    n_main, n_tail = xm.shape[0] // ROW_TILE, xt.shape[0] // ROW_TILE
    rows = lambda: list(_row_specs(D_MODEL, n_main, n_tail))
    vec = lambda: _const_spec((1, D_MODEL))
    hbm = lambda: pl.BlockSpec(memory_space=pl.ANY)
    return pl.pallas_call(
        functools.partial(_post_body, layer, n_main),
        out_shape=[jax.ShapeDtypeStruct(xm.shape, f32), jax.ShapeDtypeStruct(xt.shape, f32)],
        grid=(n_main + n_tail,),
        in_specs=rows() + rows() + [vec(), vec(), vec(), vec(), hbm(), hbm(), hbm(), hbm()],
        out_specs=rows(),
        scratch_shapes=[pltpu.VMEM((D_MODEL, D_MODEL), bf16), pltpu.VMEM((D_MODEL, D_FF), bf16),
                        pltpu.VMEM((D_MODEL, D_FF), bf16), pltpu.VMEM((D_FF, D_MODEL), bf16),
                        pltpu.VMEM((2, POST_STAGE_ROWS_WIDE, D_FF), f32),
                        pltpu.VMEM((2, POST_STAGE_ROWS_NARROW, D_MODEL), f32),
                        pltpu.SemaphoreType.DMA((2,))],
        compiler_params=pltpu.CompilerParams(dimension_semantics=("arbitrary",),
                                             vmem_limit_bytes=VMEM_LIMIT),
        name="post_ffn",
    )(xm, xt, om, ot, g1, b1, g2, b2, w_out, w_gate, w_up, w_down)


def _levels(c):
    m, out = c // 2, []
    while m >= 1:
        out.append(m)
        m //= 2
    return tuple(out)


def _chunk_constants(c):
    lv = _levels(c)
    t = np.arange(c)
    rows, masks = [], []
    for m in lv:
        blk = t // m
        upper = (blk % 2) == 1
        lo = np.where(upper, blk * m, t + 1)
        hi = np.where(upper, t, (blk + 1) * m - 1)
        rows.append(((t[None, :] >= lo[:, None]) & (t[None, :] <= hi[:, None])).astype(np.float32))
        same = (t[:, None] // (2 * m)) == (t[None, :] // (2 * m))
        masks.append((same & upper[:, None] & (~upper)[None, :]).astype(np.float32))
    rows.append((t[None, :] <= t[:, None]).astype(np.float32))
    rows.append((t[None, :] > t[:, None]).astype(np.float32))
    masks.append(np.eye(c, dtype=np.float32))
    masks = np.stack(masks, 0)
    return np.concatenate(rows, 0), np.tile(masks, (1, 1, 2)), np.tile(masks, (1, 1, N_HEADS))


def _mm(a, b):
    return jnp.einsum('gmk,gkn->gmn', a.astype(bf16), b.astype(bf16), preferred_element_type=f32)


def _mm_nt(a, b):
    return jnp.einsum('gmk,gnk->gmn', a.astype(bf16), b.astype(bf16), preferred_element_type=f32)


def _mm_tn(a, b):
    return lax.dot_general(a.astype(bf16), b.astype(bf16), (((1,), (1,)), ((0,), (0,))),
                           preferred_element_type=f32)


def _iota(shape, dim):
    return lax.broadcasted_iota(jnp.int32, shape, dim)


def _bd(z, nblk):
    z = z.astype(bf16)
    d = z.shape[-1] // nblk
    blk = _iota((1, 1, z.shape[-1]), 2) // d
    return jnp.concatenate([jnp.where(blk == h, z, jnp.zeros_like(z)) for h in range(nblk)], axis=1)


def _hmm(x, z, nblk):
    return _mm(x, _bd(z, nblk))


def _hmm_nt(x, z, nblk):
    return _mm_nt(x, _bd(z, nblk))


def _hmm_split(x, z, nblk):
    m = x.shape[1]
    x_hi, z_hi = x.astype(bf16), z.astype(bf16)
    x_lo = (x - x_hi.astype(f32)).astype(bf16)
    z_lo = (z - z_hi.astype(f32)).astype(bf16)
    both = _mm(jnp.concatenate([x_hi, x_lo], 1), _bd(z_hi, nblk))
    return both[:, 0:m] + (both[:, m:2 * m] + _mm(x_hi, _bd(z_lo, nblk)))


def _block_mask(rows, lanes, dr, dl):
    return (_iota((rows, lanes), 0) // dr) == (_iota((rows, lanes), 1) // dl)


def _dot01(m01, x):
    nb = x.shape[0]
    x2 = jnp.concatenate([x[i] for i in range(nb)], -1) if nb > 1 else x[0]
    hi = x2.astype(bf16)
    r1 = x2 - hi.astype(f32)
    mid = r1.astype(bf16)
    lo = (r1 - mid.astype(f32)).astype(bf16)
    m01 = m01.astype(bf16)
    d = lambda y: jnp.dot(m01, y, preferred_element_type=f32)
    out = d(hi) + d(mid) + d(lo)
    w = x.shape[-1]
    return jnp.stack([out[:, i * w:(i + 1) * w] for i in range(nb)], 0)


def _seg_sum(x, ones_bd):
    nb, c, w = x.shape
    x2 = x.reshape(nb * c, w)
    hi = x2.astype(bf16)
    lo = (x2 - hi.astype(f32)).astype(bf16)
    out = jnp.dot(hi, ones_bd, preferred_element_type=f32) + jnp.dot(lo, ones_bd, preferred_element_type=f32)
    return out.reshape(nb, c, w)


def _last_row_as_col(b, c):
    return jnp.stack([b[i, c - SUBLANE:c, :].T[:, SUBLANE - 1:SUBLANE] for i in range(b.shape[0])], 0)


def _channel_decay_mixer(q, k, v, gl, s, nblk, sums_ref, masks_ref, bdmask, c):
    lv = _levels(c)
    nl = len(lv)
    sums = _dot01(sums_ref[...], gl)
    b = sums[:, nl * c:(nl + 1) * c]
    rev = sums[:, (nl + 1) * c:(nl + 2) * c]
    row = _iota((1, c, 1), 1)
    att = masks_ref[nl] * _hmm_nt(q, k, nblk)
    for li, m in enumerate(lv):
        upper = ((row // m) % 2) == 1
        w = jnp.where(upper, q, k) * jnp.exp(sums[:, li * c:(li + 1) * c])
        att = att + masks_ref[li] * _hmm_nt(w, w, nblk)
    o = _mm(q * jnp.exp(b), s) + _hmm(att, v, nblk)
    s_new = jnp.exp(_last_row_as_col(b, c)) * s + jnp.where(bdmask, _mm_tn(k * jnp.exp(rev), v), 0.0)
    return o, s_new


MixCfg = collections.namedtuple("MixCfg", "c g r0 r1 layer n_chunks init p4d o_rows n_alias stacked")


def _pairs_to_bd(s):
    z = jnp.zeros(s.shape[:1] + (HEAD_DIM, HEAD_DIM), f32)
    pair = lambda p: jnp.concatenate([jnp.concatenate([s[:, 2 * p], z], -1),
                                      jnp.concatenate([z, s[:, 2 * p + 1]], -1)], 1)
    return jnp.concatenate([pair(0), pair(1)], 0)


def _quad_to_bd(s):
    z = lambda n: jnp.zeros(s.shape[:1] + (GLA_DK, n * HEAD_DIM), f32)
    rows = []
    for h in range(N_HEADS):
        parts = ([z(h)] if h else []) + [s[:, h]] + ([z(N_HEADS - 1 - h)] if h < N_HEADS - 1 else [])
        rows.append(jnp.concatenate(parts, -1))
    return jnp.concatenate(rows, 1)


def _mixer_body(cfg, *refs):
    c, g = cfg.c, cfg.g
    n_w, n_t, n_c = 7, 3, 3
    p_ref = refs[0]
    cw_ref, alog_ref, dtb_ref, lbl_ref, wg_ref, bg_ref, gains_ref = refs[1:1 + n_w]
    cos_ref, sina_ref, sinb_ref = refs[1 + n_w:1 + n_w + n_t]
    sums_ref, masks2_ref, masks4_ref = refs[1 + n_w + n_t:1 + n_w + n_t + n_c]
    pos = 1 + n_w + n_t + n_c
    if cfg.init:
        convi_ref, sai_ref, sbi_ref, sci_ref, sdi_ref = refs[pos:pos + 5]
        pos += 5
    pos += cfg.n_alias
    o_ref, convo_ref, sao_ref, sbo_ref, sco_ref, sdo_ref = refs[pos:pos + 6]
    if cfg.stacked:
        convo_ref, sao_ref, sbo_ref, sco_ref, sdo_ref = (r.at[0] for r in (convo_ref, sao_ref, sbo_ref, sco_ref, sdo_ref))
    cbuf, sa_scr, sb_scr, sc_scr, sd_scr = refs[pos + 6:]
    ci = pl.program_id(1)

    @pl.when(ci == 0)
    def _():
        cbuf[:, 0:CONV_HIST, :] = jnp.zeros((g, CONV_HIST, 3 * GROUP), f32)
        if not cfg.init:
            for r in (sa_scr, sb_scr, sc_scr, sd_scr):
                r[...] = jnp.zeros(r.shape, f32)
        else:
            shared = cfg.init == "shared"
            take = lambda r: (jnp.broadcast_to(r[...], (g,) + r.shape[1:]) if shared else r[0])
            cbuf[:, CONV_HIST - (CONV_W - 1):CONV_HIST, :] = take(convi_ref)
            sa_scr[...] = _pairs_to_bd(take(sai_ref))
            sb_scr[...] = _pairs_to_bd(take(sbi_ref))
            sc_scr[...] = _quad_to_bd(take(sci_ref))
            sd_scr[...] = _pairs_to_bd(take(sdi_ref))

    def load(a, b):
        if cfg.p4d:
            return p_ref[0, :, :, a:b]
        return p_ref[:, a:b].reshape(g, c, b - a)

    r0, r1 = cfg.r0, cfg.r1
    n_valid = float(r1 - r0)
    row = _iota((1, c, 1), 1)
    vm = ((row >= r0) & (row < r1)).astype(f32)
    lane_v = _iota((1, 1, 2 * HEAD_DIM), 2)
    lane_s = _iota((1, 1, 2 * c), 2)
    s_idx = lane_s % c
    incl = s_idx <= row
    strict = s_idx < row
    bd2 = _block_mask(2 * HEAD_DIM, 2 * HEAD_DIM, HEAD_DIM, HEAD_DIM)
    bd4 = _block_mask(N_HEADS * GLA_DK, GROUP, GLA_DK, HEAD_DIM)
    ones2 = bd2.astype(bf16)
    pairs = lambda x: jnp.concatenate([x[..., :LANE], x[..., LANE:]], 0)
    unpairs = lambda x: jnp.concatenate([x[:g], x[g:]], -1)
    n_solve = max(1, math.ceil(math.log2(r1 - r0)))
    sm = load(SMALL, SMALL + LANE)

    cbuf[:, CONV_HIST:CONV_HIST + c, :] = load(A_QKV, A_QKV + 3 * GROUP) * vm
    acc = cbuf[:, pl.ds(CONV_HIST - 3, c), :] * cw_ref[0:1, :]
    for j in range(1, CONV_W):
        acc = acc + cbuf[:, pl.ds(CONV_HIST - 3 + j, c), :] * cw_ref[j:j + 1, :]
    conv = jax.nn.silu(acc)
    beta = jax.nn.sigmoid(sm) * vm
    g_a = -jnp.exp(alog_ref[...]) * jax.nn.softplus(sm + dtb_ref[...]) * vm
    nl = len(_levels(c))
    b_a = _dot01(sums_ref[nl * c:(nl + 1) * c, :], g_a)

    def head_cols(x, lane0):
        col = lambda h: x[..., lane0 + h:lane0 + h + 1]
        return jnp.concatenate([col(0), col(2)], 0), jnp.concatenate([col(1), col(3)], 0)

    def spread(c0, c1, lane, width):
        return jnp.where(lane < width, c0, c1)

    bc0, bc1 = head_cols(b_a, SM_ALPHA)
    bt0, bt1 = head_cols(beta, SM_BETA)
    bc_v, bet_v = spread(bc0, bc1, lane_v, HEAD_DIM), spread(bt0, bt1, lane_v, HEAD_DIM)
    bc_s, bet_s = spread(bc0, bc1, lane_s, c), spread(bt0, bt1, lane_s, c)
    b_rows = [b_a[i].T for i in range(g)]
    b_row = lambda h: jnp.stack([t[SM_ALPHA + h:SM_ALPHA + h + 1, :] for t in b_rows], 0)
    br_s = jnp.concatenate([jnp.concatenate([b_row(0), b_row(1)], -1),
                            jnp.concatenate([b_row(2), b_row(3)], -1)], 0)
    dec = jnp.where(incl, jnp.exp(jnp.where(incl, bc_s - br_s, 0.0)), 0.0)
    q = pairs(conv[..., 0:GROUP])
    k = pairs(conv[..., GROUP:2 * GROUP])
    v = pairs(conv[..., 2 * GROUP:3 * GROUP] * vm)
    q = q * lax.rsqrt(_seg_sum(q * q, ones2) + NORM_EPS) * HEAD_DIM ** -0.5
    k = k * lax.rsqrt(_seg_sum(k * k, ones2) + NORM_EPS) * vm
    s_a = sa_scr[...]
    kq = jnp.concatenate([k, q], 1)
    kq_k = _hmm_nt(kq, k, 2)
    kq_s = _mm(kq, s_a)
    a_mat = jnp.where(strict, bet_s * dec * kq_k[:, 0:c], 0.0)
    gam = jnp.exp(bc_v)
    u = bet_v * (v - gam * kq_s[:, 0:c])
    p = -a_mat
    for i in range(n_solve):
        u = u + _hmm_split(p, u, 2)
        if i + 1 < n_solve:
            p = _hmm_split(p, p, 2)
    o_a = gam * kq_s[:, c:2 * c] + _hmm(dec * kq_k[:, c:2 * c], u, 2)
    bl = bc_v[:, c - 1:c, :]
    sa_scr[...] = jnp.exp(bl) * s_a + jnp.where(bd2, _mm_tn(k * jnp.exp(bl - bc_v), u), 0.0)
    o_a = unpairs(o_a * lax.rsqrt(_seg_sum(o_a * o_a, ones2) * (1.0 / HEAD_DIM) + NORM_EPS))

    lbl = lbl_ref[...]
    e = jnp.exp(lbl - jnp.max(lbl, axis=0, keepdims=True))
    prob = e / jnp.sum(e, axis=0, keepdims=True)
    lb = (jnp.sum(prob[1:cfg.layer + 1], axis=0, keepdims=True) if cfg.layer > 0
          else jnp.zeros((1, GROUP), f32))
    kb = (1.0 - lb) * jax.nn.sigmoid(-load(B_F, B_F + GROUP))
    g_b = jnp.log1p(-jnp.clip(kb, 0.0, 1.0 - GATE_CLAMP)) * vm
    o_b, s_b_new = _channel_decay_mixer(
        pairs(jax.nn.silu(load(B_Q, B_Q + GROUP))), pairs(kb * vm), pairs(load(B_I, B_I + GROUP) * vm),
        pairs(g_b), sb_scr[...], 2, sums_ref, masks2_ref, bd2, c)
    sb_scr[...] = s_b_new
    o_b = unpairs(o_b * lax.rsqrt(_seg_sum(o_b * o_b, ones2) * (1.0 / HEAD_DIM) + NORM_EPS))

    logit = jnp.dot(sm.reshape(g * c, LANE).astype(bf16), wg_ref[...].astype(bf16),
                    preferred_element_type=f32).reshape(g, c, LANE) + bg_ref[...]
    g_c = jax.nn.log_sigmoid(logit) * (vm / GLA_GATE_NORM)
    o_c, s_c_new = _channel_decay_mixer(
        load(C_Q, C_Q + N_HEADS * GLA_DK) * GLA_DK ** -0.5, load(C_K, C_K + N_HEADS * GLA_DK) * vm,
        load(C_V, C_V + GROUP) * vm, g_c, sc_scr[...], N_HEADS, sums_ref, masks4_ref, bd4, c)
    sc_scr[...] = s_c_new
    ones4 = _block_mask(GROUP, GROUP, HEAD_DIM, HEAD_DIM).astype(bf16)
    o_c = o_c * lax.rsqrt(_seg_sum(o_c * o_c, ones4) * (1.0 / HEAD_DIM) + NORM_EPS)

    def rope(x):
        x2 = x.reshape(g * c, GROUP)
        rot = lambda sh: pltpu.roll(x2, sh, 1).reshape(g, c, GROUP)
        return x * cos_ref[...] + rot(GROUP - HEAD_DIM // 2) * sina_ref[...] + rot(HEAD_DIM // 2) * sinb_ref[...]
    q = pairs(rope(load(D_Q, D_Q + GROUP)))
    k = pairs(rope(load(D_K, D_K + GROUP)) * (vm * HEAD_DIM ** -0.5))
    v = pairs(load(D_V, D_V + GROUP) * vm)
    pair_idx = _iota((2 * g, 1, 1), 0) // g
    log_gamma = lambda lane, width: jnp.log1p(-jnp.exp2(-5.0 - (2 * pair_idx + lane // width).astype(f32)))
    lg_v, lg_s = log_gamma(lane_v, HEAD_DIM), log_gamma(lane_s, c)
    cnt_c = jnp.clip(row + 1 - r0, 0, r1 - r0).astype(f32)
    cnt_s = jnp.clip(s_idx + 1 - r0, 0, r1 - r0).astype(f32)
    dec = jnp.where(incl, jnp.exp(jnp.where(incl, (cnt_c - cnt_s) * lg_s, 0.0)), 0.0)
    s_d = sd_scr[...]
    o_d = _mm(q * jnp.exp(cnt_c * lg_v), s_d) + _hmm(dec * _hmm_nt(q, k, 2), v, 2)
    sd_scr[...] = (jnp.exp(n_valid * lg_v) * s_d
                   + jnp.where(bd2, _mm_tn(k * jnp.exp((n_valid - cnt_c) * lg_v), v), 0.0))
    oc = o_d - _seg_sum(o_d, ones2) * (1.0 / HEAD_DIM)
    o_d = unpairs(oc * lax.rsqrt(_seg_sum(oc * oc, ones2) * (1.0 / HEAD_DIM) + LN_EPS))

    for i, (o_m, gate_col) in enumerate(zip((o_a, o_b, o_c, o_d), (A_GATE, B_GATE, C_GATE, D_GATE))):
        val = (o_m * gains_ref[i:i + 1, :] * jax.nn.silu(load(gate_col, gate_col + GROUP))).astype(o_ref.dtype)
        if cfg.p4d:
            o_ref[0, :, :, i * GROUP:(i + 1) * GROUP] = val
        else:
            o_ref[0:g * c, i * GROUP:(i + 1) * GROUP] = val.reshape(g * c, GROUP)
    if cfg.o_rows > g * c:
        o_ref[g * c:cfg.o_rows, :] = jnp.zeros((cfg.o_rows - g * c, D_MODEL), o_ref.dtype)
    cbuf[:, 0:CONV_HIST, :] = cbuf[:, c:c + CONV_HIST, :]

    @pl.when(ci == cfg.n_chunks - 1)
    def _():
        convo_ref[...] = cbuf[:, pl.ds(CONV_HIST + r1 - (CONV_W - 1), CONV_W - 1), :]
        for scr, out in ((sa_scr, sao_ref), (sb_scr, sbo_ref), (sd_scr, sdo_ref)):
            s = scr[...]
            for p in range(2):
                out[:, 2 * p] = s[p * g:(p + 1) * g, 0:HEAD_DIM, 0:HEAD_DIM]
                out[:, 2 * p + 1] = s[p * g:(p + 1) * g, HEAD_DIM:2 * HEAD_DIM, HEAD_DIM:2 * HEAD_DIM]
        s = sc_scr[...]
        for h in range(N_HEADS):
            sco_ref[:, h] = s[:, h * GLA_DK:(h + 1) * GLA_DK, h * HEAD_DIM:(h + 1) * HEAD_DIM]


def _state_shapes(n, stacked):
    lead = (DEPTH, n) if stacked else (n,)
    return [jax.ShapeDtypeStruct(lead + (CONV_W - 1, 3 * GROUP), f32),
            jax.ShapeDtypeStruct(lead + (N_HEADS, HEAD_DIM, HEAD_DIM), f32),
            jax.ShapeDtypeStruct(lead + (N_HEADS, HEAD_DIM, HEAD_DIM), f32),
            jax.ShapeDtypeStruct(lead + (N_HEADS, GLA_DK, HEAD_DIM), f32),
            jax.ShapeDtypeStruct(lead + (N_HEADS, HEAD_DIM, HEAD_DIM), f32)]


def _state_out_specs(g, layer, stacked):
    if stacked:
        st = lambda k: pl.BlockSpec((1, g, N_HEADS, k, HEAD_DIM), lambda b, i: (layer, b, 0, 0, 0))
        conv = pl.BlockSpec((1, g, CONV_W - 1, 3 * GROUP), lambda b, i: (layer, b, 0, 0))
    else:
        st = lambda k: pl.BlockSpec((g, N_HEADS, k, HEAD_DIM), lambda b, i: (b, 0, 0, 0))
        conv = pl.BlockSpec((g, CONV_W - 1, 3 * GROUP), lambda b, i: (b, 0, 0))
    return [conv, st(HEAD_DIM), st(HEAD_DIM), st(GLA_DK), st(HEAD_DIM)]


def _mixer_call(cfg, name, n_groups, p, p_spec, wts, tables, table_spec, consts, init, init_specs,
                o_shape, o_spec, n_state_rows, o_prev=None, states_prev=None):
    c, g = cfg.c, cfg.g
    nl = len(_levels(c))
    wspecs = [_const_spec((CONV_W, 3 * GROUP)), _const_spec((1, LANE)), _const_spec((1, LANE)),
              _const_spec((DEPTH, GROUP)), _const_spec((LANE, LANE)), _const_spec((1, LANE)),
              _const_spec((4, GROUP))]
    cspecs = [_const_spec(((nl + 2) * c, c)), _const_spec((nl + 1, c, 2 * c)), _const_spec((nl + 1, c, N_HEADS * c))]
    in_specs = [p_spec] + wspecs + [table_spec] * 3 + cspecs + list(init_specs)
    args = [p, *wts, *tables, *consts, *init]
    aliases = {}
    if o_prev is not None:
        aliases[len(args)] = 0
        args.append(o_prev)
    for j, s_prev in enumerate(states_prev or ()):
        aliases[len(args)] = 1 + j
        args.append(s_prev)
    in_specs += [pl.BlockSpec(memory_space=pl.ANY)] * len(aliases)
    assert len(aliases) == cfg.n_alias
    return pl.pallas_call(
        functools.partial(_mixer_body, cfg),
        out_shape=[o_shape] + _state_shapes(n_state_rows, cfg.stacked),
        grid=(n_groups, cfg.n_chunks),
        in_specs=in_specs,
        out_specs=[o_spec] + _state_out_specs(g, cfg.layer, cfg.stacked),
        scratch_shapes=[pltpu.VMEM((g, c + CONV_HIST, 3 * GROUP), f32),
                        pltpu.VMEM((2 * g, 2 * HEAD_DIM, 2 * HEAD_DIM), f32),
                        pltpu.VMEM((2 * g, 2 * HEAD_DIM, 2 * HEAD_DIM), f32),
                        pltpu.VMEM((g, N_HEADS * GLA_DK, GROUP), f32),
                        pltpu.VMEM((2 * g, 2 * HEAD_DIM, 2 * HEAD_DIM), f32)],
        input_output_aliases=aliases,
        compiler_params=pltpu.CompilerParams(dimension_semantics=("arbitrary", "arbitrary"),
                                             vmem_limit_bytes=VMEM_LIMIT),
        name=name,
    )(*args)


def _rope_tables(pos):
    half = HEAD_DIM // 2
    inv = 1.0 / (RET_THETA_BASE ** jnp.linspace(0.0, 1.0, half, dtype=f32))
    ang = pos[:, None] * inv[None, :]
    cos, sin = jnp.cos(ang), jnp.sin(ang)
    zero = jnp.zeros_like(sin)
    tile = lambda a, b: jnp.tile(jnp.concatenate([a, b], -1), (1, N_HEADS))
    return tile(cos, cos), tile(-sin, zero), tile(zero, sin)


def kernel(x_prompt, x_sample, state_delta_conv, state_delta, state_hgrn, state_gla, state_ret, meta_tokens, emb_ln_g, emb_ln_b, w_in, conv_w, delta_a_log, delta_dt_bias, delta_norm_g, hgrn_lb_logits, hgrn_norm_g, gla_w_gate, gla_b_gate, gla_norm_g, ret_norm_g, w_out, ln1_g, ln1_b, w_ffn_gate, w_ffn_up, w_ffn_down, ln2_g, ln2_b):
    bsz, seq, _ = x_prompt.shape
    n_seq, n_tok, _ = x_sample.shape
    pg = PROMPT_GROUP if bsz % PROMPT_GROUP == 0 else 1
    sg = SAMPLE_GROUP if n_seq % SAMPLE_GROUP == 0 else 1
    assert seq % CHUNK == 0 and (bsz * seq) % ROW_TILE == 0 and N_META <= CHUNK
    assert CONV_W - 1 <= n_tok <= SAMPLE_ROWS and ROW_TILE % (sg * SAMPLE_ROWS) == 0
    lead_pad = CHUNK - N_META
    rows_main = bsz * seq
    rows_s = n_seq * SAMPLE_ROWS
    rows_tail = ROW_TILE + -(-rows_s // ROW_TILE) * ROW_TILE

    xs = jnp.pad(x_sample, ((0, 0), (0, SAMPLE_ROWS - n_tok), (0, 0))).reshape(rows_s, D_MODEL)
    xt = jnp.concatenate([jnp.zeros((lead_pad, D_MODEL), f32), meta_tokens.astype(f32),
                          jnp.zeros((ROW_TILE - CHUNK, D_MODEL), f32), xs,
                          jnp.zeros((rows_tail - ROW_TILE - rows_s, D_MODEL), f32)], 0)
    xm = x_prompt.reshape(rows_main, D_MODEL)

    tab_m = _rope_tables(jnp.arange(CHUNK, dtype=f32) - lead_pad)
    tab_p = _rope_tables(N_META + jnp.arange(seq, dtype=f32))
    tab_s = _rope_tables(PAST_LEN + jnp.arange(SAMPLE_ROWS, dtype=f32))
    consts_c = tuple(jnp.asarray(a, f32) for a in _chunk_constants(CHUNK))
    consts_s = tuple(jnp.asarray(a, f32) for a in _chunk_constants(SAMPLE_ROWS))
    row_vec = lambda v: v.reshape(1, -1).astype(f32)
    small_row = lambda v: jnp.zeros((1, LANE), f32).at[0, SM_ALPHA:SM_ALPHA + N_HEADS].set(v)
    n_chunks = seq // CHUNK
    blocks_per_tile = ROW_TILE // (sg * SAMPLE_ROWS)

    st_p = st_s = None
    for l in range(DEPTH):
        res = _proj(xm, xt, row_vec(emb_ln_g), row_vec(emb_ln_b), w_in, l, with_ln=(l == 0))
        pm, pt = res[0], res[1]
        if l == 0:
            xm, xt = res[2], res[3]
        wg = jnp.zeros((LANE, N_HEADS * GLA_DK), f32).at[SM_LR:SM_LR + GLA_RANK].set(gla_w_gate[l])
        gains = jnp.stack([delta_norm_g[l], hgrn_norm_g[l], gla_norm_g[l], ret_norm_g[l]], 0)
        wts = (conv_w[l], small_row(delta_a_log[l]), small_row(delta_dt_bias[l]), hgrn_lb_logits,
               wg, row_vec(gla_b_gate[l]), gains)

        cfg = MixCfg(CHUNK, 1, lead_pad, CHUNK, l, 1, None, False, ROW_TILE, 0, False)
        ot, *st_meta = _mixer_call(
            cfg, "mixers_meta", 1, pt, pl.BlockSpec((CHUNK, D_IN_AL), lambda b, i: (0, 0)),
            wts, tab_m, pl.BlockSpec((CHUNK, GROUP), lambda b, i: (0, 0)), consts_c, (), (),
            jax.ShapeDtypeStruct((rows_tail, D_MODEL), bf16),
            pl.BlockSpec((ROW_TILE, D_MODEL), lambda b, i: (0, 0)), 1)

        n_prev = 0 if st_p is None else len(st_p)
        cfg = MixCfg(CHUNK, pg, 0, CHUNK, l, n_chunks, "shared", True, pg * CHUNK, n_prev, True)
        st1 = lambda k: pl.BlockSpec((1, N_HEADS, k, HEAD_DIM), lambda b, i: (0, 0, 0, 0))
        om, *st_p = _mixer_call(
            cfg, "mixers_prompt", bsz // pg, pm.reshape(bsz // pg, pg, seq, D_IN_AL),
            pl.BlockSpec((1, pg, CHUNK, D_IN_AL), lambda b, i: (b, 0, i, 0)),
            wts, tab_p, pl.BlockSpec((CHUNK, GROUP), lambda b, i: (i, 0)), consts_c, st_meta,
            [pl.BlockSpec((1, CONV_W - 1, 3 * GROUP), lambda b, i: (0, 0, 0)),
             st1(HEAD_DIM), st1(HEAD_DIM), st1(GLA_DK), st1(HEAD_DIM)],
            jax.ShapeDtypeStruct((bsz // pg, pg, seq, D_MODEL), bf16),
            pl.BlockSpec((1, pg, CHUNK, D_MODEL), lambda b, i: (b, 0, i, 0)), bsz, states_prev=st_p)
        om = om.reshape(rows_main, D_MODEL)

        cfg = MixCfg(SAMPLE_ROWS, sg, 0, n_tok, l, 1, "per_seq", False, sg * SAMPLE_ROWS, 1 + n_prev, True)
        sti = lambda k: pl.BlockSpec((1, sg, N_HEADS, k, HEAD_DIM), lambda b, i: (l, b, 0, 0, 0))
        ot, *st_s = _mixer_call(
            cfg, "mixers_sample", n_seq // sg, pt,
            pl.BlockSpec((sg * SAMPLE_ROWS, D_IN_AL), lambda b, i: (blocks_per_tile + b, 0)),
            wts, tab_s, pl.BlockSpec((SAMPLE_ROWS, GROUP), lambda b, i: (0, 0)), consts_s,
            (state_delta_conv, state_delta, state_hgrn, state_gla, state_ret),
            [pl.BlockSpec((1, sg, CONV_W - 1, 3 * GROUP), lambda b, i: (l, b, 0, 0)),
             sti(HEAD_DIM), sti(HEAD_DIM), sti(GLA_DK), sti(HEAD_DIM)],
            jax.ShapeDtypeStruct((rows_tail, D_MODEL), bf16),
            pl.BlockSpec((sg * SAMPLE_ROWS, D_MODEL), lambda b, i: (blocks_per_tile + b, 0)), n_seq,
            o_prev=ot, states_prev=st_s)

        xm, xt = _post(xm, xt, om, ot, row_vec(ln1_g[l]), row_vec(ln1_b[l]), row_vec(ln2_g[l]), row_vec(ln2_b[l]),
                       w_out, w_ffn_gate, w_ffn_up, w_ffn_down, l)

    y_prompt = xm.reshape(bsz, seq, D_MODEL)
    y_sample = xt[ROW_TILE:ROW_TILE + rows_s].reshape(n_seq, SAMPLE_ROWS, D_MODEL)[:, :n_tok]
    return (y_prompt, y_sample) + tuple(v for pair in zip(st_p, st_s) for v in pair)
```

```python
import collections
import functools
import math

import numpy as np
import jax
import jax.numpy as jnp
from jax import lax
from jax.experimental import pallas as pl
from jax.experimental.pallas import tpu as pltpu

f32 = jnp.float32
bf16 = jnp.bfloat16

D_MODEL = 1024
N_META = 16
CHUNK = 64
GROUP = 256
N_HEADS = 4
HEAD_DIM = 64
GLA_DK = 32
GLA_RANK = 16
GLA_GATE_NORM = 16.0
CONV_W = 4
D_FF = 2816
DEPTH = 2
ALPHA = (2 * DEPTH) ** 0.25
PAST_LEN = 16384
RET_THETA_BASE = 10000.0
LN_EPS = 1e-5
NORM_EPS = 1e-6
GATE_CLAMP = 1e-6
IN_SPLITS = (GROUP, GROUP, GROUP, N_HEADS, N_HEADS, GROUP,
             GROUP, GROUP, GROUP, GROUP,
             N_HEADS * GLA_DK, N_HEADS * GLA_DK, GROUP, GLA_RANK, GROUP,
             GROUP, GROUP, GROUP, GROUP)

LANE = 128
SUBLANE = 8
VMEM_LIMIT = 56 * 1024 * 1024

COL_ORDER = (0, 1, 2, 5, 6, 7, 8, 9, 10, 11, 12, 14, 15, 16, 17, 18)
A_QKV, A_GATE = 0, 768
B_Q, B_F, B_I, B_GATE = 1024, 1280, 1536, 1792
C_Q, C_K, C_V, C_GATE = 2048, 2176, 2304, 2560
D_Q, D_K, D_V, D_GATE = 2816, 3072, 3328, 3584
SMALL = 3840
SM_BETA, SM_ALPHA, SM_LR = 0, N_HEADS, 2 * N_HEADS
D_IN_AL = SMALL + LANE
SAMPLE_ROWS = SUBLANE
CONV_HIST = SUBLANE
ROW_TILE = 512
PROJ_ROW_TILE = 256
PROJ_STAGE_ROWS = 256
POST_STAGE_ROWS_WIDE = 128
POST_STAGE_ROWS_NARROW = 352
PROMPT_GROUP = 8
SAMPLE_GROUP = 8


def _layer_norm(x, g, b):
    mu = jnp.mean(x, -1, keepdims=True)
    xc = x - mu
    var = jnp.mean(xc * xc, -1, keepdims=True)
    return xc * lax.rsqrt(var + LN_EPS) * g + b


def _row_specs(width, n_main, n_tail, tile=ROW_TILE):
    return (pl.BlockSpec((tile, width), lambda i: (jnp.minimum(i, n_main - 1), 0)),
            pl.BlockSpec((tile, width), lambda i: (jnp.clip(i - n_main, 0, n_tail - 1), 0)))


def _const_spec(shape, single_buffer=False):
    nd = len(shape)
    kw = dict(pipeline_mode=pl.Buffered(1)) if single_buffer else {}
    return pl.BlockSpec(shape, lambda *_: (0,) * nd, **kw)


def _on_region(n_main, main_fn, tail_fn):
    i = pl.program_id(0)
    pl.when(i < n_main)(main_fn)
    pl.when(i >= n_main)(tail_fn)


def _stage_rows(w_hbm, layer, n_rows, chunk, stage, sem, store):
    assert n_rows % chunk == 0 and chunk <= stage.shape[1]
    n = n_rows // chunk

    def copy(i):
        return pltpu.make_async_copy(w_hbm.at[layer, pl.ds(i * chunk, chunk), :],
                                     stage.at[i % 2, pl.ds(0, chunk), :], sem.at[i % 2])
    copy(0).start()
    for i in range(n):
        if i + 1 < n:
            copy(i + 1).start()
        copy(i).wait()
        store(i * chunk, stage.at[i % 2, pl.ds(0, chunk), :])


def _w_in_segments():
    offs = np.concatenate([[0], np.cumsum(IN_SPLITS)])
    runs, dst = [], 0
    for i in COL_ORDER + (3, 4, 13):
        if runs and runs[-1][1] == offs[i]:
            runs[-1][1] = int(offs[i + 1])
        else:
            runs.append([int(offs[i]), int(offs[i + 1]), dst])
        dst += IN_SPLITS[i]
    return [tuple(r) for r in runs]


def _proj_tile(with_ln, x_ref, g_ref, b_ref, w_ref, p_ref, h_ref):
    x = x_ref[...]
    if with_ln:
        x = _layer_norm(x, g_ref[...], b_ref[...])
        h_ref[...] = x
    h = x.astype(bf16)
    n = w_ref.shape[1]
    step = 4 * LANE
    for n0 in range(0, n, step):
        n1 = min(n0 + step, n)
        p_ref[:, n0:n1] = jnp.dot(h, w_ref[:, n0:n1], preferred_element_type=f32)


def _stage_w_in(wt_hbm, layer, w_scr, stage, sem):
    rows = PROJ_STAGE_ROWS
    big, small = [], []
    for s0, s1, d0 in _w_in_segments():
        if (s1 - s0) % rows == 0 and d0 % LANE == 0:
            big += [(s0 + j, d0 + j) for j in range(0, s1 - s0, rows)]
        else:
            small.append((s0, s1 - s0, d0 - SMALL))
    assert all(0 <= d and d + n <= LANE and d % SUBLANE == 0 for _, n, d in small) and len(small) <= 2

    def copy(j):
        return pltpu.make_async_copy(wt_hbm.at[layer, pl.ds(big[j][0], rows), :], stage.at[j % 2], sem.at[j % 2])
    copy(0).start()
    for j in range(len(big)):
        if j + 1 < len(big):
            copy(j + 1).start()
        copy(j).wait()
        w_scr[:, big[j][1]:big[j][1] + rows] = stage[j % 2].T.astype(bf16)
    used = max(d + n for _, n, d in small)
    stage[0, used:LANE, :] = jnp.zeros((LANE - used, D_MODEL), f32)
    copies = [pltpu.make_async_copy(wt_hbm.at[layer, pl.ds(s0, n), :], stage.at[0, pl.ds(d, n), :], sem.at[i])
              for i, (s0, n, d) in enumerate(small)]
    for cp in copies:
        cp.start()
    for cp in copies:
        cp.wait()
    w_scr[:, SMALL:SMALL + LANE] = stage[0, 0:LANE, :].T.astype(bf16)


def _proj_body(with_ln, layer, n_main, xm_ref, xt_ref, g_ref, b_ref, wt_hbm, pm_ref, pt_ref, *rest):
    (hm, ht), (w_scr, stage, sem) = (rest[:2], rest[2:]) if with_ln else ((None, None), rest)
    pl.when(pl.program_id(0) == 0)(lambda: _stage_w_in(wt_hbm, layer, w_scr, stage, sem))

    _on_region(n_main,
               lambda: _proj_tile(with_ln, xm_ref, g_ref, b_ref, w_scr, pm_ref, hm),
               lambda: _proj_tile(with_ln, xt_ref, g_ref, b_ref, w_scr, pt_ref, ht))


def _proj(xm, xt, g, b, w_in_t, layer, with_ln):
    tile = PROJ_ROW_TILE
    n_main, n_tail = xm.shape[0] // tile, xt.shape[0] // tile
    n = D_IN_AL
    out_shape = [jax.ShapeDtypeStruct((xm.shape[0], n), f32), jax.ShapeDtypeStruct((xt.shape[0], n), f32)]
    out_specs = list(_row_specs(n, n_main, n_tail, tile))
    if with_ln:
        out_shape += [jax.ShapeDtypeStruct(xm.shape, f32), jax.ShapeDtypeStruct(xt.shape, f32)]
        out_specs += list(_row_specs(D_MODEL, n_main, n_tail, tile))
    return pl.pallas_call(
        functools.partial(_proj_body, with_ln, layer, n_main),
        out_shape=out_shape,
        grid=(n_main + n_tail,),
        in_specs=list(_row_specs(D_MODEL, n_main, n_tail, tile))
                 + [_const_spec((1, D_MODEL)), _const_spec((1, D_MODEL)), pl.BlockSpec(memory_space=pl.ANY)],
        out_specs=out_specs,
        scratch_shapes=[pltpu.VMEM((D_MODEL, n), bf16),
                        pltpu.VMEM((2, PROJ_STAGE_ROWS, D_MODEL), f32),
                        pltpu.SemaphoreType.DMA((2,))],
        compiler_params=pltpu.CompilerParams(dimension_semantics=("arbitrary",),
                                             vmem_limit_bytes=VMEM_LIMIT),
        name="in_proj",
    )(xm, xt, g, b, w_in_t)


def _post_tile(x_ref, o_ref, wo_ref, g1_ref, b1_ref, wg_ref, wu_ref, wd_ref, g2_ref, b2_ref, y_ref):
    m = jnp.dot(o_ref[...], wo_ref[...], preferred_element_type=f32)
    y1 = _layer_norm(ALPHA * x_ref[...] + m, g1_ref[...], b1_ref[...])
    y1b = y1.astype(bf16)
    dff = wg_ref.shape[1]
    step = 4 * LANE
    acc = jnp.zeros(y1.shape, f32)
    for n0 in range(0, dff, step):
        n1 = min(n0 + step, dff)
        gate = jnp.dot(y1b, wg_ref[:, n0:n1], preferred_element_type=f32)
        up = jnp.dot(y1b, wu_ref[:, n0:n1], preferred_element_type=f32)
        act = (jax.nn.silu(gate) * up).astype(bf16)
        acc = acc + jnp.dot(act, wd_ref[n0:n1, :], preferred_element_type=f32)
    y_ref[...] = _layer_norm(ALPHA * y1 + acc, g2_ref[...], b2_ref[...])


def _post_body(layer, n_main, xm_ref, xt_ref, om_ref, ot_ref, g1_ref, b1_ref, g2_ref, b2_ref,
               wo_hbm, wg_hbm, wu_hbm, wd_hbm, ym_ref, yt_ref,
               wo_scr, wg_scr, wu_scr, wd_scr, stage_wide, stage_narrow, sem):
    @pl.when(pl.program_id(0) == 0)
    def _():
        def into(dst):
            def store(r0, ref):
                dst[r0:r0 + ref.shape[0], :] = ref[...].astype(bf16)
            return store
        _stage_rows(wo_hbm, layer, D_MODEL, PROJ_STAGE_ROWS, stage_narrow, sem, into(wo_scr))
        _stage_rows(wg_hbm, layer, D_MODEL, POST_STAGE_ROWS_WIDE, stage_wide, sem, into(wg_scr))
        _stage_rows(wu_hbm, layer, D_MODEL, POST_STAGE_ROWS_WIDE, stage_wide, sem, into(wu_scr))
        _stage_rows(wd_hbm, layer, D_FF, POST_STAGE_ROWS_NARROW, stage_narrow, sem, into(wd_scr))

    w = (wo_scr, g1_ref, b1_ref, wg_scr, wu_scr, wd_scr, g2_ref, b2_ref)
    _on_region(n_main, lambda: _post_tile(xm_ref, om_ref, *w, ym_ref), lambda: _post_tile(xt_ref, ot_ref, *w, yt_ref))


def _post(xm, xt, om, ot, g1, b1, g2, b2, w_out, w_gate, w_up, w_down, layer):
    n_main, n_tail = xm.shape[0] // ROW_TILE, xt.shape[0] // ROW_TILE
    rows = lambda: list(_row_specs(D_MODEL, n_main, n_tail))
    vec = lambda: _const_spec((1, D_MODEL))
    hbm = lambda: pl.BlockSpec(memory_space=pl.ANY)
    return pl.pallas_call(
        functools.partial(_post_body, layer, n_main),
        out_shape=[jax.ShapeDtypeStruct(xm.shape, f32), jax.ShapeDtypeStruct(xt.shape, f32)],
        grid=(n_main + n_tail,),
        in_specs=rows() + rows() + [vec(), vec(), vec(), vec(), hbm(), hbm(), hbm(), hbm()],
        out_specs=rows(),
        scratch_shapes=[pltpu.VMEM((D_MODEL, D_MODEL), bf16), pltpu.VMEM((D_MODEL, D_FF), bf16),
                        pltpu.VMEM((D_MODEL, D_FF), bf16), pltpu.VMEM((D_FF, D_MODEL), bf16),
                        pltpu.VMEM((2, POST_STAGE_ROWS_WIDE, D_FF), f32),
                        pltpu.VMEM((2, POST_STAGE_ROWS_NARROW, D_MODEL), f32),
                        pltpu.SemaphoreType.DMA((2,))],
        compiler_params=pltpu.CompilerParams(dimension_semantics=("arbitrary",),
                                             vmem_limit_bytes=VMEM_LIMIT),
        name="post_ffn",
    )(xm, xt, om, ot, g1, b1, g2, b2, w_out, w_gate, w_up, w_down)


def _levels(c):
    m, out = c // 2, []
    while m >= 1:
        out.append(m)
        m //= 2
    return tuple(out)


def _chunk_constants(c):
    lv = _levels(c)
    t = np.arange(c)
    rows, masks = [], []
    for m in lv:
        blk = t // m
        upper = (blk % 2) == 1
        lo = np.where(upper, blk * m, t + 1)
        hi = np.where(upper, t, (blk + 1) * m - 1)
        rows.append(((t[None, :] >= lo[:, None]) & (t[None, :] <= hi[:, None])).astype(np.float32))
        same = (t[:, None] // (2 * m)) == (t[None, :] // (2 * m))
        masks.append((same & upper[:, None] & (~upper)[None, :]).astype(np.float32))
    rows.append((t[None, :] <= t[:, None]).astype(np.float32))
    rows.append((t[None, :] > t[:, None]).astype(np.float32))
    masks.append(np.eye(c, dtype=np.float32))
    masks = np.stack(masks, 0)
    return np.concatenate(rows, 0), np.tile(masks, (1, 1, 2)), np.tile(masks, (1, 1, N_HEADS))


def _mm(a, b):
    return jnp.einsum('gmk,gkn->gmn', a.astype(bf16), b.astype(bf16), preferred_element_type=f32)


def _mm_nt(a, b):
    return jnp.einsum('gmk,gnk->gmn', a.astype(bf16), b.astype(bf16), preferred_element_type=f32)


def _mm_tn(a, b):
    return lax.dot_general(a.astype(bf16), b.astype(bf16), (((1,), (1,)), ((0,), (0,))),
                           preferred_element_type=f32)


def _iota(shape, dim):
    return lax.broadcasted_iota(jnp.int32, shape, dim)


def _bd(z, nblk, blk=None):
    z = z.astype(bf16)
    if blk is None:
        blk = _iota((1, 1, z.shape[-1]), 2) // (z.shape[-1] // nblk)
    return jnp.concatenate([jnp.where(blk == h, z, jnp.zeros_like(z)) for h in range(nblk)], axis=1)


def _hmm(x, z, nblk):
    return _mm(x, _bd(z, nblk))


def _hmm_nt(x, z, nblk):
    return _mm_nt(x, _bd(z, nblk))


def _hmm_split(x, z, nblk, blk=None):
    m = x.shape[1]
    x_hi, z_hi = x.astype(bf16), z.astype(bf16)
    x_lo = (x - x_hi.astype(f32)).astype(bf16)
    z_lo = (z - z_hi.astype(f32)).astype(bf16)
    both = _mm(jnp.concatenate([x_hi, x_lo], 1), _bd(z_hi, nblk, blk))
    return both[:, 0:m] + (both[:, m:2 * m] + _mm(x_hi, _bd(z_lo, nblk, blk)))


def _block_mask(rows, lanes, dr, dl):
    return (_iota((rows, lanes), 0) // dr) == (_iota((rows, lanes), 1) // dl)


def _dot01(m01, x):
    nb = x.shape[0]
    x2 = jnp.concatenate([x[i] for i in range(nb)], -1) if nb > 1 else x[0]
    hi = x2.astype(bf16)
    lo = (x2 - hi.astype(f32)).astype(bf16)
    m01 = m01.astype(bf16)
    d = lambda y: jnp.dot(m01, y, preferred_element_type=f32)
    out = d(hi) + d(lo)
    w = x.shape[-1]
    return jnp.stack([out[:, i * w:(i + 1) * w] for i in range(nb)], 0)


def _seg_sum(x, ones_bd):
    nb, c, w = x.shape
    x2 = x.reshape(nb * c, w)
    hi = x2.astype(bf16)
    lo = (x2 - hi.astype(f32)).astype(bf16)
    out = jnp.dot(hi, ones_bd, preferred_element_type=f32) + jnp.dot(lo, ones_bd, preferred_element_type=f32)
    return out.reshape(nb, c, w)


def _last_row_as_col(b, c):
    return jnp.stack([b[i, c - SUBLANE:c, :].T[:, SUBLANE - 1:SUBLANE] for i in range(b.shape[0])], 0)


def _channel_decay_mixer(q, k, v, gl, s, nblk, sums_ref, masks_ref, bdmask, c):
    lv = _levels(c)
    nl = len(lv)
    sums = _dot01(sums_ref[...], gl)
    b = sums[:, nl * c:(nl + 1) * c]
    rev = sums[:, (nl + 1) * c:(nl + 2) * c]
    row = _iota((1, c, 1), 1)
    att = masks_ref[nl] * _hmm_nt(q, k, nblk)
    for li, m in enumerate(lv):
        upper = ((row // m) % 2) == 1
        w = jnp.where(upper, q, k) * jnp.exp(sums[:, li * c:(li + 1) * c])
        att = att + masks_ref[li] * _hmm_nt(w, w, nblk)
    o = _mm(jnp.concatenate([q * jnp.exp(b), att], -1), jnp.concatenate([s.astype(bf16), _bd(v, nblk)], 1))
    s_new = jnp.exp(_last_row_as_col(b, c)) * s + jnp.where(bdmask, _mm_tn(k * jnp.exp(rev), v), 0.0)
    return o, s_new


MixCfg = collections.namedtuple("MixCfg", "c g r0 r1 layer n_chunks init p4d o_rows n_alias stacked")


def _pairs_to_bd(s):
    z = jnp.zeros(s.shape[:1] + (HEAD_DIM, HEAD_DIM), f32)
    pair = lambda p: jnp.concatenate([jnp.concatenate([s[:, 2 * p], z], -1),
                                      jnp.concatenate([z, s[:, 2 * p + 1]], -1)], 1)
    return jnp.concatenate([pair(0), pair(1)], 0)


def _quad_to_bd(s):
    z = lambda n: jnp.zeros(s.shape[:1] + (GLA_DK, n * HEAD_DIM), f32)
    rows = []
    for h in range(N_HEADS):
        parts = ([z(h)] if h else []) + [s[:, h]] + ([z(N_HEADS - 1 - h)] if h < N_HEADS - 1 else [])
        rows.append(jnp.concatenate(parts, -1))
    return jnp.concatenate(rows, 1)


def _mixer_body(cfg, *refs):
    c, g = cfg.c, cfg.g
    n_w, n_t, n_c = 7, 3, 3
    p_ref = refs[0]
    cw_ref, alog_ref, dtb_ref, lbl_ref, wg_ref, bg_ref, gains_ref = refs[1:1 + n_w]
    cos_ref, sina_ref, sinb_ref = refs[1 + n_w:1 + n_w + n_t]
    sums_ref, masks2_ref, masks4_ref = refs[1 + n_w + n_t:1 + n_w + n_t + n_c]
    pos = 1 + n_w + n_t + n_c
    if cfg.init:
        convi_ref, sai_ref, sbi_ref, sci_ref, sdi_ref = refs[pos:pos + 5]
        pos += 5
    pos += cfg.n_alias
    o_ref, convo_ref, sao_ref, sbo_ref, sco_ref, sdo_ref = refs[pos:pos + 6]
    if cfg.stacked:
        convo_ref, sao_ref, sbo_ref, sco_ref, sdo_ref = (r.at[0] for r in (convo_ref, sao_ref, sbo_ref, sco_ref, sdo_ref))
    cbuf, sa_scr, sb_scr, sc_scr, sd_scr = refs[pos + 6:]
    ci = pl.program_id(1)

    @pl.when(ci == 0)
    def _():
        cbuf[:, 0:CONV_HIST, :] = jnp.zeros((g, CONV_HIST, 3 * GROUP), f32)
        if not cfg.init:
            for r in (sa_scr, sb_scr, sc_scr, sd_scr):
                r[...] = jnp.zeros(r.shape, f32)
        else:
            shared = cfg.init == "shared"
            take = lambda r: (jnp.broadcast_to(r[...], (g,) + r.shape[1:]) if shared else r[0])
            cbuf[:, CONV_HIST - (CONV_W - 1):CONV_HIST, :] = take(convi_ref)
            sa_scr[...] = _pairs_to_bd(take(sai_ref))
            sb_scr[...] = _pairs_to_bd(take(sbi_ref))
            sc_scr[...] = _quad_to_bd(take(sci_ref))
            sd_scr[...] = _pairs_to_bd(take(sdi_ref))

    def load(a, b):
        if cfg.p4d:
            return p_ref[0, :, :, a:b]
        return p_ref[:, a:b].reshape(g, c, b - a)

    r0, r1 = cfg.r0, cfg.r1
    n_valid = float(r1 - r0)
    row = _iota((1, c, 1), 1)
    vm = ((row >= r0) & (row < r1)).astype(f32)
    lane_v = _iota((1, 1, 2 * HEAD_DIM), 2)
    lane_s = _iota((1, 1, 2 * c), 2)
    s_idx = lane_s % c
    incl = s_idx <= row
    strict = s_idx < row
    bd2 = _block_mask(2 * HEAD_DIM, 2 * HEAD_DIM, HEAD_DIM, HEAD_DIM)
    bd4 = _block_mask(N_HEADS * GLA_DK, GROUP, GLA_DK, HEAD_DIM)
    ones2 = bd2.astype(bf16)
    pairs = lambda x: jnp.concatenate([x[..., :LANE], x[..., LANE:]], 0)
    unpairs = lambda x: jnp.concatenate([x[:g], x[g:]], -1)
    n_solve = max(1, math.ceil(math.log2(r1 - r0)))
    sm = load(SMALL, SMALL + LANE)

    cbuf[:, CONV_HIST:CONV_HIST + c, :] = load(A_QKV, A_QKV + 3 * GROUP) * vm
    acc = cbuf[:, pl.ds(CONV_HIST - 3, c), :] * cw_ref[0:1, :]
    for j in range(1, CONV_W):
        acc = acc + cbuf[:, pl.ds(CONV_HIST - 3 + j, c), :] * cw_ref[j:j + 1, :]
    conv = jax.nn.silu(acc)
    beta = jax.nn.sigmoid(sm) * vm
    g_a = -jnp.exp(alog_ref[...]) * jax.nn.softplus(sm + dtb_ref[...]) * vm
    nl = len(_levels(c))
    b_a = _dot01(sums_ref[nl * c:(nl + 1) * c, :], g_a)

    def head_cols(x, lane0):
        col = lambda h: x[..., lane0 + h:lane0 + h + 1]
        return jnp.concatenate([col(0), col(2)], 0), jnp.concatenate([col(1), col(3)], 0)

    def spread(c0, c1, lane, width):
        return jnp.where(lane < width, c0, c1)

    bc0, bc1 = head_cols(b_a, SM_ALPHA)
    bt0, bt1 = head_cols(beta, SM_BETA)
    bc_v, bet_v = spread(bc0, bc1, lane_v, HEAD_DIM), spread(bt0, bt1, lane_v, HEAD_DIM)
    bc_s, bet_s = spread(bc0, bc1, lane_s, c), spread(bt0, bt1, lane_s, c)
    b_rows = [b_a[i].T for i in range(g)]
    b_row = lambda h: jnp.stack([t[SM_ALPHA + h:SM_ALPHA + h + 1, :] for t in b_rows], 0)
    br_s = jnp.concatenate([jnp.concatenate([b_row(0), b_row(1)], -1),
                            jnp.concatenate([b_row(2), b_row(3)], -1)], 0)
    dec = jnp.where(incl, jnp.exp(jnp.where(incl, bc_s - br_s, 0.0)), 0.0)
    q = pairs(conv[..., 0:GROUP])
    k = pairs(conv[..., GROUP:2 * GROUP])
    v = pairs(conv[..., 2 * GROUP:3 * GROUP] * vm)
    q = q * lax.rsqrt(_seg_sum(q * q, ones2) + NORM_EPS) * HEAD_DIM ** -0.5
    k = k * lax.rsqrt(_seg_sum(k * k, ones2) + NORM_EPS) * vm
    s_a = sa_scr[...]
    kq = jnp.concatenate([k, q], 1)
    kq_k = _hmm_nt(kq, k, 2)
    kq_s = _mm(kq, s_a)
    a_mat = jnp.where(strict, bet_s * dec * kq_k[:, 0:c], 0.0)
    gam = jnp.exp(bc_v)
    u = bet_v * (v - gam * kq_s[:, 0:c])
    p = -a_mat
    lane_pu = _iota((1, 1, 2 * c + 2 * HEAD_DIM), 2)
    head_pu = jnp.where(lane_pu < 2 * c, lane_pu // c, (lane_pu - 2 * c) // HEAD_DIM)
    for i in range(n_solve):
        if i + 1 < n_solve:
            pu = _hmm_split(p, jnp.concatenate([p, u], -1), 2, head_pu)
            p, u = pu[..., 0:2 * c], u + pu[..., 2 * c:]
        else:
            u = u + _hmm_split(p, u, 2)
    o_a = gam * kq_s[:, c:2 * c] + _hmm(dec * kq_k[:, c:2 * c], u, 2)
    bl = bc_v[:, c - 1:c, :]
    sa_scr[...] = jnp.exp(bl) * s_a + jnp.where(bd2, _mm_tn(k * jnp.exp(bl - bc_v), u), 0.0)
    o_a = unpairs(o_a * lax.rsqrt(_seg_sum(o_a * o_a, ones2) * (1.0 / HEAD_DIM) + NORM_EPS))

    lbl = lbl_ref[...]
    e = jnp.exp(lbl - jnp.max(lbl, axis=0, keepdims=True))
    prob = e / jnp.sum(e, axis=0, keepdims=True)
    lb = (jnp.sum(prob[1:cfg.layer + 1], axis=0, keepdims=True) if cfg.layer > 0
          else jnp.zeros((1, GROUP), f32))
    kb = (1.0 - lb) * jax.nn.sigmoid(-load(B_F, B_F + GROUP))
    g_b = jnp.log1p(-jnp.clip(kb, 0.0, 1.0 - GATE_CLAMP)) * vm
    o_b, s_b_new = _channel_decay_mixer(
        pairs(jax.nn.silu(load(B_Q, B_Q + GROUP))), pairs(kb * vm), pairs(load(B_I, B_I + GROUP) * vm),
        pairs(g_b), sb_scr[...], 2, sums_ref, masks2_ref, bd2, c)
    sb_scr[...] = s_b_new
    o_b = unpairs(o_b * lax.rsqrt(_seg_sum(o_b * o_b, ones2) * (1.0 / HEAD_DIM) + NORM_EPS))

    logit = jnp.dot(sm.reshape(g * c, LANE).astype(bf16), wg_ref[...].astype(bf16),
                    preferred_element_type=f32).reshape(g, c, LANE) + bg_ref[...]
    g_c = jax.nn.log_sigmoid(logit) * (vm / GLA_GATE_NORM)
    o_c, s_c_new = _channel_decay_mixer(
        load(C_Q, C_Q + N_HEADS * GLA_DK) * GLA_DK ** -0.5, load(C_K, C_K + N_HEADS * GLA_DK) * vm,
        load(C_V, C_V + GROUP) * vm, g_c, sc_scr[...], N_HEADS, sums_ref, masks4_ref, bd4, c)
    sc_scr[...] = s_c_new
    ones4 = _block_mask(GROUP, GROUP, HEAD_DIM, HEAD_DIM).astype(bf16)
    o_c = o_c * lax.rsqrt(_seg_sum(o_c * o_c, ones4) * (1.0 / HEAD_DIM) + NORM_EPS)

    def rope(x):
        x2 = x.reshape(g * c, GROUP)
        rot = lambda sh: pltpu.roll(x2, sh, 1).reshape(g, c, GROUP)
        return x * cos_ref[...] + rot(GROUP - HEAD_DIM // 2) * sina_ref[...] + rot(HEAD_DIM // 2) * sinb_ref[...]
    q = pairs(rope(load(D_Q, D_Q + GROUP)))
    k = pairs(rope(load(D_K, D_K + GROUP)) * (vm * HEAD_DIM ** -0.5))
    v = pairs(load(D_V, D_V + GROUP) * vm)
    pair_idx = _iota((2 * g, 1, 1), 0) // g
    log_gamma = lambda lane, width: jnp.log1p(-jnp.exp2(-5.0 - (2 * pair_idx + lane // width).astype(f32)))
    lg_v, lg_s = log_gamma(lane_v, HEAD_DIM), log_gamma(lane_s, c)
    cnt_c = jnp.clip(row + 1 - r0, 0, r1 - r0).astype(f32)
    cnt_s = jnp.clip(s_idx + 1 - r0, 0, r1 - r0).astype(f32)
    dec = jnp.where(incl, jnp.exp(jnp.where(incl, (cnt_c - cnt_s) * lg_s, 0.0)), 0.0)
    s_d = sd_scr[...]
    o_d = _mm(jnp.concatenate([q * jnp.exp(cnt_c * lg_v), dec * _hmm_nt(q, k, 2)], -1),
              jnp.concatenate([s_d.astype(bf16), _bd(v, 2)], 1))
    sd_scr[...] = (jnp.exp(n_valid * lg_v) * s_d
                   + jnp.where(bd2, _mm_tn(k * jnp.exp((n_valid - cnt_c) * lg_v), v), 0.0))
    oc = o_d - _seg_sum(o_d, ones2) * (1.0 / HEAD_DIM)
    o_d = unpairs(oc * lax.rsqrt(_seg_sum(oc * oc, ones2) * (1.0 / HEAD_DIM) + LN_EPS))

    for i, (o_m, gate_col) in enumerate(zip((o_a, o_b, o_c, o_d), (A_GATE, B_GATE, C_GATE, D_GATE))):
        val = (o_m * gains_ref[i:i + 1, :] * jax.nn.silu(load(gate_col, gate_col + GROUP))).astype(o_ref.dtype)
        if cfg.p4d:
            o_ref[0, :, :, i * GROUP:(i + 1) * GROUP] = val
        else:
            o_ref[0:g * c, i * GROUP:(i + 1) * GROUP] = val.reshape(g * c, GROUP)
    if cfg.o_rows > g * c:
        o_ref[g * c:cfg.o_rows, :] = jnp.zeros((cfg.o_rows - g * c, D_MODEL), o_ref.dtype)
    cbuf[:, 0:CONV_HIST, :] = cbuf[:, c:c + CONV_HIST, :]

    @pl.when(ci == cfg.n_chunks - 1)
    def _():
        convo_ref[...] = cbuf[:, pl.ds(CONV_HIST + r1 - (CONV_W - 1), CONV_W - 1), :]
        for scr, out in ((sa_scr, sao_ref), (sb_scr, sbo_ref), (sd_scr, sdo_ref)):
            s = scr[...]
            for p in range(2):
                out[:, 2 * p] = s[p * g:(p + 1) * g, 0:HEAD_DIM, 0:HEAD_DIM]
                out[:, 2 * p + 1] = s[p * g:(p + 1) * g, HEAD_DIM:2 * HEAD_DIM, HEAD_DIM:2 * HEAD_DIM]
        s = sc_scr[...]
        for h in range(N_HEADS):
            sco_ref[:, h] = s[:, h * GLA_DK:(h + 1) * GLA_DK, h * HEAD_DIM:(h + 1) * HEAD_DIM]


def _state_shapes(n, stacked):
    lead = (DEPTH, n) if stacked else (n,)
    return [jax.ShapeDtypeStruct(lead + (CONV_W - 1, 3 * GROUP), f32),
            jax.ShapeDtypeStruct(lead + (N_HEADS, HEAD_DIM, HEAD_DIM), f32),
            jax.ShapeDtypeStruct(lead + (N_HEADS, HEAD_DIM, HEAD_DIM), f32),
            jax.ShapeDtypeStruct(lead + (N_HEADS, GLA_DK, HEAD_DIM), f32),
            jax.ShapeDtypeStruct(lead + (N_HEADS, HEAD_DIM, HEAD_DIM), f32)]


def _state_out_specs(g, layer, stacked):
    if stacked:
        st = lambda k: pl.BlockSpec((1, g, N_HEADS, k, HEAD_DIM), lambda b, i: (layer, b, 0, 0, 0))
        conv = pl.BlockSpec((1, g, CONV_W - 1, 3 * GROUP), lambda b, i: (layer, b, 0, 0))
    else:
        st = lambda k: pl.BlockSpec((g, N_HEADS, k, HEAD_DIM), lambda b, i: (b, 0, 0, 0))
        conv = pl.BlockSpec((g, CONV_W - 1, 3 * GROUP), lambda b, i: (b, 0, 0))
    return [conv, st(HEAD_DIM), st(HEAD_DIM), st(GLA_DK), st(HEAD_DIM)]


def _mixer_call(cfg, name, n_groups, p, p_spec, wts, tables, table_spec, consts, init, init_specs,
                o_shape, o_spec, n_state_rows, o_prev=None, states_prev=None):
    c, g = cfg.c, cfg.g
    nl = len(_levels(c))
    wspecs = [_const_spec((CONV_W, 3 * GROUP)), _const_spec((1, LANE)), _const_spec((1, LANE)),
              _const_spec((DEPTH, GROUP)), _const_spec((LANE, LANE)), _const_spec((1, LANE)),
              _const_spec((4, GROUP))]
    cspecs = [_const_spec(((nl + 2) * c, c)), _const_spec((nl + 1, c, 2 * c)), _const_spec((nl + 1, c, N_HEADS * c))]
    in_specs = [p_spec] + wspecs + [table_spec] * 3 + cspecs + list(init_specs)
    args = [p, *wts, *tables, *consts, *init]
    aliases = {}
    if o_prev is not None:
        aliases[len(args)] = 0
        args.append(o_prev)
    for j, s_prev in enumerate(states_prev or ()):
        aliases[len(args)] = 1 + j
        args.append(s_prev)
    in_specs += [pl.BlockSpec(memory_space=pl.ANY)] * len(aliases)
    assert len(aliases) == cfg.n_alias
    return pl.pallas_call(
        functools.partial(_mixer_body, cfg),
        out_shape=[o_shape] + _state_shapes(n_state_rows, cfg.stacked),
        grid=(n_groups, cfg.n_chunks),
        in_specs=in_specs,
        out_specs=[o_spec] + _state_out_specs(g, cfg.layer, cfg.stacked),
        scratch_shapes=[pltpu.VMEM((g, c + CONV_HIST, 3 * GROUP), f32),
                        pltpu.VMEM((2 * g, 2 * HEAD_DIM, 2 * HEAD_DIM), f32),
                        pltpu.VMEM((2 * g, 2 * HEAD_DIM, 2 * HEAD_DIM), f32),
                        pltpu.VMEM((g, N_HEADS * GLA_DK, GROUP), f32),
                        pltpu.VMEM((2 * g, 2 * HEAD_DIM, 2 * HEAD_DIM), f32)],
        input_output_aliases=aliases,
        compiler_params=pltpu.CompilerParams(dimension_semantics=("arbitrary", "arbitrary"),
                                             vmem_limit_bytes=VMEM_LIMIT),
        name=name,
    )(*args)


def _rope_tables(pos):
    half = HEAD_DIM // 2
    inv = 1.0 / (RET_THETA_BASE ** jnp.linspace(0.0, 1.0, half, dtype=f32))
    ang = pos[:, None] * inv[None, :]
    cos, sin = jnp.cos(ang), jnp.sin(ang)
    zero = jnp.zeros_like(sin)
    tile = lambda a, b: jnp.tile(jnp.concatenate([a, b], -1), (1, N_HEADS))
    return tile(cos, cos), tile(-sin, zero), tile(zero, sin)


def kernel(x_prompt, x_sample, state_delta_conv, state_delta, state_hgrn, state_gla, state_ret, meta_tokens, emb_ln_g, emb_ln_b, w_in, conv_w, delta_a_log, delta_dt_bias, delta_norm_g, hgrn_lb_logits, hgrn_norm_g, gla_w_gate, gla_b_gate, gla_norm_g, ret_norm_g, w_out, ln1_g, ln1_b, w_ffn_gate, w_ffn_up, w_ffn_down, ln2_g, ln2_b):
    bsz, seq, _ = x_prompt.shape
    n_seq, n_tok, _ = x_sample.shape
    pg = PROMPT_GROUP if bsz % PROMPT_GROUP == 0 else 1
    sg = SAMPLE_GROUP if n_seq % SAMPLE_GROUP == 0 else 1
    assert seq % CHUNK == 0 and (bsz * seq) % ROW_TILE == 0 and N_META <= CHUNK
    assert CONV_W - 1 <= n_tok <= SAMPLE_ROWS and ROW_TILE % (sg * SAMPLE_ROWS) == 0
    lead_pad = CHUNK - N_META
    rows_main = bsz * seq
    rows_s = n_seq * SAMPLE_ROWS
    rows_tail = ROW_TILE + -(-rows_s // ROW_TILE) * ROW_TILE

    xs = jnp.pad(x_sample, ((0, 0), (0, SAMPLE_ROWS - n_tok), (0, 0))).reshape(rows_s, D_MODEL)
    xt = jnp.concatenate([jnp.zeros((lead_pad, D_MODEL), f32), meta_tokens.astype(f32),
                          jnp.zeros((ROW_TILE - CHUNK, D_MODEL), f32), xs,
                          jnp.zeros((rows_tail - ROW_TILE - rows_s, D_MODEL), f32)], 0)
    xm = x_prompt.reshape(rows_main, D_MODEL)

    tab_m = _rope_tables(jnp.arange(CHUNK, dtype=f32) - lead_pad)
    tab_p = _rope_tables(N_META + jnp.arange(seq, dtype=f32))
    tab_s = _rope_tables(PAST_LEN + jnp.arange(SAMPLE_ROWS, dtype=f32))
    consts_c = tuple(jnp.asarray(a, f32) for a in _chunk_constants(CHUNK))
    consts_s = tuple(jnp.asarray(a, f32) for a in _chunk_constants(SAMPLE_ROWS))
    row_vec = lambda v: v.reshape(1, -1).astype(f32)
    small_row = lambda v: jnp.zeros((1, LANE), f32).at[0, SM_ALPHA:SM_ALPHA + N_HEADS].set(v)
    n_chunks = seq // CHUNK
    blocks_per_tile = ROW_TILE // (sg * SAMPLE_ROWS)

    w_in_t = jnp.swapaxes(w_in, 1, 2)
    st_p = st_s = None
    for l in range(DEPTH):
        res = _proj(xm, xt, row_vec(emb_ln_g), row_vec(emb_ln_b), w_in_t, l, with_ln=(l == 0))
        pm, pt = res[0], res[1]
        if l == 0:
            xm, xt = res[2], res[3]
        wg = jnp.zeros((LANE, N_HEADS * GLA_DK), f32).at[SM_LR:SM_LR + GLA_RANK].set(gla_w_gate[l])
        gains = jnp.stack([delta_norm_g[l], hgrn_norm_g[l], gla_norm_g[l], ret_norm_g[l]], 0)
        wts = (conv_w[l], small_row(delta_a_log[l]), small_row(delta_dt_bias[l]), hgrn_lb_logits,
               wg, row_vec(gla_b_gate[l]), gains)

        cfg = MixCfg(CHUNK, 1, lead_pad, CHUNK, l, 1, None, False, ROW_TILE, 0, False)
        ot, *st_meta = _mixer_call(
            cfg, "mixers_meta", 1, pt, pl.BlockSpec((CHUNK, D_IN_AL), lambda b, i: (0, 0)),
            wts, tab_m, pl.BlockSpec((CHUNK, GROUP), lambda b, i: (0, 0)), consts_c, (), (),
            jax.ShapeDtypeStruct((rows_tail, D_MODEL), bf16),
            pl.BlockSpec((ROW_TILE, D_MODEL), lambda b, i: (0, 0)), 1)

        n_prev = 0 if st_p is None else len(st_p)
        cfg = MixCfg(CHUNK, pg, 0, CHUNK, l, n_chunks, "shared", True, pg * CHUNK, n_prev, True)
        st1 = lambda k: pl.BlockSpec((1, N_HEADS, k, HEAD_DIM), lambda b, i: (0, 0, 0, 0))
        om, *st_p = _mixer_call(
            cfg, "mixers_prompt", bsz // pg, pm.reshape(bsz // pg, pg, seq, D_IN_AL),
            pl.BlockSpec((1, pg, CHUNK, D_IN_AL), lambda b, i: (b, 0, i, 0)),
            wts, tab_p, pl.BlockSpec((CHUNK, GROUP), lambda b, i: (i, 0)), consts_c, st_meta,
            [pl.BlockSpec((1, CONV_W - 1, 3 * GROUP), lambda b, i: (0, 0, 0)),
             st1(HEAD_DIM), st1(HEAD_DIM), st1(GLA_DK), st1(HEAD_DIM)],
            jax.ShapeDtypeStruct((bsz // pg, pg, seq, D_MODEL), bf16),
            pl.BlockSpec((1, pg, CHUNK, D_MODEL), lambda b, i: (b, 0, i, 0)), bsz, states_prev=st_p)
        om = om.reshape(rows_main, D_MODEL)

        cfg = MixCfg(SAMPLE_ROWS, sg, 0, n_tok, l, 1, "per_seq", False, sg * SAMPLE_ROWS, 1 + n_prev, True)
        sti = lambda k: pl.BlockSpec((1, sg, N_HEADS, k, HEAD_DIM), lambda b, i: (l, b, 0, 0, 0))
        ot, *st_s = _mixer_call(
            cfg, "mixers_sample", n_seq // sg, pt,
            pl.BlockSpec((sg * SAMPLE_ROWS, D_IN_AL), lambda b, i: (blocks_per_tile + b, 0)),
            wts, tab_s, pl.BlockSpec((SAMPLE_ROWS, GROUP), lambda b, i: (0, 0)), consts_s,
            (state_delta_conv, state_delta, state_hgrn, state_gla, state_ret),
            [pl.BlockSpec((1, sg, CONV_W - 1, 3 * GROUP), lambda b, i: (l, b, 0, 0)),
             sti(HEAD_DIM), sti(HEAD_DIM), sti(GLA_DK), sti(HEAD_DIM)],
            jax.ShapeDtypeStruct((rows_tail, D_MODEL), bf16),
            pl.BlockSpec((sg * SAMPLE_ROWS, D_MODEL), lambda b, i: (blocks_per_tile + b, 0)), n_seq,
            o_prev=ot, states_prev=st_s)

        xm, xt = _post(xm, xt, om, ot, row_vec(ln1_g[l]), row_vec(ln1_b[l]), row_vec(ln2_g[l]), row_vec(ln2_b[l]),
                       w_out, w_ffn_gate, w_ffn_up, w_ffn_down, l)

    y_prompt = xm.reshape(bsz, seq, D_MODEL)
    y_sample = xt[ROW_TILE:ROW_TILE + rows_s].reshape(n_seq, SAMPLE_ROWS, D_MODEL)[:, :n_tok]
    return (y_prompt, y_sample) + tuple(v for pair in zip(st_p, st_s) for v in pair)
```

```python
import collections
import functools
import math

import numpy as np
import jax
import jax.numpy as jnp
from jax import lax
from jax.experimental import pallas as pl
from jax.experimental.pallas import tpu as pltpu

f32 = jnp.float32
bf16 = jnp.bfloat16

D_MODEL = 1024
N_META = 16
CHUNK = 64
GROUP = 256
N_HEADS = 4
HEAD_DIM = 64
GLA_DK = 32
GLA_RANK = 16
GLA_GATE_NORM = 16.0
CONV_W = 4
D_FF = 2816
DEPTH = 2
ALPHA = (2 * DEPTH) ** 0.25
PAST_LEN = 16384
RET_THETA_BASE = 10000.0
LN_EPS = 1e-5
NORM_EPS = 1e-6
GATE_CLAMP = 1e-6
IN_SPLITS = (GROUP, GROUP, GROUP, N_HEADS, N_HEADS, GROUP,
             GROUP, GROUP, GROUP, GROUP,
             N_HEADS * GLA_DK, N_HEADS * GLA_DK, GROUP, GLA_RANK, GROUP,
             GROUP, GROUP, GROUP, GROUP)

LANE = 128
SUBLANE = 8
VMEM_LIMIT = 56 * 1024 * 1024

COL_ORDER = (0, 1, 2, 5, 6, 7, 8, 9, 10, 11, 12, 14, 15, 16, 17, 18)
A_QKV, A_GATE = 0, 768
B_Q, B_F, B_I, B_GATE = 1024, 1280, 1536, 1792
C_Q, C_K, C_V, C_GATE = 2048, 2176, 2304, 2560
D_Q, D_K, D_V, D_GATE = 2816, 3072, 3328, 3584
SMALL = 3840
SM_BETA, SM_ALPHA, SM_LR = 0, N_HEADS, 2 * N_HEADS
D_IN_AL = SMALL + LANE
SAMPLE_ROWS = SUBLANE
CONV_HIST = SUBLANE
ROW_TILE = 512
PROJ_STAGE_ROWS = 256
POST_STAGE_ROWS_WIDE = 128
POST_STAGE_ROWS_NARROW = 352
PROMPT_GROUP = 8
SAMPLE_GROUP = 8


def _layer_norm(x, g, b):
    mu = jnp.mean(x, -1, keepdims=True)
    xc = x - mu
    var = jnp.mean(xc * xc, -1, keepdims=True)
    return xc * lax.rsqrt(var + LN_EPS) * g + b


def _row_specs(width, n_main, n_tail, tile=ROW_TILE):
    return (pl.BlockSpec((tile, width), lambda i: (jnp.minimum(i, n_main - 1), 0)),
            pl.BlockSpec((tile, width), lambda i: (jnp.clip(i - n_main, 0, n_tail - 1), 0)))


def _const_spec(shape, single_buffer=False):
    nd = len(shape)
    kw = dict(pipeline_mode=pl.Buffered(1)) if single_buffer else {}
    return pl.BlockSpec(shape, lambda *_: (0,) * nd, **kw)


def _on_region(n_main, main_fn, tail_fn):
    i = pl.program_id(0)
    pl.when(i < n_main)(main_fn)
    pl.when(i >= n_main)(tail_fn)


def _stage_rows(w_hbm, layer, n_rows, chunk, stage, sem, store):
    assert n_rows % chunk == 0 and chunk <= stage.shape[1]
    n = n_rows // chunk

    def copy(i):
        return pltpu.make_async_copy(w_hbm.at[layer, pl.ds(i * chunk, chunk), :],
                                     stage.at[i % 2, pl.ds(0, chunk), :], sem.at[i % 2])
    copy(0).start()
    for i in range(n):
        if i + 1 < n:
            copy(i + 1).start()
        copy(i).wait()
        store(i * chunk, stage.at[i % 2, pl.ds(0, chunk), :])


def _w_in_segments():
    offs = np.concatenate([[0], np.cumsum(IN_SPLITS)])
    runs, dst = [], 0
    for i in COL_ORDER + (3, 4, 13):
        if runs and runs[-1][1] == offs[i]:
            runs[-1][1] = int(offs[i + 1])
        else:
            runs.append([int(offs[i]), int(offs[i + 1]), dst])
        dst += IN_SPLITS[i]
    return [tuple(r) for r in runs]


def _proj_tile(with_ln, x_ref, g_ref, b_ref, w_ref, p_ref):
    x = x_ref[...]
    if with_ln:
        x = _layer_norm(x, g_ref[...], b_ref[...])
    h = x.astype(bf16)
    n = w_ref.shape[1]
    step = 4 * LANE
    for n0 in range(0, n, step):
        n1 = min(n0 + step, n)
        p_ref[:, n0:n1] = jnp.dot(h, w_ref[:, n0:n1], preferred_element_type=f32)


def _stage_w_in(wt_hbm, layer, w_scr, stage, sem):
    rows = PROJ_STAGE_ROWS
    big, small = [], []
    for s0, s1, d0 in _w_in_segments():
        if (s1 - s0) % rows == 0 and d0 % LANE == 0:
            big += [(s0 + j, d0 + j) for j in range(0, s1 - s0, rows)]
        else:
            small.append((s0, s1 - s0, d0 - SMALL))
    assert all(0 <= d and d + n <= LANE and d % SUBLANE == 0 for _, n, d in small) and len(small) <= 2

    def copy(j):
        return pltpu.make_async_copy(wt_hbm.at[layer, pl.ds(big[j][0], rows), :], stage.at[j % 2], sem.at[j % 2])
    copy(0).start()
    for j in range(len(big)):
        if j + 1 < len(big):
            copy(j + 1).start()
        copy(j).wait()
        w_scr[:, big[j][1]:big[j][1] + rows] = stage[j % 2].T.astype(bf16)
    used = max(d + n for _, n, d in small)
    stage[0, used:LANE, :] = jnp.zeros((LANE - used, D_MODEL), f32)
    copies = [pltpu.make_async_copy(wt_hbm.at[layer, pl.ds(s0, n), :], stage.at[0, pl.ds(d, n), :], sem.at[i])
              for i, (s0, n, d) in enumerate(small)]
    for cp in copies:
        cp.start()
    for cp in copies:
        cp.wait()
    w_scr[:, SMALL:SMALL + LANE] = stage[0, 0:LANE, :].T.astype(bf16)


def _proj_body(with_ln, layer, n_main, xm_ref, xt_ref, g_ref, b_ref, wt_hbm, pm_ref, pt_ref, w_scr, stage, sem):
    pl.when(pl.program_id(0) == 0)(lambda: _stage_w_in(wt_hbm, layer, w_scr, stage, sem))
    _on_region(n_main,
               lambda: _proj_tile(with_ln, xm_ref, g_ref, b_ref, w_scr, pm_ref),
               lambda: _proj_tile(with_ln, xt_ref, g_ref, b_ref, w_scr, pt_ref))


def _proj(xm, xt, g, b, w_in_t, layer, with_ln):
    tile = ROW_TILE
    n_main, n_tail = xm.shape[0] // tile, xt.shape[0] // tile
    n = D_IN_AL
    out_shape = [jax.ShapeDtypeStruct((xm.shape[0], n), f32), jax.ShapeDtypeStruct((xt.shape[0], n), f32)]
    out_specs = list(_row_specs(n, n_main, n_tail, tile))
    return pl.pallas_call(
        functools.partial(_proj_body, with_ln, layer, n_main),
        out_shape=out_shape,
        grid=(n_main + n_tail,),
        in_specs=list(_row_specs(D_MODEL, n_main, n_tail, tile))
                 + [_const_spec((1, D_MODEL)), _const_spec((1, D_MODEL)), pl.BlockSpec(memory_space=pl.ANY)],
        out_specs=out_specs,
        scratch_shapes=[pltpu.VMEM((D_MODEL, n), bf16),
                        pltpu.VMEM((2, PROJ_STAGE_ROWS, D_MODEL), f32),
                        pltpu.SemaphoreType.DMA((2,))],
        compiler_params=pltpu.CompilerParams(dimension_semantics=("arbitrary",),
                                             vmem_limit_bytes=VMEM_LIMIT),
        name="in_proj",
    )(xm, xt, g, b, w_in_t)


def _post_tile(emb_ln, x_ref, o_ref, ge_ref, be_ref, wo_ref, g1_ref, b1_ref, wg_ref, wu_ref, wd_ref, g2_ref, b2_ref,
               y_ref):
    x = x_ref[...]
    if emb_ln:
        x = _layer_norm(x, ge_ref[...], be_ref[...])
    m = jnp.dot(o_ref[...], wo_ref[...], preferred_element_type=f32)
    y1 = _layer_norm(ALPHA * x + m, g1_ref[...], b1_ref[...])
    y1b = y1.astype(bf16)
    dff = wg_ref.shape[1]
    step = 4 * LANE
    acc = jnp.zeros(y1.shape, f32)
    for n0 in range(0, dff, step):
        n1 = min(n0 + step, dff)
        gate = jnp.dot(y1b, wg_ref[:, n0:n1], preferred_element_type=f32)
        up = jnp.dot(y1b, wu_ref[:, n0:n1], preferred_element_type=f32)
        act = (jax.nn.silu(gate) * up).astype(bf16)
        acc = acc + jnp.dot(act, wd_ref[n0:n1, :], preferred_element_type=f32)
    y_ref[...] = _layer_norm(ALPHA * y1 + acc, g2_ref[...], b2_ref[...])


def _post_body(layer, n_main, xm_ref, xt_ref, om_ref, ot_ref, ge_ref, be_ref, g1_ref, b1_ref, g2_ref, b2_ref,
               wo_hbm, wg_hbm, wu_hbm, wd_hbm, ym_ref, yt_ref,
               wo_scr, wg_scr, wu_scr, wd_scr, stage_wide, stage_narrow, sem):
    @pl.when(pl.program_id(0) == 0)
    def _():
        def into(dst):
            def store(r0, ref):
                dst[r0:r0 + ref.shape[0], :] = ref[...].astype(bf16)
            return store
        _stage_rows(wo_hbm, layer, D_MODEL, PROJ_STAGE_ROWS, stage_narrow, sem, into(wo_scr))
        _stage_rows(wg_hbm, layer, D_MODEL, POST_STAGE_ROWS_WIDE, stage_wide, sem, into(wg_scr))
        _stage_rows(wu_hbm, layer, D_MODEL, POST_STAGE_ROWS_WIDE, stage_wide, sem, into(wu_scr))
        _stage_rows(wd_hbm, layer, D_FF, POST_STAGE_ROWS_NARROW, stage_narrow, sem, into(wd_scr))

    w = (ge_ref, be_ref, wo_scr, g1_ref, b1_ref, wg_scr, wu_scr, wd_scr, g2_ref, b2_ref)
    _on_region(n_main, lambda: _post_tile(layer == 0, xm_ref, om_ref, *w, ym_ref),
               lambda: _post_tile(layer == 0, xt_ref, ot_ref, *w, yt_ref))


def _post(xm, xt, om, ot, ge, be, g1, b1, g2, b2, w_out, w_gate, w_up, w_down, layer):
    n_main, n_tail = xm.shape[0] // ROW_TILE, xt.shape[0] // ROW_TILE
    rows = lambda: list(_row_specs(D_MODEL, n_main, n_tail))
    vec = lambda: _const_spec((1, D_MODEL))
    hbm = lambda: pl.BlockSpec(memory_space=pl.ANY)
    return pl.pallas_call(
        functools.partial(_post_body, layer, n_main),
        out_shape=[jax.ShapeDtypeStruct(xm.shape, f32), jax.ShapeDtypeStruct(xt.shape, f32)],
        grid=(n_main + n_tail,),
        in_specs=rows() + rows() + [vec() for _ in range(6)] + [hbm() for _ in range(4)],
        out_specs=rows(),
        scratch_shapes=[pltpu.VMEM((D_MODEL, D_MODEL), bf16), pltpu.VMEM((D_MODEL, D_FF), bf16),
                        pltpu.VMEM((D_MODEL, D_FF), bf16), pltpu.VMEM((D_FF, D_MODEL), bf16),
                        pltpu.VMEM((2, POST_STAGE_ROWS_WIDE, D_FF), f32),
                        pltpu.VMEM((2, POST_STAGE_ROWS_NARROW, D_MODEL), f32),
                        pltpu.SemaphoreType.DMA((2,))],
        compiler_params=pltpu.CompilerParams(dimension_semantics=("arbitrary",),
                                             vmem_limit_bytes=VMEM_LIMIT),
        name="post_ffn",
    )(xm, xt, om, ot, ge, be, g1, b1, g2, b2, w_out, w_gate, w_up, w_down)


def _levels(c):
    m, out = c // 2, []
    while m >= 1:
        out.append(m)
        m //= 2
    return tuple(out)


def _chunk_constants(c):
    lv = _levels(c)
    t = np.arange(c)
    rows, masks = [], []
    for m in lv:
        blk = t // m
        upper = (blk % 2) == 1
        lo = np.where(upper, blk * m, t + 1)
        hi = np.where(upper, t, (blk + 1) * m - 1)
        rows.append(((t[None, :] >= lo[:, None]) & (t[None, :] <= hi[:, None])).astype(np.float32))
        same = (t[:, None] // (2 * m)) == (t[None, :] // (2 * m))
        masks.append((same & upper[:, None] & (~upper)[None, :]).astype(np.float32))
    rows.append((t[None, :] <= t[:, None]).astype(np.float32))
    rows.append((t[None, :] > t[:, None]).astype(np.float32))
    masks.append(np.eye(c, dtype=np.float32))
    masks = np.stack(masks, 0)
    return np.concatenate(rows, 0), np.tile(masks, (1, 1, 2)), np.tile(masks, (1, 1, N_HEADS))


def _mm(a, b):
    return jnp.einsum('gmk,gkn->gmn', a.astype(bf16), b.astype(bf16), preferred_element_type=f32)


def _mm_nt(a, b):
    return jnp.einsum('gmk,gnk->gmn', a.astype(bf16), b.astype(bf16), preferred_element_type=f32)


def _mm_tn(a, b):
    return lax.dot_general(a.astype(bf16), b.astype(bf16), (((1,), (1,)), ((0,), (0,))),
                           preferred_element_type=f32)


def _iota(shape, dim):
    return lax.broadcasted_iota(jnp.int32, shape, dim)


def _bd(z, nblk, blk=None):
    z = z.astype(bf16)
    if blk is None:
        blk = _iota((1, 1, z.shape[-1]), 2) // (z.shape[-1] // nblk)
    return jnp.concatenate([jnp.where(blk == h, z, jnp.zeros_like(z)) for h in range(nblk)], axis=1)


def _hmm(x, z, nblk):
    return _mm(x, _bd(z, nblk))


def _hmm_nt(x, z, nblk):
    return _mm_nt(x, _bd(z, nblk))


def _hmm_split(x, z, nblk, blk=None):
    m = x.shape[1]
    x_hi, z_hi = x.astype(bf16), z.astype(bf16)
    x_lo = (x - x_hi.astype(f32)).astype(bf16)
    z_lo = (z - z_hi.astype(f32)).astype(bf16)
    both = _mm(jnp.concatenate([x_hi, x_lo], 1), _bd(z_hi, nblk, blk))
    return both[:, 0:m] + (both[:, m:2 * m] + _mm(x_hi, _bd(z_lo, nblk, blk)))


def _block_mask(rows, lanes, dr, dl):
    return (_iota((rows, lanes), 0) // dr) == (_iota((rows, lanes), 1) // dl)


def _dot01(m01, x):
    nb = x.shape[0]
    x2 = jnp.concatenate([x[i] for i in range(nb)], -1) if nb > 1 else x[0]
    hi = x2.astype(bf16)
    lo = (x2 - hi.astype(f32)).astype(bf16)
    m01 = m01.astype(bf16)
    d = lambda y: jnp.dot(m01, y, preferred_element_type=f32)
    out = d(hi) + d(lo)
    w = x.shape[-1]
    return jnp.stack([out[:, i * w:(i + 1) * w] for i in range(nb)], 0)


def _seg_sum(x, ones_bd):
    nb, c, w = x.shape
    x2 = x.reshape(nb * c, w)
    hi = x2.astype(bf16)
    lo = (x2 - hi.astype(f32)).astype(bf16)
    out = jnp.dot(hi, ones_bd, preferred_element_type=f32) + jnp.dot(lo, ones_bd, preferred_element_type=f32)
    return out.reshape(nb, c, w)


def _last_row_as_col(b, c):
    return jnp.stack([b[i, c - SUBLANE:c, :].T[:, SUBLANE - 1:SUBLANE] for i in range(b.shape[0])], 0)


def _channel_decay_mixer(q, k, v, gl, s, nblk, sums_ref, masks_ref, bdmask, c):
    lv = _levels(c)
    nl = len(lv)
    sums = _dot01(sums_ref[...], gl)
    b = sums[:, nl * c:(nl + 1) * c]
    rev = sums[:, (nl + 1) * c:(nl + 2) * c]
    row = _iota((1, c, 1), 1)
    att = masks_ref[nl] * _hmm_nt(q, k, nblk)
    for li, m in enumerate(lv):
        upper = ((row // m) % 2) == 1
        w = jnp.where(upper, q, k) * jnp.exp(sums[:, li * c:(li + 1) * c])
        att = att + masks_ref[li] * _hmm_nt(w, w, nblk)
    o = _mm(jnp.concatenate([q * jnp.exp(b), att], -1), jnp.concatenate([s.astype(bf16), _bd(v, nblk)], 1))
    s_new = jnp.exp(_last_row_as_col(b, c)) * s + jnp.where(bdmask, _mm_tn(k * jnp.exp(rev), v), 0.0)
    return o, s_new


MixCfg = collections.namedtuple("MixCfg", "c g r0 r1 layer n_chunks init p4d o_rows n_alias stacked")


def _pairs_to_bd(s):
    z = jnp.zeros(s.shape[:1] + (HEAD_DIM, HEAD_DIM), f32)
    pair = lambda p: jnp.concatenate([jnp.concatenate([s[:, 2 * p], z], -1),
                                      jnp.concatenate([z, s[:, 2 * p + 1]], -1)], 1)
    return jnp.concatenate([pair(0), pair(1)], 0)


def _quad_to_bd(s):
    z = lambda n: jnp.zeros(s.shape[:1] + (GLA_DK, n * HEAD_DIM), f32)
    rows = []
    for h in range(N_HEADS):
        parts = ([z(h)] if h else []) + [s[:, h]] + ([z(N_HEADS - 1 - h)] if h < N_HEADS - 1 else [])
        rows.append(jnp.concatenate(parts, -1))
    return jnp.concatenate(rows, 1)


def _mixer_body(cfg, *refs):
    c, g = cfg.c, cfg.g
    n_w, n_t, n_c = 7, 3, 3
    p_ref = refs[0]
    cw_ref, alog_ref, dtb_ref, lbl_ref, wg_ref, bg_ref, gains_ref = refs[1:1 + n_w]
    cos_ref, sina_ref, sinb_ref = refs[1 + n_w:1 + n_w + n_t]
    sums_ref, masks2_ref, masks4_ref = refs[1 + n_w + n_t:1 + n_w + n_t + n_c]
    pos = 1 + n_w + n_t + n_c
    if cfg.init:
        convi_ref, sai_ref, sbi_ref, sci_ref, sdi_ref = refs[pos:pos + 5]
        pos += 5
    pos += cfg.n_alias
    o_ref, convo_ref, sao_ref, sbo_ref, sco_ref, sdo_ref = refs[pos:pos + 6]
    if cfg.stacked:
        convo_ref, sao_ref, sbo_ref, sco_ref, sdo_ref = (r.at[0] for r in (convo_ref, sao_ref, sbo_ref, sco_ref, sdo_ref))
    cbuf, sa_scr, sb_scr, sc_scr, sd_scr = refs[pos + 6:]
    ci = pl.program_id(1)

    @pl.when(ci == 0)
    def _():
        cbuf[:, 0:CONV_HIST, :] = jnp.zeros((g, CONV_HIST, 3 * GROUP), f32)
        if not cfg.init:
            for r in (sa_scr, sb_scr, sc_scr, sd_scr):
                r[...] = jnp.zeros(r.shape, f32)
        else:
            shared = cfg.init == "shared"
            take = lambda r: (jnp.broadcast_to(r[...], (g,) + r.shape[1:]) if shared else r[0])
            cbuf[:, CONV_HIST - (CONV_W - 1):CONV_HIST, :] = take(convi_ref)
            sa_scr[...] = _pairs_to_bd(take(sai_ref))
            sb_scr[...] = _pairs_to_bd(take(sbi_ref))
            sc_scr[...] = _quad_to_bd(take(sci_ref))
            sd_scr[...] = _pairs_to_bd(take(sdi_ref))

    def load(a, b):
        if cfg.p4d:
            return p_ref[0, :, :, a:b]
        return p_ref[:, a:b].reshape(g, c, b - a)

    r0, r1 = cfg.r0, cfg.r1
    n_valid = float(r1 - r0)
    row = _iota((1, c, 1), 1)
    vm = ((row >= r0) & (row < r1)).astype(f32)
    lane_v = _iota((1, 1, 2 * HEAD_DIM), 2)
    lane_s = _iota((1, 1, 2 * c), 2)
    s_idx = lane_s % c
    incl = s_idx <= row
    strict = s_idx < row
    bd2 = _block_mask(2 * HEAD_DIM, 2 * HEAD_DIM, HEAD_DIM, HEAD_DIM)
    bd4 = _block_mask(N_HEADS * GLA_DK, GROUP, GLA_DK, HEAD_DIM)
    ones2 = bd2.astype(bf16)
    pairs = lambda x: jnp.concatenate([x[..., :LANE], x[..., LANE:]], 0)
    unpairs = lambda x: jnp.concatenate([x[:g], x[g:]], -1)
    n_solve = max(1, math.ceil(math.log2(r1 - r0)))
    sm = load(SMALL, SMALL + LANE)

    cbuf[:, CONV_HIST:CONV_HIST + c, :] = load(A_QKV, A_QKV + 3 * GROUP) * vm
    acc = cbuf[:, pl.ds(CONV_HIST - 3, c), :] * cw_ref[0:1, :]
    for j in range(1, CONV_W):
        acc = acc + cbuf[:, pl.ds(CONV_HIST - 3 + j, c), :] * cw_ref[j:j + 1, :]
    conv = jax.nn.silu(acc)
    beta = jax.nn.sigmoid(sm) * vm
    g_a = -jnp.exp(alog_ref[...]) * jax.nn.softplus(sm + dtb_ref[...]) * vm
    nl = len(_levels(c))
    b_a = _dot01(sums_ref[nl * c:(nl + 1) * c, :], g_a)

    def head_cols(x, lane0):
        col = lambda h: x[..., lane0 + h:lane0 + h + 1]
        return jnp.concatenate([col(0), col(2)], 0), jnp.concatenate([col(1), col(3)], 0)

    def spread(c0, c1, lane, width):
        return jnp.where(lane < width, c0, c1)

    bc0, bc1 = head_cols(b_a, SM_ALPHA)
    bt0, bt1 = head_cols(beta, SM_BETA)
    bc_v, bet_v = spread(bc0, bc1, lane_v, HEAD_DIM), spread(bt0, bt1, lane_v, HEAD_DIM)
    bc_s, bet_s = spread(bc0, bc1, lane_s, c), spread(bt0, bt1, lane_s, c)
    b_rows = [b_a[i].T for i in range(g)]
    b_row = lambda h: jnp.stack([t[SM_ALPHA + h:SM_ALPHA + h + 1, :] for t in b_rows], 0)
    br_s = jnp.concatenate([jnp.concatenate([b_row(0), b_row(1)], -1),
                            jnp.concatenate([b_row(2), b_row(3)], -1)], 0)
    dec = jnp.where(incl, jnp.exp(jnp.where(incl, bc_s - br_s, 0.0)), 0.0)
    q = pairs(conv[..., 0:GROUP])
    k = pairs(conv[..., GROUP:2 * GROUP])
    v = pairs(conv[..., 2 * GROUP:3 * GROUP] * vm)
    q = q * lax.rsqrt(_seg_sum(q * q, ones2) + NORM_EPS) * HEAD_DIM ** -0.5
    k = k * lax.rsqrt(_seg_sum(k * k, ones2) + NORM_EPS) * vm
    s_a = sa_scr[...]
    kq = jnp.concatenate([k, q], 1)
    kq_k = _hmm_nt(kq, k, 2)
    kq_s = _mm(kq, s_a)
    a_mat = jnp.where(strict, bet_s * dec * kq_k[:, 0:c], 0.0)
    gam = jnp.exp(bc_v)
    u = bet_v * (v - gam * kq_s[:, 0:c])
    p = -a_mat
    lane_pu = _iota((1, 1, 2 * c + 2 * HEAD_DIM), 2)
    head_pu = jnp.where(lane_pu < 2 * c, lane_pu // c, (lane_pu - 2 * c) // HEAD_DIM)
    for i in range(n_solve):
        if i + 1 < n_solve:
            pu = _hmm_split(p, jnp.concatenate([p, u], -1), 2, head_pu)
            p, u = pu[..., 0:2 * c], u + pu[..., 2 * c:]
        else:
            u = u + _hmm_split(p, u, 2)
    o_a = gam * kq_s[:, c:2 * c] + _hmm(dec * kq_k[:, c:2 * c], u, 2)
    bl = bc_v[:, c - 1:c, :]
    sa_scr[...] = jnp.exp(bl) * s_a + jnp.where(bd2, _mm_tn(k * jnp.exp(bl - bc_v), u), 0.0)
    o_a = unpairs(o_a * lax.rsqrt(_seg_sum(o_a * o_a, ones2) * (1.0 / HEAD_DIM) + NORM_EPS))

    lbl = lbl_ref[...]
    e = jnp.exp(lbl - jnp.max(lbl, axis=0, keepdims=True))
    prob = e / jnp.sum(e, axis=0, keepdims=True)
    lb = (jnp.sum(prob[1:cfg.layer + 1], axis=0, keepdims=True) if cfg.layer > 0
          else jnp.zeros((1, GROUP), f32))
    kb = (1.0 - lb) * jax.nn.sigmoid(-load(B_F, B_F + GROUP))
    g_b = jnp.log1p(-jnp.clip(kb, 0.0, 1.0 - GATE_CLAMP)) * vm
    o_b, s_b_new = _channel_decay_mixer(
        pairs(jax.nn.silu(load(B_Q, B_Q + GROUP))), pairs(kb * vm), pairs(load(B_I, B_I + GROUP) * vm),
        pairs(g_b), sb_scr[...], 2, sums_ref, masks2_ref, bd2, c)
    sb_scr[...] = s_b_new
    o_b = unpairs(o_b * lax.rsqrt(_seg_sum(o_b * o_b, ones2) * (1.0 / HEAD_DIM) + NORM_EPS))

    logit = jnp.dot(sm.reshape(g * c, LANE).astype(bf16), wg_ref[...].astype(bf16),
                    preferred_element_type=f32).reshape(g, c, LANE) + bg_ref[...]
    g_c = jax.nn.log_sigmoid(logit) * (vm / GLA_GATE_NORM)
    o_c, s_c_new = _channel_decay_mixer(
        load(C_Q, C_Q + N_HEADS * GLA_DK) * GLA_DK ** -0.5, load(C_K, C_K + N_HEADS * GLA_DK) * vm,
        load(C_V, C_V + GROUP) * vm, g_c, sc_scr[...], N_HEADS, sums_ref, masks4_ref, bd4, c)
    sc_scr[...] = s_c_new
    ones4 = _block_mask(GROUP, GROUP, HEAD_DIM, HEAD_DIM).astype(bf16)
    o_c = o_c * lax.rsqrt(_seg_sum(o_c * o_c, ones4) * (1.0 / HEAD_DIM) + NORM_EPS)

    def rope(x):
        x2 = x.reshape(g * c, GROUP)
        rot = lambda sh: pltpu.roll(x2, sh, 1).reshape(g, c, GROUP)
        return x * cos_ref[...] + rot(GROUP - HEAD_DIM // 2) * sina_ref[...] + rot(HEAD_DIM // 2) * sinb_ref[...]
    q = pairs(rope(load(D_Q, D_Q + GROUP)))
    k = pairs(rope(load(D_K, D_K + GROUP)) * (vm * HEAD_DIM ** -0.5))
    v = pairs(load(D_V, D_V + GROUP) * vm)
    pair_idx = _iota((2 * g, 1, 1), 0) // g
    log_gamma = lambda lane, width: jnp.log1p(-jnp.exp2(-5.0 - (2 * pair_idx + lane // width).astype(f32)))
    lg_v, lg_s = log_gamma(lane_v, HEAD_DIM), log_gamma(lane_s, c)
    cnt_c = jnp.clip(row + 1 - r0, 0, r1 - r0).astype(f32)
    cnt_s = jnp.clip(s_idx + 1 - r0, 0, r1 - r0).astype(f32)
    dec = jnp.where(incl, jnp.exp(jnp.where(incl, (cnt_c - cnt_s) * lg_s, 0.0)), 0.0)
    s_d = sd_scr[...]
    o_d = _mm(jnp.concatenate([q * jnp.exp(cnt_c * lg_v), dec * _hmm_nt(q, k, 2)], -1),
              jnp.concatenate([s_d.astype(bf16), _bd(v, 2)], 1))
    sd_scr[...] = (jnp.exp(n_valid * lg_v) * s_d
                   + jnp.where(bd2, _mm_tn(k * jnp.exp((n_valid - cnt_c) * lg_v), v), 0.0))
    oc = o_d - _seg_sum(o_d, ones2) * (1.0 / HEAD_DIM)
    o_d = unpairs(oc * lax.rsqrt(_seg_sum(oc * oc, ones2) * (1.0 / HEAD_DIM) + LN_EPS))

    for i, (o_m, gate_col) in enumerate(zip((o_a, o_b, o_c, o_d), (A_GATE, B_GATE, C_GATE, D_GATE))):
        val = (o_m * gains_ref[i:i + 1, :] * jax.nn.silu(load(gate_col, gate_col + GROUP))).astype(o_ref.dtype)
        if cfg.p4d:
            o_ref[0, :, :, i * GROUP:(i + 1) * GROUP] = val
        else:
            o_ref[0:g * c, i * GROUP:(i + 1) * GROUP] = val.reshape(g * c, GROUP)
    if cfg.o_rows > g * c:
        o_ref[g * c:cfg.o_rows, :] = jnp.zeros((cfg.o_rows - g * c, D_MODEL), o_ref.dtype)
    cbuf[:, 0:CONV_HIST, :] = cbuf[:, c:c + CONV_HIST, :]

    @pl.when(ci == cfg.n_chunks - 1)
    def _():
        convo_ref[...] = cbuf[:, pl.ds(CONV_HIST + r1 - (CONV_W - 1), CONV_W - 1), :]
        for scr, out in ((sa_scr, sao_ref), (sb_scr, sbo_ref), (sd_scr, sdo_ref)):
            s = scr[...]
            for p in range(2):
                out[:, 2 * p] = s[p * g:(p + 1) * g, 0:HEAD_DIM, 0:HEAD_DIM]
                out[:, 2 * p + 1] = s[p * g:(p + 1) * g, HEAD_DIM:2 * HEAD_DIM, HEAD_DIM:2 * HEAD_DIM]
        s = sc_scr[...]
        for h in range(N_HEADS):
            sco_ref[:, h] = s[:, h * GLA_DK:(h + 1) * GLA_DK, h * HEAD_DIM:(h + 1) * HEAD_DIM]


def _state_shapes(n, stacked):
    lead = (DEPTH, n) if stacked else (n,)
    return [jax.ShapeDtypeStruct(lead + (CONV_W - 1, 3 * GROUP), f32),
            jax.ShapeDtypeStruct(lead + (N_HEADS, HEAD_DIM, HEAD_DIM), f32),
            jax.ShapeDtypeStruct(lead + (N_HEADS, HEAD_DIM, HEAD_DIM), f32),
            jax.ShapeDtypeStruct(lead + (N_HEADS, GLA_DK, HEAD_DIM), f32),
            jax.ShapeDtypeStruct(lead + (N_HEADS, HEAD_DIM, HEAD_DIM), f32)]


def _state_out_specs(g, layer, stacked):
    if stacked:
        st = lambda k: pl.BlockSpec((1, g, N_HEADS, k, HEAD_DIM), lambda b, i: (layer, b, 0, 0, 0))
        conv = pl.BlockSpec((1, g, CONV_W - 1, 3 * GROUP), lambda b, i: (layer, b, 0, 0))
    else:
        st = lambda k: pl.BlockSpec((g, N_HEADS, k, HEAD_DIM), lambda b, i: (b, 0, 0, 0))
        conv = pl.BlockSpec((g, CONV_W - 1, 3 * GROUP), lambda b, i: (b, 0, 0))
    return [conv, st(HEAD_DIM), st(HEAD_DIM), st(GLA_DK), st(HEAD_DIM)]


def _mixer_call(cfg, name, n_groups, p, p_spec, wts, tables, table_spec, consts, init, init_specs,
                o_shape, o_spec, n_state_rows, o_prev=None, states_prev=None):
    c, g = cfg.c, cfg.g
    nl = len(_levels(c))
    wspecs = [_const_spec((CONV_W, 3 * GROUP)), _const_spec((1, LANE)), _const_spec((1, LANE)),
              _const_spec((DEPTH, GROUP)), _const_spec((LANE, LANE)), _const_spec((1, LANE)),
              _const_spec((4, GROUP))]
    cspecs = [_const_spec(((nl + 2) * c, c)), _const_spec((nl + 1, c, 2 * c)), _const_spec((nl + 1, c, N_HEADS * c))]
    in_specs = [p_spec] + wspecs + [table_spec] * 3 + cspecs + list(init_specs)
    args = [p, *wts, *tables, *consts, *init]
    aliases = {}
    if o_prev is not None:
        aliases[len(args)] = 0
        args.append(o_prev)
    for j, s_prev in enumerate(states_prev or ()):
        aliases[len(args)] = 1 + j
        args.append(s_prev)
    in_specs += [pl.BlockSpec(memory_space=pl.ANY)] * len(aliases)
    assert len(aliases) == cfg.n_alias
    return pl.pallas_call(
        functools.partial(_mixer_body, cfg),
        out_shape=[o_shape] + _state_shapes(n_state_rows, cfg.stacked),
        grid=(n_groups, cfg.n_chunks),
        in_specs=in_specs,
        out_specs=[o_spec] + _state_out_specs(g, cfg.layer, cfg.stacked),
        scratch_shapes=[pltpu.VMEM((g, c + CONV_HIST, 3 * GROUP), f32),
                        pltpu.VMEM((2 * g, 2 * HEAD_DIM, 2 * HEAD_DIM), f32),
                        pltpu.VMEM((2 * g, 2 * HEAD_DIM, 2 * HEAD_DIM), f32),
                        pltpu.VMEM((g, N_HEADS * GLA_DK, GROUP), f32),
                        pltpu.VMEM((2 * g, 2 * HEAD_DIM, 2 * HEAD_DIM), f32)],
        input_output_aliases=aliases,
        compiler_params=pltpu.CompilerParams(dimension_semantics=("arbitrary", "arbitrary"),
                                             vmem_limit_bytes=VMEM_LIMIT),
        name=name,
    )(*args)


def _tok(ref, t, nb):
    return ref[t * nb:(t + 1) * nb, :].T


def _bcast_row(ref, t, r):
    return jnp.broadcast_to(ref[t, pl.ds(r, 1), :], (HEAD_DIM, ref.shape[-1]))


def _bl_diag_loop(n_tok, kdim, j, row0, s_in, s_out, d_scr, k_scr, q_scr, v_scr, d_const=None):
    nb = s_in.shape[-1]

    def body(k, os):
        s_k = s_in[0, j, k]
        r = row0 + k
        outs = []
        for t in range(n_tok):
            d = d_const if d_const is not None else _bcast_row(d_scr, t, r)
            s_k = d * s_k + _bcast_row(k_scr, t, r) * v_scr[t, j * HEAD_DIM:(j + 1) * HEAD_DIM, :]
            outs.append(os[t] + _bcast_row(q_scr, t, r) * s_k)
        s_out[0, j, k] = s_k
        return tuple(outs)
    return lax.fori_loop(0, kdim, body, (jnp.zeros((HEAD_DIM, nb), f32),) * n_tok)


def _bl_store(o_ref, os, gate_ref, gain_ref, n_tok, nb, group_norm):
    for t in range(n_tok):
        gate = jax.nn.silu(_tok(gate_ref, t, nb))
        halves = []
        for j in range(2):
            o = os[j][t]
            if group_norm:
                o = o - jnp.mean(o, axis=0, keepdims=True)
                o = o * lax.rsqrt(jnp.mean(o * o, axis=0, keepdims=True) + LN_EPS)
            else:
                o = o * lax.rsqrt(jnp.mean(o * o, axis=0, keepdims=True) + NORM_EPS)
            hs = slice(j * HEAD_DIM, (j + 1) * HEAD_DIM)
            halves.append(o * gain_ref[hs, :] * gate[hs, :])
        o_ref[t * nb:(t + 1) * nb, :] = jnp.concatenate(halves, 0).T.astype(o_ref.dtype)


def _bl_hgrn_body(layer, n_tok, q_ref, f_ref, i_ref, gate_ref, lbl_ref, gain_ref, s_in, o_prev, s_prev,
                  o_ref, s_out, d_scr, k_scr, q_scr, v_scr):
    del o_prev, s_prev
    nb = s_in.shape[-1]
    lbl = lbl_ref[...]
    e = jnp.exp(lbl - jnp.max(lbl, axis=1, keepdims=True))
    prob = e / jnp.sum(e, axis=1, keepdims=True)
    lb = (jnp.sum(prob[:, 1:layer + 1], axis=1, keepdims=True) if layer > 0 else jnp.zeros((LANE, 1), f32))
    for t in range(n_tok):
        kk = (1.0 - lb) * jax.nn.sigmoid(-_tok(f_ref, t, nb))
        d_scr[t] = 1.0 - jnp.clip(kk, 0.0, 1.0 - GATE_CLAMP)
        k_scr[t] = kk
        q_scr[t] = jax.nn.silu(_tok(q_ref, t, nb))
        v_scr[t] = _tok(i_ref, t, nb)
    os = [_bl_diag_loop(n_tok, HEAD_DIM, j, j * HEAD_DIM, s_in, s_out, d_scr, k_scr, q_scr, v_scr) for j in range(2)]
    _bl_store(o_ref, os, gate_ref, gain_ref, n_tok, nb, False)


def _bl_gla_body(n_tok, q_ref, k_ref, v_ref, gate_ref, small_ref, wg_ref, bg_ref, gain_ref, s_in, o_prev, s_prev,
                 o_ref, s_out, d_scr, k_scr, q_scr, v_scr):
    del o_prev, s_prev
    nb = s_in.shape[-1]
    for t in range(n_tok):
        logit = jnp.dot(wg_ref[...].astype(bf16), _tok(small_ref, t, nb).astype(bf16),
                        preferred_element_type=f32) + bg_ref[...]
        d_scr[t] = jnp.exp(jax.nn.log_sigmoid(logit) * (1.0 / GLA_GATE_NORM))
        k_scr[t] = _tok(k_ref, t, nb)
        q_scr[t] = _tok(q_ref, t, nb) * GLA_DK ** -0.5
        v_scr[t] = _tok(v_ref, t, nb)
    p = pl.program_id(0)
    os = [_bl_diag_loop(n_tok, GLA_DK, j, (2 * p + j) * GLA_DK, s_in, s_out, d_scr, k_scr, q_scr, v_scr)
          for j in range(2)]
    _bl_store(o_ref, os, gate_ref, gain_ref, n_tok, nb, False)


def _bl_ret_body(n_tok, q_ref, k_ref, v_ref, gate_ref, cos_ref, sin_ref, gain_ref, s_in, o_prev, s_prev,
                 o_ref, s_out, k_scr, q_scr, v_scr):
    del o_prev, s_prev
    nb = s_in.shape[-1]
    half = HEAD_DIM // 2

    def rope(x, t):
        swapped = jnp.concatenate([x[half:2 * half], x[0:half], x[3 * half:4 * half], x[2 * half:3 * half]], 0)
        return x * cos_ref[t] + swapped * sin_ref[t]
    for t in range(n_tok):
        q_scr[t] = rope(_tok(q_ref, t, nb), t)
        k_scr[t] = rope(_tok(k_ref, t, nb), t) * HEAD_DIM ** -0.5
        v_scr[t] = _tok(v_ref, t, nb)
    p = pl.program_id(0)
    os = []
    for j in range(2):
        head = jnp.zeros((1, nb), f32) + (2 * p + j).astype(f32)
        gamma = jnp.broadcast_to(1.0 - jnp.exp2(-5.0 - head), (HEAD_DIM, nb))
        os.append(_bl_diag_loop(n_tok, HEAD_DIM, j, j * HEAD_DIM, s_in, s_out, None, k_scr, q_scr, v_scr, gamma))
    _bl_store(o_ref, os, gate_ref, gain_ref, n_tok, nb, True)


def _bl_delta_body(n_tok, q_ref, k_ref, v_ref, gate_ref, small_ref, hq_ref, hk_ref, hv_ref, cwq_ref, cwk_ref, cwv_ref,
                   alog_ref, dtb_ref, gain_ref, s_in, o_prev, s_prev, cq_prev, ck_prev, cv_prev,
                   o_ref, s_out, cq_out, ck_out, cv_out, k_scr, q_scr, kh_scr, u_scr, bg_scr):
    del o_prev, s_prev, cq_prev, ck_prev, cv_prev
    nb = s_in.shape[-1]
    p = pl.program_id(0)

    def conv(x_ref, h_ref, cw_ref):
        seq = [h_ref[0, r].T for r in range(CONV_W - 1)] + [_tok(x_ref, t, nb) for t in range(n_tok)]
        outs = []
        for t in range(n_tok):
            acc = cw_ref[:, 0:1] * seq[t]
            for jj in range(1, CONV_W):
                acc = acc + cw_ref[:, jj:jj + 1] * seq[t + jj]
            outs.append(jax.nn.silu(acc))
        return outs

    def l2norm(x):
        parts = []
        for j in range(2):
            xj = x[j * HEAD_DIM:(j + 1) * HEAD_DIM]
            parts.append(xj * lax.rsqrt(jnp.sum(xj * xj, axis=0, keepdims=True) + NORM_EPS))
        return jnp.concatenate(parts, 0)

    qs, ks, vs = conv(q_ref, hq_ref, cwq_ref), conv(k_ref, hk_ref, cwk_ref), conv(v_ref, hv_ref, cwv_ref)
    for t in range(n_tok):
        q_scr[t] = l2norm(qs[t]) * HEAD_DIM ** -0.5
        k_scr[t] = l2norm(ks[t])
        sm = _tok(small_ref, t, nb)
        both = jnp.where(_iota((LANE, 1), 0) < SM_ALPHA, jax.nn.sigmoid(sm),
                         -jnp.exp(alog_ref[...]) * jax.nn.softplus(sm + dtb_ref[...]))
        bg_scr[t] = both[0:SUBLANE]
    for r in range(CONV_W - 1):
        rows = slice((n_tok - (CONV_W - 1) + r) * nb, (n_tok - (CONV_W - 1) + r + 1) * nb)
        cq_out[0, r] = q_ref[rows, :]
        ck_out[0, r] = k_ref[rows, :]
        cv_out[0, r] = v_ref[rows, :]

    os = []
    for j in range(2):
        hs = slice(j * HEAD_DIM, (j + 1) * HEAD_DIM)
        h = 2 * p + j
        beta = [bg_scr[t, pl.ds(SM_BETA + h, 1), :] for t in range(n_tok)]
        b, acc = [], jnp.zeros((1, nb), f32)
        for t in range(n_tok):
            acc = acc + bg_scr[t, pl.ds(SM_ALPHA + h, 1), :]
            b.append(acc)
        kt = [k_scr[t, hs, :] for t in range(n_tok)]
        qt = [q_scr[t, hs, :] for t in range(n_tok)]
        dot = lambda x, y: jnp.sum(x * y, axis=0, keepdims=True)

        def sweep(scr):
            def body(k, accs):
                s_k = s_in[0, j, k]
                return tuple(a + _bcast_row(scr, t, j * HEAD_DIM + k) * s_k for t, a in enumerate(accs))
            return lax.fori_loop(0, HEAD_DIM, body, (jnp.zeros((HEAD_DIM, nb), f32),) * n_tok)
        k_s, q_s = sweep(k_scr), sweep(q_scr)
        us = []
        for t in range(n_tok):
            u = beta[t] * (vs[t][hs] - jnp.exp(b[t]) * k_s[t])
            for s in range(t):
                u = u - (beta[t] * jnp.exp(b[t] - b[s]) * dot(kt[t], kt[s])) * us[s]
            us.append(u)
        o_h = []
        for t in range(n_tok):
            o = jnp.exp(b[t]) * q_s[t]
            for s in range(t + 1):
                o = o + (jnp.exp(b[t] - b[s]) * dot(qt[t], kt[s])) * us[s]
            o_h.append(o)
        os.append(o_h)
        for t in range(n_tok):
            kh_scr[t, hs, :] = kt[t] * jnp.exp(b[-1] - b[t])
            u_scr[t] = us[t]
        decay = jnp.broadcast_to(jnp.exp(b[-1]), (HEAD_DIM, nb))

        def update(k, carry):
            s_k = decay * s_in[0, j, k]
            for t in range(n_tok):
                s_k = s_k + _bcast_row(kh_scr, t, j * HEAD_DIM + k) * u_scr[t]
            s_out[0, j, k] = s_k
            return carry
        lax.fori_loop(0, HEAD_DIM, update, 0)
    _bl_store(o_ref, os, gate_ref, gain_ref, n_tok, nb, False)


def _bl_call(body, name, mixer, n_tok, nb, blk_row, pt, cols, extra, extra_specs, state_t, kdim, layer,
             o_prev, s_prev, scratch, conv_prev=None):
    def tok_spec(c0, per_pair):
        return pl.BlockSpec((n_tok * nb, LANE),
                            (lambda p: (blk_row, c0 // LANE + p)) if per_pair else (lambda p: (blk_row, c0 // LANE)))
    st_spec = pl.BlockSpec((1, 2, kdim, HEAD_DIM, nb), lambda p: (layer, p, 0, 0, 0))
    args = [pt] * len(cols) + list(extra) + [state_t]
    in_specs = [tok_spec(c0, pp) for c0, pp in cols] + list(extra_specs) + [st_spec]
    aliases = {}

    def donate(arr, out_idx):
        if arr is not None:
            aliases[len(args)] = out_idx
        args.append(arr if arr is not None else jnp.zeros((1,), f32))
        in_specs.append(pl.BlockSpec(memory_space=pl.ANY))
    donate(o_prev, 0)
    donate(s_prev, 1)
    out_shape = [jax.ShapeDtypeStruct(o_prev.shape, o_prev.dtype),
                 jax.ShapeDtypeStruct((DEPTH,) + state_t.shape[1:], f32)]
    out_specs = [pl.BlockSpec((n_tok * nb, LANE), lambda p: (blk_row, 2 * mixer + p)), st_spec]
    if conv_prev is not None:
        for i in range(3):
            donate(conv_prev[i] if conv_prev else None, 2 + i)
            out_shape.append(jax.ShapeDtypeStruct((DEPTH, CONV_W - 1, nb, GROUP), f32))
            out_specs.append(pl.BlockSpec((1, CONV_W - 1, nb, LANE), lambda p: (layer, 0, 0, p)))
    return pl.pallas_call(
        body, out_shape=out_shape, grid=(2,), in_specs=in_specs, out_specs=out_specs,
        scratch_shapes=scratch, input_output_aliases=aliases,
        compiler_params=pltpu.CompilerParams(dimension_semantics=("arbitrary",), vmem_limit_bytes=VMEM_LIMIT),
        name=name,
    )(*args)


def _rope_tables(pos):
    half = HEAD_DIM // 2
    inv = 1.0 / (RET_THETA_BASE ** jnp.linspace(0.0, 1.0, half, dtype=f32))
    ang = pos[:, None] * inv[None, :]
    cos, sin = jnp.cos(ang), jnp.sin(ang)
    zero = jnp.zeros_like(sin)
    tile = lambda a, b: jnp.tile(jnp.concatenate([a, b], -1), (1, N_HEADS))
    return tile(cos, cos), tile(-sin, zero), tile(zero, sin)


def kernel(x_prompt, x_sample, state_delta_conv, state_delta, state_hgrn, state_gla, state_ret, meta_tokens, emb_ln_g, emb_ln_b, w_in, conv_w, delta_a_log, delta_dt_bias, delta_norm_g, hgrn_lb_logits, hgrn_norm_g, gla_w_gate, gla_b_gate, gla_norm_g, ret_norm_g, w_out, ln1_g, ln1_b, w_ffn_gate, w_ffn_up, w_ffn_down, ln2_g, ln2_b):
    bsz, seq, _ = x_prompt.shape
    n_seq, n_tok, _ = x_sample.shape
    pg = PROMPT_GROUP if bsz % PROMPT_GROUP == 0 else 1
    rows_main = bsz * seq
    rows_s = n_tok * n_seq
    assert seq % CHUNK == 0 and rows_main % ROW_TILE == 0 and N_META <= CHUNK
    assert n_tok >= CONV_W - 1 and ROW_TILE % rows_s == 0 and (n_seq % LANE == 0 or n_seq < LANE)
    lead_pad = CHUNK - N_META
    rows_tail = 2 * ROW_TILE
    s_blk = ROW_TILE // rows_s

    xs = jnp.swapaxes(x_sample, 0, 1).reshape(rows_s, D_MODEL)
    xt = jnp.concatenate([jnp.zeros((lead_pad, D_MODEL), f32), meta_tokens.astype(f32),
                          jnp.zeros((ROW_TILE - CHUNK, D_MODEL), f32), xs,
                          jnp.zeros((ROW_TILE - rows_s, D_MODEL), f32)], 0)
    xm = x_prompt.reshape(rows_main, D_MODEL)

    tab_m = _rope_tables(jnp.arange(CHUNK, dtype=f32) - lead_pad)
    tab_p = _rope_tables(N_META + jnp.arange(seq, dtype=f32))
    consts_c = tuple(jnp.asarray(a, f32) for a in _chunk_constants(CHUNK))
    row_vec = lambda v: v.reshape(1, -1).astype(f32)
    col_vec = lambda v: v.reshape(-1, 1).astype(f32)
    small_row = lambda v: jnp.zeros((1, LANE), f32).at[0, SM_ALPHA:SM_ALPHA + N_HEADS].set(v)
    n_chunks = seq // CHUNK

    seq_last = lambda s: jnp.transpose(s, (0, 2, 3, 4, 1))
    sd_t, sh_t, sg_t, sr_t = (seq_last(s) for s in (state_delta, state_hgrn, state_gla, state_ret))
    conv_t = jnp.transpose(state_delta_conv, (0, 2, 1, 3))
    half = HEAD_DIM // 2
    ang = ((PAST_LEN + jnp.arange(n_tok, dtype=f32))[:, None]
           * (1.0 / (RET_THETA_BASE ** jnp.linspace(0.0, 1.0, half, dtype=f32)))[None, :])
    cos_s = jnp.tile(jnp.cos(ang), (1, 4))[..., None]
    sin_s = jnp.tile(jnp.concatenate([-jnp.sin(ang), jnp.sin(ang)], -1), (1, 2))[..., None]
    vm = lambda shape: pltpu.VMEM(shape, f32)
    tok_scr = lambda: vm((n_tok, LANE, n_seq))

    w_in_t = jnp.swapaxes(w_in, 1, 2)
    st_p = None
    sa_s = sb_s = sc_s = sd_s = conv_s = None
    for l in range(DEPTH):
        pm, pt = _proj(xm, xt, row_vec(emb_ln_g), row_vec(emb_ln_b), w_in_t, l, with_ln=(l == 0))
        wg = jnp.zeros((LANE, N_HEADS * GLA_DK), f32).at[SM_LR:SM_LR + GLA_RANK].set(gla_w_gate[l])
        gains = jnp.stack([delta_norm_g[l], hgrn_norm_g[l], gla_norm_g[l], ret_norm_g[l]], 0)
        wts = (conv_w[l], small_row(delta_a_log[l]), small_row(delta_dt_bias[l]), hgrn_lb_logits,
               wg, row_vec(gla_b_gate[l]), gains)

        cfg = MixCfg(CHUNK, 1, lead_pad, CHUNK, l, 1, None, False, ROW_TILE, 0, False)
        ot, *st_meta = _mixer_call(
            cfg, "mixers_meta", 1, pt, pl.BlockSpec((CHUNK, D_IN_AL), lambda b, i: (0, 0)),
            wts, tab_m, pl.BlockSpec((CHUNK, GROUP), lambda b, i: (0, 0)), consts_c, (), (),
            jax.ShapeDtypeStruct((rows_tail, D_MODEL), bf16),
            pl.BlockSpec((ROW_TILE, D_MODEL), lambda b, i: (0, 0)), 1)

        n_prev = 0 if st_p is None else len(st_p)
        cfg = MixCfg(CHUNK, pg, 0, CHUNK, l, n_chunks, "shared", True, pg * CHUNK, n_prev, True)
        st1 = lambda k: pl.BlockSpec((1, N_HEADS, k, HEAD_DIM), lambda b, i: (0, 0, 0, 0))
        om, *st_p = _mixer_call(
            cfg, "mixers_prompt", bsz // pg, pm.reshape(bsz // pg, pg, seq, D_IN_AL),
            pl.BlockSpec((1, pg, CHUNK, D_IN_AL), lambda b, i: (b, 0, i, 0)),
            wts, tab_p, pl.BlockSpec((CHUNK, GROUP), lambda b, i: (i, 0)), consts_c, st_meta,
            [pl.BlockSpec((1, CONV_W - 1, 3 * GROUP), lambda b, i: (0, 0, 0)),
             st1(HEAD_DIM), st1(HEAD_DIM), st1(GLA_DK), st1(HEAD_DIM)],
            jax.ShapeDtypeStruct((bsz // pg, pg, seq, D_MODEL), bf16),
            pl.BlockSpec((1, pg, CHUNK, D_MODEL), lambda b, i: (b, 0, i, 0)), bsz, states_prev=st_p)
        om = om.reshape(rows_main, D_MODEL)

        pair_col = lambda v: (col_vec(v), pl.BlockSpec((LANE, 1), lambda p: (p, 0)))
        full = lambda a: (a, pl.BlockSpec(a.shape, lambda p: (0,) * a.ndim))
        split = lambda pairs: ([a for a, _ in pairs], [s for _, s in pairs])
        hist = lambda part: (conv_t, pl.BlockSpec((1, CONV_W - 1, n_seq, LANE), lambda p: (l, 0, 0, 2 * part + p)))
        cw_t = conv_w[l].T
        cw = lambda part: (cw_t, pl.BlockSpec((LANE, CONV_W), lambda p: (2 * part + p, 0)))
        small_col = lambda v: jnp.zeros((LANE, 1), f32).at[SM_ALPHA:SM_ALPHA + N_HEADS, 0].set(v)
        extra, especs = split([hist(0), hist(1), hist(2), cw(0), cw(1), cw(2), full(small_col(delta_a_log[l])),
                               full(small_col(delta_dt_bias[l])), pair_col(delta_norm_g[l])])
        ot, sa_s, *conv_s = _bl_call(
            functools.partial(_bl_delta_body, n_tok), "sample_delta", 0, n_tok, n_seq, s_blk, pt,
            [(A_QKV, True), (A_QKV + GROUP, True), (A_QKV + 2 * GROUP, True), (A_GATE, True), (SMALL, False)],
            extra, especs, sd_t, HEAD_DIM, l, ot, sa_s,
            [tok_scr(), tok_scr(), tok_scr(), vm((n_tok, HEAD_DIM, n_seq)), vm((n_tok, SUBLANE, n_seq))],
            conv_prev=conv_s or [])
        extra, especs = split([(hgrn_lb_logits.T, pl.BlockSpec((LANE, DEPTH), lambda p: (p, 0))),
                               pair_col(hgrn_norm_g[l])])
        ot, sb_s = _bl_call(
            functools.partial(_bl_hgrn_body, l, n_tok), "sample_hgrn", 1, n_tok, n_seq, s_blk, pt,
            [(B_Q, True), (B_F, True), (B_I, True), (B_GATE, True)], extra, especs, sh_t, HEAD_DIM, l, ot, sb_s,
            [tok_scr() for _ in range(4)])
        wg_t = jnp.zeros((N_HEADS * GLA_DK, LANE), f32).at[:, SM_LR:SM_LR + GLA_RANK].set(gla_w_gate[l].T)
        extra, especs = split([full(wg_t), full(col_vec(gla_b_gate[l])), pair_col(gla_norm_g[l])])
        ot, sc_s = _bl_call(
            functools.partial(_bl_gla_body, n_tok), "sample_gla", 2, n_tok, n_seq, s_blk, pt,
            [(C_Q, False), (C_K, False), (C_V, True), (C_GATE, True), (SMALL, False)], extra, especs,
            sg_t, GLA_DK, l, ot, sc_s, [tok_scr() for _ in range(4)])
        extra, especs = split([full(cos_s), full(sin_s), pair_col(ret_norm_g[l])])
        ot, sd_s = _bl_call(
            functools.partial(_bl_ret_body, n_tok), "sample_ret", 3, n_tok, n_seq, s_blk, pt,
            [(D_Q, True), (D_K, True), (D_V, True), (D_GATE, True)], extra, especs, sr_t, HEAD_DIM, l, ot, sd_s,
            [tok_scr() for _ in range(3)])

        xm, xt = _post(xm, xt, om, ot, row_vec(emb_ln_g), row_vec(emb_ln_b),
                       row_vec(ln1_g[l]), row_vec(ln1_b[l]), row_vec(ln2_g[l]), row_vec(ln2_b[l]),
                       w_out, w_ffn_gate, w_ffn_up, w_ffn_down, l)

    y_prompt = xm.reshape(bsz, seq, D_MODEL)
    y_sample = jnp.swapaxes(xt[ROW_TILE:ROW_TILE + rows_s].reshape(n_tok, n_seq, D_MODEL), 0, 1)
    seq_first = lambda s: jnp.transpose(s, (0, 4, 1, 2, 3))
    st_s = (jnp.transpose(jnp.concatenate(conv_s, -1), (0, 2, 1, 3)),
            seq_first(sa_s), seq_first(sb_s), seq_first(sc_s), seq_first(sd_s))
    return (y_prompt, y_sample) + tuple(v for pair in zip(st_p, st_s) for v in pair)
```

```python
import collections
import functools
import math

import numpy as np
import jax
import jax.numpy as jnp
from jax import lax
from jax.experimental import pallas as pl
from jax.experimental.pallas import tpu as pltpu

f32 = jnp.float32
bf16 = jnp.bfloat16

D_MODEL = 1024
N_META = 16
CHUNK = 64
GROUP = 256
N_HEADS = 4
HEAD_DIM = 64
GLA_DK = 32
GLA_RANK = 16
GLA_GATE_NORM = 16.0
CONV_W = 4
D_FF = 2816
DEPTH = 2
ALPHA = (2 * DEPTH) ** 0.25
PAST_LEN = 16384
RET_THETA_BASE = 10000.0
LN_EPS = 1e-5
NORM_EPS = 1e-6
GATE_CLAMP = 1e-6
IN_SPLITS = (GROUP, GROUP, GROUP, N_HEADS, N_HEADS, GROUP,
             GROUP, GROUP, GROUP, GROUP,
             N_HEADS * GLA_DK, N_HEADS * GLA_DK, GROUP, GLA_RANK, GROUP,
             GROUP, GROUP, GROUP, GROUP)

LANE = 128
SUBLANE = 8
VMEM_LIMIT = 56 * 1024 * 1024

COL_ORDER = (0, 1, 2, 5, 6, 7, 8, 9, 10, 11, 12, 14, 15, 16, 17, 18)
A_QKV, A_GATE = 0, 768
B_Q, B_F, B_I, B_GATE = 1024, 1280, 1536, 1792
C_Q, C_K, C_V, C_GATE = 2048, 2176, 2304, 2560
D_Q, D_K, D_V, D_GATE = 2816, 3072, 3328, 3584
SMALL = 3840
SM_BETA, SM_ALPHA, SM_LR = 0, N_HEADS, 2 * N_HEADS
D_IN_AL = SMALL + LANE
SILU_COLS = (A_GATE, B_Q, B_GATE, C_GATE, D_GATE)
CONV_HIST = SUBLANE
ROW_TILE = 512
PROJ_STAGE_ROWS = 256
POST_STAGE_ROWS_WIDE = 128
POST_STAGE_ROWS_NARROW = 352
PROMPT_GROUP = 8


def _layer_norm(x, g, b):
    mu = jnp.mean(x, -1, keepdims=True)
    xc = x - mu
    var = jnp.mean(xc * xc, -1, keepdims=True)
    return xc * lax.rsqrt(var + LN_EPS) * g + b


def _row_specs(width, n_main, n_tail, tile=ROW_TILE):
    return (pl.BlockSpec((tile, width), lambda i: (jnp.minimum(i, n_main - 1), 0)),
            pl.BlockSpec((tile, width), lambda i: (jnp.clip(i - n_main, 0, n_tail - 1), 0)))


def _const_spec(shape, single_buffer=False):
    nd = len(shape)
    kw = dict(pipeline_mode=pl.Buffered(1)) if single_buffer else {}
    return pl.BlockSpec(shape, lambda *_: (0,) * nd, **kw)


def _on_region(n_main, main_fn, tail_fn):
    i = pl.program_id(0)
    pl.when(i < n_main)(main_fn)
    pl.when(i >= n_main)(tail_fn)


def _stage_rows(w_hbm, layer, n_rows, chunk, stage, sem, store):
    assert n_rows % chunk == 0 and chunk <= stage.shape[1]
    n = n_rows // chunk

    def copy(i):
        return pltpu.make_async_copy(w_hbm.at[layer, pl.ds(i * chunk, chunk), :],
                                     stage.at[i % 2, pl.ds(0, chunk), :], sem.at[i % 2])
    copy(0).start()
    for i in range(n):
        if i + 1 < n:
            copy(i + 1).start()
        copy(i).wait()
        store(i * chunk, stage.at[i % 2, pl.ds(0, chunk), :])


def _w_in_segments():
    offs = np.concatenate([[0], np.cumsum(IN_SPLITS)])
    runs, dst = [], 0
    for i in COL_ORDER + (3, 4, 13):
        if runs and runs[-1][1] == offs[i]:
            runs[-1][1] = int(offs[i + 1])
        else:
            runs.append([int(offs[i]), int(offs[i + 1]), dst])
        dst += IN_SPLITS[i]
    return [tuple(r) for r in runs]


def _proj_tile(with_ln, x_ref, g_ref, b_ref, w_ref, p_ref):
    x = x_ref[...]
    if with_ln:
        x = _layer_norm(x, g_ref[...], b_ref[...])
    h = x.astype(bf16)
    n = w_ref.shape[1]
    for n0 in range(0, n, GROUP):
        n1 = min(n0 + GROUP, n)
        p = jnp.dot(h, w_ref[:, n0:n1], preferred_element_type=f32)
        p_ref[:, n0:n1] = jax.nn.silu(p) if n0 in SILU_COLS else p


def _stage_w_in(wt_hbm, layer, w_scr, stage, sem):
    rows = PROJ_STAGE_ROWS
    big, small = [], []
    for s0, s1, d0 in _w_in_segments():
        if (s1 - s0) % rows == 0 and d0 % LANE == 0:
            big += [(s0 + j, d0 + j) for j in range(0, s1 - s0, rows)]
        else:
            small.append((s0, s1 - s0, d0 - SMALL))
    assert all(0 <= d and d + n <= LANE and d % SUBLANE == 0 for _, n, d in small) and len(small) <= 2

    def copy(j):
        return pltpu.make_async_copy(wt_hbm.at[layer, pl.ds(big[j][0], rows), :], stage.at[j % 2], sem.at[j % 2])
    copy(0).start()
    for j in range(len(big)):
        if j + 1 < len(big):
            copy(j + 1).start()
        copy(j).wait()
        w_scr[:, big[j][1]:big[j][1] + rows] = stage[j % 2].T.astype(bf16)
    used = max(d + n for _, n, d in small)
    stage[0, used:LANE, :] = jnp.zeros((LANE - used, D_MODEL), f32)
    copies = [pltpu.make_async_copy(wt_hbm.at[layer, pl.ds(s0, n), :], stage.at[0, pl.ds(d, n), :], sem.at[i])
              for i, (s0, n, d) in enumerate(small)]
    for cp in copies:
        cp.start()
    for cp in copies:
        cp.wait()
    w_scr[:, SMALL:SMALL + LANE] = stage[0, 0:LANE, :].T.astype(bf16)


def _proj_body(with_ln, layer, n_main, xm_ref, xt_ref, g_ref, b_ref, wt_hbm, pm_ref, pt_ref, w_scr, stage, sem):
    pl.when(pl.program_id(0) == 0)(lambda: _stage_w_in(wt_hbm, layer, w_scr, stage, sem))
    _on_region(n_main,
               lambda: _proj_tile(with_ln, xm_ref, g_ref, b_ref, w_scr, pm_ref),
               lambda: _proj_tile(with_ln, xt_ref, g_ref, b_ref, w_scr, pt_ref))


def _proj(xm, xt, g, b, w_in_t, layer, with_ln):
    tile = ROW_TILE
    n_main, n_tail = xm.shape[0] // tile, xt.shape[0] // tile
    n = D_IN_AL
    out_shape = [jax.ShapeDtypeStruct((xm.shape[0], n), f32), jax.ShapeDtypeStruct((xt.shape[0], n), f32)]
    out_specs = list(_row_specs(n, n_main, n_tail, tile))
    return pl.pallas_call(
        functools.partial(_proj_body, with_ln, layer, n_main),
        out_shape=out_shape,
        grid=(n_main + n_tail,),
        in_specs=list(_row_specs(D_MODEL, n_main, n_tail, tile))
                 + [_const_spec((1, D_MODEL)), _const_spec((1, D_MODEL)), pl.BlockSpec(memory_space=pl.ANY)],
        out_specs=out_specs,
        scratch_shapes=[pltpu.VMEM((D_MODEL, n), bf16),
                        pltpu.VMEM((2, PROJ_STAGE_ROWS, D_MODEL), f32),
                        pltpu.SemaphoreType.DMA((2,))],
        compiler_params=pltpu.CompilerParams(dimension_semantics=("arbitrary",),
                                             vmem_limit_bytes=VMEM_LIMIT),
        name="in_proj",
    )(xm, xt, g, b, w_in_t)


def _post_tile(emb_ln, x_ref, o_ref, ge_ref, be_ref, wo_ref, g1_ref, b1_ref, wg_ref, wu_ref, wd_ref, g2_ref, b2_ref,
               y_ref):
    x = x_ref[...]
    if emb_ln:
        x = _layer_norm(x, ge_ref[...], be_ref[...])
    m = jnp.dot(o_ref[...], wo_ref[...], preferred_element_type=f32)
    y1 = _layer_norm(ALPHA * x + m, g1_ref[...], b1_ref[...])
    y1b = y1.astype(bf16)
    dff = wg_ref.shape[1]
    step = 4 * LANE
    acc = jnp.zeros(y1.shape, f32)
    for n0 in range(0, dff, step):
        n1 = min(n0 + step, dff)
        gate = jnp.dot(y1b, wg_ref[:, n0:n1], preferred_element_type=f32)
        up = jnp.dot(y1b, wu_ref[:, n0:n1], preferred_element_type=f32)
        act = (jax.nn.silu(gate) * up).astype(bf16)
        acc = acc + jnp.dot(act, wd_ref[n0:n1, :], preferred_element_type=f32)
    y_ref[...] = _layer_norm(ALPHA * y1 + acc, g2_ref[...], b2_ref[...])


def _post_body(layer, n_main, xm_ref, xt_ref, om_ref, ot_ref, ge_ref, be_ref, g1_ref, b1_ref, g2_ref, b2_ref,
               wo_hbm, wg_hbm, wu_hbm, wd_hbm, ym_ref, yt_ref,
               wo_scr, wg_scr, wu_scr, wd_scr, stage_wide, stage_narrow, sem):
    @pl.when(pl.program_id(0) == 0)
    def _():
        def into(dst):
            def store(r0, ref):
                dst[r0:r0 + ref.shape[0], :] = ref[...].astype(bf16)
            return store
        _stage_rows(wo_hbm, layer, D_MODEL, PROJ_STAGE_ROWS, stage_narrow, sem, into(wo_scr))
        _stage_rows(wg_hbm, layer, D_MODEL, POST_STAGE_ROWS_WIDE, stage_wide, sem, into(wg_scr))
        _stage_rows(wu_hbm, layer, D_MODEL, POST_STAGE_ROWS_WIDE, stage_wide, sem, into(wu_scr))
        _stage_rows(wd_hbm, layer, D_FF, POST_STAGE_ROWS_NARROW, stage_narrow, sem, into(wd_scr))

    w = (ge_ref, be_ref, wo_scr, g1_ref, b1_ref, wg_scr, wu_scr, wd_scr, g2_ref, b2_ref)
    _on_region(n_main, lambda: _post_tile(layer == 0, xm_ref, om_ref, *w, ym_ref),
               lambda: _post_tile(layer == 0, xt_ref, ot_ref, *w, yt_ref))


def _post(xm, xt, om, ot, ge, be, g1, b1, g2, b2, w_out, w_gate, w_up, w_down, layer):
    n_main, n_tail = xm.shape[0] // ROW_TILE, xt.shape[0] // ROW_TILE
    rows = lambda: list(_row_specs(D_MODEL, n_main, n_tail))
    vec = lambda: _const_spec((1, D_MODEL))
    hbm = lambda: pl.BlockSpec(memory_space=pl.ANY)
    return pl.pallas_call(
        functools.partial(_post_body, layer, n_main),
        out_shape=[jax.ShapeDtypeStruct(xm.shape, f32), jax.ShapeDtypeStruct(xt.shape, f32)],
        grid=(n_main + n_tail,),
        in_specs=rows() + rows() + [vec() for _ in range(6)] + [hbm() for _ in range(4)],
        out_specs=rows(),
        scratch_shapes=[pltpu.VMEM((D_MODEL, D_MODEL), bf16), pltpu.VMEM((D_MODEL, D_FF), bf16),
                        pltpu.VMEM((D_MODEL, D_FF), bf16), pltpu.VMEM((D_FF, D_MODEL), bf16),
                        pltpu.VMEM((2, POST_STAGE_ROWS_WIDE, D_FF), f32),
                        pltpu.VMEM((2, POST_STAGE_ROWS_NARROW, D_MODEL), f32),
                        pltpu.SemaphoreType.DMA((2,))],
        compiler_params=pltpu.CompilerParams(dimension_semantics=("arbitrary",),
                                             vmem_limit_bytes=VMEM_LIMIT),
        name="post_ffn",
    )(xm, xt, om, ot, ge, be, g1, b1, g2, b2, w_out, w_gate, w_up, w_down)


def _levels(c):
    m, out = c // 2, []
    while m >= 1:
        out.append(m)
        m //= 2
    return tuple(out)


def _chunk_constants(c):
    lv = _levels(c)
    t = np.arange(c)
    rows, masks = [], []
    for m in lv:
        blk = t // m
        upper = (blk % 2) == 1
        lo = np.where(upper, blk * m, t + 1)
        hi = np.where(upper, t, (blk + 1) * m - 1)
        rows.append(((t[None, :] >= lo[:, None]) & (t[None, :] <= hi[:, None])).astype(np.float32))
        same = (t[:, None] // (2 * m)) == (t[None, :] // (2 * m))
        masks.append((same & upper[:, None] & (~upper)[None, :]).astype(np.float32))
    rows.append((t[None, :] <= t[:, None]).astype(np.float32))
    rows.append((t[None, :] > t[:, None]).astype(np.float32))
    masks.append(np.eye(c, dtype=np.float32))
    masks = np.stack(masks, 0)
    return np.concatenate(rows, 0), np.tile(masks, (1, 1, 2)), np.tile(masks, (1, 1, N_HEADS))


def _mm(a, b):
    return jnp.einsum('gmk,gkn->gmn', a.astype(bf16), b.astype(bf16), preferred_element_type=f32)


def _mm_nt(a, b):
    return jnp.einsum('gmk,gnk->gmn', a.astype(bf16), b.astype(bf16), preferred_element_type=f32)


def _mm_tn(a, b):
    return lax.dot_general(a.astype(bf16), b.astype(bf16), (((1,), (1,)), ((0,), (0,))),
                           preferred_element_type=f32)


def _iota(shape, dim):
    return lax.broadcasted_iota(jnp.int32, shape, dim)


def _bd(z, nblk, blk=None):
    z = z.astype(bf16)
    if blk is None:
        blk = _iota((1, 1, z.shape[-1]), 2) // (z.shape[-1] // nblk)
    return jnp.concatenate([jnp.where(blk == h, z, jnp.zeros_like(z)) for h in range(nblk)], axis=1)


def _hmm(x, z, nblk):
    return _mm(x, _bd(z, nblk))


def _hmm_nt(x, z, nblk):
    return _mm_nt(x, _bd(z, nblk))


def _hmm_split(x, z, nblk, blk=None):
    m = x.shape[1]
    x_hi, z_hi = x.astype(bf16), z.astype(bf16)
    x_lo = (x - x_hi.astype(f32)).astype(bf16)
    z_lo = (z - z_hi.astype(f32)).astype(bf16)
    both = _mm(jnp.concatenate([x_hi, x_lo], 1), _bd(z_hi, nblk, blk))
    return both[:, 0:m] + (both[:, m:2 * m] + _mm(x_hi, _bd(z_lo, nblk, blk)))


def _block_mask(rows, lanes, dr, dl):
    return (_iota((rows, lanes), 0) // dr) == (_iota((rows, lanes), 1) // dl)


def _dot01(m01, x):
    nb = x.shape[0]
    x2 = jnp.concatenate([x[i] for i in range(nb)], -1) if nb > 1 else x[0]
    hi = x2.astype(bf16)
    lo = (x2 - hi.astype(f32)).astype(bf16)
    m01 = m01.astype(bf16)
    d = lambda y: jnp.dot(m01, y, preferred_element_type=f32)
    out = d(hi) + d(lo)
    w = x.shape[-1]
    return jnp.stack([out[:, i * w:(i + 1) * w] for i in range(nb)], 0)


def _seg_sum(x, ones_bd, pieces=2):
    nb, c, w = x.shape
    x2 = x.reshape(nb * c, w)
    hi = x2.astype(bf16)
    out = jnp.dot(hi, ones_bd, preferred_element_type=f32)
    if pieces == 2:
        lo = (x2 - hi.astype(f32)).astype(bf16)
        out = out + jnp.dot(lo, ones_bd, preferred_element_type=f32)
    return out.reshape(nb, c, w)


def _last_row_as_col(b, c):
    return jnp.stack([b[i, c - SUBLANE:c, :].T[:, SUBLANE - 1:SUBLANE] for i in range(b.shape[0])], 0)


def _channel_decay_mixer(q, k, v, gl, s, nblk, sums_ref, masks_ref, bdmask, c):
    lv = _levels(c)
    nl = len(lv)
    sums = _dot01(sums_ref[...], gl)
    b = sums[:, nl * c:(nl + 1) * c]
    rev = sums[:, (nl + 1) * c:(nl + 2) * c]
    row = _iota((1, c, 1), 1)
    att = masks_ref[nl] * _hmm_nt(q, k, nblk)
    for li, m in enumerate(lv):
        upper = ((row // m) % 2) == 1
        w = jnp.where(upper, q, k) * jnp.exp(sums[:, li * c:(li + 1) * c])
        att = att + masks_ref[li] * _hmm_nt(w, w, nblk)
    o = _mm(jnp.concatenate([q * jnp.exp(b), att], -1), jnp.concatenate([s.astype(bf16), _bd(v, nblk)], 1))
    s_new = jnp.exp(_last_row_as_col(b, c)) * s + jnp.where(bdmask, _mm_tn(k * jnp.exp(rev), v), 0.0)
    return o, s_new


MixCfg = collections.namedtuple("MixCfg", "c g r0 r1 layer n_chunks init p4d o_rows n_alias stacked")


def _pairs_to_bd(s):
    z = jnp.zeros(s.shape[:1] + (HEAD_DIM, HEAD_DIM), f32)
    pair = lambda p: jnp.concatenate([jnp.concatenate([s[:, 2 * p], z], -1),
                                      jnp.concatenate([z, s[:, 2 * p + 1]], -1)], 1)
    return jnp.concatenate([pair(0), pair(1)], 0)


def _quad_to_bd(s):
    z = lambda n: jnp.zeros(s.shape[:1] + (GLA_DK, n * HEAD_DIM), f32)
    rows = []
    for h in range(N_HEADS):
        parts = ([z(h)] if h else []) + [s[:, h]] + ([z(N_HEADS - 1 - h)] if h < N_HEADS - 1 else [])
        rows.append(jnp.concatenate(parts, -1))
    return jnp.concatenate(rows, 1)


def _mixer_body(cfg, *refs):
    c, g = cfg.c, cfg.g
    n_w, n_t, n_c = 7, 3, 3
    p_ref = refs[0]
    cw_ref, alog_ref, dtb_ref, lbl_ref, wg_ref, bg_ref, gains_ref = refs[1:1 + n_w]
    cos_ref, sina_ref, sinb_ref = refs[1 + n_w:1 + n_w + n_t]
    sums_ref, masks2_ref, masks4_ref = refs[1 + n_w + n_t:1 + n_w + n_t + n_c]
    pos = 1 + n_w + n_t + n_c
    if cfg.init:
        convi_ref, sai_ref, sbi_ref, sci_ref, sdi_ref = refs[pos:pos + 5]
        pos += 5
    pos += cfg.n_alias
    o_ref, convo_ref, sao_ref, sbo_ref, sco_ref, sdo_ref = refs[pos:pos + 6]
    if cfg.stacked:
        convo_ref, sao_ref, sbo_ref, sco_ref, sdo_ref = (r.at[0] for r in (convo_ref, sao_ref, sbo_ref, sco_ref, sdo_ref))
    cbuf, sa_scr, sb_scr, sc_scr, sd_scr = refs[pos + 6:]
    ci = pl.program_id(1)

    @pl.when(ci == 0)
    def _():
        cbuf[:, 0:CONV_HIST, :] = jnp.zeros((g, CONV_HIST, 3 * GROUP), f32)
        if not cfg.init:
            for r in (sa_scr, sb_scr, sc_scr, sd_scr):
                r[...] = jnp.zeros(r.shape, f32)
        else:
            take = lambda r: jnp.broadcast_to(r[...], (g,) + r.shape[1:])
            cbuf[:, CONV_HIST - (CONV_W - 1):CONV_HIST, :] = take(convi_ref)
            sa_scr[...] = _pairs_to_bd(take(sai_ref))
            sb_scr[...] = _pairs_to_bd(take(sbi_ref))
            sc_scr[...] = _quad_to_bd(take(sci_ref))
            sd_scr[...] = _pairs_to_bd(take(sdi_ref))

    def load(a, b):
        if cfg.p4d:
            return p_ref[0, :, :, a:b]
        return p_ref[:, a:b].reshape(g, c, b - a)

    r0, r1 = cfg.r0, cfg.r1
    n_valid = float(r1 - r0)
    row = _iota((1, c, 1), 1)
    vm = ((row >= r0) & (row < r1)).astype(f32)
    lane_v = _iota((1, 1, 2 * HEAD_DIM), 2)
    lane_s = _iota((1, 1, 2 * c), 2)
    s_idx = lane_s % c
    incl = s_idx <= row
    strict = s_idx < row
    bd2 = _block_mask(2 * HEAD_DIM, 2 * HEAD_DIM, HEAD_DIM, HEAD_DIM)
    bd4 = _block_mask(N_HEADS * GLA_DK, GROUP, GLA_DK, HEAD_DIM)
    ones2 = bd2.astype(bf16)
    pairs = lambda x: jnp.concatenate([x[..., :LANE], x[..., LANE:]], 0)
    unpairs = lambda x: jnp.concatenate([x[:g], x[g:]], -1)
    n_solve = max(1, math.ceil(math.log2(r1 - r0)))
    sm = load(SMALL, SMALL + LANE)
    valid = (lambda x: x) if (r0 == 0 and r1 == c) else (lambda x: x * vm)

    cbuf[:, CONV_HIST:CONV_HIST + c, :] = valid(load(A_QKV, A_QKV + 3 * GROUP))
    acc = cbuf[:, pl.ds(CONV_HIST - 3, c), :] * cw_ref[0:1, :]
    for j in range(1, CONV_W):
        acc = acc + cbuf[:, pl.ds(CONV_HIST - 3 + j, c), :] * cw_ref[j:j + 1, :]
    conv = jax.nn.silu(acc)
    beta = valid(jax.nn.sigmoid(sm))
    g_a = valid(-jnp.exp(alog_ref[...]) * jax.nn.softplus(sm + dtb_ref[...]))
    nl = len(_levels(c))
    b_a = _dot01(sums_ref[nl * c:(nl + 1) * c, :], g_a)

    def head_cols(x, lane0):
        col = lambda h: x[..., lane0 + h:lane0 + h + 1]
        return jnp.concatenate([col(0), col(2)], 0), jnp.concatenate([col(1), col(3)], 0)

    def spread(c0, c1, lane, width):
        return jnp.where(lane < width, c0, c1)

    bc0, bc1 = head_cols(b_a, SM_ALPHA)
    bt0, bt1 = head_cols(beta, SM_BETA)
    bc_v, bet_v = spread(bc0, bc1, lane_v, HEAD_DIM), spread(bt0, bt1, lane_v, HEAD_DIM)
    bc_s, bet_s = spread(bc0, bc1, lane_s, c), spread(bt0, bt1, lane_s, c)
    b_rows = [b_a[i].T for i in range(g)]
    b_row = lambda h: jnp.stack([t[SM_ALPHA + h:SM_ALPHA + h + 1, :] for t in b_rows], 0)
    br_s = jnp.concatenate([jnp.concatenate([b_row(0), b_row(1)], -1),
                            jnp.concatenate([b_row(2), b_row(3)], -1)], 0)
    dec = jnp.where(incl, jnp.exp(jnp.where(incl, bc_s - br_s, 0.0)), 0.0)
    q = pairs(conv[..., 0:GROUP])
    k = pairs(conv[..., GROUP:2 * GROUP])
    v = pairs(valid(conv[..., 2 * GROUP:3 * GROUP]))
    q = q * lax.rsqrt(_seg_sum(q * q, ones2) + NORM_EPS) * HEAD_DIM ** -0.5
    k = valid(k * lax.rsqrt(_seg_sum(k * k, ones2) + NORM_EPS))
    s_a = sa_scr[...]
    kq = jnp.concatenate([k, q], 1)
    kq_k = _hmm_nt(kq, k, 2)
    kq_s = _mm(kq, s_a)
    a_mat = jnp.where(strict, bet_s * dec * kq_k[:, 0:c], 0.0)
    gam = jnp.exp(bc_v)
    u = bet_v * (v - gam * kq_s[:, 0:c])
    p = -a_mat
    lane_pu = _iota((1, 1, 2 * c + 2 * HEAD_DIM), 2)
    head_pu = jnp.where(lane_pu < 2 * c, lane_pu // c, (lane_pu - 2 * c) // HEAD_DIM)
    for i in range(n_solve):
        if i + 1 < n_solve:
            pu = _hmm_split(p, jnp.concatenate([p, u], -1), 2, head_pu)
            p, u = pu[..., 0:2 * c], u + pu[..., 2 * c:]
        else:
            u = u + _hmm_split(p, u, 2)
    o_a = gam * kq_s[:, c:2 * c] + _hmm(dec * kq_k[:, c:2 * c], u, 2)
    bl = bc_v[:, c - 1:c, :]
    sa_scr[...] = jnp.exp(bl) * s_a + jnp.where(bd2, _mm_tn(k * jnp.exp(bl - bc_v), u), 0.0)
    o_a = unpairs(o_a * lax.rsqrt(_seg_sum(o_a * o_a, ones2, 1) * (1.0 / HEAD_DIM) + NORM_EPS))

    lbl = lbl_ref[...]
    e = jnp.exp(lbl - jnp.max(lbl, axis=0, keepdims=True))
    prob = e / jnp.sum(e, axis=0, keepdims=True)
    lb = (jnp.sum(prob[1:cfg.layer + 1], axis=0, keepdims=True) if cfg.layer > 0
          else jnp.zeros((1, GROUP), f32))
    kb = (1.0 - lb) * jax.nn.sigmoid(-load(B_F, B_F + GROUP))
    g_b = valid(jnp.log1p(-jnp.clip(kb, 0.0, 1.0 - GATE_CLAMP)))
    o_b, s_b_new = _channel_decay_mixer(
        pairs(load(B_Q, B_Q + GROUP)), pairs(valid(kb)), pairs(valid(load(B_I, B_I + GROUP))),
        pairs(g_b), sb_scr[...], 2, sums_ref, masks2_ref, bd2, c)
    sb_scr[...] = s_b_new
    o_b = unpairs(o_b * lax.rsqrt(_seg_sum(o_b * o_b, ones2, 1) * (1.0 / HEAD_DIM) + NORM_EPS))

    logit = jnp.dot(sm.reshape(g * c, LANE).astype(bf16), wg_ref[...].astype(bf16),
                    preferred_element_type=f32).reshape(g, c, LANE) + bg_ref[...]
    g_c = valid(jax.nn.log_sigmoid(logit) * (1.0 / GLA_GATE_NORM))
    o_c, s_c_new = _channel_decay_mixer(
        load(C_Q, C_Q + N_HEADS * GLA_DK) * GLA_DK ** -0.5, valid(load(C_K, C_K + N_HEADS * GLA_DK)),
        valid(load(C_V, C_V + GROUP)), g_c, sc_scr[...], N_HEADS, sums_ref, masks4_ref, bd4, c)
    sc_scr[...] = s_c_new
    ones4 = _block_mask(GROUP, GROUP, HEAD_DIM, HEAD_DIM).astype(bf16)
    o_c = o_c * lax.rsqrt(_seg_sum(o_c * o_c, ones4, 1) * (1.0 / HEAD_DIM) + NORM_EPS)

    def rope(x):
        x2 = x.reshape(g * c, GROUP)
        rot = lambda sh: pltpu.roll(x2, sh, 1).reshape(g, c, GROUP)
        return x * cos_ref[...] + rot(GROUP - HEAD_DIM // 2) * sina_ref[...] + rot(HEAD_DIM // 2) * sinb_ref[...]
    q = pairs(rope(load(D_Q, D_Q + GROUP)))
    k = pairs(valid(rope(load(D_K, D_K + GROUP)) * HEAD_DIM ** -0.5))
    v = pairs(valid(load(D_V, D_V + GROUP)))
    pair_idx = _iota((2, 1, 1), 0)
    log_gamma = lambda lane, width: jnp.log1p(-jnp.exp2(-5.0 - (2 * pair_idx + lane // width).astype(f32)))
    per_seq = lambda x: jnp.concatenate([jnp.broadcast_to(x[p:p + 1], (g,) + x.shape[1:]) for p in range(2)], 0)
    lg_v, lg_s = log_gamma(lane_v, HEAD_DIM), log_gamma(lane_s, c)
    cnt_c = jnp.clip(row + 1 - r0, 0, r1 - r0).astype(f32)
    cnt_s = jnp.clip(s_idx + 1 - r0, 0, r1 - r0).astype(f32)
    dec = per_seq(jnp.where(incl, jnp.exp(jnp.where(incl, (cnt_c - cnt_s) * lg_s, 0.0)), 0.0))
    s_d = sd_scr[...]
    o_d = _mm(jnp.concatenate([q * per_seq(jnp.exp(cnt_c * lg_v)), dec * _hmm_nt(q, k, 2)], -1),
              jnp.concatenate([s_d.astype(bf16), _bd(v, 2)], 1))
    sd_scr[...] = (per_seq(jnp.exp(n_valid * lg_v)) * s_d
                   + jnp.where(bd2, _mm_tn(k * per_seq(jnp.exp((n_valid - cnt_c) * lg_v)), v), 0.0))
    oc = o_d - _seg_sum(o_d, ones2) * (1.0 / HEAD_DIM)
    o_d = unpairs(oc * lax.rsqrt(_seg_sum(oc * oc, ones2, 1) * (1.0 / HEAD_DIM) + LN_EPS))

    for i, (o_m, gate_col) in enumerate(zip((o_a, o_b, o_c, o_d), (A_GATE, B_GATE, C_GATE, D_GATE))):
        val = (o_m * gains_ref[i:i + 1, :] * load(gate_col, gate_col + GROUP)).astype(o_ref.dtype)
        if cfg.p4d:
            o_ref[0, :, :, i * GROUP:(i + 1) * GROUP] = val
        else:
            o_ref[0:g * c, i * GROUP:(i + 1) * GROUP] = val.reshape(g * c, GROUP)
    if cfg.o_rows > g * c:
        o_ref[g * c:cfg.o_rows, :] = jnp.zeros((cfg.o_rows - g * c, D_MODEL), o_ref.dtype)
    cbuf[:, 0:CONV_HIST, :] = cbuf[:, c:c + CONV_HIST, :]

    @pl.when(ci == cfg.n_chunks - 1)
    def _():
        convo_ref[...] = cbuf[:, pl.ds(CONV_HIST + r1 - (CONV_W - 1), CONV_W - 1), :]
        for scr, out in ((sa_scr, sao_ref), (sb_scr, sbo_ref), (sd_scr, sdo_ref)):
            s = scr[...]
            for p in range(2):
                out[:, 2 * p] = s[p * g:(p + 1) * g, 0:HEAD_DIM, 0:HEAD_DIM]
                out[:, 2 * p + 1] = s[p * g:(p + 1) * g, HEAD_DIM:2 * HEAD_DIM, HEAD_DIM:2 * HEAD_DIM]
        s = sc_scr[...]
        for h in range(N_HEADS):
            sco_ref[:, h] = s[:, h * GLA_DK:(h + 1) * GLA_DK, h * HEAD_DIM:(h + 1) * HEAD_DIM]


def _state_shapes(n, stacked):
    lead = (DEPTH, n) if stacked else (n,)
    return [jax.ShapeDtypeStruct(lead + (CONV_W - 1, 3 * GROUP), f32),
            jax.ShapeDtypeStruct(lead + (N_HEADS, HEAD_DIM, HEAD_DIM), f32),
            jax.ShapeDtypeStruct(lead + (N_HEADS, HEAD_DIM, HEAD_DIM), f32),
            jax.ShapeDtypeStruct(lead + (N_HEADS, GLA_DK, HEAD_DIM), f32),
            jax.ShapeDtypeStruct(lead + (N_HEADS, HEAD_DIM, HEAD_DIM), f32)]


def _state_out_specs(g, layer, stacked):
    if stacked:
        st = lambda k: pl.BlockSpec((1, g, N_HEADS, k, HEAD_DIM), lambda b, i: (layer, b, 0, 0, 0))
        conv = pl.BlockSpec((1, g, CONV_W - 1, 3 * GROUP), lambda b, i: (layer, b, 0, 0))
    else:
        st = lambda k: pl.BlockSpec((g, N_HEADS, k, HEAD_DIM), lambda b, i: (b, 0, 0, 0))
        conv = pl.BlockSpec((g, CONV_W - 1, 3 * GROUP), lambda b, i: (b, 0, 0))
    return [conv, st(HEAD_DIM), st(HEAD_DIM), st(GLA_DK), st(HEAD_DIM)]


def _mixer_call(cfg, name, n_groups, p, p_spec, wts, tables, table_spec, consts, init, init_specs,
                o_shape, o_spec, n_state_rows, o_prev=None, states_prev=None):
    c, g = cfg.c, cfg.g
    nl = len(_levels(c))
    wspecs = [_const_spec((CONV_W, 3 * GROUP)), _const_spec((1, LANE)), _const_spec((1, LANE)),
              _const_spec((DEPTH, GROUP)), _const_spec((LANE, LANE)), _const_spec((1, LANE)),
              _const_spec((4, GROUP))]
    cspecs = [_const_spec(((nl + 2) * c, c)), _const_spec((nl + 1, c, 2 * c)), _const_spec((nl + 1, c, N_HEADS * c))]
    in_specs = [p_spec] + wspecs + [table_spec] * 3 + cspecs + list(init_specs)
    args = [p, *wts, *tables, *consts, *init]
    aliases = {}
    if o_prev is not None:
        aliases[len(args)] = 0
        args.append(o_prev)
    for j, s_prev in enumerate(states_prev or ()):
        aliases[len(args)] = 1 + j
        args.append(s_prev)
    in_specs += [pl.BlockSpec(memory_space=pl.ANY)] * len(aliases)
    assert len(aliases) == cfg.n_alias
    return pl.pallas_call(
        functools.partial(_mixer_body, cfg),
        out_shape=[o_shape] + _state_shapes(n_state_rows, cfg.stacked),
        grid=(n_groups, cfg.n_chunks),
        in_specs=in_specs,
        out_specs=[o_spec] + _state_out_specs(g, cfg.layer, cfg.stacked),
        scratch_shapes=[pltpu.VMEM((g, c + CONV_HIST, 3 * GROUP), f32),
                        pltpu.VMEM((2 * g, 2 * HEAD_DIM, 2 * HEAD_DIM), f32),
                        pltpu.VMEM((2 * g, 2 * HEAD_DIM, 2 * HEAD_DIM), f32),
                        pltpu.VMEM((g, N_HEADS * GLA_DK, GROUP), f32),
                        pltpu.VMEM((2 * g, 2 * HEAD_DIM, 2 * HEAD_DIM), f32)],
        input_output_aliases=aliases,
        compiler_params=pltpu.CompilerParams(dimension_semantics=("arbitrary", "arbitrary"),
                                             vmem_limit_bytes=VMEM_LIMIT),
        name=name,
    )(*args)


def _tok(ref, t, nb):
    return ref[t * nb:(t + 1) * nb, :].T


def _bcast_row(ref, t, r):
    return jnp.broadcast_to(ref[t, pl.ds(r, 1), :], (HEAD_DIM, ref.shape[-1]))


def _bl_diag_loop(n_tok, kdim, j, row0, s_in, s_out, d_scr, k_scr, q_scr, v_scr, d_const=None):
    nb = s_in.shape[-1]

    def body(k, os):
        s_k = s_in[0, j, k]
        r = row0 + k
        outs = []
        for t in range(n_tok):
            d = d_const if d_const is not None else _bcast_row(d_scr, t, r)
            s_k = d * s_k + _bcast_row(k_scr, t, r) * v_scr[t, j * HEAD_DIM:(j + 1) * HEAD_DIM, :]
            outs.append(os[t] + _bcast_row(q_scr, t, r) * s_k)
        s_out[0, j, k] = s_k
        return tuple(outs)
    return lax.fori_loop(0, kdim, body, (jnp.zeros((HEAD_DIM, nb), f32),) * n_tok)


def _bl_store(o_ref, os, gate_ref, gain_ref, n_tok, nb, group_norm):
    for t in range(n_tok):
        gate = _tok(gate_ref, t, nb)
        halves = []
        for j in range(2):
            o = os[j][t]
            if group_norm:
                o = o - jnp.mean(o, axis=0, keepdims=True)
                o = o * lax.rsqrt(jnp.mean(o * o, axis=0, keepdims=True) + LN_EPS)
            else:
                o = o * lax.rsqrt(jnp.mean(o * o, axis=0, keepdims=True) + NORM_EPS)
            hs = slice(j * HEAD_DIM, (j + 1) * HEAD_DIM)
            halves.append(o * gain_ref[hs, :] * gate[hs, :])
        o_ref[t * nb:(t + 1) * nb, :] = jnp.concatenate(halves, 0).T.astype(o_ref.dtype)


def _bl_hgrn_body(layer, n_tok, q_ref, f_ref, i_ref, gate_ref, lbl_ref, gain_ref, s_in, o_prev, s_prev,
                  o_ref, s_out, d_scr, k_scr, q_scr, v_scr):
    del o_prev, s_prev
    nb = s_in.shape[-1]
    lbl = lbl_ref[...]
    e = jnp.exp(lbl - jnp.max(lbl, axis=1, keepdims=True))
    prob = e / jnp.sum(e, axis=1, keepdims=True)
    lb = (jnp.sum(prob[:, 1:layer + 1], axis=1, keepdims=True) if layer > 0 else jnp.zeros((LANE, 1), f32))
    for t in range(n_tok):
        kk = (1.0 - lb) * jax.nn.sigmoid(-_tok(f_ref, t, nb))
        d_scr[t] = 1.0 - jnp.clip(kk, 0.0, 1.0 - GATE_CLAMP)
        k_scr[t] = kk
        q_scr[t] = _tok(q_ref, t, nb)
        v_scr[t] = _tok(i_ref, t, nb)
    os = [_bl_diag_loop(n_tok, HEAD_DIM, j, j * HEAD_DIM, s_in, s_out, d_scr, k_scr, q_scr, v_scr) for j in range(2)]
    _bl_store(o_ref, os, gate_ref, gain_ref, n_tok, nb, False)


def _bl_gla_body(n_tok, q_ref, k_ref, v_ref, gate_ref, small_ref, wg_ref, bg_ref, gain_ref, s_in, o_prev, s_prev,
                 o_ref, s_out, d_scr, k_scr, q_scr, v_scr):
    del o_prev, s_prev
    nb = s_in.shape[-1]
    for t in range(n_tok):
        logit = jnp.dot(wg_ref[...].astype(bf16), _tok(small_ref, t, nb).astype(bf16),
                        preferred_element_type=f32) + bg_ref[...]
        d_scr[t] = jnp.exp(jax.nn.log_sigmoid(logit) * (1.0 / GLA_GATE_NORM))
        k_scr[t] = _tok(k_ref, t, nb)
        q_scr[t] = _tok(q_ref, t, nb) * GLA_DK ** -0.5
        v_scr[t] = _tok(v_ref, t, nb)
    p = pl.program_id(0)
    os = [_bl_diag_loop(n_tok, GLA_DK, j, (2 * p + j) * GLA_DK, s_in, s_out, d_scr, k_scr, q_scr, v_scr)
          for j in range(2)]
    _bl_store(o_ref, os, gate_ref, gain_ref, n_tok, nb, False)


def _bl_ret_body(n_tok, q_ref, k_ref, v_ref, gate_ref, cos_ref, sin_ref, gain_ref, s_in, o_prev, s_prev,
                 o_ref, s_out, k_scr, q_scr, v_scr):
    del o_prev, s_prev
    nb = s_in.shape[-1]
    half = HEAD_DIM // 2

    def rope(x, t):
        swapped = jnp.concatenate([x[half:2 * half], x[0:half], x[3 * half:4 * half], x[2 * half:3 * half]], 0)
        return x * cos_ref[t] + swapped * sin_ref[t]
    for t in range(n_tok):
        q_scr[t] = rope(_tok(q_ref, t, nb), t)
        k_scr[t] = rope(_tok(k_ref, t, nb), t) * HEAD_DIM ** -0.5
        v_scr[t] = _tok(v_ref, t, nb)
    p = pl.program_id(0)
    os = []
    for j in range(2):
        head = jnp.zeros((1, nb), f32) + (2 * p + j).astype(f32)
        gamma = jnp.broadcast_to(1.0 - jnp.exp2(-5.0 - head), (HEAD_DIM, nb))
        os.append(_bl_diag_loop(n_tok, HEAD_DIM, j, j * HEAD_DIM, s_in, s_out, None, k_scr, q_scr, v_scr, gamma))
    _bl_store(o_ref, os, gate_ref, gain_ref, n_tok, nb, True)


def _bl_delta_body(n_tok, q_ref, k_ref, v_ref, gate_ref, small_ref, hq_ref, hk_ref, hv_ref, cwq_ref, cwk_ref, cwv_ref,
                   alog_ref, dtb_ref, gain_ref, s_in, o_prev, s_prev, cq_prev, ck_prev, cv_prev,
                   o_ref, s_out, cq_out, ck_out, cv_out, k_scr, q_scr, kh_scr, u_scr, bg_scr):
    del o_prev, s_prev, cq_prev, ck_prev, cv_prev
    nb = s_in.shape[-1]
    p = pl.program_id(0)

    def conv(x_ref, h_ref, cw_ref):
        seq = [h_ref[0, r].T for r in range(CONV_W - 1)] + [_tok(x_ref, t, nb) for t in range(n_tok)]
        outs = []
        for t in range(n_tok):
            acc = cw_ref[:, 0:1] * seq[t]
            for jj in range(1, CONV_W):
                acc = acc + cw_ref[:, jj:jj + 1] * seq[t + jj]
            outs.append(jax.nn.silu(acc))
        return outs

    def l2norm(x):
        parts = []
        for j in range(2):
            xj = x[j * HEAD_DIM:(j + 1) * HEAD_DIM]
            parts.append(xj * lax.rsqrt(jnp.sum(xj * xj, axis=0, keepdims=True) + NORM_EPS))
        return jnp.concatenate(parts, 0)

    qs, ks, vs = conv(q_ref, hq_ref, cwq_ref), conv(k_ref, hk_ref, cwk_ref), conv(v_ref, hv_ref, cwv_ref)
    for t in range(n_tok):
        q_scr[t] = l2norm(qs[t]) * HEAD_DIM ** -0.5
        k_scr[t] = l2norm(ks[t])
        sm = _tok(small_ref, t, nb)
        both = jnp.where(_iota((LANE, 1), 0) < SM_ALPHA, jax.nn.sigmoid(sm),
                         -jnp.exp(alog_ref[...]) * jax.nn.softplus(sm + dtb_ref[...]))
        bg_scr[t] = both[0:SUBLANE]
    for r in range(CONV_W - 1):
        rows = slice((n_tok - (CONV_W - 1) + r) * nb, (n_tok - (CONV_W - 1) + r + 1) * nb)
        cq_out[0, r] = q_ref[rows, :]
        ck_out[0, r] = k_ref[rows, :]
        cv_out[0, r] = v_ref[rows, :]

    os = []
    for j in range(2):
        hs = slice(j * HEAD_DIM, (j + 1) * HEAD_DIM)
        h = 2 * p + j
        beta = [bg_scr[t, pl.ds(SM_BETA + h, 1), :] for t in range(n_tok)]
        b, acc = [], jnp.zeros((1, nb), f32)
        for t in range(n_tok):
            acc = acc + bg_scr[t, pl.ds(SM_ALPHA + h, 1), :]
            b.append(acc)
        kt = [k_scr[t, hs, :] for t in range(n_tok)]
        qt = [q_scr[t, hs, :] for t in range(n_tok)]
        dot = lambda x, y: jnp.sum(x * y, axis=0, keepdims=True)

        def sweep(scr):
            def body(k, accs):
                s_k = s_in[0, j, k]
                return tuple(a + _bcast_row(scr, t, j * HEAD_DIM + k) * s_k for t, a in enumerate(accs))
            return lax.fori_loop(0, HEAD_DIM, body, (jnp.zeros((HEAD_DIM, nb), f32),) * n_tok)
        k_s, q_s = sweep(k_scr), sweep(q_scr)
        us = []
        for t in range(n_tok):
            u = beta[t] * (vs[t][hs] - jnp.exp(b[t]) * k_s[t])
            for s in range(t):
                u = u - (beta[t] * jnp.exp(b[t] - b[s]) * dot(kt[t], kt[s])) * us[s]
            us.append(u)
        o_h = []
        for t in range(n_tok):
            o = jnp.exp(b[t]) * q_s[t]
            for s in range(t + 1):
                o = o + (jnp.exp(b[t] - b[s]) * dot(qt[t], kt[s])) * us[s]
            o_h.append(o)
        os.append(o_h)
        for t in range(n_tok):
            kh_scr[t, hs, :] = kt[t] * jnp.exp(b[-1] - b[t])
            u_scr[t] = us[t]
        decay = jnp.broadcast_to(jnp.exp(b[-1]), (HEAD_DIM, nb))

        def update(k, carry):
            s_k = decay * s_in[0, j, k]
            for t in range(n_tok):
                s_k = s_k + _bcast_row(kh_scr, t, j * HEAD_DIM + k) * u_scr[t]
            s_out[0, j, k] = s_k
            return carry
        lax.fori_loop(0, HEAD_DIM, update, 0)
    _bl_store(o_ref, os, gate_ref, gain_ref, n_tok, nb, False)


def _bl_call(body, name, mixer, n_tok, nb, blk_row, pt, cols, extra, extra_specs, state_t, kdim, layer,
             o_prev, s_prev, scratch, conv_prev=None):
    def tok_spec(c0, per_pair):
        return pl.BlockSpec((n_tok * nb, LANE),
                            (lambda p: (blk_row, c0 // LANE + p)) if per_pair else (lambda p: (blk_row, c0 // LANE)))
    st_spec = pl.BlockSpec((1, 2, kdim, HEAD_DIM, nb), lambda p: (layer, p, 0, 0, 0))
    args = [pt] * len(cols) + list(extra) + [state_t]
    in_specs = [tok_spec(c0, pp) for c0, pp in cols] + list(extra_specs) + [st_spec]
    aliases = {}

    def donate(arr, out_idx):
        if arr is not None:
            aliases[len(args)] = out_idx
        args.append(arr if arr is not None else jnp.zeros((1,), f32))
        in_specs.append(pl.BlockSpec(memory_space=pl.ANY))
    donate(o_prev, 0)
    donate(s_prev, 1)
    out_shape = [jax.ShapeDtypeStruct(o_prev.shape, o_prev.dtype),
                 jax.ShapeDtypeStruct((DEPTH,) + state_t.shape[1:], f32)]
    out_specs = [pl.BlockSpec((n_tok * nb, LANE), lambda p: (blk_row, 2 * mixer + p)), st_spec]
    if conv_prev is not None:
        for i in range(3):
            donate(conv_prev[i] if conv_prev else None, 2 + i)
            out_shape.append(jax.ShapeDtypeStruct((DEPTH, CONV_W - 1, nb, GROUP), f32))
            out_specs.append(pl.BlockSpec((1, CONV_W - 1, nb, LANE), lambda p: (layer, 0, 0, p)))
    return pl.pallas_call(
        body, out_shape=out_shape, grid=(2,), in_specs=in_specs, out_specs=out_specs,
        scratch_shapes=scratch, input_output_aliases=aliases,
        compiler_params=pltpu.CompilerParams(dimension_semantics=("arbitrary",), vmem_limit_bytes=VMEM_LIMIT),
        name=name,
    )(*args)


def _rope_tables(pos):
    half = HEAD_DIM // 2
    inv = 1.0 / (RET_THETA_BASE ** jnp.linspace(0.0, 1.0, half, dtype=f32))
    ang = pos[:, None] * inv[None, :]
    cos, sin = jnp.cos(ang), jnp.sin(ang)
    zero = jnp.zeros_like(sin)
    tile = lambda a, b: jnp.tile(jnp.concatenate([a, b], -1), (1, N_HEADS))
    return tile(cos, cos), tile(-sin, zero), tile(zero, sin)


def kernel(x_prompt, x_sample, state_delta_conv, state_delta, state_hgrn, state_gla, state_ret, meta_tokens, emb_ln_g, emb_ln_b, w_in, conv_w, delta_a_log, delta_dt_bias, delta_norm_g, hgrn_lb_logits, hgrn_norm_g, gla_w_gate, gla_b_gate, gla_norm_g, ret_norm_g, w_out, ln1_g, ln1_b, w_ffn_gate, w_ffn_up, w_ffn_down, ln2_g, ln2_b):
    bsz, seq, _ = x_prompt.shape
    n_seq, n_tok, _ = x_sample.shape
    pg = PROMPT_GROUP if bsz % PROMPT_GROUP == 0 else 1
    rows_main = bsz * seq
    rows_s = n_tok * n_seq
    assert seq % CHUNK == 0 and rows_main % ROW_TILE == 0 and N_META <= CHUNK
    assert n_tok >= CONV_W - 1 and ROW_TILE % rows_s == 0 and (n_seq % LANE == 0 or n_seq < LANE)
    lead_pad = CHUNK - N_META
    rows_tail = 2 * ROW_TILE
    s_blk = ROW_TILE // rows_s

    xs = jnp.swapaxes(x_sample, 0, 1).reshape(rows_s, D_MODEL)
    xt = jnp.concatenate([jnp.zeros((lead_pad, D_MODEL), f32), meta_tokens.astype(f32),
                          jnp.zeros((ROW_TILE - CHUNK, D_MODEL), f32), xs,
                          jnp.zeros((ROW_TILE - rows_s, D_MODEL), f32)], 0)
    xm = x_prompt.reshape(rows_main, D_MODEL)

    tab_m = _rope_tables(jnp.arange(CHUNK, dtype=f32) - lead_pad)
    tab_p = _rope_tables(N_META + jnp.arange(seq, dtype=f32))
    consts_c = tuple(jnp.asarray(a, f32) for a in _chunk_constants(CHUNK))
    row_vec = lambda v: v.reshape(1, -1).astype(f32)
    col_vec = lambda v: v.reshape(-1, 1).astype(f32)
    small_row = lambda v: jnp.zeros((1, LANE), f32).at[0, SM_ALPHA:SM_ALPHA + N_HEADS].set(v)
    n_chunks = seq // CHUNK

    seq_last = lambda s: jnp.transpose(s, (0, 2, 3, 4, 1))
    sd_t, sh_t, sg_t, sr_t = (seq_last(s) for s in (state_delta, state_hgrn, state_gla, state_ret))
    conv_t = jnp.transpose(state_delta_conv, (0, 2, 1, 3))
    half = HEAD_DIM // 2
    ang = ((PAST_LEN + jnp.arange(n_tok, dtype=f32))[:, None]
           * (1.0 / (RET_THETA_BASE ** jnp.linspace(0.0, 1.0, half, dtype=f32)))[None, :])
    cos_s = jnp.tile(jnp.cos(ang), (1, 4))[..., None]
    sin_s = jnp.tile(jnp.concatenate([-jnp.sin(ang), jnp.sin(ang)], -1), (1, 2))[..., None]
    vm = lambda shape: pltpu.VMEM(shape, f32)
    tok_scr = lambda: vm((n_tok, LANE, n_seq))

    w_in_t = jnp.swapaxes(w_in, 1, 2)
    st_p = None
    sa_s = sb_s = sc_s = sd_s = conv_s = None
    for l in range(DEPTH):
        pm, pt = _proj(xm, xt, row_vec(emb_ln_g), row_vec(emb_ln_b), w_in_t, l, with_ln=(l == 0))
        wg = jnp.zeros((LANE, N_HEADS * GLA_DK), f32).at[SM_LR:SM_LR + GLA_RANK].set(gla_w_gate[l])
        gains = jnp.stack([delta_norm_g[l], hgrn_norm_g[l], gla_norm_g[l], ret_norm_g[l]], 0)
        wts = (conv_w[l], small_row(delta_a_log[l]), small_row(delta_dt_bias[l]), hgrn_lb_logits,
               wg, row_vec(gla_b_gate[l]), gains)

        cfg = MixCfg(CHUNK, 1, lead_pad, CHUNK, l, 1, None, False, ROW_TILE, 0, False)
        ot, *st_meta = _mixer_call(
            cfg, "mixers_meta", 1, pt, pl.BlockSpec((CHUNK, D_IN_AL), lambda b, i: (0, 0)),
            wts, tab_m, pl.BlockSpec((CHUNK, GROUP), lambda b, i: (0, 0)), consts_c, (), (),
            jax.ShapeDtypeStruct((rows_tail, D_MODEL), bf16),
            pl.BlockSpec((ROW_TILE, D_MODEL), lambda b, i: (0, 0)), 1)

        n_prev = 0 if st_p is None else len(st_p)
        cfg = MixCfg(CHUNK, pg, 0, CHUNK, l, n_chunks, "shared", True, pg * CHUNK, n_prev, True)
        st1 = lambda k: pl.BlockSpec((1, N_HEADS, k, HEAD_DIM), lambda b, i: (0, 0, 0, 0))
        om, *st_p = _mixer_call(
            cfg, "mixers_prompt", bsz // pg, pm.reshape(bsz // pg, pg, seq, D_IN_AL),
            pl.BlockSpec((1, pg, CHUNK, D_IN_AL), lambda b, i: (b, 0, i, 0)),
            wts, tab_p, pl.BlockSpec((CHUNK, GROUP), lambda b, i: (i, 0)), consts_c, st_meta,
            [pl.BlockSpec((1, CONV_W - 1, 3 * GROUP), lambda b, i: (0, 0, 0)),
             st1(HEAD_DIM), st1(HEAD_DIM), st1(GLA_DK), st1(HEAD_DIM)],
            jax.ShapeDtypeStruct((bsz // pg, pg, seq, D_MODEL), bf16),
            pl.BlockSpec((1, pg, CHUNK, D_MODEL), lambda b, i: (b, 0, i, 0)), bsz, states_prev=st_p)
        om = om.reshape(rows_main, D_MODEL)

        pair_col = lambda v: (col_vec(v), pl.BlockSpec((LANE, 1), lambda p: (p, 0)))
        full = lambda a: (a, pl.BlockSpec(a.shape, lambda p: (0,) * a.ndim))
        split = lambda pairs: ([a for a, _ in pairs], [s for _, s in pairs])
        hist = lambda part: (conv_t, pl.BlockSpec((1, CONV_W - 1, n_seq, LANE), lambda p: (l, 0, 0, 2 * part + p)))
        cw_t = conv_w[l].T
        cw = lambda part: (cw_t, pl.BlockSpec((LANE, CONV_W), lambda p: (2 * part + p, 0)))
        small_col = lambda v: jnp.zeros((LANE, 1), f32).at[SM_ALPHA:SM_ALPHA + N_HEADS, 0].set(v)
        extra, especs = split([hist(0), hist(1), hist(2), cw(0), cw(1), cw(2), full(small_col(delta_a_log[l])),
                               full(small_col(delta_dt_bias[l])), pair_col(delta_norm_g[l])])
        ot, sa_s, *conv_s = _bl_call(
            functools.partial(_bl_delta_body, n_tok), "sample_delta", 0, n_tok, n_seq, s_blk, pt,
            [(A_QKV, True), (A_QKV + GROUP, True), (A_QKV + 2 * GROUP, True), (A_GATE, True), (SMALL, False)],
            extra, especs, sd_t, HEAD_DIM, l, ot, sa_s,
            [tok_scr(), tok_scr(), tok_scr(), vm((n_tok, HEAD_DIM, n_seq)), vm((n_tok, SUBLANE, n_seq))],
            conv_prev=conv_s or [])
        extra, especs = split([(hgrn_lb_logits.T, pl.BlockSpec((LANE, DEPTH), lambda p: (p, 0))),
                               pair_col(hgrn_norm_g[l])])
        ot, sb_s = _bl_call(
            functools.partial(_bl_hgrn_body, l, n_tok), "sample_hgrn", 1, n_tok, n_seq, s_blk, pt,
            [(B_Q, True), (B_F, True), (B_I, True), (B_GATE, True)], extra, especs, sh_t, HEAD_DIM, l, ot, sb_s,
            [tok_scr() for _ in range(4)])
        wg_t = jnp.zeros((N_HEADS * GLA_DK, LANE), f32).at[:, SM_LR:SM_LR + GLA_RANK].set(gla_w_gate[l].T)
        extra, especs = split([full(wg_t), full(col_vec(gla_b_gate[l])), pair_col(gla_norm_g[l])])
        ot, sc_s = _bl_call(
            functools.partial(_bl_gla_body, n_tok), "sample_gla", 2, n_tok, n_seq, s_blk, pt,
            [(C_Q, False), (C_K, False), (C_V, True), (C_GATE, True), (SMALL, False)], extra, especs,
            sg_t, GLA_DK, l, ot, sc_s, [tok_scr() for _ in range(4)])
        extra, especs = split([full(cos_s), full(sin_s), pair_col(ret_norm_g[l])])
        ot, sd_s = _bl_call(
            functools.partial(_bl_ret_body, n_tok), "sample_ret", 3, n_tok, n_seq, s_blk, pt,
            [(D_Q, True), (D_K, True), (D_V, True), (D_GATE, True)], extra, especs, sr_t, HEAD_DIM, l, ot, sd_s,
            [tok_scr() for _ in range(3)])

        xm, xt = _post(xm, xt, om, ot, row_vec(emb_ln_g), row_vec(emb_ln_b),
                       row_vec(ln1_g[l]), row_vec(ln1_b[l]), row_vec(ln2_g[l]), row_vec(ln2_b[l]),
                       w_out, w_ffn_gate, w_ffn_up, w_ffn_down, l)

    y_prompt = xm.reshape(bsz, seq, D_MODEL)
    y_sample = jnp.swapaxes(xt[ROW_TILE:ROW_TILE + rows_s].reshape(n_tok, n_seq, D_MODEL), 0, 1)
    seq_first = lambda s: jnp.transpose(s, (0, 4, 1, 2, 3))
    st_s = (jnp.transpose(jnp.concatenate(conv_s, -1), (0, 2, 1, 3)),
            seq_first(sa_s), seq_first(sb_s), seq_first(sc_s), seq_first(sd_s))
    return (y_prompt, y_sample) + tuple(v for pair in zip(st_p, st_s) for v in pair)
```

```python
import collections
import functools
import math

import numpy as np
import jax
import jax.numpy as jnp
from jax import lax
from jax.experimental import pallas as pl
from jax.experimental.pallas import tpu as pltpu

f32 = jnp.float32
bf16 = jnp.bfloat16

D_MODEL = 1024
N_META = 16
CHUNK = 64
GROUP = 256
N_HEADS = 4
HEAD_DIM = 64
GLA_DK = 32
GLA_RANK = 16
GLA_GATE_NORM = 16.0
CONV_W = 4
D_FF = 2816
DEPTH = 2
ALPHA = (2 * DEPTH) ** 0.25
PAST_LEN = 16384
RET_THETA_BASE = 10000.0
LN_EPS = 1e-5
NORM_EPS = 1e-6
GATE_CLAMP = 1e-6
IN_SPLITS = (GROUP, GROUP, GROUP, N_HEADS, N_HEADS, GROUP,
             GROUP, GROUP, GROUP, GROUP,
             N_HEADS * GLA_DK, N_HEADS * GLA_DK, GROUP, GLA_RANK, GROUP,
             GROUP, GROUP, GROUP, GROUP)

LANE = 128
SUBLANE = 8
VMEM_LIMIT = 56 * 1024 * 1024

COL_ORDER = (0, 1, 2, 5, 6, 7, 8, 9, 10, 11, 12, 14, 15, 16, 17, 18)
A_QKV, A_GATE = 0, 768
B_Q, B_F, B_I, B_GATE = 1024, 1280, 1536, 1792
C_Q, C_K, C_V, C_GATE = 2048, 2176, 2304, 2560
D_Q, D_K, D_V, D_GATE = 2816, 3072, 3328, 3584
SMALL = 3840
SM_BETA, SM_ALPHA, SM_LR = 0, N_HEADS, 2 * N_HEADS
D_IN_AL = SMALL + LANE
SILU_COLS = (A_GATE, B_Q, B_GATE, C_GATE, D_GATE)
CONV_HIST = SUBLANE
ROW_TILE = 512
PROJ_STAGE_ROWS = 256
POST_STAGE_ROWS_WIDE = 128
POST_STAGE_ROWS_NARROW = 352
PROMPT_GROUP = 8


def _layer_norm(x, g, b):
    mu = jnp.mean(x, -1, keepdims=True)
    xc = x - mu
    var = jnp.mean(xc * xc, -1, keepdims=True)
    return xc * lax.rsqrt(var + LN_EPS) * g + b


def _row_specs(width, n_main, rows_tail):
    return (pl.BlockSpec((ROW_TILE, width), lambda i: (jnp.minimum(i, n_main - 1), 0)),
            pl.BlockSpec((rows_tail, width), lambda i: (0, 0)))


def _const_spec(shape, single_buffer=False):
    nd = len(shape)
    kw = dict(pipeline_mode=pl.Buffered(1)) if single_buffer else {}
    return pl.BlockSpec(shape, lambda *_: (0,) * nd, **kw)


def _on_region(n_main, main_fn, tail_fn):
    i = pl.program_id(0)
    pl.when(i < n_main)(main_fn)
    pl.when(i >= n_main)(tail_fn)


def _stage_rows(w_hbm, layer, n_rows, chunk, stage, sem, store):
    assert n_rows % chunk == 0 and chunk <= stage.shape[1]
    n = n_rows // chunk

    def copy(i):
        return pltpu.make_async_copy(w_hbm.at[layer, pl.ds(i * chunk, chunk), :],
                                     stage.at[i % 2, pl.ds(0, chunk), :], sem.at[i % 2])
    copy(0).start()
    for i in range(n):
        if i + 1 < n:
            copy(i + 1).start()
        copy(i).wait()
        store(i * chunk, stage.at[i % 2, pl.ds(0, chunk), :])


def _w_in_segments():
    offs = np.concatenate([[0], np.cumsum(IN_SPLITS)])
    runs, dst = [], 0
    for i in COL_ORDER + (3, 4, 13):
        if runs and runs[-1][1] == offs[i]:
            runs[-1][1] = int(offs[i + 1])
        else:
            runs.append([int(offs[i]), int(offs[i + 1]), dst])
        dst += IN_SPLITS[i]
    return [tuple(r) for r in runs]


def _proj_tile(with_ln, x_ref, g_ref, b_ref, w_ref, p_ref):
    x = x_ref[...]
    if with_ln:
        x = _layer_norm(x, g_ref[...], b_ref[...])
    h = x.astype(bf16)
    n = w_ref.shape[1]
    for n0 in range(0, n, GROUP):
        n1 = min(n0 + GROUP, n)
        p = jnp.dot(h, w_ref[:, n0:n1], preferred_element_type=f32)
        p_ref[:, n0:n1] = jax.nn.silu(p) if n0 in SILU_COLS else p


def _stage_w_in(wt_hbm, layer, w_scr, stage, sem):
    rows = PROJ_STAGE_ROWS
    big, small = [], []
    for s0, s1, d0 in _w_in_segments():
        if (s1 - s0) % rows == 0 and d0 % LANE == 0:
            big += [(s0 + j, d0 + j) for j in range(0, s1 - s0, rows)]
        else:
            small.append((s0, s1 - s0, d0 - SMALL))
    assert all(0 <= d and d + n <= LANE and d % SUBLANE == 0 for _, n, d in small) and len(small) <= 2

    def copy(j):
        return pltpu.make_async_copy(wt_hbm.at[layer, pl.ds(big[j][0], rows), :], stage.at[j % 2], sem.at[j % 2])
    copy(0).start()
    for j in range(len(big)):
        if j + 1 < len(big):
            copy(j + 1).start()
        copy(j).wait()
        w_scr[:, big[j][1]:big[j][1] + rows] = stage[j % 2].T.astype(bf16)
    used = max(d + n for _, n, d in small)
    stage[0, used:LANE, :] = jnp.zeros((LANE - used, D_MODEL), f32)
    copies = [pltpu.make_async_copy(wt_hbm.at[layer, pl.ds(s0, n), :], stage.at[0, pl.ds(d, n), :], sem.at[i])
              for i, (s0, n, d) in enumerate(small)]
    for cp in copies:
        cp.start()
    for cp in copies:
        cp.wait()
    w_scr[:, SMALL:SMALL + LANE] = stage[0, 0:LANE, :].T.astype(bf16)


def _proj_body(with_ln, layer, n_main, xm_ref, xt_ref, g_ref, b_ref, wt_hbm, pm_ref, pt_ref, w_scr, stage, sem):
    pl.when(pl.program_id(0) == 0)(lambda: _stage_w_in(wt_hbm, layer, w_scr, stage, sem))
    _on_region(n_main,
               lambda: _proj_tile(with_ln, xm_ref, g_ref, b_ref, w_scr, pm_ref),
               lambda: _proj_tile(with_ln, xt_ref, g_ref, b_ref, w_scr, pt_ref))


def _proj(xm, xt, g, b, w_in_t, layer, with_ln):
    n_main, rows_tail = xm.shape[0] // ROW_TILE, xt.shape[0]
    n = D_IN_AL
    out_shape = [jax.ShapeDtypeStruct((xm.shape[0], n), f32), jax.ShapeDtypeStruct((rows_tail, n), f32)]
    out_specs = list(_row_specs(n, n_main, rows_tail))
    return pl.pallas_call(
        functools.partial(_proj_body, with_ln, layer, n_main),
        out_shape=out_shape,
        grid=(n_main + 1,),
        in_specs=list(_row_specs(D_MODEL, n_main, rows_tail))
                 + [_const_spec((1, D_MODEL)), _const_spec((1, D_MODEL)), pl.BlockSpec(memory_space=pl.ANY)],
        out_specs=out_specs,
        scratch_shapes=[pltpu.VMEM((D_MODEL, n), bf16),
                        pltpu.VMEM((2, PROJ_STAGE_ROWS, D_MODEL), f32),
                        pltpu.SemaphoreType.DMA((2,))],
        compiler_params=pltpu.CompilerParams(dimension_semantics=("arbitrary",),
                                             vmem_limit_bytes=VMEM_LIMIT),
        name="in_proj",
    )(xm, xt, g, b, w_in_t)


def _post_tile(emb_ln, x_ref, o_ref, ge_ref, be_ref, wo_ref, g1_ref, b1_ref, wg_ref, wu_ref, wd_ref, g2_ref, b2_ref,
               y_ref):
    x = x_ref[...]
    if emb_ln:
        x = _layer_norm(x, ge_ref[...], be_ref[...])
    m = jnp.dot(o_ref[...], wo_ref[...], preferred_element_type=f32)
    y1 = _layer_norm(ALPHA * x + m, g1_ref[...], b1_ref[...])
    y1b = y1.astype(bf16)
    dff = wg_ref.shape[1]
    step = 4 * LANE
    acc = jnp.zeros(y1.shape, f32)
    for n0 in range(0, dff, step):
        n1 = min(n0 + step, dff)
        gate = jnp.dot(y1b, wg_ref[:, n0:n1], preferred_element_type=f32)
        up = jnp.dot(y1b, wu_ref[:, n0:n1], preferred_element_type=f32)
        act = (jax.nn.silu(gate) * up).astype(bf16)
        acc = acc + jnp.dot(act, wd_ref[n0:n1, :], preferred_element_type=f32)
    y_ref[...] = _layer_norm(ALPHA * y1 + acc, g2_ref[...], b2_ref[...])


def _post_body(layer, n_main, xm_ref, xt_ref, om_ref, ot_ref, ge_ref, be_ref, g1_ref, b1_ref, g2_ref, b2_ref,
               wo_hbm, wg_hbm, wu_hbm, wd_hbm, ym_ref, yt_ref,
               wo_scr, wg_scr, wu_scr, wd_scr, stage_wide, stage_narrow, sem):
    @pl.when(pl.program_id(0) == 0)
    def _():
        def into(dst):
            def store(r0, ref):
                dst[r0:r0 + ref.shape[0], :] = ref[...].astype(bf16)
            return store
        _stage_rows(wo_hbm, layer, D_MODEL, PROJ_STAGE_ROWS, stage_narrow, sem, into(wo_scr))
        _stage_rows(wg_hbm, layer, D_MODEL, POST_STAGE_ROWS_WIDE, stage_wide, sem, into(wg_scr))
        _stage_rows(wu_hbm, layer, D_MODEL, POST_STAGE_ROWS_WIDE, stage_wide, sem, into(wu_scr))
        _stage_rows(wd_hbm, layer, D_FF, POST_STAGE_ROWS_NARROW, stage_narrow, sem, into(wd_scr))

    w = (ge_ref, be_ref, wo_scr, g1_ref, b1_ref, wg_scr, wu_scr, wd_scr, g2_ref, b2_ref)
    _on_region(n_main, lambda: _post_tile(layer == 0, xm_ref, om_ref, *w, ym_ref),
               lambda: _post_tile(layer == 0, xt_ref, ot_ref, *w, yt_ref))


def _post(xm, xt, om, ot, ge, be, g1, b1, g2, b2, w_out, w_gate, w_up, w_down, layer):
    n_main = xm.shape[0] // ROW_TILE
    rows = lambda: list(_row_specs(D_MODEL, n_main, xt.shape[0]))
    vec = lambda: _const_spec((1, D_MODEL))
    hbm = lambda: pl.BlockSpec(memory_space=pl.ANY)
    return pl.pallas_call(
        functools.partial(_post_body, layer, n_main),
        out_shape=[jax.ShapeDtypeStruct(xm.shape, f32), jax.ShapeDtypeStruct(xt.shape, f32)],
        grid=(n_main + 1,),
        in_specs=rows() + rows() + [vec() for _ in range(6)] + [hbm() for _ in range(4)],
        out_specs=rows(),
        scratch_shapes=[pltpu.VMEM((D_MODEL, D_MODEL), bf16), pltpu.VMEM((D_MODEL, D_FF), bf16),
                        pltpu.VMEM((D_MODEL, D_FF), bf16), pltpu.VMEM((D_FF, D_MODEL), bf16),
                        pltpu.VMEM((2, POST_STAGE_ROWS_WIDE, D_FF), f32),
                        pltpu.VMEM((2, POST_STAGE_ROWS_NARROW, D_MODEL), f32),
                        pltpu.SemaphoreType.DMA((2,))],
        compiler_params=pltpu.CompilerParams(dimension_semantics=("arbitrary",),
                                             vmem_limit_bytes=VMEM_LIMIT),
        name="post_ffn",
    )(xm, xt, om, ot, ge, be, g1, b1, g2, b2, w_out, w_gate, w_up, w_down)


BASE_BLOCK = 8
assert (BASE_BLOCK // 2) * -math.log(GATE_CLAMP) < 80.0


def _levels(c, base=0):
    m, out = c // 2, []
    while m >= max(base, 1):
        out.append(m)
        m //= 2
    return tuple(out)


def _chunk_constants(c, base, nblk):
    t = np.arange(c)
    rows = [(t[None, :] <= t[:, None]).astype(np.float32), (t[None, :] > t[:, None]).astype(np.float32)]
    masks = []
    for m in _levels(c, base):
        blk = t // m
        upper = (blk % 2) == 1
        lo = np.where(upper, blk * m, t + 1)
        hi = np.where(upper, t, (blk + 1) * m - 1)
        rows.append(((t[None, :] >= lo[:, None]) & (t[None, :] <= hi[:, None])).astype(np.float32))
        same = (t[:, None] // (2 * m)) == (t[None, :] // (2 * m))
        masks.append((same & upper[:, None] & (~upper)[None, :]).astype(np.float32))
    if base:
        mid = (t // base) * base + base // 2 - 1
        after = (t[None, :] > mid[:, None]) & (t[None, :] <= t[:, None])
        before = (t[None, :] > t[:, None]) & (t[None, :] <= mid[:, None])
        rows.append(after.astype(np.float32) - before.astype(np.float32))
        masks.append(((t[:, None] // base == t[None, :] // base) & (t[None, :] <= t[:, None])).astype(np.float32))
    else:
        masks.append(np.eye(c, dtype=np.float32))
    return np.concatenate(rows, 0), np.tile(np.stack(masks, 0), (1, 1, nblk))


def _mm(a, b):
    return jnp.einsum('gmk,gkn->gmn', a.astype(bf16), b.astype(bf16), preferred_element_type=f32)


def _mm_nt(a, b):
    return jnp.einsum('gmk,gnk->gmn', a.astype(bf16), b.astype(bf16), preferred_element_type=f32)


def _mm_tn(a, b):
    return lax.dot_general(a.astype(bf16), b.astype(bf16), (((1,), (1,)), ((0,), (0,))),
                           preferred_element_type=f32)


def _iota(shape, dim):
    return lax.broadcasted_iota(jnp.int32, shape, dim)


def _bd(z, nblk, blk=None):
    z = z.astype(bf16)
    if blk is None:
        blk = _iota((1, 1, z.shape[-1]), 2) // (z.shape[-1] // nblk)
    return jnp.concatenate([jnp.where(blk == h, z, jnp.zeros_like(z)) for h in range(nblk)], axis=1)


def _hmm(x, z, nblk):
    return _mm(x, _bd(z, nblk))


def _hmm_nt(x, z, nblk):
    return _mm_nt(x, _bd(z, nblk))


def _hmm_split(x, z, nblk, blk=None):
    m = x.shape[1]
    x_hi, z_hi = x.astype(bf16), z.astype(bf16)
    x_lo = (x - x_hi.astype(f32)).astype(bf16)
    z_lo = (z - z_hi.astype(f32)).astype(bf16)
    both = _mm(jnp.concatenate([x_hi, x_lo], 1), _bd(z_hi, nblk, blk))
    return both[:, 0:m] + (both[:, m:2 * m] + _mm(x_hi, _bd(z_lo, nblk, blk)))


def _block_mask(rows, lanes, dr, dl):
    return (_iota((rows, lanes), 0) // dr) == (_iota((rows, lanes), 1) // dl)


def _dot01(m01, x):
    nb = x.shape[0]
    x2 = jnp.concatenate([x[i] for i in range(nb)], -1) if nb > 1 else x[0]
    hi = x2.astype(bf16)
    lo = (x2 - hi.astype(f32)).astype(bf16)
    m01 = m01.astype(bf16)
    d = lambda y: jnp.dot(m01, y, preferred_element_type=f32)
    out = d(hi) + d(lo)
    w = x.shape[-1]
    return jnp.stack([out[:, i * w:(i + 1) * w] for i in range(nb)], 0)


def _seg_sum(x, ones_bd, pieces=2):
    nb, c, w = x.shape
    x2 = x.reshape(nb * c, w)
    hi = x2.astype(bf16)
    out = jnp.dot(hi, ones_bd, preferred_element_type=f32)
    if pieces == 2:
        lo = (x2 - hi.astype(f32)).astype(bf16)
        out = out + jnp.dot(lo, ones_bd, preferred_element_type=f32)
    return out.reshape(nb, c, w)


def _last_row_as_col(b, c):
    return jnp.stack([b[i, c - SUBLANE:c, :].T[:, SUBLANE - 1:SUBLANE] for i in range(b.shape[0])], 0)


def _channel_decay_mixer(q, k, v, gl, s, nblk, sums_ref, masks_ref, bdmask, c, base):
    lv = _levels(c, base)
    nl = len(lv)
    sums = _dot01(sums_ref[...], gl)
    b = sums[:, 0:c]
    rev = sums[:, c:2 * c]
    row = _iota((1, c, 1), 1)
    if base:
        e0 = sums[:, (nl + 2) * c:(nl + 3) * c]
        att = jnp.where(masks_ref[nl] > 0.0, _hmm_nt(q * jnp.exp(e0), k * jnp.exp(-e0), nblk), 0.0)
    else:
        att = masks_ref[nl] * _hmm_nt(q, k, nblk)
    for li, m in enumerate(lv):
        upper = ((row // m) % 2) == 1
        w = jnp.where(upper, q, k) * jnp.exp(sums[:, (li + 2) * c:(li + 3) * c])
        att = att + masks_ref[li] * _hmm_nt(w, w, nblk)
    o = _mm(jnp.concatenate([q * jnp.exp(b), att], -1), jnp.concatenate([s.astype(bf16), _bd(v, nblk)], 1))
    s_new = jnp.exp(_last_row_as_col(b, c)) * s + jnp.where(bdmask, _mm_tn(k * jnp.exp(rev), v), 0.0)
    return o, s_new


MixCfg = collections.namedtuple("MixCfg", "c g r0 r1 layer n_chunks init p4d o_rows n_alias stacked")


def _pairs_to_bd(s):
    z = jnp.zeros(s.shape[:1] + (HEAD_DIM, HEAD_DIM), f32)
    pair = lambda p: jnp.concatenate([jnp.concatenate([s[:, 2 * p], z], -1),
                                      jnp.concatenate([z, s[:, 2 * p + 1]], -1)], 1)
    return jnp.concatenate([pair(0), pair(1)], 0)


def _quad_to_bd(s):
    z = lambda n: jnp.zeros(s.shape[:1] + (GLA_DK, n * HEAD_DIM), f32)
    rows = []
    for h in range(N_HEADS):
        parts = ([z(h)] if h else []) + [s[:, h]] + ([z(N_HEADS - 1 - h)] if h < N_HEADS - 1 else [])
        rows.append(jnp.concatenate(parts, -1))
    return jnp.concatenate(rows, 1)


def _mixer_body(cfg, *refs):
    c, g = cfg.c, cfg.g
    n_w, n_t, n_c = 7, 3, 4
    p_ref = refs[0]
    cw_ref, alog_ref, dtb_ref, lbl_ref, wg_ref, bg_ref, gains_ref = refs[1:1 + n_w]
    cos_ref, sina_ref, sinb_ref = refs[1 + n_w:1 + n_w + n_t]
    sumsb_ref, masksb_ref, sumsc_ref, masksc_ref = refs[1 + n_w + n_t:1 + n_w + n_t + n_c]
    pos = 1 + n_w + n_t + n_c
    if cfg.init:
        convi_ref, sai_ref, sbi_ref, sci_ref, sdi_ref = refs[pos:pos + 5]
        pos += 5
    pos += cfg.n_alias
    o_ref, convo_ref, sao_ref, sbo_ref, sco_ref, sdo_ref = refs[pos:pos + 6]
    if cfg.stacked:
        convo_ref, sao_ref, sbo_ref, sco_ref, sdo_ref = (r.at[0] for r in (convo_ref, sao_ref, sbo_ref, sco_ref, sdo_ref))
    cbuf, sa_scr, sb_scr, sc_scr, sd_scr = refs[pos + 6:]
    ci = pl.program_id(1)

    @pl.when(ci == 0)
    def _():
        cbuf[:, 0:CONV_HIST, :] = jnp.zeros((g, CONV_HIST, 3 * GROUP), f32)
        if not cfg.init:
            for r in (sa_scr, sb_scr, sc_scr, sd_scr):
                r[...] = jnp.zeros(r.shape, f32)
        else:
            take = lambda r: jnp.broadcast_to(r[...], (g,) + r.shape[1:])
            cbuf[:, CONV_HIST - (CONV_W - 1):CONV_HIST, :] = take(convi_ref)
            sa_scr[...] = _pairs_to_bd(take(sai_ref))
            sb_scr[...] = _pairs_to_bd(take(sbi_ref))
            sc_scr[...] = _quad_to_bd(take(sci_ref))
            sd_scr[...] = _pairs_to_bd(take(sdi_ref))

    def load(a, b):
        if cfg.p4d:
            return p_ref[0, :, :, a:b]
        return p_ref[:, a:b].reshape(g, c, b - a)

    r0, r1 = cfg.r0, cfg.r1
    n_valid = float(r1 - r0)
    row = _iota((1, c, 1), 1)
    vm = ((row >= r0) & (row < r1)).astype(f32)
    lane_v = _iota((1, 1, 2 * HEAD_DIM), 2)
    lane_s = _iota((1, 1, 2 * c), 2)
    s_idx = lane_s % c
    incl = s_idx <= row
    strict = s_idx < row
    bd2 = _block_mask(2 * HEAD_DIM, 2 * HEAD_DIM, HEAD_DIM, HEAD_DIM)
    bd4 = _block_mask(N_HEADS * GLA_DK, GROUP, GLA_DK, HEAD_DIM)
    ones2 = bd2.astype(bf16)
    pairs = lambda x: jnp.concatenate([x[..., :LANE], x[..., LANE:]], 0)
    unpairs = lambda x: jnp.concatenate([x[:g], x[g:]], -1)
    n_solve = max(1, math.ceil(math.log2(r1 - r0)))
    sm = load(SMALL, SMALL + LANE)
    valid = (lambda x: x) if (r0 == 0 and r1 == c) else (lambda x: x * vm)

    cbuf[:, CONV_HIST:CONV_HIST + c, :] = valid(load(A_QKV, A_QKV + 3 * GROUP))
    acc = cbuf[:, pl.ds(CONV_HIST - 3, c), :] * cw_ref[0:1, :]
    for j in range(1, CONV_W):
        acc = acc + cbuf[:, pl.ds(CONV_HIST - 3 + j, c), :] * cw_ref[j:j + 1, :]
    conv = jax.nn.silu(acc)
    beta = valid(jax.nn.sigmoid(sm))
    g_a = valid(-jnp.exp(alog_ref[...]) * jax.nn.softplus(sm + dtb_ref[...]))
    b_a = _dot01(sumsc_ref[0:c, :], g_a)

    def head_cols(x, lane0):
        col = lambda h: x[..., lane0 + h:lane0 + h + 1]
        return jnp.concatenate([col(0), col(2)], 0), jnp.concatenate([col(1), col(3)], 0)

    def spread(c0, c1, lane, width):
        return jnp.where(lane < width, c0, c1)

    bc0, bc1 = head_cols(b_a, SM_ALPHA)
    bt0, bt1 = head_cols(beta, SM_BETA)
    bc_v, bet_v = spread(bc0, bc1, lane_v, HEAD_DIM), spread(bt0, bt1, lane_v, HEAD_DIM)
    bc_s, bet_s = spread(bc0, bc1, lane_s, c), spread(bt0, bt1, lane_s, c)
    b_rows = [b_a[i].T for i in range(g)]
    b_row = lambda h: jnp.stack([t[SM_ALPHA + h:SM_ALPHA + h + 1, :] for t in b_rows], 0)
    br_s = jnp.concatenate([jnp.concatenate([b_row(0), b_row(1)], -1),
                            jnp.concatenate([b_row(2), b_row(3)], -1)], 0)
    dec = jnp.where(incl, jnp.exp(jnp.where(incl, bc_s - br_s, 0.0)), 0.0)
    q = pairs(conv[..., 0:GROUP])
    k = pairs(conv[..., GROUP:2 * GROUP])
    v = pairs(valid(conv[..., 2 * GROUP:3 * GROUP]))
    q = q * lax.rsqrt(_seg_sum(q * q, ones2) + NORM_EPS) * HEAD_DIM ** -0.5
    k = valid(k * lax.rsqrt(_seg_sum(k * k, ones2) + NORM_EPS))
    s_a = sa_scr[...]
    kq = jnp.concatenate([k, q], 1)
    kq_k = _hmm_nt(kq, k, 2)
    kq_s = _mm(kq, s_a)
    a_mat = jnp.where(strict, bet_s * dec * kq_k[:, 0:c], 0.0)
    gam = jnp.exp(bc_v)
    u = bet_v * (v - gam * kq_s[:, 0:c])
    p = -a_mat
    lane_pu = _iota((1, 1, 2 * c + 2 * HEAD_DIM), 2)
    head_pu = jnp.where(lane_pu < 2 * c, lane_pu // c, (lane_pu - 2 * c) // HEAD_DIM)
    for i in range(n_solve):
        if i + 1 < n_solve:
            pu = _hmm_split(p, jnp.concatenate([p, u], -1), 2, head_pu)
            p, u = pu[..., 0:2 * c], u + pu[..., 2 * c:]
        else:
            u = u + _hmm_split(p, u, 2)
    o_a = gam * kq_s[:, c:2 * c] + _hmm(dec * kq_k[:, c:2 * c], u, 2)
    bl = bc_v[:, c - 1:c, :]
    sa_scr[...] = jnp.exp(bl) * s_a + jnp.where(bd2, _mm_tn(k * jnp.exp(bl - bc_v), u), 0.0)
    o_a = unpairs(o_a * lax.rsqrt(_seg_sum(o_a * o_a, ones2, 1) * (1.0 / HEAD_DIM) + NORM_EPS))

    lbl = lbl_ref[...]
    e = jnp.exp(lbl - jnp.max(lbl, axis=0, keepdims=True))
    prob = e / jnp.sum(e, axis=0, keepdims=True)
    lb = (jnp.sum(prob[1:cfg.layer + 1], axis=0, keepdims=True) if cfg.layer > 0
          else jnp.zeros((1, GROUP), f32))
    kb = (1.0 - lb) * jax.nn.sigmoid(-load(B_F, B_F + GROUP))
    g_b = valid(jnp.log1p(-jnp.clip(kb, 0.0, 1.0 - GATE_CLAMP)))
    o_b, s_b_new = _channel_decay_mixer(
        pairs(load(B_Q, B_Q + GROUP)), pairs(valid(kb)), pairs(valid(load(B_I, B_I + GROUP))),
        pairs(g_b), sb_scr[...], 2, sumsb_ref, masksb_ref, bd2, c, BASE_BLOCK if c > BASE_BLOCK else 0)
    sb_scr[...] = s_b_new
    o_b = unpairs(o_b * lax.rsqrt(_seg_sum(o_b * o_b, ones2, 1) * (1.0 / HEAD_DIM) + NORM_EPS))

    logit = jnp.dot(sm.reshape(g * c, LANE).astype(bf16), wg_ref[...].astype(bf16),
                    preferred_element_type=f32).reshape(g, c, LANE) + bg_ref[...]
    g_c = valid(jax.nn.log_sigmoid(logit) * (1.0 / GLA_GATE_NORM))
    o_c, s_c_new = _channel_decay_mixer(
        load(C_Q, C_Q + N_HEADS * GLA_DK) * GLA_DK ** -0.5, valid(load(C_K, C_K + N_HEADS * GLA_DK)),
        valid(load(C_V, C_V + GROUP)), g_c, sc_scr[...], N_HEADS, sumsc_ref, masksc_ref, bd4, c, 0)
    sc_scr[...] = s_c_new
    ones4 = _block_mask(GROUP, GROUP, HEAD_DIM, HEAD_DIM).astype(bf16)
    o_c = o_c * lax.rsqrt(_seg_sum(o_c * o_c, ones4, 1) * (1.0 / HEAD_DIM) + NORM_EPS)

    def rope(x):
        x2 = x.reshape(g * c, GROUP)
        rot = lambda sh: pltpu.roll(x2, sh, 1).reshape(g, c, GROUP)
        return x * cos_ref[...] + rot(GROUP - HEAD_DIM // 2) * sina_ref[...] + rot(HEAD_DIM // 2) * sinb_ref[...]
    q = pairs(rope(load(D_Q, D_Q + GROUP)))
    k = pairs(valid(rope(load(D_K, D_K + GROUP)) * HEAD_DIM ** -0.5))
    v = pairs(valid(load(D_V, D_V + GROUP)))
    pair_idx = _iota((2, 1, 1), 0)
    log_gamma = lambda lane, width: jnp.log1p(-jnp.exp2(-5.0 - (2 * pair_idx + lane // width).astype(f32)))
    per_seq = lambda x: jnp.concatenate([jnp.broadcast_to(x[p:p + 1], (g,) + x.shape[1:]) for p in range(2)], 0)
    lg_v, lg_s = log_gamma(lane_v, HEAD_DIM), log_gamma(lane_s, c)
    cnt_c = jnp.clip(row + 1 - r0, 0, r1 - r0).astype(f32)
    cnt_s = jnp.clip(s_idx + 1 - r0, 0, r1 - r0).astype(f32)
    dec = per_seq(jnp.where(incl, jnp.exp(jnp.where(incl, (cnt_c - cnt_s) * lg_s, 0.0)), 0.0))
    s_d = sd_scr[...]
    o_d = _mm(jnp.concatenate([q * per_seq(jnp.exp(cnt_c * lg_v)), dec * _hmm_nt(q, k, 2)], -1),
              jnp.concatenate([s_d.astype(bf16), _bd(v, 2)], 1))
    sd_scr[...] = (per_seq(jnp.exp(n_valid * lg_v)) * s_d
                   + jnp.where(bd2, _mm_tn(k * per_seq(jnp.exp((n_valid - cnt_c) * lg_v)), v), 0.0))
    oc = o_d - _seg_sum(o_d, ones2) * (1.0 / HEAD_DIM)
    o_d = unpairs(oc * lax.rsqrt(_seg_sum(oc * oc, ones2, 1) * (1.0 / HEAD_DIM) + LN_EPS))

    for i, (o_m, gate_col) in enumerate(zip((o_a, o_b, o_c, o_d), (A_GATE, B_GATE, C_GATE, D_GATE))):
        val = (o_m * gains_ref[i:i + 1, :] * load(gate_col, gate_col + GROUP)).astype(o_ref.dtype)
        if cfg.p4d:
            o_ref[0, :, :, i * GROUP:(i + 1) * GROUP] = val
        else:
            o_ref[0:g * c, i * GROUP:(i + 1) * GROUP] = val.reshape(g * c, GROUP)
    if cfg.o_rows > g * c:
        o_ref[g * c:cfg.o_rows, :] = jnp.zeros((cfg.o_rows - g * c, D_MODEL), o_ref.dtype)
    cbuf[:, 0:CONV_HIST, :] = cbuf[:, c:c + CONV_HIST, :]

    @pl.when(ci == cfg.n_chunks - 1)
    def _():
        convo_ref[...] = cbuf[:, pl.ds(CONV_HIST + r1 - (CONV_W - 1), CONV_W - 1), :]
        for scr, out in ((sa_scr, sao_ref), (sb_scr, sbo_ref), (sd_scr, sdo_ref)):
            s = scr[...]
            for p in range(2):
                out[:, 2 * p] = s[p * g:(p + 1) * g, 0:HEAD_DIM, 0:HEAD_DIM]
                out[:, 2 * p + 1] = s[p * g:(p + 1) * g, HEAD_DIM:2 * HEAD_DIM, HEAD_DIM:2 * HEAD_DIM]
        s = sc_scr[...]
        for h in range(N_HEADS):
            sco_ref[:, h] = s[:, h * GLA_DK:(h + 1) * GLA_DK, h * HEAD_DIM:(h + 1) * HEAD_DIM]


def _state_shapes(n, stacked):
    lead = (DEPTH, n) if stacked else (n,)
    return [jax.ShapeDtypeStruct(lead + (CONV_W - 1, 3 * GROUP), f32),
            jax.ShapeDtypeStruct(lead + (N_HEADS, HEAD_DIM, HEAD_DIM), f32),
            jax.ShapeDtypeStruct(lead + (N_HEADS, HEAD_DIM, HEAD_DIM), f32),
            jax.ShapeDtypeStruct(lead + (N_HEADS, GLA_DK, HEAD_DIM), f32),
            jax.ShapeDtypeStruct(lead + (N_HEADS, HEAD_DIM, HEAD_DIM), f32)]


def _state_out_specs(g, layer, stacked):
    if stacked:
        st = lambda k: pl.BlockSpec((1, g, N_HEADS, k, HEAD_DIM), lambda b, i: (layer, b, 0, 0, 0))
        conv = pl.BlockSpec((1, g, CONV_W - 1, 3 * GROUP), lambda b, i: (layer, b, 0, 0))
    else:
        st = lambda k: pl.BlockSpec((g, N_HEADS, k, HEAD_DIM), lambda b, i: (b, 0, 0, 0))
        conv = pl.BlockSpec((g, CONV_W - 1, 3 * GROUP), lambda b, i: (b, 0, 0))
    return [conv, st(HEAD_DIM), st(HEAD_DIM), st(GLA_DK), st(HEAD_DIM)]


def _mixer_call(cfg, name, n_groups, p, p_spec, wts, tables, table_spec, consts, init, init_specs,
                o_shape, o_spec, n_state_rows, o_prev=None, states_prev=None):
    c, g = cfg.c, cfg.g
    wspecs =[_const_spec((CONV_W, 3 * GROUP)), _const_spec((1, LANE)), _const_spec((1, LANE)),
              _const_spec((DEPTH, GROUP)), _const_spec((LANE, LANE)), _const_spec((1, LANE)),
              _const_spec((4, GROUP))]
    cspecs = [_const_spec(a.shape) for a in consts]
    in_specs = [p_spec] + wspecs + [table_spec] * 3 + cspecs + list(init_specs)
    args = [p, *wts, *tables, *consts, *init]
    aliases = {}
    if o_prev is not None:
        aliases[len(args)] = 0
        args.append(o_prev)
    for j, s_prev in enumerate(states_prev or ()):
        aliases[len(args)] = 1 + j
        args.append(s_prev)
    in_specs += [pl.BlockSpec(memory_space=pl.ANY)] * len(aliases)
    assert len(aliases) == cfg.n_alias
    return pl.pallas_call(
        functools.partial(_mixer_body, cfg),
        out_shape=[o_shape] + _state_shapes(n_state_rows, cfg.stacked),
        grid=(n_groups, cfg.n_chunks),
        in_specs=in_specs,
        out_specs=[o_spec] + _state_out_specs(g, cfg.layer, cfg.stacked),
        scratch_shapes=[pltpu.VMEM((g, c + CONV_HIST, 3 * GROUP), f32),
                        pltpu.VMEM((2 * g, 2 * HEAD_DIM, 2 * HEAD_DIM), f32),
                        pltpu.VMEM((2 * g, 2 * HEAD_DIM, 2 * HEAD_DIM), f32),
                        pltpu.VMEM((g, N_HEADS * GLA_DK, GROUP), f32),
                        pltpu.VMEM((2 * g, 2 * HEAD_DIM, 2 * HEAD_DIM), f32)],
        input_output_aliases=aliases,
        compiler_params=pltpu.CompilerParams(dimension_semantics=("arbitrary", "arbitrary"),
                                             vmem_limit_bytes=VMEM_LIMIT),
        name=name,
    )(*args)


def _tok(ref, t, nb):
    return ref[t * nb:(t + 1) * nb, :].T


def _bcast_row(ref, t, r):
    return jnp.broadcast_to(ref[t, pl.ds(r, 1), :], (HEAD_DIM, ref.shape[-1]))


def _bl_diag_loop(n_tok, kdim, j, row0, s_in, s_out, d_scr, k_scr, q_scr, v_scr, d_const=None):
    nb = s_in.shape[-1]

    def body(k, os):
        s_k = s_in[0, j, k]
        r = row0 + k
        outs = []
        for t in range(n_tok):
            d = d_const if d_const is not None else _bcast_row(d_scr, t, r)
            s_k = d * s_k + _bcast_row(k_scr, t, r) * v_scr[t, j * HEAD_DIM:(j + 1) * HEAD_DIM, :]
            outs.append(os[t] + _bcast_row(q_scr, t, r) * s_k)
        s_out[0, j, k] = s_k
        return tuple(outs)
    return lax.fori_loop(0, kdim, body, (jnp.zeros((HEAD_DIM, nb), f32),) * n_tok)


def _bl_store(o_ref, os, gate_ref, gain_ref, n_tok, nb, group_norm):
    for t in range(n_tok):
        gate = _tok(gate_ref, t, nb)
        halves = []
        for j in range(2):
            o = os[j][t]
            if group_norm:
                o = o - jnp.mean(o, axis=0, keepdims=True)
                o = o * lax.rsqrt(jnp.mean(o * o, axis=0, keepdims=True) + LN_EPS)
            else:
                o = o * lax.rsqrt(jnp.mean(o * o, axis=0, keepdims=True) + NORM_EPS)
            hs = slice(j * HEAD_DIM, (j + 1) * HEAD_DIM)
            halves.append(o * gain_ref[hs, :] * gate[hs, :])
        o_ref[t * nb:(t + 1) * nb, :] = jnp.concatenate(halves, 0).T.astype(o_ref.dtype)


def _bl_hgrn_body(layer, n_tok, q_ref, f_ref, i_ref, gate_ref, lbl_ref, gain_ref, s_in, o_prev, s_prev,
                  o_ref, s_out, d_scr, k_scr, q_scr, v_scr):
    del o_prev, s_prev
    nb = s_in.shape[-1]
    lbl = lbl_ref[...]
    e = jnp.exp(lbl - jnp.max(lbl, axis=1, keepdims=True))
    prob = e / jnp.sum(e, axis=1, keepdims=True)
    lb = (jnp.sum(prob[:, 1:layer + 1], axis=1, keepdims=True) if layer > 0 else jnp.zeros((LANE, 1), f32))
    for t in range(n_tok):
        kk = (1.0 - lb) * jax.nn.sigmoid(-_tok(f_ref, t, nb))
        d_scr[t] = 1.0 - jnp.clip(kk, 0.0, 1.0 - GATE_CLAMP)
        k_scr[t] = kk
        q_scr[t] = _tok(q_ref, t, nb)
        v_scr[t] = _tok(i_ref, t, nb)
    os = [_bl_diag_loop(n_tok, HEAD_DIM, j, j * HEAD_DIM, s_in, s_out, d_scr, k_scr, q_scr, v_scr) for j in range(2)]
    _bl_store(o_ref, os, gate_ref, gain_ref, n_tok, nb, False)


def _bl_gla_body(n_tok, q_ref, k_ref, v_ref, gate_ref, small_ref, wg_ref, bg_ref, gain_ref, s_in, o_prev, s_prev,
                 o_ref, s_out, d_scr, k_scr, q_scr, v_scr):
    del o_prev, s_prev
    nb = s_in.shape[-1]
    for t in range(n_tok):
        logit = jnp.dot(wg_ref[...].astype(bf16), _tok(small_ref, t, nb).astype(bf16),
                        preferred_element_type=f32) + bg_ref[...]
        d_scr[t] = jnp.exp(jax.nn.log_sigmoid(logit) * (1.0 / GLA_GATE_NORM))
        k_scr[t] = _tok(k_ref, t, nb)
        q_scr[t] = _tok(q_ref, t, nb) * GLA_DK ** -0.5
        v_scr[t] = _tok(v_ref, t, nb)
    p = pl.program_id(0)
    os = [_bl_diag_loop(n_tok, GLA_DK, j, (2 * p + j) * GLA_DK, s_in, s_out, d_scr, k_scr, q_scr, v_scr)
          for j in range(2)]
    _bl_store(o_ref, os, gate_ref, gain_ref, n_tok, nb, False)


def _bl_ret_body(n_tok, q_ref, k_ref, v_ref, gate_ref, cos_ref, sin_ref, gain_ref, s_in, o_prev, s_prev,
                 o_ref, s_out, k_scr, q_scr, v_scr):
    del o_prev, s_prev
    nb = s_in.shape[-1]
    half = HEAD_DIM // 2

    def rope(x, t):
        swapped = jnp.concatenate([x[half:2 * half], x[0:half], x[3 * half:4 * half], x[2 * half:3 * half]], 0)
        return x * cos_ref[t] + swapped * sin_ref[t]
    for t in range(n_tok):
        q_scr[t] = rope(_tok(q_ref, t, nb), t)
        k_scr[t] = rope(_tok(k_ref, t, nb), t) * HEAD_DIM ** -0.5
        v_scr[t] = _tok(v_ref, t, nb)
    p = pl.program_id(0)
    os = []
    for j in range(2):
        head = jnp.zeros((1, nb), f32) + (2 * p + j).astype(f32)
        gamma = jnp.broadcast_to(1.0 - jnp.exp2(-5.0 - head), (HEAD_DIM, nb))
        os.append(_bl_diag_loop(n_tok, HEAD_DIM, j, j * HEAD_DIM, s_in, s_out, None, k_scr, q_scr, v_scr, gamma))
    _bl_store(o_ref, os, gate_ref, gain_ref, n_tok, nb, True)


def _bl_delta_body(n_tok, q_ref, k_ref, v_ref, gate_ref, small_ref, hq_ref, hk_ref, hv_ref, cwq_ref, cwk_ref, cwv_ref,
                   alog_ref, dtb_ref, gain_ref, s_in, o_prev, s_prev, cq_prev, ck_prev, cv_prev,
                   o_ref, s_out, cq_out, ck_out, cv_out, k_scr, q_scr, kh_scr, u_scr, bg_scr):
    del o_prev, s_prev, cq_prev, ck_prev, cv_prev
    nb = s_in.shape[-1]
    p = pl.program_id(0)

    def conv(x_ref, h_ref, cw_ref):
        seq = [h_ref[0, r].T for r in range(CONV_W - 1)] + [_tok(x_ref, t, nb) for t in range(n_tok)]
        outs = []
        for t in range(n_tok):
            acc = cw_ref[:, 0:1] * seq[t]
            for jj in range(1, CONV_W):
                acc = acc + cw_ref[:, jj:jj + 1] * seq[t + jj]
            outs.append(jax.nn.silu(acc))
        return outs

    def l2norm(x):
        parts = []
        for j in range(2):
            xj = x[j * HEAD_DIM:(j + 1) * HEAD_DIM]
            parts.append(xj * lax.rsqrt(jnp.sum(xj * xj, axis=0, keepdims=True) + NORM_EPS))
        return jnp.concatenate(parts, 0)

    qs, ks, vs = conv(q_ref, hq_ref, cwq_ref), conv(k_ref, hk_ref, cwk_ref), conv(v_ref, hv_ref, cwv_ref)
    for t in range(n_tok):
        q_scr[t] = l2norm(qs[t]) * HEAD_DIM ** -0.5
        k_scr[t] = l2norm(ks[t])
        sm = _tok(small_ref, t, nb)
        both = jnp.where(_iota((LANE, 1), 0) < SM_ALPHA, jax.nn.sigmoid(sm),
                         -jnp.exp(alog_ref[...]) * jax.nn.softplus(sm + dtb_ref[...]))
        bg_scr[t] = both[0:SUBLANE]
    for r in range(CONV_W - 1):
        rows = slice((n_tok - (CONV_W - 1) + r) * nb, (n_tok - (CONV_W - 1) + r + 1) * nb)
        cq_out[0, r] = q_ref[rows, :]
        ck_out[0, r] = k_ref[rows, :]
        cv_out[0, r] = v_ref[rows, :]

    os = []
    for j in range(2):
        hs = slice(j * HEAD_DIM, (j + 1) * HEAD_DIM)
        h = 2 * p + j
        beta = [bg_scr[t, pl.ds(SM_BETA + h, 1), :] for t in range(n_tok)]
        b, acc = [], jnp.zeros((1, nb), f32)
        for t in range(n_tok):
            acc = acc + bg_scr[t, pl.ds(SM_ALPHA + h, 1), :]
            b.append(acc)
        kt = [k_scr[t, hs, :] for t in range(n_tok)]
        qt = [q_scr[t, hs, :] for t in range(n_tok)]
        dot = lambda x, y: jnp.sum(x * y, axis=0, keepdims=True)

        def sweep(scr):
            def body(k, accs):
                s_k = s_in[0, j, k]
                return tuple(a + _bcast_row(scr, t, j * HEAD_DIM + k) * s_k for t, a in enumerate(accs))
            return lax.fori_loop(0, HEAD_DIM, body, (jnp.zeros((HEAD_DIM, nb), f32),) * n_tok)
        k_s, q_s = sweep(k_scr), sweep(q_scr)
        us = []
        for t in range(n_tok):
            u = beta[t] * (vs[t][hs] - jnp.exp(b[t]) * k_s[t])
            for s in range(t):
                u = u - (beta[t] * jnp.exp(b[t] - b[s]) * dot(kt[t], kt[s])) * us[s]
            us.append(u)
        o_h = []
        for t in range(n_tok):
            o = jnp.exp(b[t]) * q_s[t]
            for s in range(t + 1):
                o = o + (jnp.exp(b[t] - b[s]) * dot(qt[t], kt[s])) * us[s]
            o_h.append(o)
        os.append(o_h)
        for t in range(n_tok):
            kh_scr[t, hs, :] = kt[t] * jnp.exp(b[-1] - b[t])
            u_scr[t] = us[t]
        decay = jnp.broadcast_to(jnp.exp(b[-1]), (HEAD_DIM, nb))

        def update(k, carry):
            s_k = decay * s_in[0, j, k]
            for t in range(n_tok):
                s_k = s_k + _bcast_row(kh_scr, t, j * HEAD_DIM + k) * u_scr[t]
            s_out[0, j, k] = s_k
            return carry
        lax.fori_loop(0, HEAD_DIM, update, 0)
    _bl_store(o_ref, os, gate_ref, gain_ref, n_tok, nb, False)


def _bl_call(body, name, mixer, n_tok, nb, blk_row, pt, cols, extra, extra_specs, state_t, kdim, layer,
             o_prev, s_prev, scratch, conv_prev=None):
    def tok_spec(c0, per_pair):
        return pl.BlockSpec((n_tok * nb, LANE),
                            (lambda p: (blk_row, c0 // LANE + p)) if per_pair else (lambda p: (blk_row, c0 // LANE)))
    st_spec = pl.BlockSpec((1, 2, kdim, HEAD_DIM, nb), lambda p: (layer, p, 0, 0, 0))
    args = [pt] * len(cols) + list(extra) + [state_t]
    in_specs = [tok_spec(c0, pp) for c0, pp in cols] + list(extra_specs) + [st_spec]
    aliases = {}

    def donate(arr, out_idx):
        if arr is not None:
            aliases[len(args)] = out_idx
        args.append(arr if arr is not None else jnp.zeros((1,), f32))
        in_specs.append(pl.BlockSpec(memory_space=pl.ANY))
    donate(o_prev, 0)
    donate(s_prev, 1)
    out_shape = [jax.ShapeDtypeStruct(o_prev.shape, o_prev.dtype),
                 jax.ShapeDtypeStruct((DEPTH,) + state_t.shape[1:], f32)]
    out_specs = [pl.BlockSpec((n_tok * nb, LANE), lambda p: (blk_row, 2 * mixer + p)), st_spec]
    if conv_prev is not None:
        for i in range(3):
            donate(conv_prev[i] if conv_prev else None, 2 + i)
            out_shape.append(jax.ShapeDtypeStruct((DEPTH, CONV_W - 1, nb, GROUP), f32))
            out_specs.append(pl.BlockSpec((1, CONV_W - 1, nb, LANE), lambda p: (layer, 0, 0, p)))
    return pl.pallas_call(
        body, out_shape=out_shape, grid=(2,), in_specs=in_specs, out_specs=out_specs,
        scratch_shapes=scratch, input_output_aliases=aliases,
        compiler_params=pltpu.CompilerParams(dimension_semantics=("arbitrary",), vmem_limit_bytes=VMEM_LIMIT),
        name=name,
    )(*args)


def _rope_tables(pos):
    half = HEAD_DIM // 2
    inv = 1.0 / (RET_THETA_BASE ** jnp.linspace(0.0, 1.0, half, dtype=f32))
    ang = pos[:, None] * inv[None, :]
    cos, sin = jnp.cos(ang), jnp.sin(ang)
    zero = jnp.zeros_like(sin)
    tile = lambda a, b: jnp.tile(jnp.concatenate([a, b], -1), (1, N_HEADS))
    return tile(cos, cos), tile(-sin, zero), tile(zero, sin)


def kernel(x_prompt, x_sample, state_delta_conv, state_delta, state_hgrn, state_gla, state_ret, meta_tokens, emb_ln_g, emb_ln_b, w_in, conv_w, delta_a_log, delta_dt_bias, delta_norm_g, hgrn_lb_logits, hgrn_norm_g, gla_w_gate, gla_b_gate, gla_norm_g, ret_norm_g, w_out, ln1_g, ln1_b, w_ffn_gate, w_ffn_up, w_ffn_down, ln2_g, ln2_b):
    bsz, seq, _ = x_prompt.shape
    n_seq, n_tok, _ = x_sample.shape
    pg = PROMPT_GROUP if bsz % PROMPT_GROUP == 0 else 1
    rows_main = bsz * seq
    rows_s = n_tok * n_seq
    assert seq % CHUNK == 0 and rows_main % ROW_TILE == 0 and N_META <= CHUNK
    assert n_tok >= CONV_W - 1 and rows_s % CHUNK == 0 and (n_seq % LANE == 0 or n_seq < LANE)
    lead_pad = CHUNK - N_META
    rows_tail = rows_s + CHUNK
    s_blk = 0
    meta_blk = rows_s // CHUNK

    xs = jnp.swapaxes(x_sample, 0, 1).reshape(rows_s, D_MODEL)
    xt = jnp.concatenate([xs, jnp.zeros((lead_pad, D_MODEL), f32), meta_tokens.astype(f32)], 0)
    xm = x_prompt.reshape(rows_main, D_MODEL)

    tab_m = _rope_tables(jnp.arange(CHUNK, dtype=f32) - lead_pad)
    tab_p = _rope_tables(N_META + jnp.arange(seq, dtype=f32))
    consts_c = tuple(jnp.asarray(a, f32) for a in (_chunk_constants(CHUNK, BASE_BLOCK, 2)
                                                  + _chunk_constants(CHUNK, 0, N_HEADS)))
    row_vec = lambda v: v.reshape(1, -1).astype(f32)
    col_vec = lambda v: v.reshape(-1, 1).astype(f32)
    small_row = lambda v: jnp.zeros((1, LANE), f32).at[0, SM_ALPHA:SM_ALPHA + N_HEADS].set(v)
    n_chunks = seq // CHUNK

    seq_last = lambda s: jnp.transpose(s, (0, 2, 3, 4, 1))
    sd_t, sh_t, sg_t, sr_t = (seq_last(s) for s in (state_delta, state_hgrn, state_gla, state_ret))
    conv_t = jnp.transpose(state_delta_conv, (0, 2, 1, 3))
    half = HEAD_DIM // 2
    ang = ((PAST_LEN + jnp.arange(n_tok, dtype=f32))[:, None]
           * (1.0 / (RET_THETA_BASE ** jnp.linspace(0.0, 1.0, half, dtype=f32)))[None, :])
    cos_s = jnp.tile(jnp.cos(ang), (1, 4))[..., None]
    sin_s = jnp.tile(jnp.concatenate([-jnp.sin(ang), jnp.sin(ang)], -1), (1, 2))[..., None]
    vm = lambda shape: pltpu.VMEM(shape, f32)
    tok_scr = lambda: vm((n_tok, LANE, n_seq))

    w_in_t = jnp.swapaxes(w_in, 1, 2)
    st_p = None
    sa_s = sb_s = sc_s = sd_s = conv_s = None
    for l in range(DEPTH):
        pm, pt = _proj(xm, xt, row_vec(emb_ln_g), row_vec(emb_ln_b), w_in_t, l, with_ln=(l == 0))
        wg = jnp.zeros((LANE, N_HEADS * GLA_DK), f32).at[SM_LR:SM_LR + GLA_RANK].set(gla_w_gate[l])
        gains = jnp.stack([delta_norm_g[l], hgrn_norm_g[l], gla_norm_g[l], ret_norm_g[l]], 0)
        wts = (conv_w[l], small_row(delta_a_log[l]), small_row(delta_dt_bias[l]), hgrn_lb_logits,
               wg, row_vec(gla_b_gate[l]), gains)

        cfg = MixCfg(CHUNK, 1, lead_pad, CHUNK, l, 1, None, False, CHUNK, 0, False)
        ot, *st_meta = _mixer_call(
            cfg, "mixers_meta", 1, pt, pl.BlockSpec((CHUNK, D_IN_AL), lambda b, i: (meta_blk, 0)),
            wts, tab_m, pl.BlockSpec((CHUNK, GROUP), lambda b, i: (0, 0)), consts_c, (), (),
            jax.ShapeDtypeStruct((rows_tail, D_MODEL), bf16),
            pl.BlockSpec((CHUNK, D_MODEL), lambda b, i: (meta_blk, 0)), 1)

        n_prev = 0 if st_p is None else len(st_p)
        cfg = MixCfg(CHUNK, pg, 0, CHUNK, l, n_chunks, "shared", True, pg * CHUNK, n_prev, True)
        st1 = lambda k: pl.BlockSpec((1, N_HEADS, k, HEAD_DIM), lambda b, i: (0, 0, 0, 0))
        om, *st_p = _mixer_call(
            cfg, "mixers_prompt", bsz // pg, pm.reshape(bsz // pg, pg, seq, D_IN_AL),
            pl.BlockSpec((1, pg, CHUNK, D_IN_AL), lambda b, i: (b, 0, i, 0)),
            wts, tab_p, pl.BlockSpec((CHUNK, GROUP), lambda b, i: (i, 0)), consts_c, st_meta,
            [pl.BlockSpec((1, CONV_W - 1, 3 * GROUP), lambda b, i: (0, 0, 0)),
             st1(HEAD_DIM), st1(HEAD_DIM), st1(GLA_DK), st1(HEAD_DIM)],
            jax.ShapeDtypeStruct((bsz // pg, pg, seq, D_MODEL), bf16),
            pl.BlockSpec((1, pg, CHUNK, D_MODEL), lambda b, i: (b, 0, i, 0)), bsz, states_prev=st_p)
        om = om.reshape(rows_main, D_MODEL)

        pair_col = lambda v: (col_vec(v), pl.BlockSpec((LANE, 1), lambda p: (p, 0)))
        full = lambda a: (a, pl.BlockSpec(a.shape, lambda p: (0,) * a.ndim))
        split = lambda pairs: ([a for a, _ in pairs], [s for _, s in pairs])
        hist = lambda part: (conv_t, pl.BlockSpec((1, CONV_W - 1, n_seq, LANE), lambda p: (l, 0, 0, 2 * part + p)))
        cw_t = conv_w[l].T
        cw = lambda part: (cw_t, pl.BlockSpec((LANE, CONV_W), lambda p: (2 * part + p, 0)))
        small_col = lambda v: jnp.zeros((LANE, 1), f32).at[SM_ALPHA:SM_ALPHA + N_HEADS, 0].set(v)
        extra, especs = split([hist(0), hist(1), hist(2), cw(0), cw(1), cw(2), full(small_col(delta_a_log[l])),
                               full(small_col(delta_dt_bias[l])), pair_col(delta_norm_g[l])])
        ot, sa_s, *conv_s = _bl_call(
            functools.partial(_bl_delta_body, n_tok), "sample_delta", 0, n_tok, n_seq, s_blk, pt,
            [(A_QKV, True), (A_QKV + GROUP, True), (A_QKV + 2 * GROUP, True), (A_GATE, True), (SMALL, False)],
            extra, especs, sd_t, HEAD_DIM, l, ot, sa_s,
            [tok_scr(), tok_scr(), tok_scr(), vm((n_tok, HEAD_DIM, n_seq)), vm((n_tok, SUBLANE, n_seq))],
            conv_prev=conv_s or [])
        extra, especs = split([(hgrn_lb_logits.T, pl.BlockSpec((LANE, DEPTH), lambda p: (p, 0))),
                               pair_col(hgrn_norm_g[l])])
        ot, sb_s = _bl_call(
            functools.partial(_bl_hgrn_body, l, n_tok), "sample_hgrn", 1, n_tok, n_seq, s_blk, pt,
            [(B_Q, True), (B_F, True), (B_I, True), (B_GATE, True)], extra, especs, sh_t, HEAD_DIM, l, ot, sb_s,
            [tok_scr() for _ in range(4)])
        wg_t = jnp.zeros((N_HEADS * GLA_DK, LANE), f32).at[:, SM_LR:SM_LR + GLA_RANK].set(gla_w_gate[l].T)
        extra, especs = split([full(wg_t), full(col_vec(gla_b_gate[l])), pair_col(gla_norm_g[l])])
        ot, sc_s = _bl_call(
            functools.partial(_bl_gla_body, n_tok), "sample_gla", 2, n_tok, n_seq, s_blk, pt,
            [(C_Q, False), (C_K, False), (C_V, True), (C_GATE, True), (SMALL, False)], extra, especs,
            sg_t, GLA_DK, l, ot, sc_s, [tok_scr() for _ in range(4)])
        extra, especs = split([full(cos_s), full(sin_s), pair_col(ret_norm_g[l])])
        ot, sd_s = _bl_call(
            functools.partial(_bl_ret_body, n_tok), "sample_ret", 3, n_tok, n_seq, s_blk, pt,
            [(D_Q, True), (D_K, True), (D_V, True), (D_GATE, True)], extra, especs, sr_t, HEAD_DIM, l, ot, sd_s,
            [tok_scr() for _ in range(3)])

        xm, xt = _post(xm, xt, om, ot, row_vec(emb_ln_g), row_vec(emb_ln_b),
                       row_vec(ln1_g[l]), row_vec(ln1_b[l]), row_vec(ln2_g[l]), row_vec(ln2_b[l]),
                       w_out, w_ffn_gate, w_ffn_up, w_ffn_down, l)

    y_prompt = xm.reshape(bsz, seq, D_MODEL)
    y_sample = jnp.swapaxes(xt[0:rows_s].reshape(n_tok, n_seq, D_MODEL), 0, 1)
    seq_first = lambda s: jnp.transpose(s, (0, 4, 1, 2, 3))
    st_s = (jnp.transpose(jnp.concatenate(conv_s, -1), (0, 2, 1, 3)),
            seq_first(sa_s), seq_first(sb_s), seq_first(sc_s), seq_first(sd_s))
    return (y_prompt, y_sample) + tuple(v for pair in zip(st_p, st_s) for v in pair)
```

```python
import collections
import functools
import math

import numpy as np
import jax
import jax.numpy as jnp
from jax import lax
from jax.experimental import pallas as pl
from jax.experimental.pallas import tpu as pltpu

f32 = jnp.float32
bf16 = jnp.bfloat16

D_MODEL = 1024
N_META = 16
CHUNK = 64
GROUP = 256
N_HEADS = 4
HEAD_DIM = 64
GLA_DK = 32
GLA_RANK = 16
GLA_GATE_NORM = 16.0
CONV_W = 4
D_FF = 2816
DEPTH = 2
ALPHA = (2 * DEPTH) ** 0.25
PAST_LEN = 16384
RET_THETA_BASE = 10000.0
LN_EPS = 1e-5
NORM_EPS = 1e-6
GATE_CLAMP = 1e-6
IN_SPLITS = (GROUP, GROUP, GROUP, N_HEADS, N_HEADS, GROUP,
             GROUP, GROUP, GROUP, GROUP,
             N_HEADS * GLA_DK, N_HEADS * GLA_DK, GROUP, GLA_RANK, GROUP,
             GROUP, GROUP, GROUP, GROUP)

LANE = 128
SUBLANE = 8
VMEM_LIMIT = 56 * 1024 * 1024

COL_ORDER = (0, 1, 2, 5, 6, 7, 8, 9, 10, 11, 12, 14, 15, 16, 17, 18)
A_QKV, A_GATE = 0, 768
B_Q, B_F, B_I, B_GATE = 1024, 1280, 1536, 1792
C_Q, C_K, C_V, C_GATE = 2048, 2176, 2304, 2560
D_Q, D_K, D_V, D_GATE = 2816, 3072, 3328, 3584
SMALL = 3840
SM_BETA, SM_ALPHA, SM_LR = 0, N_HEADS, 2 * N_HEADS
D_IN_AL = SMALL + LANE
SILU_COLS = (A_GATE, B_Q, B_GATE, C_GATE, D_GATE)
CONV_HIST = SUBLANE
ROW_TILE = 512
PROJ_STAGE_ROWS = 256
PROMPT_GROUP = 8


def _layer_norm(x, g, b):
    mu = jnp.mean(x, -1, keepdims=True)
    xc = x - mu
    var = jnp.mean(xc * xc, -1, keepdims=True)
    return xc * lax.rsqrt(var + LN_EPS) * g + b


def _row_specs(width, n_main, rows_tail):
    return (pl.BlockSpec((ROW_TILE, width), lambda i: (jnp.minimum(i, n_main - 1), 0)),
            pl.BlockSpec((rows_tail, width), lambda i: (0, 0)))


def _const_spec(shape, single_buffer=False):
    nd = len(shape)
    kw = dict(pipeline_mode=pl.Buffered(1)) if single_buffer else {}
    return pl.BlockSpec(shape, lambda *_: (0,) * nd, **kw)


def _on_region(n_main, main_fn, tail_fn):
    i = pl.program_id(0)
    pl.when(i < n_main)(main_fn)
    pl.when(i >= n_main)(tail_fn)


def _w_in_segments():
    offs = np.concatenate([[0], np.cumsum(IN_SPLITS)])
    runs, dst = [], 0
    for i in COL_ORDER + (3, 4, 13):
        if runs and runs[-1][1] == offs[i]:
            runs[-1][1] = int(offs[i + 1])
        else:
            runs.append([int(offs[i]), int(offs[i + 1]), dst])
        dst += IN_SPLITS[i]
    return [tuple(r) for r in runs]


def _proj_tile(with_ln, x_ref, g_ref, b_ref, w_ref, p_ref):
    x = x_ref[...]
    if with_ln:
        x = _layer_norm(x, g_ref[...], b_ref[...])
    h = x.astype(bf16)
    n = w_ref.shape[1]
    for n0 in range(0, n, GROUP):
        n1 = min(n0 + GROUP, n)
        p = jnp.dot(h, w_ref[:, n0:n1], preferred_element_type=f32)
        p_ref[:, n0:n1] = jax.nn.silu(p) if n0 in SILU_COLS else p


def _stage_w_in(wt_hbm, layer, w_scr, stage, sem):
    rows = PROJ_STAGE_ROWS
    big, small = [], []
    for s0, s1, d0 in _w_in_segments():
        if (s1 - s0) % rows == 0 and d0 % LANE == 0:
            big += [(s0 + j, d0 + j) for j in range(0, s1 - s0, rows)]
        else:
            small.append((s0, s1 - s0, d0 - SMALL))
    assert all(0 <= d and d + n <= LANE and d % SUBLANE == 0 for _, n, d in small) and len(small) <= 2

    def copy(j):
        return pltpu.make_async_copy(wt_hbm.at[layer, pl.ds(big[j][0], rows), :], stage.at[j % 2], sem.at[j % 2])
    copy(0).start()
    for j in range(len(big)):
        if j + 1 < len(big):
            copy(j + 1).start()
        copy(j).wait()
        w_scr[:, big[j][1]:big[j][1] + rows] = stage[j % 2].T.astype(bf16)
    used = max(d + n for _, n, d in small)
    stage[0, used:LANE, :] = jnp.zeros((LANE - used, D_MODEL), f32)
    copies = [pltpu.make_async_copy(wt_hbm.at[layer, pl.ds(s0, n), :], stage.at[0, pl.ds(d, n), :], sem.at[i])
              for i, (s0, n, d) in enumerate(small)]
    for cp in copies:
        cp.start()
    for cp in copies:
        cp.wait()
    w_scr[:, SMALL:SMALL + LANE] = stage[0, 0:LANE, :].T.astype(bf16)


def _proj_body(with_ln, layer, n_main, xm_ref, xt_ref, g_ref, b_ref, wt_hbm, pm_ref, pt_ref, w_scr, stage, sem):
    pl.when(pl.program_id(0) == 0)(lambda: _stage_w_in(wt_hbm, layer, w_scr, stage, sem))
    _on_region(n_main,
               lambda: _proj_tile(with_ln, xm_ref, g_ref, b_ref, w_scr, pm_ref),
               lambda: _proj_tile(with_ln, xt_ref, g_ref, b_ref, w_scr, pt_ref))


def _proj(xm, xt, g, b, w_in_t, layer, with_ln):
    n_main, rows_tail = xm.shape[0] // ROW_TILE, xt.shape[0]
    n = D_IN_AL
    out_shape = [jax.ShapeDtypeStruct((xm.shape[0], n), f32), jax.ShapeDtypeStruct((rows_tail, n), f32)]
    out_specs = list(_row_specs(n, n_main, rows_tail))
    return pl.pallas_call(
        functools.partial(_proj_body, with_ln, layer, n_main),
        out_shape=out_shape,
        grid=(n_main + 1,),
        in_specs=list(_row_specs(D_MODEL, n_main, rows_tail))
                 + [_const_spec((1, D_MODEL)), _const_spec((1, D_MODEL)), pl.BlockSpec(memory_space=pl.ANY)],
        out_specs=out_specs,
        scratch_shapes=[pltpu.VMEM((D_MODEL, n), bf16),
                        pltpu.VMEM((2, PROJ_STAGE_ROWS, D_MODEL), f32),
                        pltpu.SemaphoreType.DMA((2,))],
        compiler_params=pltpu.CompilerParams(dimension_semantics=("arbitrary",),
                                             vmem_limit_bytes=VMEM_LIMIT),
        name="in_proj",
    )(xm, xt, g, b, w_in_t)


def _post_tile(emb_ln, x_ref, o_ref, ge_ref, be_ref, wo_ref, g1_ref, b1_ref, wg_ref, wu_ref, wd_ref, g2_ref, b2_ref,
               y_ref):
    x = x_ref[...]
    if emb_ln:
        x = _layer_norm(x, ge_ref[...], be_ref[...])
    m = jnp.dot(o_ref[...], wo_ref[...], preferred_element_type=f32)
    y1 = _layer_norm(ALPHA * x + m, g1_ref[...], b1_ref[...])
    y1b = y1.astype(bf16)
    dff = wg_ref.shape[1]
    step = 4 * LANE
    acc = jnp.zeros(y1.shape, f32)
    for n0 in range(0, dff, step):
        n1 = min(n0 + step, dff)
        gate = jnp.dot(y1b, wg_ref[:, n0:n1], preferred_element_type=f32)
        up = jnp.dot(y1b, wu_ref[:, n0:n1], preferred_element_type=f32)
        act = (jax.nn.silu(gate) * up).astype(bf16)
        acc = acc + jnp.dot(act, wd_ref[n0:n1, :], preferred_element_type=f32)
    y_ref[...] = _layer_norm(ALPHA * y1 + acc, g2_ref[...], b2_ref[...])


def _post_body(layer, n_main, xm_ref, xt_ref, om_ref, ot_ref, ge_ref, be_ref, g1_ref, b1_ref, g2_ref, b2_ref,
               wo_ref, wg_ref, wu_ref, wd_ref, ym_ref, yt_ref):
    w = (ge_ref, be_ref, wo_ref, g1_ref, b1_ref, wg_ref, wu_ref, wd_ref, g2_ref, b2_ref)
    _on_region(n_main, lambda: _post_tile(layer == 0, xm_ref, om_ref, *w, ym_ref),
               lambda: _post_tile(layer == 0, xt_ref, ot_ref, *w, yt_ref))


def _post(xm, xt, om, ot, ge, be, g1, b1, g2, b2, w_out, w_gate, w_up, w_down, layer):
    n_main = xm.shape[0] // ROW_TILE
    rows = lambda: list(_row_specs(D_MODEL, n_main, xt.shape[0]))
    vec = lambda: _const_spec((1, D_MODEL))
    return pl.pallas_call(
        functools.partial(_post_body, layer, n_main),
        out_shape=[jax.ShapeDtypeStruct(xm.shape, f32), jax.ShapeDtypeStruct(xt.shape, f32)],
        grid=(n_main + 1,),
        in_specs=rows() + rows() + [vec() for _ in range(6)]
                 + [_const_spec(w.shape, True) for w in (w_out, w_gate, w_up, w_down)],
        out_specs=rows(),
        compiler_params=pltpu.CompilerParams(dimension_semantics=("arbitrary",),
                                             vmem_limit_bytes=VMEM_LIMIT),
        name="post_ffn",
    )(xm, xt, om, ot, ge, be, g1, b1, g2, b2, w_out, w_gate, w_up, w_down)


BASE_BLOCK = 8
assert (BASE_BLOCK // 2) * -math.log(GATE_CLAMP) < 80.0


def _levels(c, base=0):
    m, out = c // 2, []
    while m >= max(base, 1):
        out.append(m)
        m //= 2
    return tuple(out)


def _chunk_constants(c, base, nblk):
    t = np.arange(c)
    rows = [(t[None, :] <= t[:, None]).astype(np.float32), (t[None, :] > t[:, None]).astype(np.float32)]
    masks = []
    for m in _levels(c, base):
        blk = t // m
        upper = (blk % 2) == 1
        lo = np.where(upper, blk * m, t + 1)
        hi = np.where(upper, t, (blk + 1) * m - 1)
        rows.append(((t[None, :] >= lo[:, None]) & (t[None, :] <= hi[:, None])).astype(np.float32))
        same = (t[:, None] // (2 * m)) == (t[None, :] // (2 * m))
        masks.append((same & upper[:, None] & (~upper)[None, :]).astype(np.float32))
    if base:
        mid = (t // base) * base + base // 2 - 1
        after = (t[None, :] > mid[:, None]) & (t[None, :] <= t[:, None])
        before = (t[None, :] > t[:, None]) & (t[None, :] <= mid[:, None])
        rows.append(after.astype(np.float32) - before.astype(np.float32))
        masks.append(((t[:, None] // base == t[None, :] // base) & (t[None, :] <= t[:, None])).astype(np.float32))
    else:
        masks.append(np.eye(c, dtype=np.float32))
    return np.concatenate(rows, 0), np.tile(np.stack(masks, 0), (1, 1, nblk))


def _mm(a, b):
    return jnp.einsum('gmk,gkn->gmn', a.astype(bf16), b.astype(bf16), preferred_element_type=f32)


def _mm_nt(a, b):
    return jnp.einsum('gmk,gnk->gmn', a.astype(bf16), b.astype(bf16), preferred_element_type=f32)


def _mm_tn(a, b):
    return lax.dot_general(a.astype(bf16), b.astype(bf16), (((1,), (1,)), ((0,), (0,))),
                           preferred_element_type=f32)


def _iota(shape, dim):
    return lax.broadcasted_iota(jnp.int32, shape, dim)


def _bd(z, nblk, blk=None):
    z = z.astype(bf16)
    if blk is None:
        blk = _iota((1, 1, z.shape[-1]), 2) // (z.shape[-1] // nblk)
    return jnp.concatenate([jnp.where(blk == h, z, jnp.zeros_like(z)) for h in range(nblk)], axis=1)


def _hmm(x, z, nblk):
    return _mm(x, _bd(z, nblk))


def _hmm_nt(x, z, nblk):
    return _mm_nt(x, _bd(z, nblk))


def _hmm_split(x, z, nblk, blk=None):
    m = x.shape[1]
    x_hi, z_hi = x.astype(bf16), z.astype(bf16)
    x_lo = (x - x_hi.astype(f32)).astype(bf16)
    z_lo = (z - z_hi.astype(f32)).astype(bf16)
    both = _mm(jnp.concatenate([x_hi, x_lo], 1), _bd(z_hi, nblk, blk))
    return both[:, 0:m] + (both[:, m:2 * m] + _mm(x_hi, _bd(z_lo, nblk, blk)))


def _block_mask(rows, lanes, dr, dl):
    return (_iota((rows, lanes), 0) // dr) == (_iota((rows, lanes), 1) // dl)


def _dot01(m01, x):
    nb = x.shape[0]
    x2 = jnp.concatenate([x[i] for i in range(nb)], -1) if nb > 1 else x[0]
    hi = x2.astype(bf16)
    lo = (x2 - hi.astype(f32)).astype(bf16)
    m01 = m01.astype(bf16)
    d = lambda y: jnp.dot(m01, y, preferred_element_type=f32)
    out = d(hi) + d(lo)
    w = x.shape[-1]
    return jnp.stack([out[:, i * w:(i + 1) * w] for i in range(nb)], 0)


def _seg_sum(x, ones_bd, pieces=2):
    nb, c, w = x.shape
    x2 = x.reshape(nb * c, w)
    hi = x2.astype(bf16)
    out = jnp.dot(hi, ones_bd, preferred_element_type=f32)
    if pieces == 2:
        lo = (x2 - hi.astype(f32)).astype(bf16)
        out = out + jnp.dot(lo, ones_bd, preferred_element_type=f32)
    return out.reshape(nb, c, w)


def _last_row_as_col(b, c):
    return jnp.stack([b[i, c - SUBLANE:c, :].T[:, SUBLANE - 1:SUBLANE] for i in range(b.shape[0])], 0)


def _channel_decay_mixer(q, k, v, gl, s, nblk, sums_ref, masks_ref, bdmask, c, base):
    lv = _levels(c, base)
    nl = len(lv)
    sums = _dot01(sums_ref[...], gl)
    b = sums[:, 0:c]
    rev = sums[:, c:2 * c]
    row = _iota((1, c, 1), 1)
    if base:
        e0 = sums[:, (nl + 2) * c:(nl + 3) * c]
        att = jnp.where(masks_ref[nl] > 0.0, _hmm_nt(q * jnp.exp(e0), k * jnp.exp(-e0), nblk), 0.0)
    else:
        att = masks_ref[nl] * _hmm_nt(q, k, nblk)
    for li, m in enumerate(lv):
        upper = ((row // m) % 2) == 1
        w = jnp.where(upper, q, k) * jnp.exp(sums[:, (li + 2) * c:(li + 3) * c])
        att = att + masks_ref[li] * _hmm_nt(w, w, nblk)
    o = _mm(jnp.concatenate([q * jnp.exp(b), att], -1), jnp.concatenate([s.astype(bf16), _bd(v, nblk)], 1))
    s_new = jnp.exp(_last_row_as_col(b, c)) * s + jnp.where(bdmask, _mm_tn(k * jnp.exp(rev), v), 0.0)
    return o, s_new


MixCfg = collections.namedtuple("MixCfg", "c g r0 r1 layer n_chunks init p4d o_rows n_alias stacked n_cast")


def _pairs_to_bd(s):
    z = jnp.zeros(s.shape[:1] + (HEAD_DIM, HEAD_DIM), f32)
    pair = lambda p: jnp.concatenate([jnp.concatenate([s[:, 2 * p], z], -1),
                                      jnp.concatenate([z, s[:, 2 * p + 1]], -1)], 1)
    return jnp.concatenate([pair(0), pair(1)], 0)


def _quad_to_bd(s):
    z = lambda n: jnp.zeros(s.shape[:1] + (GLA_DK, n * HEAD_DIM), f32)
    rows = []
    for h in range(N_HEADS):
        parts = ([z(h)] if h else []) + [s[:, h]] + ([z(N_HEADS - 1 - h)] if h < N_HEADS - 1 else [])
        rows.append(jnp.concatenate(parts, -1))
    return jnp.concatenate(rows, 1)


def _mixer_body(cfg, *refs):
    c, g = cfg.c, cfg.g
    n_w, n_t, n_c = 7, 3, 4
    p_ref = refs[0]
    cw_ref, alog_ref, dtb_ref, lbl_ref, wg_ref, bg_ref, gains_ref = refs[1:1 + n_w]
    cos_ref, sina_ref, sinb_ref = refs[1 + n_w:1 + n_w + n_t]
    sumsb_ref, masksb_ref, sumsc_ref, masksc_ref = refs[1 + n_w + n_t:1 + n_w + n_t + n_c]
    pos = 1 + n_w + n_t + n_c
    if cfg.init:
        convi_ref, sai_ref, sbi_ref, sci_ref, sdi_ref = refs[pos:pos + 5]
        pos += 5
    cast_in = refs[pos:pos + cfg.n_cast]
    pos += cfg.n_cast + cfg.n_alias
    o_ref, convo_ref, sao_ref, sbo_ref, sco_ref, sdo_ref = refs[pos:pos + 6]
    if cfg.stacked:
        convo_ref, sao_ref, sbo_ref, sco_ref, sdo_ref = (r.at[0] for r in (convo_ref, sao_ref, sbo_ref, sco_ref, sdo_ref))
    cast_out = refs[pos + 6:pos + 6 + cfg.n_cast]
    cbuf, sa_scr, sb_scr, sc_scr, sd_scr = refs[pos + 6 + cfg.n_cast:]
    ci = pl.program_id(1)
    for w_f32, w_bf16 in zip(cast_in, cast_out):
        w_bf16[...] = w_f32[0].astype(bf16)

    @pl.when(ci == 0)
    def _():
        cbuf[:, 0:CONV_HIST, :] = jnp.zeros((g, CONV_HIST, 3 * GROUP), f32)
        if not cfg.init:
            for r in (sa_scr, sb_scr, sc_scr, sd_scr):
                r[...] = jnp.zeros(r.shape, f32)
        else:
            take = lambda r: jnp.broadcast_to(r[...], (g,) + r.shape[1:])
            cbuf[:, CONV_HIST - (CONV_W - 1):CONV_HIST, :] = take(convi_ref)
            sa_scr[...] = _pairs_to_bd(take(sai_ref))
            sb_scr[...] = _pairs_to_bd(take(sbi_ref))
            sc_scr[...] = _quad_to_bd(take(sci_ref))
            sd_scr[...] = _pairs_to_bd(take(sdi_ref))

    def load(a, b):
        if cfg.p4d:
            return p_ref[0, :, :, a:b]
        return p_ref[:, a:b].reshape(g, c, b - a)

    r0, r1 = cfg.r0, cfg.r1
    n_valid = float(r1 - r0)
    row = _iota((1, c, 1), 1)
    vm = ((row >= r0) & (row < r1)).astype(f32)
    lane_v = _iota((1, 1, 2 * HEAD_DIM), 2)
    lane_s = _iota((1, 1, 2 * c), 2)
    s_idx = lane_s % c
    incl = s_idx <= row
    strict = s_idx < row
    bd2 = _block_mask(2 * HEAD_DIM, 2 * HEAD_DIM, HEAD_DIM, HEAD_DIM)
    bd4 = _block_mask(N_HEADS * GLA_DK, GROUP, GLA_DK, HEAD_DIM)
    ones2 = bd2.astype(bf16)
    pairs = lambda x: jnp.concatenate([x[..., :LANE], x[..., LANE:]], 0)
    unpairs = lambda x: jnp.concatenate([x[:g], x[g:]], -1)
    n_solve = max(1, math.ceil(math.log2(r1 - r0)))
    sm = load(SMALL, SMALL + LANE)
    valid = (lambda x: x) if (r0 == 0 and r1 == c) else (lambda x: x * vm)

    cbuf[:, CONV_HIST:CONV_HIST + c, :] = valid(load(A_QKV, A_QKV + 3 * GROUP))
    acc = cbuf[:, pl.ds(CONV_HIST - 3, c), :] * cw_ref[0:1, :]
    for j in range(1, CONV_W):
        acc = acc + cbuf[:, pl.ds(CONV_HIST - 3 + j, c), :] * cw_ref[j:j + 1, :]
    conv = jax.nn.silu(acc)
    beta = valid(jax.nn.sigmoid(sm))
    g_a = valid(-jnp.exp(alog_ref[...]) * jax.nn.softplus(sm + dtb_ref[...]))
    b_a = _dot01(sumsc_ref[0:c, :], g_a)

    def head_cols(x, lane0):
        col = lambda h: x[..., lane0 + h:lane0 + h + 1]
        return jnp.concatenate([col(0), col(2)], 0), jnp.concatenate([col(1), col(3)], 0)

    def spread(c0, c1, lane, width):
        return jnp.where(lane < width, c0, c1)

    bc0, bc1 = head_cols(b_a, SM_ALPHA)
    bt0, bt1 = head_cols(beta, SM_BETA)
    bc_v, bet_v = spread(bc0, bc1, lane_v, HEAD_DIM), spread(bt0, bt1, lane_v, HEAD_DIM)
    bc_s, bet_s = spread(bc0, bc1, lane_s, c), spread(bt0, bt1, lane_s, c)
    b_rows = [b_a[i].T for i in range(g)]
    b_row = lambda h: jnp.stack([t[SM_ALPHA + h:SM_ALPHA + h + 1, :] for t in b_rows], 0)
    br_s = jnp.concatenate([jnp.concatenate([b_row(0), b_row(1)], -1),
                            jnp.concatenate([b_row(2), b_row(3)], -1)], 0)
    dec = jnp.where(incl, jnp.exp(jnp.where(incl, bc_s - br_s, 0.0)), 0.0)
    q = pairs(conv[..., 0:GROUP])
    k = pairs(conv[..., GROUP:2 * GROUP])
    v = pairs(valid(conv[..., 2 * GROUP:3 * GROUP]))
    q = q * lax.rsqrt(_seg_sum(q * q, ones2) + NORM_EPS) * HEAD_DIM ** -0.5
    k = valid(k * lax.rsqrt(_seg_sum(k * k, ones2) + NORM_EPS))
    s_a = sa_scr[...]
    kq = jnp.concatenate([k, q], 1)
    kq_k = _hmm_nt(kq, k, 2)
    kq_s = _mm(kq, s_a)
    a_mat = jnp.where(strict, bet_s * dec * kq_k[:, 0:c], 0.0)
    gam = jnp.exp(bc_v)
    u = bet_v * (v - gam * kq_s[:, 0:c])
    p = -a_mat
    lane_pu = _iota((1, 1, 2 * c + 2 * HEAD_DIM), 2)
    head_pu = jnp.where(lane_pu < 2 * c, lane_pu // c, (lane_pu - 2 * c) // HEAD_DIM)
    for i in range(n_solve):
        if i + 1 < n_solve:
            pu = _hmm_split(p, jnp.concatenate([p, u], -1), 2, head_pu)
            p, u = pu[..., 0:2 * c], u + pu[..., 2 * c:]
        else:
            u = u + _hmm_split(p, u, 2)
    o_a = gam * kq_s[:, c:2 * c] + _hmm(dec * kq_k[:, c:2 * c], u, 2)
    bl = bc_v[:, c - 1:c, :]
    sa_scr[...] = jnp.exp(bl) * s_a + jnp.where(bd2, _mm_tn(k * jnp.exp(bl - bc_v), u), 0.0)
    o_a = unpairs(o_a * lax.rsqrt(_seg_sum(o_a * o_a, ones2, 1) * (1.0 / HEAD_DIM) + NORM_EPS))

    lbl = lbl_ref[...]
    e = jnp.exp(lbl - jnp.max(lbl, axis=0, keepdims=True))
    prob = e / jnp.sum(e, axis=0, keepdims=True)
    lb = (jnp.sum(prob[1:cfg.layer + 1], axis=0, keepdims=True) if cfg.layer > 0
          else jnp.zeros((1, GROUP), f32))
    kb = (1.0 - lb) * jax.nn.sigmoid(-load(B_F, B_F + GROUP))
    g_b = valid(jnp.log1p(-jnp.clip(kb, 0.0, 1.0 - GATE_CLAMP)))
    o_b, s_b_new = _channel_decay_mixer(
        pairs(load(B_Q, B_Q + GROUP)), pairs(valid(kb)), pairs(valid(load(B_I, B_I + GROUP))),
        pairs(g_b), sb_scr[...], 2, sumsb_ref, masksb_ref, bd2, c, BASE_BLOCK if c > BASE_BLOCK else 0)
    sb_scr[...] = s_b_new
    o_b = unpairs(o_b * lax.rsqrt(_seg_sum(o_b * o_b, ones2, 1) * (1.0 / HEAD_DIM) + NORM_EPS))

    logit = jnp.dot(sm.reshape(g * c, LANE).astype(bf16), wg_ref[...].astype(bf16),
                    preferred_element_type=f32).reshape(g, c, LANE) + bg_ref[...]
    g_c = valid(jax.nn.log_sigmoid(logit) * (1.0 / GLA_GATE_NORM))
    o_c, s_c_new = _channel_decay_mixer(
        load(C_Q, C_Q + N_HEADS * GLA_DK) * GLA_DK ** -0.5, valid(load(C_K, C_K + N_HEADS * GLA_DK)),
        valid(load(C_V, C_V + GROUP)), g_c, sc_scr[...], N_HEADS, sumsc_ref, masksc_ref, bd4, c, 0)
    sc_scr[...] = s_c_new
    ones4 = _block_mask(GROUP, GROUP, HEAD_DIM, HEAD_DIM).astype(bf16)
    o_c = o_c * lax.rsqrt(_seg_sum(o_c * o_c, ones4, 1) * (1.0 / HEAD_DIM) + NORM_EPS)

    def rope(x):
        x2 = x.reshape(g * c, GROUP)
        rot = lambda sh: pltpu.roll(x2, sh, 1).reshape(g, c, GROUP)
        return x * cos_ref[...] + rot(GROUP - HEAD_DIM // 2) * sina_ref[...] + rot(HEAD_DIM // 2) * sinb_ref[...]
    q = pairs(rope(load(D_Q, D_Q + GROUP)))
    k = pairs(valid(rope(load(D_K, D_K + GROUP)) * HEAD_DIM ** -0.5))
    v = pairs(valid(load(D_V, D_V + GROUP)))
    pair_idx = _iota((2, 1, 1), 0)
    log_gamma = lambda lane, width: jnp.log1p(-jnp.exp2(-5.0 - (2 * pair_idx + lane // width).astype(f32)))
    per_seq = lambda x: jnp.concatenate([jnp.broadcast_to(x[p:p + 1], (g,) + x.shape[1:]) for p in range(2)], 0)
    lg_v, lg_s = log_gamma(lane_v, HEAD_DIM), log_gamma(lane_s, c)
    cnt_c = jnp.clip(row + 1 - r0, 0, r1 - r0).astype(f32)
    cnt_s = jnp.clip(s_idx + 1 - r0, 0, r1 - r0).astype(f32)
    dec = per_seq(jnp.where(incl, jnp.exp(jnp.where(incl, (cnt_c - cnt_s) * lg_s, 0.0)), 0.0))
    s_d = sd_scr[...]
    o_d = _mm(jnp.concatenate([q * per_seq(jnp.exp(cnt_c * lg_v)), dec * _hmm_nt(q, k, 2)], -1),
              jnp.concatenate([s_d.astype(bf16), _bd(v, 2)], 1))
    sd_scr[...] = (per_seq(jnp.exp(n_valid * lg_v)) * s_d
                   + jnp.where(bd2, _mm_tn(k * per_seq(jnp.exp((n_valid - cnt_c) * lg_v)), v), 0.0))
    oc = o_d - _seg_sum(o_d, ones2) * (1.0 / HEAD_DIM)
    o_d = unpairs(oc * lax.rsqrt(_seg_sum(oc * oc, ones2, 1) * (1.0 / HEAD_DIM) + LN_EPS))

    for i, (o_m, gate_col) in enumerate(zip((o_a, o_b, o_c, o_d), (A_GATE, B_GATE, C_GATE, D_GATE))):
        val = (o_m * gains_ref[i:i + 1, :] * load(gate_col, gate_col + GROUP)).astype(o_ref.dtype)
        if cfg.p4d:
            o_ref[0, :, :, i * GROUP:(i + 1) * GROUP] = val
        else:
            o_ref[0:g * c, i * GROUP:(i + 1) * GROUP] = val.reshape(g * c, GROUP)
    if cfg.o_rows > g * c:
        o_ref[g * c:cfg.o_rows, :] = jnp.zeros((cfg.o_rows - g * c, D_MODEL), o_ref.dtype)
    cbuf[:, 0:CONV_HIST, :] = cbuf[:, c:c + CONV_HIST, :]

    @pl.when(ci == cfg.n_chunks - 1)
    def _():
        convo_ref[...] = cbuf[:, pl.ds(CONV_HIST + r1 - (CONV_W - 1), CONV_W - 1), :]
        for scr, out in ((sa_scr, sao_ref), (sb_scr, sbo_ref), (sd_scr, sdo_ref)):
            s = scr[...]
            for p in range(2):
                out[:, 2 * p] = s[p * g:(p + 1) * g, 0:HEAD_DIM, 0:HEAD_DIM]
                out[:, 2 * p + 1] = s[p * g:(p + 1) * g, HEAD_DIM:2 * HEAD_DIM, HEAD_DIM:2 * HEAD_DIM]
        s = sc_scr[...]
        for h in range(N_HEADS):
            sco_ref[:, h] = s[:, h * GLA_DK:(h + 1) * GLA_DK, h * HEAD_DIM:(h + 1) * HEAD_DIM]


def _state_shapes(n, stacked):
    lead = (DEPTH, n) if stacked else (n,)
    return [jax.ShapeDtypeStruct(lead + (CONV_W - 1, 3 * GROUP), f32),
            jax.ShapeDtypeStruct(lead + (N_HEADS, HEAD_DIM, HEAD_DIM), f32),
            jax.ShapeDtypeStruct(lead + (N_HEADS, HEAD_DIM, HEAD_DIM), f32),
            jax.ShapeDtypeStruct(lead + (N_HEADS, GLA_DK, HEAD_DIM), f32),
            jax.ShapeDtypeStruct(lead + (N_HEADS, HEAD_DIM, HEAD_DIM), f32)]


def _state_out_specs(g, layer, stacked):
    if stacked:
        st = lambda k: pl.BlockSpec((1, g, N_HEADS, k, HEAD_DIM), lambda b, i: (layer, b, 0, 0, 0))
        conv = pl.BlockSpec((1, g, CONV_W - 1, 3 * GROUP), lambda b, i: (layer, b, 0, 0))
    else:
        st = lambda k: pl.BlockSpec((g, N_HEADS, k, HEAD_DIM), lambda b, i: (b, 0, 0, 0))
        conv = pl.BlockSpec((g, CONV_W - 1, 3 * GROUP), lambda b, i: (b, 0, 0))
    return [conv, st(HEAD_DIM), st(HEAD_DIM), st(GLA_DK), st(HEAD_DIM)]


def _mixer_call(cfg, name, n_groups, p, p_spec, wts, tables, table_spec, consts, init, init_specs,
                o_shape, o_spec, n_state_rows, o_prev=None, states_prev=None, cast=()):
    c, g = cfg.c, cfg.g
    n_steps = n_groups * cfg.n_chunks
    wspecs =[_const_spec((CONV_W, 3 * GROUP)), _const_spec((1, LANE)), _const_spec((1, LANE)),
              _const_spec((DEPTH, GROUP)), _const_spec((LANE, LANE)), _const_spec((1, LANE)),
              _const_spec((4, GROUP))]
    cspecs = [_const_spec(a.shape) for a in consts]
    in_specs = [p_spec] + wspecs + [table_spec] * 3 + cspecs + list(init_specs)
    args = [p, *wts, *tables, *consts, *init]
    cast_shapes, cast_specs = [], []
    for w in cast:
        slab = w.shape[1] // n_steps
        assert w.shape[1] % n_steps == 0 and slab % (2 * SUBLANE) == 0
        args.append(w)
        in_specs.append(pl.BlockSpec((1, slab, w.shape[2]), lambda b, i: (cfg.layer, b * cfg.n_chunks + i, 0)))
        cast_shapes.append(jax.ShapeDtypeStruct(w.shape[1:], bf16))
        cast_specs.append(pl.BlockSpec((slab, w.shape[2]), lambda b, i: (b * cfg.n_chunks + i, 0)))
    assert len(cast) == cfg.n_cast
    aliases = {}
    if o_prev is not None:
        aliases[len(args)] = 0
        args.append(o_prev)
    for j, s_prev in enumerate(states_prev or ()):
        aliases[len(args)] = 1 + j
        args.append(s_prev)
    in_specs += [pl.BlockSpec(memory_space=pl.ANY)] * len(aliases)
    assert len(aliases) == cfg.n_alias
    return pl.pallas_call(
        functools.partial(_mixer_body, cfg),
        out_shape=[o_shape] + _state_shapes(n_state_rows, cfg.stacked) + cast_shapes,
        grid=(n_groups, cfg.n_chunks),
        in_specs=in_specs,
        out_specs=[o_spec] + _state_out_specs(g, cfg.layer, cfg.stacked) + cast_specs,
        scratch_shapes=[pltpu.VMEM((g, c + CONV_HIST, 3 * GROUP), f32),
                        pltpu.VMEM((2 * g, 2 * HEAD_DIM, 2 * HEAD_DIM), f32),
                        pltpu.VMEM((2 * g, 2 * HEAD_DIM, 2 * HEAD_DIM), f32),
                        pltpu.VMEM((g, N_HEADS * GLA_DK, GROUP), f32),
                        pltpu.VMEM((2 * g, 2 * HEAD_DIM, 2 * HEAD_DIM), f32)],
        input_output_aliases=aliases,
        compiler_params=pltpu.CompilerParams(dimension_semantics=("arbitrary", "arbitrary"),
                                             vmem_limit_bytes=VMEM_LIMIT),
        name=name,
    )(*args)


def _tok(ref, t, nb):
    return ref[t * nb:(t + 1) * nb, :].T


def _bcast_row(ref, t, r):
    return jnp.broadcast_to(ref[t, pl.ds(r, 1), :], (HEAD_DIM, ref.shape[-1]))


def _bl_diag_loop(n_tok, kdim, j, row0, s_in, s_out, d_scr, k_scr, q_scr, v_scr, d_const=None):
    nb = s_in.shape[-1]

    def body(k, os):
        s_k = s_in[0, j, k]
        r = row0 + k
        outs = []
        for t in range(n_tok):
            d = d_const if d_const is not None else _bcast_row(d_scr, t, r)
            s_k = d * s_k + _bcast_row(k_scr, t, r) * v_scr[t, j * HEAD_DIM:(j + 1) * HEAD_DIM, :]
            outs.append(os[t] + _bcast_row(q_scr, t, r) * s_k)
        s_out[0, j, k] = s_k
        return tuple(outs)
    return lax.fori_loop(0, kdim, body, (jnp.zeros((HEAD_DIM, nb), f32),) * n_tok)


def _bl_store(o_ref, os, gate_ref, gain_ref, n_tok, nb, group_norm):
    for t in range(n_tok):
        gate = _tok(gate_ref, t, nb)
        halves = []
        for j in range(2):
            o = os[j][t]
            if group_norm:
                o = o - jnp.mean(o, axis=0, keepdims=True)
                o = o * lax.rsqrt(jnp.mean(o * o, axis=0, keepdims=True) + LN_EPS)
            else:
                o = o * lax.rsqrt(jnp.mean(o * o, axis=0, keepdims=True) + NORM_EPS)
            hs = slice(j * HEAD_DIM, (j + 1) * HEAD_DIM)
            halves.append(o * gain_ref[hs, :] * gate[hs, :])
        o_ref[t * nb:(t + 1) * nb, :] = jnp.concatenate(halves, 0).T.astype(o_ref.dtype)


def _bl_hgrn_body(layer, n_tok, q_ref, f_ref, i_ref, gate_ref, lbl_ref, gain_ref, s_in, o_prev, s_prev,
                  o_ref, s_out, d_scr, k_scr, q_scr, v_scr):
    del o_prev, s_prev
    nb = s_in.shape[-1]
    lbl = lbl_ref[...]
    e = jnp.exp(lbl - jnp.max(lbl, axis=1, keepdims=True))
    prob = e / jnp.sum(e, axis=1, keepdims=True)
    lb = (jnp.sum(prob[:, 1:layer + 1], axis=1, keepdims=True) if layer > 0 else jnp.zeros((LANE, 1), f32))
    for t in range(n_tok):
        kk = (1.0 - lb) * jax.nn.sigmoid(-_tok(f_ref, t, nb))
        d_scr[t] = 1.0 - jnp.clip(kk, 0.0, 1.0 - GATE_CLAMP)
        k_scr[t] = kk
        q_scr[t] = _tok(q_ref, t, nb)
        v_scr[t] = _tok(i_ref, t, nb)
    os = [_bl_diag_loop(n_tok, HEAD_DIM, j, j * HEAD_DIM, s_in, s_out, d_scr, k_scr, q_scr, v_scr) for j in range(2)]
    _bl_store(o_ref, os, gate_ref, gain_ref, n_tok, nb, False)


def _bl_gla_body(n_tok, q_ref, k_ref, v_ref, gate_ref, small_ref, wg_ref, bg_ref, gain_ref, s_in, o_prev, s_prev,
                 o_ref, s_out, d_scr, k_scr, q_scr, v_scr):
    del o_prev, s_prev
    nb = s_in.shape[-1]
    for t in range(n_tok):
        logit = jnp.dot(wg_ref[...].astype(bf16), _tok(small_ref, t, nb).astype(bf16),
                        preferred_element_type=f32) + bg_ref[...]
        d_scr[t] = jnp.exp(jax.nn.log_sigmoid(logit) * (1.0 / GLA_GATE_NORM))
        k_scr[t] = _tok(k_ref, t, nb)
        q_scr[t] = _tok(q_ref, t, nb) * GLA_DK ** -0.5
        v_scr[t] = _tok(v_ref, t, nb)
    p = pl.program_id(0)
    os = [_bl_diag_loop(n_tok, GLA_DK, j, (2 * p + j) * GLA_DK, s_in, s_out, d_scr, k_scr, q_scr, v_scr)
          for j in range(2)]
    _bl_store(o_ref, os, gate_ref, gain_ref, n_tok, nb, False)


def _bl_ret_body(n_tok, q_ref, k_ref, v_ref, gate_ref, cos_ref, sin_ref, gain_ref, s_in, o_prev, s_prev,
                 o_ref, s_out, k_scr, q_scr, v_scr):
    del o_prev, s_prev
    nb = s_in.shape[-1]
    half = HEAD_DIM // 2

    def rope(x, t):
        swapped = jnp.concatenate([x[half:2 * half], x[0:half], x[3 * half:4 * half], x[2 * half:3 * half]], 0)
        return x * cos_ref[t] + swapped * sin_ref[t]
    for t in range(n_tok):
        q_scr[t] = rope(_tok(q_ref, t, nb), t)
        k_scr[t] = rope(_tok(k_ref, t, nb), t) * HEAD_DIM ** -0.5
        v_scr[t] = _tok(v_ref, t, nb)
    p = pl.program_id(0)
    os = []
    for j in range(2):
        head = jnp.zeros((1, nb), f32) + (2 * p + j).astype(f32)
        gamma = jnp.broadcast_to(1.0 - jnp.exp2(-5.0 - head), (HEAD_DIM, nb))
        os.append(_bl_diag_loop(n_tok, HEAD_DIM, j, j * HEAD_DIM, s_in, s_out, None, k_scr, q_scr, v_scr, gamma))
    _bl_store(o_ref, os, gate_ref, gain_ref, n_tok, nb, True)


def _bl_delta_body(n_tok, q_ref, k_ref, v_ref, gate_ref, small_ref, hq_ref, hk_ref, hv_ref, cwq_ref, cwk_ref, cwv_ref,
                   alog_ref, dtb_ref, gain_ref, s_in, o_prev, s_prev, cq_prev, ck_prev, cv_prev,
                   o_ref, s_out, cq_out, ck_out, cv_out, k_scr, q_scr, kh_scr, u_scr, bg_scr):
    del o_prev, s_prev, cq_prev, ck_prev, cv_prev
    nb = s_in.shape[-1]
    p = pl.program_id(0)

    def conv(x_ref, h_ref, cw_ref):
        seq = [h_ref[0, r].T for r in range(CONV_W - 1)] + [_tok(x_ref, t, nb) for t in range(n_tok)]
        outs = []
        for t in range(n_tok):
            acc = cw_ref[:, 0:1] * seq[t]
            for jj in range(1, CONV_W):
                acc = acc + cw_ref[:, jj:jj + 1] * seq[t + jj]
            outs.append(jax.nn.silu(acc))
        return outs

    def l2norm(x):
        parts = []
        for j in range(2):
            xj = x[j * HEAD_DIM:(j + 1) * HEAD_DIM]
            parts.append(xj * lax.rsqrt(jnp.sum(xj * xj, axis=0, keepdims=True) + NORM_EPS))
        return jnp.concatenate(parts, 0)

    qs, ks, vs = conv(q_ref, hq_ref, cwq_ref), conv(k_ref, hk_ref, cwk_ref), conv(v_ref, hv_ref, cwv_ref)
    for t in range(n_tok):
        q_scr[t] = l2norm(qs[t]) * HEAD_DIM ** -0.5
        k_scr[t] = l2norm(ks[t])
        sm = _tok(small_ref, t, nb)
        both = jnp.where(_iota((LANE, 1), 0) < SM_ALPHA, jax.nn.sigmoid(sm),
                         -jnp.exp(alog_ref[...]) * jax.nn.softplus(sm + dtb_ref[...]))
        bg_scr[t] = both[0:SUBLANE]
    for r in range(CONV_W - 1):
        rows = slice((n_tok - (CONV_W - 1) + r) * nb, (n_tok - (CONV_W - 1) + r + 1) * nb)
        cq_out[0, r] = q_ref[rows, :]
        ck_out[0, r] = k_ref[rows, :]
        cv_out[0, r] = v_ref[rows, :]

    os = []
    for j in range(2):
        hs = slice(j * HEAD_DIM, (j + 1) * HEAD_DIM)
        h = 2 * p + j
        beta = [bg_scr[t, pl.ds(SM_BETA + h, 1), :] for t in range(n_tok)]
        b, acc = [], jnp.zeros((1, nb), f32)
        for t in range(n_tok):
            acc = acc + bg_scr[t, pl.ds(SM_ALPHA + h, 1), :]
            b.append(acc)
        kt = [k_scr[t, hs, :] for t in range(n_tok)]
        qt = [q_scr[t, hs, :] for t in range(n_tok)]
        dot = lambda x, y: jnp.sum(x * y, axis=0, keepdims=True)

        def sweep(scr):
            def body(k, accs):
                s_k = s_in[0, j, k]
                return tuple(a + _bcast_row(scr, t, j * HEAD_DIM + k) * s_k for t, a in enumerate(accs))
            return lax.fori_loop(0, HEAD_DIM, body, (jnp.zeros((HEAD_DIM, nb), f32),) * n_tok)
        k_s, q_s = sweep(k_scr), sweep(q_scr)
        us = []
        for t in range(n_tok):
            u = beta[t] * (vs[t][hs] - jnp.exp(b[t]) * k_s[t])
            for s in range(t):
                u = u - (beta[t] * jnp.exp(b[t] - b[s]) * dot(kt[t], kt[s])) * us[s]
            us.append(u)
        o_h = []
        for t in range(n_tok):
            o = jnp.exp(b[t]) * q_s[t]
            for s in range(t + 1):
                o = o + (jnp.exp(b[t] - b[s]) * dot(qt[t], kt[s])) * us[s]
            o_h.append(o)
        os.append(o_h)
        for t in range(n_tok):
            kh_scr[t, hs, :] = kt[t] * jnp.exp(b[-1] - b[t])
            u_scr[t] = us[t]
        decay = jnp.broadcast_to(jnp.exp(b[-1]), (HEAD_DIM, nb))

        def update(k, carry):
            s_k = decay * s_in[0, j, k]
            for t in range(n_tok):
                s_k = s_k + _bcast_row(kh_scr, t, j * HEAD_DIM + k) * u_scr[t]
            s_out[0, j, k] = s_k
            return carry
        lax.fori_loop(0, HEAD_DIM, update, 0)
    _bl_store(o_ref, os, gate_ref, gain_ref, n_tok, nb, False)


def _bl_call(body, name, mixer, n_tok, nb, blk_row, pt, cols, extra, extra_specs, state_t, kdim, layer,
             o_prev, s_prev, scratch, conv_prev=None):
    def tok_spec(c0, per_pair):
        return pl.BlockSpec((n_tok * nb, LANE),
                            (lambda p: (blk_row, c0 // LANE + p)) if per_pair else (lambda p: (blk_row, c0 // LANE)))
    st_spec = pl.BlockSpec((1, 2, kdim, HEAD_DIM, nb), lambda p: (layer, p, 0, 0, 0))
    args = [pt] * len(cols) + list(extra) + [state_t]
    in_specs = [tok_spec(c0, pp) for c0, pp in cols] + list(extra_specs) + [st_spec]
    aliases = {}

    def donate(arr, out_idx):
        if arr is not None:
            aliases[len(args)] = out_idx
        args.append(arr if arr is not None else jnp.zeros((1,), f32))
        in_specs.append(pl.BlockSpec(memory_space=pl.ANY))
    donate(o_prev, 0)
    donate(s_prev, 1)
    out_shape = [jax.ShapeDtypeStruct(o_prev.shape, o_prev.dtype),
                 jax.ShapeDtypeStruct((DEPTH,) + state_t.shape[1:], f32)]
    out_specs = [pl.BlockSpec((n_tok * nb, LANE), lambda p: (blk_row, 2 * mixer + p)), st_spec]
    if conv_prev is not None:
        for i in range(3):
            donate(conv_prev[i] if conv_prev else None, 2 + i)
            out_shape.append(jax.ShapeDtypeStruct((DEPTH, CONV_W - 1, nb, GROUP), f32))
            out_specs.append(pl.BlockSpec((1, CONV_W - 1, nb, LANE), lambda p: (layer, 0, 0, p)))
    return pl.pallas_call(
        body, out_shape=out_shape, grid=(2,), in_specs=in_specs, out_specs=out_specs,
        scratch_shapes=scratch, input_output_aliases=aliases,
        compiler_params=pltpu.CompilerParams(dimension_semantics=("arbitrary",), vmem_limit_bytes=VMEM_LIMIT),
        name=name,
    )(*args)


def _rope_tables(pos):
    half = HEAD_DIM // 2
    inv = 1.0 / (RET_THETA_BASE ** jnp.linspace(0.0, 1.0, half, dtype=f32))
    ang = pos[:, None] * inv[None, :]
    cos, sin = jnp.cos(ang), jnp.sin(ang)
    zero = jnp.zeros_like(sin)
    tile = lambda a, b: jnp.tile(jnp.concatenate([a, b], -1), (1, N_HEADS))
    return tile(cos, cos), tile(-sin, zero), tile(zero, sin)


def kernel(x_prompt, x_sample, state_delta_conv, state_delta, state_hgrn, state_gla, state_ret, meta_tokens, emb_ln_g, emb_ln_b, w_in, conv_w, delta_a_log, delta_dt_bias, delta_norm_g, hgrn_lb_logits, hgrn_norm_g, gla_w_gate, gla_b_gate, gla_norm_g, ret_norm_g, w_out, ln1_g, ln1_b, w_ffn_gate, w_ffn_up, w_ffn_down, ln2_g, ln2_b):
    bsz, seq, _ = x_prompt.shape
    n_seq, n_tok, _ = x_sample.shape
    pg = PROMPT_GROUP if bsz % PROMPT_GROUP == 0 else 1
    rows_main = bsz * seq
    rows_s = n_tok * n_seq
    assert seq % CHUNK == 0 and rows_main % ROW_TILE == 0 and N_META <= CHUNK
    assert n_tok >= CONV_W - 1 and rows_s % CHUNK == 0 and (n_seq % LANE == 0 or n_seq < LANE)
    lead_pad = CHUNK - N_META
    rows_tail = rows_s + CHUNK
    s_blk = 0
    meta_blk = rows_s // CHUNK

    xs = jnp.swapaxes(x_sample, 0, 1).reshape(rows_s, D_MODEL)
    xt = jnp.concatenate([xs, jnp.zeros((lead_pad, D_MODEL), f32), meta_tokens.astype(f32)], 0)
    xm = x_prompt.reshape(rows_main, D_MODEL)

    tab_m = _rope_tables(jnp.arange(CHUNK, dtype=f32) - lead_pad)
    tab_p = _rope_tables(N_META + jnp.arange(seq, dtype=f32))
    consts_c = tuple(jnp.asarray(a, f32) for a in (_chunk_constants(CHUNK, BASE_BLOCK, 2)
                                                  + _chunk_constants(CHUNK, 0, N_HEADS)))
    row_vec = lambda v: v.reshape(1, -1).astype(f32)
    col_vec = lambda v: v.reshape(-1, 1).astype(f32)
    small_row = lambda v: jnp.zeros((1, LANE), f32).at[0, SM_ALPHA:SM_ALPHA + N_HEADS].set(v)
    n_chunks = seq // CHUNK

    seq_last = lambda s: jnp.transpose(s, (0, 2, 3, 4, 1))
    sd_t, sh_t, sg_t, sr_t = (seq_last(s) for s in (state_delta, state_hgrn, state_gla, state_ret))
    conv_t = jnp.transpose(state_delta_conv, (0, 2, 1, 3))
    half = HEAD_DIM // 2
    ang = ((PAST_LEN + jnp.arange(n_tok, dtype=f32))[:, None]
           * (1.0 / (RET_THETA_BASE ** jnp.linspace(0.0, 1.0, half, dtype=f32)))[None, :])
    cos_s = jnp.tile(jnp.cos(ang), (1, 4))[..., None]
    sin_s = jnp.tile(jnp.concatenate([-jnp.sin(ang), jnp.sin(ang)], -1), (1, 2))[..., None]
    vm = lambda shape: pltpu.VMEM(shape, f32)
    tok_scr = lambda: vm((n_tok, LANE, n_seq))

    w_in_t = jnp.swapaxes(w_in, 1, 2)
    st_p = None
    sa_s = sb_s = sc_s = sd_s = conv_s = None
    for l in range(DEPTH):
        pm, pt = _proj(xm, xt, row_vec(emb_ln_g), row_vec(emb_ln_b), w_in_t, l, with_ln=(l == 0))
        wg = jnp.zeros((LANE, N_HEADS * GLA_DK), f32).at[SM_LR:SM_LR + GLA_RANK].set(gla_w_gate[l])
        gains = jnp.stack([delta_norm_g[l], hgrn_norm_g[l], gla_norm_g[l], ret_norm_g[l]], 0)
        wts = (conv_w[l], small_row(delta_a_log[l]), small_row(delta_dt_bias[l]), hgrn_lb_logits,
               wg, row_vec(gla_b_gate[l]), gains)

        cfg = MixCfg(CHUNK, 1, lead_pad, CHUNK, l, 1, None, False, CHUNK, 0, False, 0)
        ot, *st_meta = _mixer_call(
            cfg, "mixers_meta", 1, pt, pl.BlockSpec((CHUNK, D_IN_AL), lambda b, i: (meta_blk, 0)),
            wts, tab_m, pl.BlockSpec((CHUNK, GROUP), lambda b, i: (0, 0)), consts_c, (), (),
            jax.ShapeDtypeStruct((rows_tail, D_MODEL), bf16),
            pl.BlockSpec((CHUNK, D_MODEL), lambda b, i: (meta_blk, 0)), 1)

        n_prev = 0 if st_p is None else len(st_p)
        ffn_w = (w_out, w_ffn_gate, w_ffn_up, w_ffn_down.reshape(DEPTH, D_MODEL, D_FF))
        cfg = MixCfg(CHUNK, pg, 0, CHUNK, l, n_chunks, "shared", True, pg * CHUNK, n_prev, True, len(ffn_w))
        st1 = lambda k: pl.BlockSpec((1, N_HEADS, k, HEAD_DIM), lambda b, i: (0, 0, 0, 0))
        om, *st_p = _mixer_call(
            cfg, "mixers_prompt", bsz // pg, pm.reshape(bsz // pg, pg, seq, D_IN_AL),
            pl.BlockSpec((1, pg, CHUNK, D_IN_AL), lambda b, i: (b, 0, i, 0)),
            wts, tab_p, pl.BlockSpec((CHUNK, GROUP), lambda b, i: (i, 0)), consts_c, st_meta,
            [pl.BlockSpec((1, CONV_W - 1, 3 * GROUP), lambda b, i: (0, 0, 0)),
             st1(HEAD_DIM), st1(HEAD_DIM), st1(GLA_DK), st1(HEAD_DIM)],
            jax.ShapeDtypeStruct((bsz // pg, pg, seq, D_MODEL), bf16),
            pl.BlockSpec((1, pg, CHUNK, D_MODEL), lambda b, i: (b, 0, i, 0)), bsz, states_prev=st_p, cast=ffn_w)
        om = om.reshape(rows_main, D_MODEL)
        st_p, (wo_b, wg_b, wu_b, wd_b) = st_p[:5], st_p[5:]

        pair_col = lambda v: (col_vec(v), pl.BlockSpec((LANE, 1), lambda p: (p, 0)))
        full = lambda a: (a, pl.BlockSpec(a.shape, lambda p: (0,) * a.ndim))
        split = lambda pairs: ([a for a, _ in pairs], [s for _, s in pairs])
        hist = lambda part: (conv_t, pl.BlockSpec((1, CONV_W - 1, n_seq, LANE), lambda p: (l, 0, 0, 2 * part + p)))
        cw_t = conv_w[l].T
        cw = lambda part: (cw_t, pl.BlockSpec((LANE, CONV_W), lambda p: (2 * part + p, 0)))
        small_col = lambda v: jnp.zeros((LANE, 1), f32).at[SM_ALPHA:SM_ALPHA + N_HEADS, 0].set(v)
        extra, especs = split([hist(0), hist(1), hist(2), cw(0), cw(1), cw(2), full(small_col(delta_a_log[l])),
                               full(small_col(delta_dt_bias[l])), pair_col(delta_norm_g[l])])
        ot, sa_s, *conv_s = _bl_call(
            functools.partial(_bl_delta_body, n_tok), "sample_delta", 0, n_tok, n_seq, s_blk, pt,
            [(A_QKV, True), (A_QKV + GROUP, True), (A_QKV + 2 * GROUP, True), (A_GATE, True), (SMALL, False)],
            extra, especs, sd_t, HEAD_DIM, l, ot, sa_s,
            [tok_scr(), tok_scr(), tok_scr(), vm((n_tok, HEAD_DIM, n_seq)), vm((n_tok, SUBLANE, n_seq))],
            conv_prev=conv_s or [])
        extra, especs = split([(hgrn_lb_logits.T, pl.BlockSpec((LANE, DEPTH), lambda p: (p, 0))),
                               pair_col(hgrn_norm_g[l])])
        ot, sb_s = _bl_call(
            functools.partial(_bl_hgrn_body, l, n_tok), "sample_hgrn", 1, n_tok, n_seq, s_blk, pt,
            [(B_Q, True), (B_F, True), (B_I, True), (B_GATE, True)], extra, especs, sh_t, HEAD_DIM, l, ot, sb_s,
            [tok_scr() for _ in range(4)])
        wg_t = jnp.zeros((N_HEADS * GLA_DK, LANE), f32).at[:, SM_LR:SM_LR + GLA_RANK].set(gla_w_gate[l].T)
        extra, especs = split([full(wg_t), full(col_vec(gla_b_gate[l])), pair_col(gla_norm_g[l])])
        ot, sc_s = _bl_call(
            functools.partial(_bl_gla_body, n_tok), "sample_gla", 2, n_tok, n_seq, s_blk, pt,
            [(C_Q, False), (C_K, False), (C_V, True), (C_GATE, True), (SMALL, False)], extra, especs,
            sg_t, GLA_DK, l, ot, sc_s, [tok_scr() for _ in range(4)])
        extra, especs = split([full(cos_s), full(sin_s), pair_col(ret_norm_g[l])])
        ot, sd_s = _bl_call(
            functools.partial(_bl_ret_body, n_tok), "sample_ret", 3, n_tok, n_seq, s_blk, pt,
            [(D_Q, True), (D_K, True), (D_V, True), (D_GATE, True)], extra, especs, sr_t, HEAD_DIM, l, ot, sd_s,
            [tok_scr() for _ in range(3)])

        xm, xt = _post(xm, xt, om, ot, row_vec(emb_ln_g), row_vec(emb_ln_b),
                       row_vec(ln1_g[l]), row_vec(ln1_b[l]), row_vec(ln2_g[l]), row_vec(ln2_b[l]),
                       wo_b, wg_b, wu_b, wd_b.reshape(D_FF, D_MODEL), l)

    y_prompt = xm.reshape(bsz, seq, D_MODEL)
    y_sample = jnp.swapaxes(xt[0:rows_s].reshape(n_tok, n_seq, D_MODEL), 0, 1)
    seq_first = lambda s: jnp.transpose(s, (0, 4, 1, 2, 3))
    st_s = (jnp.transpose(jnp.concatenate(conv_s, -1), (0, 2, 1, 3)),
            seq_first(sa_s), seq_first(sb_s), seq_first(sc_s), seq_first(sd_s))
    return (y_prompt, y_sample) + tuple(v for pair in zip(st_p, st_s) for v in pair)
```

```python
import collections
import functools
import math

import numpy as np
import jax
import jax.numpy as jnp
from jax import lax
from jax.experimental import pallas as pl
from jax.experimental.pallas import tpu as pltpu

f32 = jnp.float32
bf16 = jnp.bfloat16

D_MODEL = 1024
N_META = 16
CHUNK = 64
GROUP = 256
N_HEADS = 4
HEAD_DIM = 64
GLA_DK = 32
GLA_RANK = 16
GLA_GATE_NORM = 16.0
CONV_W = 4
D_FF = 2816
DEPTH = 2
ALPHA = (2 * DEPTH) ** 0.25
PAST_LEN = 16384
RET_THETA_BASE = 10000.0
LN_EPS = 1e-5
NORM_EPS = 1e-6
GATE_CLAMP = 1e-6
IN_SPLITS = (GROUP, GROUP, GROUP, N_HEADS, N_HEADS, GROUP,
             GROUP, GROUP, GROUP, GROUP,
             N_HEADS * GLA_DK, N_HEADS * GLA_DK, GROUP, GLA_RANK, GROUP,
             GROUP, GROUP, GROUP, GROUP)

LANE = 128
SUBLANE = 8
VMEM_LIMIT = 56 * 1024 * 1024

COL_ORDER = (0, 1, 2, 5, 6, 7, 8, 9, 10, 11, 12, 14, 15, 16, 17, 18)
A_QKV, A_GATE = 0, 768
B_Q, B_F, B_I, B_GATE = 1024, 1280, 1536, 1792
C_Q, C_K, C_V, C_GATE = 2048, 2176, 2304, 2560
D_Q, D_K, D_V, D_GATE = 2816, 3072, 3328, 3584
SMALL = 3840
SM_BETA, SM_ALPHA, SM_LR = 0, N_HEADS, 2 * N_HEADS
D_IN_AL = SMALL + LANE
SILU_COLS = (A_GATE, B_Q, B_GATE, C_GATE, D_GATE)
CONV_HIST = SUBLANE
ROW_TILE = 512
PROJ_STAGE_ROWS = 256
PROMPT_GROUP = 8


def _layer_norm(x, g, b):
    mu = jnp.mean(x, -1, keepdims=True)
    xc = x - mu
    var = jnp.mean(xc * xc, -1, keepdims=True)
    return xc * lax.rsqrt(var + LN_EPS) * g + b


def _row_specs(width, n_main, rows_tail):
    return (pl.BlockSpec((ROW_TILE, width), lambda i: (jnp.minimum(i, n_main - 1), 0)),
            pl.BlockSpec((rows_tail, width), lambda i: (0, 0)))


def _const_spec(shape, single_buffer=False):
    nd = len(shape)
    kw = dict(pipeline_mode=pl.Buffered(1)) if single_buffer else {}
    return pl.BlockSpec(shape, lambda *_: (0,) * nd, **kw)


def _on_region(n_main, main_fn, tail_fn):
    i = pl.program_id(0)
    pl.when(i < n_main)(main_fn)
    pl.when(i >= n_main)(tail_fn)


def _w_in_segments():
    offs = np.concatenate([[0], np.cumsum(IN_SPLITS)])
    runs, dst = [], 0
    for i in COL_ORDER + (3, 4, 13):
        if runs and runs[-1][1] == offs[i]:
            runs[-1][1] = int(offs[i + 1])
        else:
            runs.append([int(offs[i]), int(offs[i + 1]), dst])
        dst += IN_SPLITS[i]
    return [tuple(r) for r in runs]


def _proj_tile(with_ln, x_ref, g_ref, b_ref, w_ref, p_ref):
    x = x_ref[...]
    if with_ln:
        x = _layer_norm(x, g_ref[...], b_ref[...])
    h = x.astype(bf16)
    n = w_ref.shape[1]
    for n0 in range(0, n, GROUP):
        n1 = min(n0 + GROUP, n)
        p = jnp.dot(h, w_ref[:, n0:n1], preferred_element_type=f32)
        p_ref[:, n0:n1] = jax.nn.silu(p) if n0 in SILU_COLS else p


def _stage_w_in(wt_hbm, layer, w_scr, stage, sem):
    rows = PROJ_STAGE_ROWS
    big, small = [], []
    for s0, s1, d0 in _w_in_segments():
        if (s1 - s0) % rows == 0 and d0 % LANE == 0:
            big += [(s0 + j, d0 + j) for j in range(0, s1 - s0, rows)]
        else:
            small.append((s0, s1 - s0, d0 - SMALL))
    assert all(0 <= d and d + n <= LANE and d % SUBLANE == 0 for _, n, d in small) and len(small) <= 2

    def copy(j):
        return pltpu.make_async_copy(wt_hbm.at[layer, pl.ds(big[j][0], rows), :], stage.at[j % 2], sem.at[j % 2])
    copy(0).start()
    for j in range(len(big)):
        if j + 1 < len(big):
            copy(j + 1).start()
        copy(j).wait()
        w_scr[:, big[j][1]:big[j][1] + rows] = stage[j % 2].T.astype(bf16)
    used = max(d + n for _, n, d in small)
    stage[0, used:LANE, :] = jnp.zeros((LANE - used, D_MODEL), f32)
    copies = [pltpu.make_async_copy(wt_hbm.at[layer, pl.ds(s0, n), :], stage.at[0, pl.ds(d, n), :], sem.at[i])
              for i, (s0, n, d) in enumerate(small)]
    for cp in copies:
        cp.start()
    for cp in copies:
        cp.wait()
    w_scr[:, SMALL:SMALL + LANE] = stage[0, 0:LANE, :].T.astype(bf16)


def _proj_body(with_ln, layer, n_main, xm_ref, xt_ref, g_ref, b_ref, wt_hbm, pm_ref, pt_ref, w_scr, stage, sem):
    pl.when(pl.program_id(0) == 0)(lambda: _stage_w_in(wt_hbm, layer, w_scr, stage, sem))
    _on_region(n_main,
               lambda: _proj_tile(with_ln, xm_ref, g_ref, b_ref, w_scr, pm_ref),
               lambda: _proj_tile(with_ln, xt_ref, g_ref, b_ref, w_scr, pt_ref))


def _proj(xm, xt, g, b, w_in_t, layer, with_ln):
    n_main, rows_tail = xm.shape[0] // ROW_TILE, xt.shape[0]
    n = D_IN_AL
    out_shape = [jax.ShapeDtypeStruct((xm.shape[0], n), f32), jax.ShapeDtypeStruct((rows_tail, n), f32)]
    out_specs = list(_row_specs(n, n_main, rows_tail))
    return pl.pallas_call(
        functools.partial(_proj_body, with_ln, layer, n_main),
        out_shape=out_shape,
        grid=(n_main + 1,),
        in_specs=list(_row_specs(D_MODEL, n_main, rows_tail))
                 + [_const_spec((1, D_MODEL)), _const_spec((1, D_MODEL)), pl.BlockSpec(memory_space=pl.ANY)],
        out_specs=out_specs,
        scratch_shapes=[pltpu.VMEM((D_MODEL, n), bf16),
                        pltpu.VMEM((2, PROJ_STAGE_ROWS, D_MODEL), f32),
                        pltpu.SemaphoreType.DMA((2,))],
        compiler_params=pltpu.CompilerParams(dimension_semantics=("arbitrary",),
                                             vmem_limit_bytes=VMEM_LIMIT),
        name="in_proj",
    )(xm, xt, g, b, w_in_t)


def _post_tile(emb_ln, x_ref, o_ref, ge_ref, be_ref, wo_ref, g1_ref, b1_ref, wg_ref, wu_ref, wd_ref, g2_ref, b2_ref,
               y_ref):
    x = x_ref[...]
    if emb_ln:
        x = _layer_norm(x, ge_ref[...], be_ref[...])
    m = jnp.dot(o_ref[...], wo_ref[...], preferred_element_type=f32)
    y1 = _layer_norm(ALPHA * x + m, g1_ref[...], b1_ref[...])
    y1b = y1.astype(bf16)
    dff = wg_ref.shape[1]
    step = 4 * LANE
    acc = jnp.zeros(y1.shape, f32)
    for n0 in range(0, dff, step):
        n1 = min(n0 + step, dff)
        gate = jnp.dot(y1b, wg_ref[:, n0:n1], preferred_element_type=f32)
        up = jnp.dot(y1b, wu_ref[:, n0:n1], preferred_element_type=f32)
        act = (jax.nn.silu(gate) * up).astype(bf16)
        acc = acc + jnp.dot(act, wd_ref[n0:n1, :], preferred_element_type=f32)
    y_ref[...] = _layer_norm(ALPHA * y1 + acc, g2_ref[...], b2_ref[...])


def _post_body(layer, n_main, xm_ref, xt_ref, om_ref, ot_ref, ge_ref, be_ref, g1_ref, b1_ref, g2_ref, b2_ref,
               wo_ref, wg_ref, wu_ref, wd_ref, ym_ref, yt_ref):
    w = (ge_ref, be_ref, wo_ref, g1_ref, b1_ref, wg_ref, wu_ref, wd_ref, g2_ref, b2_ref)
    _on_region(n_main, lambda: _post_tile(layer == 0, xm_ref, om_ref, *w, ym_ref),
               lambda: _post_tile(layer == 0, xt_ref, ot_ref, *w, yt_ref))


def _post(xm, xt, om, ot, ge, be, g1, b1, g2, b2, w_out, w_gate, w_up, w_down, layer):
    n_main = xm.shape[0] // ROW_TILE
    rows = lambda: list(_row_specs(D_MODEL, n_main, xt.shape[0]))
    vec = lambda: _const_spec((1, D_MODEL))
    return pl.pallas_call(
        functools.partial(_post_body, layer, n_main),
        out_shape=[jax.ShapeDtypeStruct(xm.shape, f32), jax.ShapeDtypeStruct(xt.shape, f32)],
        grid=(n_main + 1,),
        in_specs=rows() + rows() + [vec() for _ in range(6)]
                 + [_const_spec(w.shape, True) for w in (w_out, w_gate, w_up, w_down)],
        out_specs=rows(),
        compiler_params=pltpu.CompilerParams(dimension_semantics=("arbitrary",),
                                             vmem_limit_bytes=VMEM_LIMIT),
        name="post_ffn",
    )(xm, xt, om, ot, ge, be, g1, b1, g2, b2, w_out, w_gate, w_up, w_down)


BASE_BLOCK = 8
assert (BASE_BLOCK // 2) * -math.log(GATE_CLAMP) < 80.0


def _levels(c, base=0):
    m, out = c // 2, []
    while m >= max(base, 1):
        out.append(m)
        m //= 2
    return tuple(out)


def _chunk_constants(c, base, nblk):
    t = np.arange(c)
    rows = [(t[None, :] <= t[:, None]).astype(np.float32), (t[None, :] > t[:, None]).astype(np.float32)]
    masks = []
    for m in _levels(c, base):
        blk = t // m
        upper = (blk % 2) == 1
        lo = np.where(upper, blk * m, t + 1)
        hi = np.where(upper, t, (blk + 1) * m - 1)
        rows.append(((t[None, :] >= lo[:, None]) & (t[None, :] <= hi[:, None])).astype(np.float32))
        same = (t[:, None] // (2 * m)) == (t[None, :] // (2 * m))
        masks.append((same & upper[:, None] & (~upper)[None, :]).astype(np.float32))
    if base:
        mid = (t // base) * base + base // 2 - 1
        after = (t[None, :] > mid[:, None]) & (t[None, :] <= t[:, None])
        before = (t[None, :] > t[:, None]) & (t[None, :] <= mid[:, None])
        rows.append(after.astype(np.float32) - before.astype(np.float32))
        masks.append(((t[:, None] // base == t[None, :] // base) & (t[None, :] <= t[:, None])).astype(np.float32))
    else:
        masks.append(np.eye(c, dtype=np.float32))
    return np.concatenate(rows, 0), np.tile(np.stack(masks, 0), (1, 1, nblk))


def _mm(a, b):
    return jnp.einsum('gmk,gkn->gmn', a.astype(bf16), b.astype(bf16), preferred_element_type=f32)


def _mm_nt(a, b):
    return jnp.einsum('gmk,gnk->gmn', a.astype(bf16), b.astype(bf16), preferred_element_type=f32)


def _mm_tn(a, b):
    return lax.dot_general(a.astype(bf16), b.astype(bf16), (((1,), (1,)), ((0,), (0,))),
                           preferred_element_type=f32)


def _iota(shape, dim):
    return lax.broadcasted_iota(jnp.int32, shape, dim)


def _bd(z, nblk, blk=None):
    z = z.astype(bf16)
    if blk is None:
        blk = _iota((1, 1, z.shape[-1]), 2) // (z.shape[-1] // nblk)
    return jnp.concatenate([jnp.where(blk == h, z, jnp.zeros_like(z)) for h in range(nblk)], axis=1)


def _hmm(x, z, nblk):
    return _mm(x, _bd(z, nblk))


def _hmm_nt(x, z, nblk):
    return _mm_nt(x, _bd(z, nblk))


def _hmm_split(x, z, nblk, blk=None):
    m = x.shape[1]
    x_hi, z_hi = x.astype(bf16), z.astype(bf16)
    x_lo = (x - x_hi.astype(f32)).astype(bf16)
    z_lo = (z - z_hi.astype(f32)).astype(bf16)
    both = _mm(jnp.concatenate([x_hi, x_lo], 1), _bd(z_hi, nblk, blk))
    return both[:, 0:m] + (both[:, m:2 * m] + _mm(x_hi, _bd(z_lo, nblk, blk)))


def _block_mask(rows, lanes, dr, dl):
    return (_iota((rows, lanes), 0) // dr) == (_iota((rows, lanes), 1) // dl)


def _dot01(m01, x):
    nb = x.shape[0]
    x2 = jnp.concatenate([x[i] for i in range(nb)], -1) if nb > 1 else x[0]
    hi = x2.astype(bf16)
    lo = (x2 - hi.astype(f32)).astype(bf16)
    m01 = m01.astype(bf16)
    d = lambda y: jnp.dot(m01, y, preferred_element_type=f32)
    out = d(hi) + d(lo)
    w = x.shape[-1]
    return jnp.stack([out[:, i * w:(i + 1) * w] for i in range(nb)], 0)


def _seg_sum(x, ones_bd, pieces=2):
    nb, c, w = x.shape
    x2 = x.reshape(nb * c, w)
    hi = x2.astype(bf16)
    out = jnp.dot(hi, ones_bd, preferred_element_type=f32)
    if pieces == 2:
        lo = (x2 - hi.astype(f32)).astype(bf16)
        out = out + jnp.dot(lo, ones_bd, preferred_element_type=f32)
    return out.reshape(nb, c, w)


def _last_row_as_col(b, c):
    return jnp.stack([b[i, c - SUBLANE:c, :].T[:, SUBLANE - 1:SUBLANE] for i in range(b.shape[0])], 0)


def _channel_decay_mixer(q, k, v, gl, s, nblk, sums_ref, masks_ref, bdmask, c, base):
    lv = _levels(c, base)
    nl = len(lv)
    sums = _dot01(sums_ref[...], gl)
    b = sums[:, 0:c]
    rev = sums[:, c:2 * c]
    row = _iota((1, c, 1), 1)
    if base:
        e0 = sums[:, (nl + 2) * c:(nl + 3) * c]
        att = jnp.where(masks_ref[nl] > 0.0, _hmm_nt(q * jnp.exp(e0), k * jnp.exp(-e0), nblk), 0.0)
    else:
        att = masks_ref[nl] * _hmm_nt(q, k, nblk)
    for li, m in enumerate(lv):
        upper = ((row // m) % 2) == 1
        w = jnp.where(upper, q, k) * jnp.exp(sums[:, (li + 2) * c:(li + 3) * c])
        att = att + masks_ref[li] * _hmm_nt(w, w, nblk)
    o = _mm(jnp.concatenate([q * jnp.exp(b), att], -1), jnp.concatenate([s.astype(bf16), _bd(v, nblk)], 1))
    s_new = jnp.exp(_last_row_as_col(b, c)) * s + jnp.where(bdmask, _mm_tn(k * jnp.exp(rev), v), 0.0)
    return o, s_new


MixCfg = collections.namedtuple("MixCfg", "c g r0 r1 layer n_chunks init p4d o_rows n_alias stacked cast_slabs")


def _pairs_to_bd(s):
    z = jnp.zeros(s.shape[:1] + (HEAD_DIM, HEAD_DIM), f32)
    pair = lambda p: jnp.concatenate([jnp.concatenate([s[:, 2 * p], z], -1),
                                      jnp.concatenate([z, s[:, 2 * p + 1]], -1)], 1)
    return jnp.concatenate([pair(0), pair(1)], 0)


def _quad_to_bd(s):
    z = lambda n: jnp.zeros(s.shape[:1] + (GLA_DK, n * HEAD_DIM), f32)
    rows = []
    for h in range(N_HEADS):
        parts = ([z(h)] if h else []) + [s[:, h]] + ([z(N_HEADS - 1 - h)] if h < N_HEADS - 1 else [])
        rows.append(jnp.concatenate(parts, -1))
    return jnp.concatenate(rows, 1)


def _mixer_body(cfg, *refs):
    c, g = cfg.c, cfg.g
    n_w, n_t, n_c = 7, 3, 4
    p_ref = refs[0]
    cw_ref, alog_ref, dtb_ref, lbl_ref, wg_ref, bg_ref, gains_ref = refs[1:1 + n_w]
    cos_ref, sina_ref, sinb_ref = refs[1 + n_w:1 + n_w + n_t]
    sumsb_ref, masksb_ref, sumsc_ref, masksc_ref = refs[1 + n_w + n_t:1 + n_w + n_t + n_c]
    pos = 1 + n_w + n_t + n_c
    if cfg.init:
        convi_ref, sai_ref, sbi_ref, sci_ref, sdi_ref = refs[pos:pos + 5]
        pos += 5
    n_cast = len(cfg.cast_slabs)
    cast_in = refs[pos:pos + n_cast]
    pos += n_cast + cfg.n_alias
    o_ref, convo_ref, sao_ref, sbo_ref, sco_ref, sdo_ref = refs[pos:pos + 6]
    if cfg.stacked:
        convo_ref, sao_ref, sbo_ref, sco_ref, sdo_ref = (r.at[0] for r in (convo_ref, sao_ref, sbo_ref, sco_ref, sdo_ref))
    cast_out = refs[pos + 6:pos + 6 + n_cast]
    cbuf, sa_scr, sb_scr, sc_scr, sd_scr = refs[pos + 6 + n_cast:]
    ci = pl.program_id(1)
    step = pl.program_id(0) * cfg.n_chunks + ci
    for w_f32, w_bf16, n_slabs in zip(cast_in, cast_out, cfg.cast_slabs):
        def narrow(src=w_f32, dst=w_bf16):
            dst[...] = src[0].astype(bf16)
        pl.when(step < n_slabs)(narrow)

    @pl.when(ci == 0)
    def _():
        cbuf[:, 0:CONV_HIST, :] = jnp.zeros((g, CONV_HIST, 3 * GROUP), f32)
        if not cfg.init:
            for r in (sa_scr, sb_scr, sc_scr, sd_scr):
                r[...] = jnp.zeros(r.shape, f32)
        else:
            take = lambda r: jnp.broadcast_to(r[...], (g,) + r.shape[1:])
            cbuf[:, CONV_HIST - (CONV_W - 1):CONV_HIST, :] = take(convi_ref)
            sa_scr[...] = _pairs_to_bd(take(sai_ref))
            sb_scr[...] = _pairs_to_bd(take(sbi_ref))
            sc_scr[...] = _quad_to_bd(take(sci_ref))
            sd_scr[...] = _pairs_to_bd(take(sdi_ref))

    def load(a, b):
        if cfg.p4d:
            return p_ref[0, :, :, a:b]
        return p_ref[:, a:b].reshape(g, c, b - a)

    r0, r1 = cfg.r0, cfg.r1
    n_valid = float(r1 - r0)
    row = _iota((1, c, 1), 1)
    vm = ((row >= r0) & (row < r1)).astype(f32)
    lane_v = _iota((1, 1, 2 * HEAD_DIM), 2)
    lane_s = _iota((1, 1, 2 * c), 2)
    s_idx = lane_s % c
    incl = s_idx <= row
    strict = s_idx < row
    bd2 = _block_mask(2 * HEAD_DIM, 2 * HEAD_DIM, HEAD_DIM, HEAD_DIM)
    bd4 = _block_mask(N_HEADS * GLA_DK, GROUP, GLA_DK, HEAD_DIM)
    ones2 = bd2.astype(bf16)
    pairs = lambda x: jnp.concatenate([x[..., :LANE], x[..., LANE:]], 0)
    unpairs = lambda x: jnp.concatenate([x[:g], x[g:]], -1)
    n_solve = max(1, math.ceil(math.log2(r1 - r0)))
    sm = load(SMALL, SMALL + LANE)
    valid = (lambda x: x) if (r0 == 0 and r1 == c) else (lambda x: x * vm)

    cbuf[:, CONV_HIST:CONV_HIST + c, :] = valid(load(A_QKV, A_QKV + 3 * GROUP))
    acc = cbuf[:, pl.ds(CONV_HIST - 3, c), :] * cw_ref[0:1, :]
    for j in range(1, CONV_W):
        acc = acc + cbuf[:, pl.ds(CONV_HIST - 3 + j, c), :] * cw_ref[j:j + 1, :]
    conv = jax.nn.silu(acc)
    beta = valid(jax.nn.sigmoid(sm))
    g_a = valid(-jnp.exp(alog_ref[...]) * jax.nn.softplus(sm + dtb_ref[...]))
    b_a = _dot01(sumsc_ref[0:c, :], g_a)

    def head_cols(x, lane0):
        col = lambda h: x[..., lane0 + h:lane0 + h + 1]
        return jnp.concatenate([col(0), col(2)], 0), jnp.concatenate([col(1), col(3)], 0)

    def spread(c0, c1, lane, width):
        return jnp.where(lane < width, c0, c1)

    bc0, bc1 = head_cols(b_a, SM_ALPHA)
    bt0, bt1 = head_cols(beta, SM_BETA)
    bc_v, bet_v = spread(bc0, bc1, lane_v, HEAD_DIM), spread(bt0, bt1, lane_v, HEAD_DIM)
    bc_s, bet_s = spread(bc0, bc1, lane_s, c), spread(bt0, bt1, lane_s, c)
    b_rows = [b_a[i].T for i in range(g)]
    b_row = lambda h: jnp.stack([t[SM_ALPHA + h:SM_ALPHA + h + 1, :] for t in b_rows], 0)
    br_s = jnp.concatenate([jnp.concatenate([b_row(0), b_row(1)], -1),
                            jnp.concatenate([b_row(2), b_row(3)], -1)], 0)
    dec = jnp.where(incl, jnp.exp(jnp.where(incl, bc_s - br_s, 0.0)), 0.0)
    q = pairs(conv[..., 0:GROUP])
    k = pairs(conv[..., GROUP:2 * GROUP])
    v = pairs(valid(conv[..., 2 * GROUP:3 * GROUP]))
    q = q * lax.rsqrt(_seg_sum(q * q, ones2) + NORM_EPS) * HEAD_DIM ** -0.5
    k = valid(k * lax.rsqrt(_seg_sum(k * k, ones2) + NORM_EPS))
    s_a = sa_scr[...]
    kq = jnp.concatenate([k, q], 1)
    kq_k = _hmm_nt(kq, k, 2)
    kq_s = _mm(kq, s_a)
    a_mat = jnp.where(strict, bet_s * dec * kq_k[:, 0:c], 0.0)
    gam = jnp.exp(bc_v)
    u = bet_v * (v - gam * kq_s[:, 0:c])
    p = -a_mat
    lane_pu = _iota((1, 1, 2 * c + 2 * HEAD_DIM), 2)
    head_pu = jnp.where(lane_pu < 2 * c, lane_pu // c, (lane_pu - 2 * c) // HEAD_DIM)
    for i in range(n_solve):
        if i + 1 < n_solve:
            pu = _hmm_split(p, jnp.concatenate([p, u], -1), 2, head_pu)
            p, u = pu[..., 0:2 * c], u + pu[..., 2 * c:]
        else:
            u = u + _hmm_split(p, u, 2)
    o_a = gam * kq_s[:, c:2 * c] + _hmm(dec * kq_k[:, c:2 * c], u, 2)
    bl = bc_v[:, c - 1:c, :]
    sa_scr[...] = jnp.exp(bl) * s_a + jnp.where(bd2, _mm_tn(k * jnp.exp(bl - bc_v), u), 0.0)
    o_a = unpairs(o_a * lax.rsqrt(_seg_sum(o_a * o_a, ones2, 1) * (1.0 / HEAD_DIM) + NORM_EPS))

    lbl = lbl_ref[...]
    e = jnp.exp(lbl - jnp.max(lbl, axis=0, keepdims=True))
    prob = e / jnp.sum(e, axis=0, keepdims=True)
    lb = (jnp.sum(prob[1:cfg.layer + 1], axis=0, keepdims=True) if cfg.layer > 0
          else jnp.zeros((1, GROUP), f32))
    kb = (1.0 - lb) * jax.nn.sigmoid(-load(B_F, B_F + GROUP))
    g_b = valid(jnp.log1p(-jnp.clip(kb, 0.0, 1.0 - GATE_CLAMP)))
    o_b, s_b_new = _channel_decay_mixer(
        pairs(load(B_Q, B_Q + GROUP)), pairs(valid(kb)), pairs(valid(load(B_I, B_I + GROUP))),
        pairs(g_b), sb_scr[...], 2, sumsb_ref, masksb_ref, bd2, c, BASE_BLOCK if c > BASE_BLOCK else 0)
    sb_scr[...] = s_b_new
    o_b = unpairs(o_b * lax.rsqrt(_seg_sum(o_b * o_b, ones2, 1) * (1.0 / HEAD_DIM) + NORM_EPS))

    logit = jnp.dot(sm.reshape(g * c, LANE).astype(bf16), wg_ref[...].astype(bf16),
                    preferred_element_type=f32).reshape(g, c, LANE) + bg_ref[...]
    g_c = valid(jax.nn.log_sigmoid(logit) * (1.0 / GLA_GATE_NORM))
    o_c, s_c_new = _channel_decay_mixer(
        load(C_Q, C_Q + N_HEADS * GLA_DK) * GLA_DK ** -0.5, valid(load(C_K, C_K + N_HEADS * GLA_DK)),
        valid(load(C_V, C_V + GROUP)), g_c, sc_scr[...], N_HEADS, sumsc_ref, masksc_ref, bd4, c, 0)
    sc_scr[...] = s_c_new
    ones4 = _block_mask(GROUP, GROUP, HEAD_DIM, HEAD_DIM).astype(bf16)
    o_c = o_c * lax.rsqrt(_seg_sum(o_c * o_c, ones4, 1) * (1.0 / HEAD_DIM) + NORM_EPS)

    def rope(x):
        x2 = x.reshape(g * c, GROUP)
        rot = lambda sh: pltpu.roll(x2, sh, 1).reshape(g, c, GROUP)
        return x * cos_ref[...] + rot(GROUP - HEAD_DIM // 2) * sina_ref[...] + rot(HEAD_DIM // 2) * sinb_ref[...]
    q = pairs(rope(load(D_Q, D_Q + GROUP)))
    k = pairs(valid(rope(load(D_K, D_K + GROUP)) * HEAD_DIM ** -0.5))
    v = pairs(valid(load(D_V, D_V + GROUP)))
    pair_idx = _iota((2, 1, 1), 0)
    log_gamma = lambda lane, width: jnp.log1p(-jnp.exp2(-5.0 - (2 * pair_idx + lane // width).astype(f32)))
    per_seq = lambda x: jnp.concatenate([jnp.broadcast_to(x[p:p + 1], (g,) + x.shape[1:]) for p in range(2)], 0)
    lg_v, lg_s = log_gamma(lane_v, HEAD_DIM), log_gamma(lane_s, c)
    cnt_c = jnp.clip(row + 1 - r0, 0, r1 - r0).astype(f32)
    cnt_s = jnp.clip(s_idx + 1 - r0, 0, r1 - r0).astype(f32)
    dec = per_seq(jnp.where(incl, jnp.exp(jnp.where(incl, (cnt_c - cnt_s) * lg_s, 0.0)), 0.0))
    s_d = sd_scr[...]
    o_d = _mm(jnp.concatenate([q * per_seq(jnp.exp(cnt_c * lg_v)), dec * _hmm_nt(q, k, 2)], -1),
              jnp.concatenate([s_d.astype(bf16), _bd(v, 2)], 1))
    sd_scr[...] = (per_seq(jnp.exp(n_valid * lg_v)) * s_d
                   + jnp.where(bd2, _mm_tn(k * per_seq(jnp.exp((n_valid - cnt_c) * lg_v)), v), 0.0))
    oc = o_d - _seg_sum(o_d, ones2) * (1.0 / HEAD_DIM)
    o_d = unpairs(oc * lax.rsqrt(_seg_sum(oc * oc, ones2, 1) * (1.0 / HEAD_DIM) + LN_EPS))

    for i, (o_m, gate_col) in enumerate(zip((o_a, o_b, o_c, o_d), (A_GATE, B_GATE, C_GATE, D_GATE))):
        val = (o_m * gains_ref[i:i + 1, :] * load(gate_col, gate_col + GROUP)).astype(o_ref.dtype)
        if cfg.p4d:
            o_ref[0, :, :, i * GROUP:(i + 1) * GROUP] = val
        else:
            o_ref[0:g * c, i * GROUP:(i + 1) * GROUP] = val.reshape(g * c, GROUP)
    if cfg.o_rows > g * c:
        o_ref[g * c:cfg.o_rows, :] = jnp.zeros((cfg.o_rows - g * c, D_MODEL), o_ref.dtype)
    cbuf[:, 0:CONV_HIST, :] = cbuf[:, c:c + CONV_HIST, :]

    @pl.when(ci == cfg.n_chunks - 1)
    def _():
        convo_ref[...] = cbuf[:, pl.ds(CONV_HIST + r1 - (CONV_W - 1), CONV_W - 1), :]
        for scr, out in ((sa_scr, sao_ref), (sb_scr, sbo_ref), (sd_scr, sdo_ref)):
            s = scr[...]
            for p in range(2):
                out[:, 2 * p] = s[p * g:(p + 1) * g, 0:HEAD_DIM, 0:HEAD_DIM]
                out[:, 2 * p + 1] = s[p * g:(p + 1) * g, HEAD_DIM:2 * HEAD_DIM, HEAD_DIM:2 * HEAD_DIM]
        s = sc_scr[...]
        for h in range(N_HEADS):
            sco_ref[:, h] = s[:, h * GLA_DK:(h + 1) * GLA_DK, h * HEAD_DIM:(h + 1) * HEAD_DIM]


def _state_shapes(n, stacked):
    lead = (DEPTH, n) if stacked else (n,)
    return [jax.ShapeDtypeStruct(lead + (CONV_W - 1, 3 * GROUP), f32),
            jax.ShapeDtypeStruct(lead + (N_HEADS, HEAD_DIM, HEAD_DIM), f32),
            jax.ShapeDtypeStruct(lead + (N_HEADS, HEAD_DIM, HEAD_DIM), f32),
            jax.ShapeDtypeStruct(lead + (N_HEADS, GLA_DK, HEAD_DIM), f32),
            jax.ShapeDtypeStruct(lead + (N_HEADS, HEAD_DIM, HEAD_DIM), f32)]


def _state_out_specs(g, layer, stacked):
    if stacked:
        st = lambda k: pl.BlockSpec((1, g, N_HEADS, k, HEAD_DIM), lambda b, i: (layer, b, 0, 0, 0))
        conv = pl.BlockSpec((1, g, CONV_W - 1, 3 * GROUP), lambda b, i: (layer, b, 0, 0))
    else:
        st = lambda k: pl.BlockSpec((g, N_HEADS, k, HEAD_DIM), lambda b, i: (b, 0, 0, 0))
        conv = pl.BlockSpec((g, CONV_W - 1, 3 * GROUP), lambda b, i: (b, 0, 0))
    return [conv, st(HEAD_DIM), st(HEAD_DIM), st(GLA_DK), st(HEAD_DIM)]


def _mixer_call(cfg, name, n_groups, p, p_spec, wts, tables, table_spec, consts, init, init_specs,
                o_shape, o_spec, n_state_rows, o_prev=None, states_prev=None, cast=()):
    c, g = cfg.c, cfg.g
    n_steps = n_groups * cfg.n_chunks
    wspecs =[_const_spec((CONV_W, 3 * GROUP)), _const_spec((1, LANE)), _const_spec((1, LANE)),
              _const_spec((DEPTH, GROUP)), _const_spec((LANE, LANE)), _const_spec((1, LANE)),
              _const_spec((4, GROUP))]
    cspecs = [_const_spec(a.shape) for a in consts]
    in_specs = [p_spec] + wspecs + [table_spec] * 3 + cspecs + list(init_specs)
    args = [p, *wts, *tables, *consts, *init]
    cast_shapes, cast_specs = [], []
    for w, n_slabs in zip(cast, cfg.cast_slabs):
        slab = w.shape[1] // n_slabs
        assert w.shape[1] % n_slabs == 0 and slab % (2 * SUBLANE) == 0 and n_slabs <= n_groups * cfg.n_chunks
        slab_idx = lambda b, i, n=n_slabs: jnp.minimum(b * cfg.n_chunks + i, n - 1)
        args.append(w)
        in_specs.append(pl.BlockSpec((1, slab, w.shape[2]), lambda b, i, f=slab_idx: (cfg.layer, f(b, i), 0)))
        cast_shapes.append(jax.ShapeDtypeStruct(w.shape[1:], bf16))
        cast_specs.append(pl.BlockSpec((slab, w.shape[2]), lambda b, i, f=slab_idx: (f(b, i), 0)))
    assert len(cast) == len(cfg.cast_slabs)
    aliases = {}
    if o_prev is not None:
        aliases[len(args)] = 0
        args.append(o_prev)
    for j, s_prev in enumerate(states_prev or ()):
        aliases[len(args)] = 1 + j
        args.append(s_prev)
    in_specs += [pl.BlockSpec(memory_space=pl.ANY)] * len(aliases)
    assert len(aliases) == cfg.n_alias
    return pl.pallas_call(
        functools.partial(_mixer_body, cfg),
        out_shape=[o_shape] + _state_shapes(n_state_rows, cfg.stacked) + cast_shapes,
        grid=(n_groups, cfg.n_chunks),
        in_specs=in_specs,
        out_specs=[o_spec] + _state_out_specs(g, cfg.layer, cfg.stacked) + cast_specs,
        scratch_shapes=[pltpu.VMEM((g, c + CONV_HIST, 3 * GROUP), f32),
                        pltpu.VMEM((2 * g, 2 * HEAD_DIM, 2 * HEAD_DIM), f32),
                        pltpu.VMEM((2 * g, 2 * HEAD_DIM, 2 * HEAD_DIM), f32),
                        pltpu.VMEM((g, N_HEADS * GLA_DK, GROUP), f32),
                        pltpu.VMEM((2 * g, 2 * HEAD_DIM, 2 * HEAD_DIM), f32)],
        input_output_aliases=aliases,
        compiler_params=pltpu.CompilerParams(dimension_semantics=("arbitrary", "arbitrary"),
                                             vmem_limit_bytes=VMEM_LIMIT),
        name=name,
    )(*args)


def _tok(ref, t, nb):
    return ref[t * nb:(t + 1) * nb, :].T


def _bcast_row(ref, t, r):
    return jnp.broadcast_to(ref[t, pl.ds(r, 1), :], (HEAD_DIM, ref.shape[-1]))


def _bl_diag_loop(n_tok, kdim, j, row0, s_in, s_out, d_scr, k_scr, q_scr, v_scr, d_const=None):
    nb = s_in.shape[-1]

    def body(k, os):
        s_k = s_in[0, j, k]
        r = row0 + k
        outs = []
        for t in range(n_tok):
            d = d_const if d_const is not None else _bcast_row(d_scr, t, r)
            s_k = d * s_k + _bcast_row(k_scr, t, r) * v_scr[t, j * HEAD_DIM:(j + 1) * HEAD_DIM, :]
            outs.append(os[t] + _bcast_row(q_scr, t, r) * s_k)
        s_out[0, j, k] = s_k
        return tuple(outs)
    return lax.fori_loop(0, kdim, body, (jnp.zeros((HEAD_DIM, nb), f32),) * n_tok)


def _bl_store(o_ref, os, gate_ref, gain_ref, n_tok, nb, group_norm):
    for t in range(n_tok):
        gate = _tok(gate_ref, t, nb)
        halves = []
        for j in range(2):
            o = os[j][t]
            if group_norm:
                o = o - jnp.mean(o, axis=0, keepdims=True)
                o = o * lax.rsqrt(jnp.mean(o * o, axis=0, keepdims=True) + LN_EPS)
            else:
                o = o * lax.rsqrt(jnp.mean(o * o, axis=0, keepdims=True) + NORM_EPS)
            hs = slice(j * HEAD_DIM, (j + 1) * HEAD_DIM)
            halves.append(o * gain_ref[hs, :] * gate[hs, :])
        o_ref[t * nb:(t + 1) * nb, :] = jnp.concatenate(halves, 0).T.astype(o_ref.dtype)


def _bl_hgrn_body(layer, n_tok, q_ref, f_ref, i_ref, gate_ref, lbl_ref, gain_ref, s_in, o_prev, s_prev,
                  o_ref, s_out, d_scr, k_scr, q_scr, v_scr):
    del o_prev, s_prev
    nb = s_in.shape[-1]
    lbl = lbl_ref[...]
    e = jnp.exp(lbl - jnp.max(lbl, axis=1, keepdims=True))
    prob = e / jnp.sum(e, axis=1, keepdims=True)
    lb = (jnp.sum(prob[:, 1:layer + 1], axis=1, keepdims=True) if layer > 0 else jnp.zeros((LANE, 1), f32))
    for t in range(n_tok):
        kk = (1.0 - lb) * jax.nn.sigmoid(-_tok(f_ref, t, nb))
        d_scr[t] = 1.0 - jnp.clip(kk, 0.0, 1.0 - GATE_CLAMP)
        k_scr[t] = kk
        q_scr[t] = _tok(q_ref, t, nb)
        v_scr[t] = _tok(i_ref, t, nb)
    os = [_bl_diag_loop(n_tok, HEAD_DIM, j, j * HEAD_DIM, s_in, s_out, d_scr, k_scr, q_scr, v_scr) for j in range(2)]
    _bl_store(o_ref, os, gate_ref, gain_ref, n_tok, nb, False)


def _bl_gla_body(n_tok, q_ref, k_ref, v_ref, gate_ref, small_ref, wg_ref, bg_ref, gain_ref, s_in, o_prev, s_prev,
                 o_ref, s_out, d_scr, k_scr, q_scr, v_scr):
    del o_prev, s_prev
    nb = s_in.shape[-1]
    for t in range(n_tok):
        logit = jnp.dot(wg_ref[...].astype(bf16), _tok(small_ref, t, nb).astype(bf16),
                        preferred_element_type=f32) + bg_ref[...]
        d_scr[t] = jnp.exp(jax.nn.log_sigmoid(logit) * (1.0 / GLA_GATE_NORM))
        k_scr[t] = _tok(k_ref, t, nb)
        q_scr[t] = _tok(q_ref, t, nb) * GLA_DK ** -0.5
        v_scr[t] = _tok(v_ref, t, nb)
    p = pl.program_id(0)
    os = [_bl_diag_loop(n_tok, GLA_DK, j, (2 * p + j) * GLA_DK, s_in, s_out, d_scr, k_scr, q_scr, v_scr)
          for j in range(2)]
    _bl_store(o_ref, os, gate_ref, gain_ref, n_tok, nb, False)


def _bl_ret_body(n_tok, q_ref, k_ref, v_ref, gate_ref, cos_ref, sin_ref, gain_ref, s_in, o_prev, s_prev,
                 o_ref, s_out, k_scr, q_scr, v_scr):
    del o_prev, s_prev
    nb = s_in.shape[-1]
    half = HEAD_DIM // 2

    def rope(x, t):
        swapped = jnp.concatenate([x[half:2 * half], x[0:half], x[3 * half:4 * half], x[2 * half:3 * half]], 0)
        return x * cos_ref[t] + swapped * sin_ref[t]
    for t in range(n_tok):
        q_scr[t] = rope(_tok(q_ref, t, nb), t)
        k_scr[t] = rope(_tok(k_ref, t, nb), t) * HEAD_DIM ** -0.5
        v_scr[t] = _tok(v_ref, t, nb)
    p = pl.program_id(0)
    os = []
    for j in range(2):
        head = jnp.zeros((1, nb), f32) + (2 * p + j).astype(f32)
        gamma = jnp.broadcast_to(1.0 - jnp.exp2(-5.0 - head), (HEAD_DIM, nb))
        os.append(_bl_diag_loop(n_tok, HEAD_DIM, j, j * HEAD_DIM, s_in, s_out, None, k_scr, q_scr, v_scr, gamma))
    _bl_store(o_ref, os, gate_ref, gain_ref, n_tok, nb, True)


def _bl_delta_body(n_tok, q_ref, k_ref, v_ref, gate_ref, small_ref, hq_ref, hk_ref, hv_ref, cwq_ref, cwk_ref, cwv_ref,
                   alog_ref, dtb_ref, gain_ref, s_in, o_prev, s_prev, cq_prev, ck_prev, cv_prev,
                   o_ref, s_out, cq_out, ck_out, cv_out, k_scr, q_scr, kh_scr, u_scr, bg_scr):
    del o_prev, s_prev, cq_prev, ck_prev, cv_prev
    nb = s_in.shape[-1]
    p = pl.program_id(0)

    def conv(x_ref, h_ref, cw_ref):
        seq = [h_ref[0, r].T for r in range(CONV_W - 1)] + [_tok(x_ref, t, nb) for t in range(n_tok)]
        outs = []
        for t in range(n_tok):
            acc = cw_ref[:, 0:1] * seq[t]
            for jj in range(1, CONV_W):
                acc = acc + cw_ref[:, jj:jj + 1] * seq[t + jj]
            outs.append(jax.nn.silu(acc))
        return outs

    def l2norm(x):
        parts = []
        for j in range(2):
            xj = x[j * HEAD_DIM:(j + 1) * HEAD_DIM]
            parts.append(xj * lax.rsqrt(jnp.sum(xj * xj, axis=0, keepdims=True) + NORM_EPS))
        return jnp.concatenate(parts, 0)

    qs, ks, vs = conv(q_ref, hq_ref, cwq_ref), conv(k_ref, hk_ref, cwk_ref), conv(v_ref, hv_ref, cwv_ref)
    for t in range(n_tok):
        q_scr[t] = l2norm(qs[t]) * HEAD_DIM ** -0.5
        k_scr[t] = l2norm(ks[t])
        sm = _tok(small_ref, t, nb)
        both = jnp.where(_iota((LANE, 1), 0) < SM_ALPHA, jax.nn.sigmoid(sm),
                         -jnp.exp(alog_ref[...]) * jax.nn.softplus(sm + dtb_ref[...]))
        bg_scr[t] = both[0:SUBLANE]
    for r in range(CONV_W - 1):
        rows = slice((n_tok - (CONV_W - 1) + r) * nb, (n_tok - (CONV_W - 1) + r + 1) * nb)
        cq_out[0, r] = q_ref[rows, :]
        ck_out[0, r] = k_ref[rows, :]
        cv_out[0, r] = v_ref[rows, :]

    os = []
    for j in range(2):
        hs = slice(j * HEAD_DIM, (j + 1) * HEAD_DIM)
        h = 2 * p + j
        beta = [bg_scr[t, pl.ds(SM_BETA + h, 1), :] for t in range(n_tok)]
        b, acc = [], jnp.zeros((1, nb), f32)
        for t in range(n_tok):
            acc = acc + bg_scr[t, pl.ds(SM_ALPHA + h, 1), :]
            b.append(acc)
        kt = [k_scr[t, hs, :] for t in range(n_tok)]
        qt = [q_scr[t, hs, :] for t in range(n_tok)]
        dot = lambda x, y: jnp.sum(x * y, axis=0, keepdims=True)

        def sweep(scr):
            def body(k, accs):
                s_k = s_in[0, j, k]
                return tuple(a + _bcast_row(scr, t, j * HEAD_DIM + k) * s_k for t, a in enumerate(accs))
            return lax.fori_loop(0, HEAD_DIM, body, (jnp.zeros((HEAD_DIM, nb), f32),) * n_tok)
        k_s, q_s = sweep(k_scr), sweep(q_scr)
        us = []
        for t in range(n_tok):
            u = beta[t] * (vs[t][hs] - jnp.exp(b[t]) * k_s[t])
            for s in range(t):
                u = u - (beta[t] * jnp.exp(b[t] - b[s]) * dot(kt[t], kt[s])) * us[s]
            us.append(u)
        o_h = []
        for t in range(n_tok):
            o = jnp.exp(b[t]) * q_s[t]
            for s in range(t + 1):
                o = o + (jnp.exp(b[t] - b[s]) * dot(qt[t], kt[s])) * us[s]
            o_h.append(o)
        os.append(o_h)
        for t in range(n_tok):
            kh_scr[t, hs, :] = kt[t] * jnp.exp(b[-1] - b[t])
            u_scr[t] = us[t]
        decay = jnp.broadcast_to(jnp.exp(b[-1]), (HEAD_DIM, nb))

        def update(k, carry):
            s_k = decay * s_in[0, j, k]
            for t in range(n_tok):
                s_k = s_k + _bcast_row(kh_scr, t, j * HEAD_DIM + k) * u_scr[t]
            s_out[0, j, k] = s_k
            return carry
        lax.fori_loop(0, HEAD_DIM, update, 0)
    _bl_store(o_ref, os, gate_ref, gain_ref, n_tok, nb, False)


def _bl_call(body, name, mixer, n_tok, nb, blk_row, pt, cols, extra, extra_specs, state_t, kdim, layer,
             o_prev, s_prev, scratch, conv_prev=None):
    def tok_spec(c0, per_pair):
        return pl.BlockSpec((n_tok * nb, LANE),
                            (lambda p: (blk_row, c0 // LANE + p)) if per_pair else (lambda p: (blk_row, c0 // LANE)))
    st_spec = pl.BlockSpec((1, 2, kdim, HEAD_DIM, nb), lambda p: (layer, p, 0, 0, 0))
    args = [pt] * len(cols) + list(extra) + [state_t]
    in_specs = [tok_spec(c0, pp) for c0, pp in cols] + list(extra_specs) + [st_spec]
    aliases = {}

    def donate(arr, out_idx):
        if arr is not None:
            aliases[len(args)] = out_idx
        args.append(arr if arr is not None else jnp.zeros((1,), f32))
        in_specs.append(pl.BlockSpec(memory_space=pl.ANY))
    donate(o_prev, 0)
    donate(s_prev, 1)
    out_shape = [jax.ShapeDtypeStruct(o_prev.shape, o_prev.dtype),
                 jax.ShapeDtypeStruct((DEPTH,) + state_t.shape[1:], f32)]
    out_specs = [pl.BlockSpec((n_tok * nb, LANE), lambda p: (blk_row, 2 * mixer + p)), st_spec]
    if conv_prev is not None:
        for i in range(3):
            donate(conv_prev[i] if conv_prev else None, 2 + i)
            out_shape.append(jax.ShapeDtypeStruct((DEPTH, CONV_W - 1, nb, GROUP), f32))
            out_specs.append(pl.BlockSpec((1, CONV_W - 1, nb, LANE), lambda p: (layer, 0, 0, p)))
    return pl.pallas_call(
        body, out_shape=out_shape, grid=(2,), in_specs=in_specs, out_specs=out_specs,
        scratch_shapes=scratch, input_output_aliases=aliases,
        compiler_params=pltpu.CompilerParams(dimension_semantics=("arbitrary",), vmem_limit_bytes=VMEM_LIMIT),
        name=name,
    )(*args)


def _rope_tables(pos):
    half = HEAD_DIM // 2
    inv = 1.0 / (RET_THETA_BASE ** jnp.linspace(0.0, 1.0, half, dtype=f32))
    ang = pos[:, None] * inv[None, :]
    cos, sin = jnp.cos(ang), jnp.sin(ang)
    zero = jnp.zeros_like(sin)
    tile = lambda a, b: jnp.tile(jnp.concatenate([a, b], -1), (1, N_HEADS))
    return tile(cos, cos), tile(-sin, zero), tile(zero, sin)


def kernel(x_prompt, x_sample, state_delta_conv, state_delta, state_hgrn, state_gla, state_ret, meta_tokens, emb_ln_g, emb_ln_b, w_in, conv_w, delta_a_log, delta_dt_bias, delta_norm_g, hgrn_lb_logits, hgrn_norm_g, gla_w_gate, gla_b_gate, gla_norm_g, ret_norm_g, w_out, ln1_g, ln1_b, w_ffn_gate, w_ffn_up, w_ffn_down, ln2_g, ln2_b):
    bsz, seq, _ = x_prompt.shape
    n_seq, n_tok, _ = x_sample.shape
    pg = PROMPT_GROUP if bsz % PROMPT_GROUP == 0 else 1
    rows_main = bsz * seq
    rows_s = n_tok * n_seq
    assert seq % CHUNK == 0 and rows_main % ROW_TILE == 0 and N_META <= CHUNK
    assert n_tok >= CONV_W - 1 and rows_s % CHUNK == 0 and (n_seq % LANE == 0 or n_seq < LANE)
    lead_pad = CHUNK - N_META
    rows_tail = rows_s + CHUNK
    s_blk = 0
    meta_blk = rows_s // CHUNK

    xs = jnp.swapaxes(x_sample, 0, 1).reshape(rows_s, D_MODEL)
    xt = jnp.concatenate([xs, jnp.zeros((lead_pad, D_MODEL), f32), meta_tokens.astype(f32)], 0)
    xm = x_prompt.reshape(rows_main, D_MODEL)

    tab_m = _rope_tables(jnp.arange(CHUNK, dtype=f32) - lead_pad)
    tab_p = _rope_tables(N_META + jnp.arange(seq, dtype=f32))
    consts_c = tuple(jnp.asarray(a, f32) for a in (_chunk_constants(CHUNK, BASE_BLOCK, 2)
                                                  + _chunk_constants(CHUNK, 0, N_HEADS)))
    row_vec = lambda v: v.reshape(1, -1).astype(f32)
    col_vec = lambda v: v.reshape(-1, 1).astype(f32)
    small_row = lambda v: jnp.zeros((1, LANE), f32).at[0, SM_ALPHA:SM_ALPHA + N_HEADS].set(v)
    n_chunks = seq // CHUNK

    seq_last = lambda s: jnp.transpose(s, (0, 2, 3, 4, 1))
    sd_t, sh_t, sg_t, sr_t = (seq_last(s) for s in (state_delta, state_hgrn, state_gla, state_ret))
    conv_t = jnp.transpose(state_delta_conv, (0, 2, 1, 3))
    half = HEAD_DIM // 2
    ang = ((PAST_LEN + jnp.arange(n_tok, dtype=f32))[:, None]
           * (1.0 / (RET_THETA_BASE ** jnp.linspace(0.0, 1.0, half, dtype=f32)))[None, :])
    cos_s = jnp.tile(jnp.cos(ang), (1, 4))[..., None]
    sin_s = jnp.tile(jnp.concatenate([-jnp.sin(ang), jnp.sin(ang)], -1), (1, 2))[..., None]
    vm = lambda shape: pltpu.VMEM(shape, f32)
    tok_scr = lambda: vm((n_tok, LANE, n_seq))

    w_in_t = jnp.swapaxes(w_in, 1, 2)
    st_p = None
    sa_s = sb_s = sc_s = sd_s = conv_s = None
    for l in range(DEPTH):
        pm, pt = _proj(xm, xt, row_vec(emb_ln_g), row_vec(emb_ln_b), w_in_t, l, with_ln=(l == 0))
        wg = jnp.zeros((LANE, N_HEADS * GLA_DK), f32).at[SM_LR:SM_LR + GLA_RANK].set(gla_w_gate[l])
        gains = jnp.stack([delta_norm_g[l], hgrn_norm_g[l], gla_norm_g[l], ret_norm_g[l]], 0)
        wts = (conv_w[l], small_row(delta_a_log[l]), small_row(delta_dt_bias[l]), hgrn_lb_logits,
               wg, row_vec(gla_b_gate[l]), gains)

        cfg = MixCfg(CHUNK, 1, lead_pad, CHUNK, l, 1, None, False, CHUNK, 0, False, ())
        ot, *st_meta = _mixer_call(
            cfg, "mixers_meta", 1, pt, pl.BlockSpec((CHUNK, D_IN_AL), lambda b, i: (meta_blk, 0)),
            wts, tab_m, pl.BlockSpec((CHUNK, GROUP), lambda b, i: (0, 0)), consts_c, (), (),
            jax.ShapeDtypeStruct((rows_tail, D_MODEL), bf16),
            pl.BlockSpec((CHUNK, D_MODEL), lambda b, i: (meta_blk, 0)), 1)

        n_prev = 0 if st_p is None else len(st_p)
        ffn_w = (w_out, w_ffn_gate, w_ffn_up, w_ffn_down)
        n_steps = (bsz // pg) * n_chunks
        slabs = tuple(max(n for n in range(1, n_steps + 1) if w.shape[1] % n == 0 and (w.shape[1] // n) % (2 * SUBLANE) == 0)
                      for w in ffn_w)
        cfg = MixCfg(CHUNK, pg, 0, CHUNK, l, n_chunks, "shared", True, pg * CHUNK, n_prev, True, slabs)
        st1 = lambda k: pl.BlockSpec((1, N_HEADS, k, HEAD_DIM), lambda b, i: (0, 0, 0, 0))
        om, *st_p = _mixer_call(
            cfg, "mixers_prompt", bsz // pg, pm.reshape(bsz // pg, pg, seq, D_IN_AL),
            pl.BlockSpec((1, pg, CHUNK, D_IN_AL), lambda b, i: (b, 0, i, 0)),
            wts, tab_p, pl.BlockSpec((CHUNK, GROUP), lambda b, i: (i, 0)), consts_c, st_meta,
            [pl.BlockSpec((1, CONV_W - 1, 3 * GROUP), lambda b, i: (0, 0, 0)),
             st1(HEAD_DIM), st1(HEAD_DIM), st1(GLA_DK), st1(HEAD_DIM)],
            jax.ShapeDtypeStruct((bsz // pg, pg, seq, D_MODEL), bf16),
            pl.BlockSpec((1, pg, CHUNK, D_MODEL), lambda b, i: (b, 0, i, 0)), bsz, states_prev=st_p, cast=ffn_w)
        om = om.reshape(rows_main, D_MODEL)
        st_p, (wo_b, wg_b, wu_b, wd_b) = st_p[:5], st_p[5:]

        pair_col = lambda v: (col_vec(v), pl.BlockSpec((LANE, 1), lambda p: (p, 0)))
        full = lambda a: (a, pl.BlockSpec(a.shape, lambda p: (0,) * a.ndim))
        split = lambda pairs: ([a for a, _ in pairs], [s for _, s in pairs])
        hist = lambda part: (conv_t, pl.BlockSpec((1, CONV_W - 1, n_seq, LANE), lambda p: (l, 0, 0, 2 * part + p)))
        cw_t = conv_w[l].T
        cw = lambda part: (cw_t, pl.BlockSpec((LANE, CONV_W), lambda p: (2 * part + p, 0)))
        small_col = lambda v: jnp.zeros((LANE, 1), f32).at[SM_ALPHA:SM_ALPHA + N_HEADS, 0].set(v)
        extra, especs = split([hist(0), hist(1), hist(2), cw(0), cw(1), cw(2), full(small_col(delta_a_log[l])),
                               full(small_col(delta_dt_bias[l])), pair_col(delta_norm_g[l])])
        ot, sa_s, *conv_s = _bl_call(
            functools.partial(_bl_delta_body, n_tok), "sample_delta", 0, n_tok, n_seq, s_blk, pt,
            [(A_QKV, True), (A_QKV + GROUP, True), (A_QKV + 2 * GROUP, True), (A_GATE, True), (SMALL, False)],
            extra, especs, sd_t, HEAD_DIM, l, ot, sa_s,
            [tok_scr(), tok_scr(), tok_scr(), vm((n_tok, HEAD_DIM, n_seq)), vm((n_tok, SUBLANE, n_seq))],
            conv_prev=conv_s or [])
        extra, especs = split([(hgrn_lb_logits.T, pl.BlockSpec((LANE, DEPTH), lambda p: (p, 0))),
                               pair_col(hgrn_norm_g[l])])
        ot, sb_s = _bl_call(
            functools.partial(_bl_hgrn_body, l, n_tok), "sample_hgrn", 1, n_tok, n_seq, s_blk, pt,
            [(B_Q, True), (B_F, True), (B_I, True), (B_GATE, True)], extra, especs, sh_t, HEAD_DIM, l, ot, sb_s,
            [tok_scr() for _ in range(4)])
        wg_t = jnp.zeros((N_HEADS * GLA_DK, LANE), f32).at[:, SM_LR:SM_LR + GLA_RANK].set(gla_w_gate[l].T)
        extra, especs = split([full(wg_t), full(col_vec(gla_b_gate[l])), pair_col(gla_norm_g[l])])
        ot, sc_s = _bl_call(
            functools.partial(_bl_gla_body, n_tok), "sample_gla", 2, n_tok, n_seq, s_blk, pt,
            [(C_Q, False), (C_K, False), (C_V, True), (C_GATE, True), (SMALL, False)], extra, especs,
            sg_t, GLA_DK, l, ot, sc_s, [tok_scr() for _ in range(4)])
        extra, especs = split([full(cos_s), full(sin_s), pair_col(ret_norm_g[l])])
        ot, sd_s = _bl_call(
            functools.partial(_bl_ret_body, n_tok), "sample_ret", 3, n_tok, n_seq, s_blk, pt,
            [(D_Q, True), (D_K, True), (D_V, True), (D_GATE, True)], extra, especs, sr_t, HEAD_DIM, l, ot, sd_s,
            [tok_scr() for _ in range(3)])

        xm, xt = _post(xm, xt, om, ot, row_vec(emb_ln_g), row_vec(emb_ln_b),
                       row_vec(ln1_g[l]), row_vec(ln1_b[l]), row_vec(ln2_g[l]), row_vec(ln2_b[l]),
                       wo_b, wg_b, wu_b, wd_b, l)

    y_prompt = xm.reshape(bsz, seq, D_MODEL)
    y_sample = jnp.swapaxes(xt[0:rows_s].reshape(n_tok, n_seq, D_MODEL), 0, 1)
    seq_first = lambda s: jnp.transpose(s, (0, 4, 1, 2, 3))
    st_s = (jnp.transpose(jnp.concatenate(conv_s, -1), (0, 2, 1, 3)),
            seq_first(sa_s), seq_first(sb_s), seq_first(sc_s), seq_first(sd_s))
    return (y_prompt, y_sample) + tuple(v for pair in zip(st_p, st_s) for v in pair)
```

```python
import collections
import functools
import math

import numpy as np
import jax
import jax.numpy as jnp
from jax import lax
from jax.experimental import pallas as pl
from jax.experimental.pallas import tpu as pltpu

f32 = jnp.float32
bf16 = jnp.bfloat16

D_MODEL = 1024
N_META = 16
CHUNK = 64
GROUP = 256
N_HEADS = 4
HEAD_DIM = 64
GLA_DK = 32
GLA_RANK = 16
GLA_GATE_NORM = 16.0
CONV_W = 4
D_FF = 2816
DEPTH = 2
ALPHA = (2 * DEPTH) ** 0.25
PAST_LEN = 16384
RET_THETA_BASE = 10000.0
LN_EPS = 1e-5
NORM_EPS = 1e-6
GATE_CLAMP = 1e-6
IN_SPLITS = (GROUP, GROUP, GROUP, N_HEADS, N_HEADS, GROUP,
             GROUP, GROUP, GROUP, GROUP,
             N_HEADS * GLA_DK, N_HEADS * GLA_DK, GROUP, GLA_RANK, GROUP,
             GROUP, GROUP, GROUP, GROUP)

LANE = 128
SUBLANE = 8
VMEM_LIMIT = 56 * 1024 * 1024

COL_ORDER = (0, 1, 2, 5, 6, 7, 8, 9, 10, 11, 12, 14, 15, 16, 17, 18)
A_QKV, A_GATE = 0, 768
B_Q, B_F, B_I, B_GATE = 1024, 1280, 1536, 1792
C_Q, C_K, C_V, C_GATE = 2048, 2176, 2304, 2560
D_Q, D_K, D_V, D_GATE = 2816, 3072, 3328, 3584
SMALL = 3840
SM_BETA, SM_ALPHA, SM_LR = 0, N_HEADS, 2 * N_HEADS
D_IN_AL = SMALL + LANE
SILU_COLS = (A_GATE, B_Q, B_GATE, C_GATE, D_GATE)
CONV_HIST = SUBLANE
ROW_TILE = 512
PROJ_STAGE_ROWS = 256
PROMPT_GROUP = 8


def _layer_norm(x, g, b):
    mu = jnp.mean(x, -1, keepdims=True)
    xc = x - mu
    var = jnp.mean(xc * xc, -1, keepdims=True)
    return xc * lax.rsqrt(var + LN_EPS) * g + b


def _row_specs(width, n_main, rows_tail):
    return (pl.BlockSpec((ROW_TILE, width), lambda i: (jnp.minimum(i, n_main - 1), 0)),
            pl.BlockSpec((rows_tail, width), lambda i: (0, 0)))


def _const_spec(shape, single_buffer=False):
    nd = len(shape)
    kw = dict(pipeline_mode=pl.Buffered(1)) if single_buffer else {}
    return pl.BlockSpec(shape, lambda *_: (0,) * nd, **kw)


def _on_region(n_main, main_fn, tail_fn):
    i = pl.program_id(0)
    pl.when(i < n_main)(main_fn)
    pl.when(i >= n_main)(tail_fn)


def _w_in_segments():
    offs = np.concatenate([[0], np.cumsum(IN_SPLITS)])
    runs, dst = [], 0
    for i in COL_ORDER + (3, 4, 13):
        if runs and runs[-1][1] == offs[i]:
            runs[-1][1] = int(offs[i + 1])
        else:
            runs.append([int(offs[i]), int(offs[i + 1]), dst])
        dst += IN_SPLITS[i]
    return [tuple(r) for r in runs]


def _proj_tile(with_ln, x_ref, g_ref, b_ref, w_ref, p_ref):
    x = x_ref[...]
    if with_ln:
        x = _layer_norm(x, g_ref[...], b_ref[...])
    h = x.astype(bf16)
    n = w_ref.shape[1]
    for n0 in range(0, n, GROUP):
        n1 = min(n0 + GROUP, n)
        p = jnp.dot(h, w_ref[:, n0:n1], preferred_element_type=f32)
        p_ref[:, n0:n1] = jax.nn.silu(p) if n0 in SILU_COLS else p


def _stage_w_in(wt_hbm, layer, w_scr, stage, sem):
    rows = PROJ_STAGE_ROWS
    big, small = [], []
    for s0, s1, d0 in _w_in_segments():
        if (s1 - s0) % rows == 0 and d0 % LANE == 0:
            big += [(s0 + j, d0 + j) for j in range(0, s1 - s0, rows)]
        else:
            small.append((s0, s1 - s0, d0 - SMALL))
    assert all(0 <= d and d + n <= LANE and d % SUBLANE == 0 for _, n, d in small) and len(small) <= 2

    def copy(j):
        return pltpu.make_async_copy(wt_hbm.at[layer, pl.ds(big[j][0], rows), :], stage.at[j % 2], sem.at[j % 2])
    copy(0).start()
    for j in range(len(big)):
        if j + 1 < len(big):
            copy(j + 1).start()
        copy(j).wait()
        w_scr[:, big[j][1]:big[j][1] + rows] = stage[j % 2].T.astype(bf16)
    used = max(d + n for _, n, d in small)
    stage[0, used:LANE, :] = jnp.zeros((LANE - used, D_MODEL), f32)
    copies = [pltpu.make_async_copy(wt_hbm.at[layer, pl.ds(s0, n), :], stage.at[0, pl.ds(d, n), :], sem.at[i])
              for i, (s0, n, d) in enumerate(small)]
    for cp in copies:
        cp.start()
    for cp in copies:
        cp.wait()
    w_scr[:, SMALL:SMALL + LANE] = stage[0, 0:LANE, :].T.astype(bf16)


def _proj_body(with_ln, layer, n_main, xm_ref, xt_ref, g_ref, b_ref, wt_hbm, pm_ref, pt_ref, w_scr, stage, sem):
    pl.when(pl.program_id(0) == 0)(lambda: _stage_w_in(wt_hbm, layer, w_scr, stage, sem))
    _on_region(n_main,
               lambda: _proj_tile(with_ln, xm_ref, g_ref, b_ref, w_scr, pm_ref),
               lambda: _proj_tile(with_ln, xt_ref, g_ref, b_ref, w_scr, pt_ref))


def _proj(xm, xt, g, b, w_in_t, layer, with_ln):
    n_main, rows_tail = xm.shape[0] // ROW_TILE, xt.shape[0]
    n = D_IN_AL
    out_shape = [jax.ShapeDtypeStruct((xm.shape[0], n), f32), jax.ShapeDtypeStruct((rows_tail, n), f32)]
    out_specs = list(_row_specs(n, n_main, rows_tail))
    return pl.pallas_call(
        functools.partial(_proj_body, with_ln, layer, n_main),
        out_shape=out_shape,
        grid=(n_main + 1,),
        in_specs=list(_row_specs(D_MODEL, n_main, rows_tail))
                 + [_const_spec((1, D_MODEL)), _const_spec((1, D_MODEL)), pl.BlockSpec(memory_space=pl.ANY)],
        out_specs=out_specs,
        scratch_shapes=[pltpu.VMEM((D_MODEL, n), bf16),
                        pltpu.VMEM((2, PROJ_STAGE_ROWS, D_MODEL), f32),
                        pltpu.SemaphoreType.DMA((2,))],
        compiler_params=pltpu.CompilerParams(dimension_semantics=("arbitrary",),
                                             vmem_limit_bytes=VMEM_LIMIT),
        name="in_proj",
    )(xm, xt, g, b, w_in_t)


def _post_tile(emb_ln, x_ref, o_ref, ge_ref, be_ref, wo_ref, g1_ref, b1_ref, wg_ref, wu_ref, wd_ref, g2_ref, b2_ref,
               y_ref):
    x = x_ref[...]
    if emb_ln:
        x = _layer_norm(x, ge_ref[...], be_ref[...])
    m = jnp.dot(o_ref[...], wo_ref[...], preferred_element_type=f32)
    y1 = _layer_norm(ALPHA * x + m, g1_ref[...], b1_ref[...])
    y1b = y1.astype(bf16)
    dff = wg_ref.shape[1]
    step = 4 * LANE
    acc = jnp.zeros(y1.shape, f32)
    for n0 in range(0, dff, step):
        n1 = min(n0 + step, dff)
        gate = jnp.dot(y1b, wg_ref[:, n0:n1], preferred_element_type=f32)
        up = jnp.dot(y1b, wu_ref[:, n0:n1], preferred_element_type=f32)
        act = (jax.nn.silu(gate) * up).astype(bf16)
        acc = acc + jnp.dot(act, wd_ref[n0:n1, :], preferred_element_type=f32)
    y_ref[...] = _layer_norm(ALPHA * y1 + acc, g2_ref[...], b2_ref[...])


def _post_body(layer, n_main, xm_ref, xt_ref, om_ref, ot_ref, ge_ref, be_ref, g1_ref, b1_ref, g2_ref, b2_ref,
               wo_ref, wg_ref, wu_ref, wd_ref, ym_ref, yt_ref):
    w = (ge_ref, be_ref, wo_ref, g1_ref, b1_ref, wg_ref, wu_ref, wd_ref, g2_ref, b2_ref)
    _on_region(n_main, lambda: _post_tile(layer == 0, xm_ref, om_ref, *w, ym_ref),
               lambda: _post_tile(layer == 0, xt_ref, ot_ref, *w, yt_ref))


def _post(xm, xt, om, ot, ge, be, g1, b1, g2, b2, w_out, w_gate, w_up, w_down, layer):
    n_main = xm.shape[0] // ROW_TILE
    rows = lambda: list(_row_specs(D_MODEL, n_main, xt.shape[0]))
    vec = lambda: _const_spec((1, D_MODEL))
    return pl.pallas_call(
        functools.partial(_post_body, layer, n_main),
        out_shape=[jax.ShapeDtypeStruct(xm.shape, f32), jax.ShapeDtypeStruct(xt.shape, f32)],
        grid=(n_main + 1,),
        in_specs=rows() + rows() + [vec() for _ in range(6)]
                 + [_const_spec(w.shape, True) for w in (w_out, w_gate, w_up, w_down)],
        out_specs=rows(),
        compiler_params=pltpu.CompilerParams(dimension_semantics=("arbitrary",),
                                             vmem_limit_bytes=VMEM_LIMIT),
        name="post_ffn",
    )(xm, xt, om, ot, ge, be, g1, b1, g2, b2, w_out, w_gate, w_up, w_down)


SAFE_EXP = 80.0
BASE_BLOCK = 8
assert (BASE_BLOCK // 2) * -math.log(GATE_CLAMP) < SAFE_EXP


def _levels(c, base=0):
    m, out = c // 2, []
    while m >= max(base, 1):
        out.append(m)
        m //= 2
    return tuple(out)


def _chunk_constants(c, base, nblk):
    t = np.arange(c)
    rows = [(t[None, :] <= t[:, None]).astype(np.float32), (t[None, :] > t[:, None]).astype(np.float32)]
    masks = []
    for m in _levels(c, base):
        blk = t // m
        upper = (blk % 2) == 1
        lo = np.where(upper, blk * m, t + 1)
        hi = np.where(upper, t, (blk + 1) * m - 1)
        rows.append(((t[None, :] >= lo[:, None]) & (t[None, :] <= hi[:, None])).astype(np.float32))
        same = (t[:, None] // (2 * m)) == (t[None, :] // (2 * m))
        masks.append((same & upper[:, None] & (~upper)[None, :]).astype(np.float32))
    if base:
        mid = (t // base) * base + base // 2 - 1
        after = (t[None, :] > mid[:, None]) & (t[None, :] <= t[:, None])
        before = (t[None, :] > t[:, None]) & (t[None, :] <= mid[:, None])
        rows.append(after.astype(np.float32) - before.astype(np.float32))
        masks.append(((t[:, None] // base == t[None, :] // base) & (t[None, :] <= t[:, None])).astype(np.float32))
    else:
        masks.append(np.eye(c, dtype=np.float32))
    return np.concatenate(rows, 0), np.tile(np.stack(masks, 0), (1, 1, nblk))


def _mm(a, b):
    return jnp.einsum('gmk,gkn->gmn', a.astype(bf16), b.astype(bf16), preferred_element_type=f32)


def _mm_nt(a, b):
    return jnp.einsum('gmk,gnk->gmn', a.astype(bf16), b.astype(bf16), preferred_element_type=f32)


def _mm_tn(a, b):
    return lax.dot_general(a.astype(bf16), b.astype(bf16), (((1,), (1,)), ((0,), (0,))),
                           preferred_element_type=f32)


def _iota(shape, dim):
    return lax.broadcasted_iota(jnp.int32, shape, dim)


def _bd(z, nblk, blk=None):
    z = z.astype(bf16)
    if blk is None:
        blk = _iota((1, 1, z.shape[-1]), 2) // (z.shape[-1] // nblk)
    return jnp.concatenate([jnp.where(blk == h, z, jnp.zeros_like(z)) for h in range(nblk)], axis=1)


def _hmm(x, z, nblk):
    return _mm(x, _bd(z, nblk))


def _hmm_nt(x, z, nblk):
    return _mm_nt(x, _bd(z, nblk))


def _hmm_split(x, z, nblk, blk=None):
    m = x.shape[1]
    x_hi, z_hi = x.astype(bf16), z.astype(bf16)
    x_lo = (x - x_hi.astype(f32)).astype(bf16)
    z_lo = (z - z_hi.astype(f32)).astype(bf16)
    both = _mm(jnp.concatenate([x_hi, x_lo], 1), _bd(z_hi, nblk, blk))
    return both[:, 0:m] + (both[:, m:2 * m] + _mm(x_hi, _bd(z_lo, nblk, blk)))


def _block_mask(rows, lanes, dr, dl):
    return (_iota((rows, lanes), 0) // dr) == (_iota((rows, lanes), 1) // dl)


def _dot01(m01, x):
    nb = x.shape[0]
    x2 = jnp.concatenate([x[i] for i in range(nb)], -1) if nb > 1 else x[0]
    hi = x2.astype(bf16)
    lo = (x2 - hi.astype(f32)).astype(bf16)
    m01 = m01.astype(bf16)
    d = lambda y: jnp.dot(m01, y, preferred_element_type=f32)
    out = d(hi) + d(lo)
    w = x.shape[-1]
    return jnp.stack([out[:, i * w:(i + 1) * w] for i in range(nb)], 0)


def _seg_sum(x, ones_bd, pieces=2):
    nb, c, w = x.shape
    x2 = x.reshape(nb * c, w)
    hi = x2.astype(bf16)
    out = jnp.dot(hi, ones_bd, preferred_element_type=f32)
    if pieces == 2:
        lo = (x2 - hi.astype(f32)).astype(bf16)
        out = out + jnp.dot(lo, ones_bd, preferred_element_type=f32)
    return out.reshape(nb, c, w)


def _last_row_as_col(b, c):
    return jnp.stack([b[i, c - SUBLANE:c, :].T[:, SUBLANE - 1:SUBLANE] for i in range(b.shape[0])], 0)


def _channel_decay_mixer(q, k, v, gl, s, nblk, sums_ref, masks_ref, bdmask, c, base):
    lv = _levels(c, base)
    nl = len(lv)
    sums = _dot01(sums_ref[...], gl)
    b = sums[:, 0:c]
    rev = sums[:, c:2 * c]
    row = _iota((1, c, 1), 1)
    if base:
        e0 = sums[:, (nl + 2) * c:(nl + 3) * c]
        att = jnp.where(masks_ref[nl] > 0.0, _hmm_nt(q * jnp.exp(e0), k * jnp.exp(-e0), nblk), 0.0)
    else:
        att = masks_ref[nl] * _hmm_nt(q, k, nblk)
    for li, m in enumerate(lv):
        upper = ((row // m) % 2) == 1
        w = jnp.where(upper, q, k) * jnp.exp(sums[:, (li + 2) * c:(li + 3) * c])
        att = att + masks_ref[li] * _hmm_nt(w, w, nblk)
    o = _mm(jnp.concatenate([q * jnp.exp(b), att], -1), jnp.concatenate([s.astype(bf16), _bd(v, nblk)], 1))
    s_new = jnp.exp(_last_row_as_col(b, c)) * s + jnp.where(bdmask, _mm_tn(k * jnp.exp(rev), v), 0.0)
    return o, s_new


MixCfg = collections.namedtuple("MixCfg",
                                "c g r0 r1 layer n_chunks init p4d o_rows n_alias stacked cast_slabs n_steps")


def _pairs_to_bd(s):
    z = jnp.zeros(s.shape[:1] + (HEAD_DIM, HEAD_DIM), f32)
    pair = lambda p: jnp.concatenate([jnp.concatenate([s[:, 2 * p], z], -1),
                                      jnp.concatenate([z, s[:, 2 * p + 1]], -1)], 1)
    return jnp.concatenate([pair(0), pair(1)], 0)


def _quad_to_bd(s):
    z = lambda n: jnp.zeros(s.shape[:1] + (GLA_DK, n * HEAD_DIM), f32)
    rows = []
    for h in range(N_HEADS):
        parts = ([z(h)] if h else []) + [s[:, h]] + ([z(N_HEADS - 1 - h)] if h < N_HEADS - 1 else [])
        rows.append(jnp.concatenate(parts, -1))
    return jnp.concatenate(rows, 1)


def _mixer_body(cfg, *refs):
    c, g = cfg.c, cfg.g
    n_w, n_t, n_c = 7, 3, 4
    p_ref = refs[0]
    cw_ref, alog_ref, dtb_ref, lbl_ref, wg_ref, bg_ref, gains_ref = refs[1:1 + n_w]
    cos_ref, sina_ref, sinb_ref = refs[1 + n_w:1 + n_w + n_t]
    sumsb_ref, masksb_ref, sumsc_ref, masksc_ref = refs[1 + n_w + n_t:1 + n_w + n_t + n_c]
    pos = 1 + n_w + n_t + n_c
    if cfg.init:
        convi_ref, sai_ref, sbi_ref, sci_ref, sdi_ref = refs[pos:pos + 5]
        pos += 5
    n_cast = len(cfg.cast_slabs)
    cast_in = refs[pos:pos + n_cast]
    pos += n_cast + cfg.n_alias
    o_ref, convo_ref, sao_ref, sbo_ref, sco_ref, sdo_ref = refs[pos:pos + 6]
    if cfg.stacked:
        convo_ref, sao_ref, sbo_ref, sco_ref, sdo_ref = (r.at[0] for r in (convo_ref, sao_ref, sbo_ref, sco_ref, sdo_ref))
    cast_out = refs[pos + 6:pos + 6 + n_cast]
    cbuf, sa_scr, sb_scr, sc_scr, sd_scr = refs[pos + 6 + n_cast:]
    ci = pl.program_id(1)

    @pl.when(ci == 0)
    def _():
        cbuf[:, 0:CONV_HIST, :] = jnp.zeros((g, CONV_HIST, 3 * GROUP), f32)
        if not cfg.init:
            for r in (sa_scr, sb_scr, sc_scr, sd_scr):
                r[...] = jnp.zeros(r.shape, f32)
        else:
            take = lambda r: jnp.broadcast_to(r[...], (g,) + r.shape[1:])
            cbuf[:, CONV_HIST - (CONV_W - 1):CONV_HIST, :] = take(convi_ref)
            sa_scr[...] = _pairs_to_bd(take(sai_ref))
            sb_scr[...] = _pairs_to_bd(take(sbi_ref))
            sc_scr[...] = _quad_to_bd(take(sci_ref))
            sd_scr[...] = _pairs_to_bd(take(sdi_ref))

    step = pl.program_id(0) * cfg.n_chunks + ci
    for w_f32, w_bf16, n_slabs in sorted(zip(cast_in, cast_out, cfg.cast_slabs), key=lambda e: e[2]):
        def narrow(src=w_f32, dst=w_bf16):
            dst[...] = src[0].astype(bf16)
        if n_slabs == cfg.n_steps:
            narrow()
        else:
            pl.when(step < n_slabs)(narrow)

    def load(a, b):
        if cfg.p4d:
            return p_ref[0, :, :, a:b]
        return p_ref[:, a:b].reshape(g, c, b - a)

    r0, r1 = cfg.r0, cfg.r1
    n_valid = float(r1 - r0)
    row = _iota((1, c, 1), 1)
    vm = ((row >= r0) & (row < r1)).astype(f32)
    lane_v = _iota((1, 1, 2 * HEAD_DIM), 2)
    lane_s = _iota((1, 1, 2 * c), 2)
    s_idx = lane_s % c
    incl = s_idx <= row
    strict = s_idx < row
    bd2 = _block_mask(2 * HEAD_DIM, 2 * HEAD_DIM, HEAD_DIM, HEAD_DIM)
    bd4 = _block_mask(N_HEADS * GLA_DK, GROUP, GLA_DK, HEAD_DIM)
    ones2 = bd2.astype(bf16)
    pairs = lambda x: jnp.concatenate([x[..., :LANE], x[..., LANE:]], 0)
    unpairs = lambda x: jnp.concatenate([x[:g], x[g:]], -1)
    n_solve = max(1, math.ceil(math.log2(r1 - r0)))
    sm = load(SMALL, SMALL + LANE)
    valid = (lambda x: x) if (r0 == 0 and r1 == c) else (lambda x: x * vm)

    cbuf[:, CONV_HIST:CONV_HIST + c, :] = valid(load(A_QKV, A_QKV + 3 * GROUP))
    acc = cbuf[:, pl.ds(CONV_HIST - 3, c), :] * cw_ref[0:1, :]
    for j in range(1, CONV_W):
        acc = acc + cbuf[:, pl.ds(CONV_HIST - 3 + j, c), :] * cw_ref[j:j + 1, :]
    conv = jax.nn.silu(acc)
    beta = valid(jax.nn.sigmoid(sm))
    g_a = valid(-jnp.exp(alog_ref[...]) * jax.nn.softplus(sm + dtb_ref[...]))
    b_a = _dot01(sumsc_ref[0:c, :], g_a)

    def head_cols(x, lane0):
        col = lambda h: x[..., lane0 + h:lane0 + h + 1]
        return jnp.concatenate([col(0), col(2)], 0), jnp.concatenate([col(1), col(3)], 0)

    def spread(c0, c1, lane, width):
        return jnp.where(lane < width, c0, c1)

    bc0, bc1 = head_cols(b_a, SM_ALPHA)
    bt0, bt1 = head_cols(beta, SM_BETA)
    bc_v, bet_v = spread(bc0, bc1, lane_v, HEAD_DIM), spread(bt0, bt1, lane_v, HEAD_DIM)
    bc_s, bet_s = spread(bc0, bc1, lane_s, c), spread(bt0, bt1, lane_s, c)
    b_rows = [b_a[i].T for i in range(g)]
    b_row = lambda h: jnp.stack([t[SM_ALPHA + h:SM_ALPHA + h + 1, :] for t in b_rows], 0)
    br_s = jnp.concatenate([jnp.concatenate([b_row(0), b_row(1)], -1),
                            jnp.concatenate([b_row(2), b_row(3)], -1)], 0)
    dec = jnp.where(incl, jnp.exp(jnp.where(incl, bc_s - br_s, 0.0)), 0.0)
    q = pairs(conv[..., 0:GROUP])
    k = pairs(conv[..., GROUP:2 * GROUP])
    v = pairs(valid(conv[..., 2 * GROUP:3 * GROUP]))
    q = q * lax.rsqrt(_seg_sum(q * q, ones2, 1) + NORM_EPS) * HEAD_DIM ** -0.5
    k = valid(k * lax.rsqrt(_seg_sum(k * k, ones2) + NORM_EPS))
    s_a = sa_scr[...]
    kq = jnp.concatenate([k, q], 1)
    kq_k = _hmm_nt(kq, k, 2)
    kq_s = _mm(kq, s_a)
    a_mat = jnp.where(strict, bet_s * dec * kq_k[:, 0:c], 0.0)
    gam = jnp.exp(bc_v)
    u = bet_v * (v - gam * kq_s[:, 0:c])
    p = -a_mat
    lane_pu = _iota((1, 1, 2 * c + 2 * HEAD_DIM), 2)
    head_pu = jnp.where(lane_pu < 2 * c, lane_pu // c, (lane_pu - 2 * c) // HEAD_DIM)
    for i in range(n_solve):
        if i + 1 < n_solve:
            pu = _hmm_split(p, jnp.concatenate([p, u], -1), 2, head_pu)
            p, u = pu[..., 0:2 * c], u + pu[..., 2 * c:]
        else:
            u = u + _hmm_split(p, u, 2)
    o_a = gam * kq_s[:, c:2 * c] + _hmm(dec * kq_k[:, c:2 * c], u, 2)
    bl = bc_v[:, c - 1:c, :]
    sa_scr[...] = jnp.exp(bl) * s_a + jnp.where(bd2, _mm_tn(k * jnp.exp(bl - bc_v), u), 0.0)
    o_a = unpairs(o_a * lax.rsqrt(_seg_sum(o_a * o_a, ones2, 1) * (1.0 / HEAD_DIM) + NORM_EPS))

    lbl = lbl_ref[...]
    e = jnp.exp(lbl - jnp.max(lbl, axis=0, keepdims=True))
    prob = e / jnp.sum(e, axis=0, keepdims=True)
    lb = (jnp.sum(prob[1:cfg.layer + 1], axis=0, keepdims=True) if cfg.layer > 0
          else jnp.zeros((1, GROUP), f32))
    kb = (1.0 - lb) * jax.nn.sigmoid(-load(B_F, B_F + GROUP))
    g_b = valid(jnp.log1p(-jnp.clip(kb, 0.0, 1.0 - GATE_CLAMP)))
    o_b, s_b_new = _channel_decay_mixer(
        pairs(load(B_Q, B_Q + GROUP)), pairs(valid(kb)), pairs(valid(load(B_I, B_I + GROUP))),
        pairs(g_b), sb_scr[...], 2, sumsb_ref, masksb_ref, bd2, c, BASE_BLOCK if c > BASE_BLOCK else 0)
    sb_scr[...] = s_b_new
    o_b = unpairs(o_b * lax.rsqrt(_seg_sum(o_b * o_b, ones2, 1) * (1.0 / HEAD_DIM) + NORM_EPS))

    logit = jnp.dot(sm.reshape(g * c, LANE).astype(bf16), wg_ref[...].astype(bf16),
                    preferred_element_type=f32).reshape(g, c, LANE) + bg_ref[...]
    g_c = valid(jax.nn.log_sigmoid(logit) * (1.0 / GLA_GATE_NORM))
    c_args = (load(C_Q, C_Q + N_HEADS * GLA_DK) * GLA_DK ** -0.5, valid(load(C_K, C_K + N_HEADS * GLA_DK)),
              valid(load(C_V, C_V + GROUP)), g_c, sc_scr[...], N_HEADS)
    o_c, s_c_new = _channel_decay_mixer(*c_args, sumsc_ref, masksc_ref, bd4, c, 0)
    sc_scr[...] = s_c_new
    ones4 = _block_mask(GROUP, GROUP, HEAD_DIM, HEAD_DIM).astype(bf16)
    o_c = o_c * lax.rsqrt(_seg_sum(o_c * o_c, ones4, 1) * (1.0 / HEAD_DIM) + NORM_EPS)

    def rope(x):
        x2 = x.reshape(g * c, GROUP)
        rot = lambda sh: pltpu.roll(x2, sh, 1).reshape(g, c, GROUP)
        return x * cos_ref[...] + rot(GROUP - HEAD_DIM // 2) * sina_ref[...] + rot(HEAD_DIM // 2) * sinb_ref[...]
    q = pairs(rope(load(D_Q, D_Q + GROUP)))
    k = pairs(valid(rope(load(D_K, D_K + GROUP)) * HEAD_DIM ** -0.5))
    v = pairs(valid(load(D_V, D_V + GROUP)))
    pair_idx = _iota((2, 1, 1), 0)
    log_gamma = lambda lane, width: jnp.log1p(-jnp.exp2(-5.0 - (2 * pair_idx + lane // width).astype(f32)))
    per_seq = lambda x: jnp.concatenate([jnp.broadcast_to(x[p:p + 1], (g,) + x.shape[1:]) for p in range(2)], 0)
    lg_v, lg_s = log_gamma(lane_v, HEAD_DIM), log_gamma(lane_s, c)
    cnt_c = jnp.clip(row + 1 - r0, 0, r1 - r0).astype(f32)
    cnt_s = jnp.clip(s_idx + 1 - r0, 0, r1 - r0).astype(f32)
    dec = per_seq(jnp.where(incl, jnp.exp(jnp.where(incl, (cnt_c - cnt_s) * lg_s, 0.0)), 0.0))
    s_d = sd_scr[...]
    o_d = _mm(jnp.concatenate([q * per_seq(jnp.exp(cnt_c * lg_v)), dec * _hmm_nt(q, k, 2)], -1),
              jnp.concatenate([s_d.astype(bf16), _bd(v, 2)], 1))
    sd_scr[...] = (per_seq(jnp.exp(n_valid * lg_v)) * s_d
                   + jnp.where(bd2, _mm_tn(k * per_seq(jnp.exp((n_valid - cnt_c) * lg_v)), v), 0.0))
    oc = o_d - _seg_sum(o_d, ones2) * (1.0 / HEAD_DIM)
    o_d = unpairs(oc * lax.rsqrt(_seg_sum(oc * oc, ones2, 1) * (1.0 / HEAD_DIM) + LN_EPS))

    for i, (o_m, gate_col) in enumerate(zip((o_a, o_b, o_c, o_d), (A_GATE, B_GATE, C_GATE, D_GATE))):
        val = (o_m * gains_ref[i:i + 1, :] * load(gate_col, gate_col + GROUP)).astype(o_ref.dtype)
        if cfg.p4d:
            o_ref[0, :, :, i * GROUP:(i + 1) * GROUP] = val
        else:
            o_ref[0:g * c, i * GROUP:(i + 1) * GROUP] = val.reshape(g * c, GROUP)
    if cfg.o_rows > g * c:
        o_ref[g * c:cfg.o_rows, :] = jnp.zeros((cfg.o_rows - g * c, D_MODEL), o_ref.dtype)
    cbuf[:, 0:CONV_HIST, :] = cbuf[:, c:c + CONV_HIST, :]

    @pl.when(ci == cfg.n_chunks - 1)
    def _():
        convo_ref[...] = cbuf[:, pl.ds(CONV_HIST + r1 - (CONV_W - 1), CONV_W - 1), :]
        for scr, out in ((sa_scr, sao_ref), (sb_scr, sbo_ref), (sd_scr, sdo_ref)):
            s = scr[...]
            for p in range(2):
                out[:, 2 * p] = s[p * g:(p + 1) * g, 0:HEAD_DIM, 0:HEAD_DIM]
                out[:, 2 * p + 1] = s[p * g:(p + 1) * g, HEAD_DIM:2 * HEAD_DIM, HEAD_DIM:2 * HEAD_DIM]
        s = sc_scr[...]
        for h in range(N_HEADS):
            sco_ref[:, h] = s[:, h * GLA_DK:(h + 1) * GLA_DK, h * HEAD_DIM:(h + 1) * HEAD_DIM]


def _state_shapes(n, stacked):
    lead = (DEPTH, n) if stacked else (n,)
    return [jax.ShapeDtypeStruct(lead + (CONV_W - 1, 3 * GROUP), f32),
            jax.ShapeDtypeStruct(lead + (N_HEADS, HEAD_DIM, HEAD_DIM), f32),
            jax.ShapeDtypeStruct(lead + (N_HEADS, HEAD_DIM, HEAD_DIM), f32),
            jax.ShapeDtypeStruct(lead + (N_HEADS, GLA_DK, HEAD_DIM), f32),
            jax.ShapeDtypeStruct(lead + (N_HEADS, HEAD_DIM, HEAD_DIM), f32)]


def _state_out_specs(g, layer, stacked):
    if stacked:
        st = lambda k: pl.BlockSpec((1, g, N_HEADS, k, HEAD_DIM), lambda b, i: (layer, b, 0, 0, 0))
        conv = pl.BlockSpec((1, g, CONV_W - 1, 3 * GROUP), lambda b, i: (layer, b, 0, 0))
    else:
        st = lambda k: pl.BlockSpec((g, N_HEADS, k, HEAD_DIM), lambda b, i: (b, 0, 0, 0))
        conv = pl.BlockSpec((g, CONV_W - 1, 3 * GROUP), lambda b, i: (b, 0, 0))
    return [conv, st(HEAD_DIM), st(HEAD_DIM), st(GLA_DK), st(HEAD_DIM)]


def _mixer_call(cfg, name, n_groups, p, p_spec, wts, tables, table_spec, consts, init, init_specs,
                o_shape, o_spec, n_state_rows, o_prev=None, states_prev=None, cast=()):
    c, g = cfg.c, cfg.g
    n_steps = n_groups * cfg.n_chunks
    wspecs =[_const_spec((CONV_W, 3 * GROUP)), _const_spec((1, LANE)), _const_spec((1, LANE)),
              _const_spec((DEPTH, GROUP)), _const_spec((LANE, LANE)), _const_spec((1, LANE)),
              _const_spec((4, GROUP))]
    cspecs = [_const_spec(a.shape) for a in consts]
    in_specs = [p_spec] + wspecs + [table_spec] * 3 + cspecs + list(init_specs)
    args = [p, *wts, *tables, *consts, *init]
    cast_shapes, cast_specs = [], []
    for w, n_slabs in zip(cast, cfg.cast_slabs):
        slab = w.shape[1] // n_slabs
        assert w.shape[1] % n_slabs == 0 and slab % (2 * SUBLANE) == 0 and n_slabs <= n_groups * cfg.n_chunks
        slab_idx = lambda b, i, n=n_slabs: jnp.minimum(b * cfg.n_chunks + i, n - 1)
        args.append(w)
        in_specs.append(pl.BlockSpec((1, slab, w.shape[2]), lambda b, i, f=slab_idx: (cfg.layer, f(b, i), 0)))
        cast_shapes.append(jax.ShapeDtypeStruct(w.shape[1:], bf16))
        cast_specs.append(pl.BlockSpec((slab, w.shape[2]), lambda b, i, f=slab_idx: (f(b, i), 0)))
    assert len(cast) == len(cfg.cast_slabs)
    aliases = {}
    if o_prev is not None:
        aliases[len(args)] = 0
        args.append(o_prev)
    for j, s_prev in enumerate(states_prev or ()):
        aliases[len(args)] = 1 + j
        args.append(s_prev)
    in_specs += [pl.BlockSpec(memory_space=pl.ANY)] * len(aliases)
    assert len(aliases) == cfg.n_alias
    return pl.pallas_call(
        functools.partial(_mixer_body, cfg),
        out_shape=[o_shape] + _state_shapes(n_state_rows, cfg.stacked) + cast_shapes,
        grid=(n_groups, cfg.n_chunks),
        in_specs=in_specs,
        out_specs=[o_spec] + _state_out_specs(g, cfg.layer, cfg.stacked) + cast_specs,
        scratch_shapes=[pltpu.VMEM((g, c + CONV_HIST, 3 * GROUP), f32),
                        pltpu.VMEM((2 * g, 2 * HEAD_DIM, 2 * HEAD_DIM), f32),
                        pltpu.VMEM((2 * g, 2 * HEAD_DIM, 2 * HEAD_DIM), f32),
                        pltpu.VMEM((g, N_HEADS * GLA_DK, GROUP), f32),
                        pltpu.VMEM((2 * g, 2 * HEAD_DIM, 2 * HEAD_DIM), f32)],
        input_output_aliases=aliases,
        compiler_params=pltpu.CompilerParams(dimension_semantics=("arbitrary", "arbitrary"),
                                             vmem_limit_bytes=VMEM_LIMIT),
        name=name,
    )(*args)


def _tok(ref, t, nb):
    return ref[t * nb:(t + 1) * nb, :].T


def _bcast_row(ref, t, r):
    return jnp.broadcast_to(ref[t, pl.ds(r, 1), :], (HEAD_DIM, ref.shape[-1]))


def _bl_diag_loop(n_tok, kdim, j, row0, s_in, s_out, d_scr, k_scr, q_scr, v_scr, d_const=None):
    nb = s_in.shape[-1]

    def body(k, os):
        s_k = s_in[0, j, k]
        r = row0 + k
        outs = []
        for t in range(n_tok):
            d = d_const if d_const is not None else _bcast_row(d_scr, t, r)
            s_k = d * s_k + _bcast_row(k_scr, t, r) * v_scr[t, j * HEAD_DIM:(j + 1) * HEAD_DIM, :]
            outs.append(os[t] + _bcast_row(q_scr, t, r) * s_k)
        s_out[0, j, k] = s_k
        return tuple(outs)
    return lax.fori_loop(0, kdim, body, (jnp.zeros((HEAD_DIM, nb), f32),) * n_tok)


def _bl_store(o_ref, os, gate_ref, gain_ref, n_tok, nb, group_norm):
    for t in range(n_tok):
        gate = _tok(gate_ref, t, nb)
        halves = []
        for j in range(2):
            o = os[j][t]
            if group_norm:
                o = o - jnp.mean(o, axis=0, keepdims=True)
                o = o * lax.rsqrt(jnp.mean(o * o, axis=0, keepdims=True) + LN_EPS)
            else:
                o = o * lax.rsqrt(jnp.mean(o * o, axis=0, keepdims=True) + NORM_EPS)
            hs = slice(j * HEAD_DIM, (j + 1) * HEAD_DIM)
            halves.append(o * gain_ref[hs, :] * gate[hs, :])
        o_ref[t * nb:(t + 1) * nb, :] = jnp.concatenate(halves, 0).T.astype(o_ref.dtype)


def _bl_hgrn_body(layer, n_tok, q_ref, f_ref, i_ref, gate_ref, lbl_ref, gain_ref, s_in, o_prev, s_prev,
                  o_ref, s_out, d_scr, k_scr, q_scr, v_scr):
    del o_prev, s_prev
    nb = s_in.shape[-1]
    lbl = lbl_ref[...]
    e = jnp.exp(lbl - jnp.max(lbl, axis=1, keepdims=True))
    prob = e / jnp.sum(e, axis=1, keepdims=True)
    lb = (jnp.sum(prob[:, 1:layer + 1], axis=1, keepdims=True) if layer > 0 else jnp.zeros((LANE, 1), f32))
    for t in range(n_tok):
        kk = (1.0 - lb) * jax.nn.sigmoid(-_tok(f_ref, t, nb))
        d_scr[t] = 1.0 - jnp.clip(kk, 0.0, 1.0 - GATE_CLAMP)
        k_scr[t] = kk
        q_scr[t] = _tok(q_ref, t, nb)
        v_scr[t] = _tok(i_ref, t, nb)
    os = [_bl_diag_loop(n_tok, HEAD_DIM, j, j * HEAD_DIM, s_in, s_out, d_scr, k_scr, q_scr, v_scr) for j in range(2)]
    _bl_store(o_ref, os, gate_ref, gain_ref, n_tok, nb, False)


def _bl_gla_body(n_tok, q_ref, k_ref, v_ref, gate_ref, small_ref, wg_ref, bg_ref, gain_ref, s_in, o_prev, s_prev,
                 o_ref, s_out, d_scr, k_scr, q_scr, v_scr):
    del o_prev, s_prev
    nb = s_in.shape[-1]
    for t in range(n_tok):
        logit = jnp.dot(wg_ref[...].astype(bf16), _tok(small_ref, t, nb).astype(bf16),
                        preferred_element_type=f32) + bg_ref[...]
        d_scr[t] = jnp.exp(jax.nn.log_sigmoid(logit) * (1.0 / GLA_GATE_NORM))
        k_scr[t] = _tok(k_ref, t, nb)
        q_scr[t] = _tok(q_ref, t, nb) * GLA_DK ** -0.5
        v_scr[t] = _tok(v_ref, t, nb)
    p = pl.program_id(0)
    os = [_bl_diag_loop(n_tok, GLA_DK, j, (2 * p + j) * GLA_DK, s_in, s_out, d_scr, k_scr, q_scr, v_scr)
          for j in range(2)]
    _bl_store(o_ref, os, gate_ref, gain_ref, n_tok, nb, False)


def _bl_ret_body(n_tok, q_ref, k_ref, v_ref, gate_ref, cos_ref, sin_ref, gain_ref, s_in, o_prev, s_prev,
                 o_ref, s_out, k_scr, q_scr, v_scr):
    del o_prev, s_prev
    nb = s_in.shape[-1]
    half = HEAD_DIM // 2

    def rope(x, t):
        swapped = jnp.concatenate([x[half:2 * half], x[0:half], x[3 * half:4 * half], x[2 * half:3 * half]], 0)
        return x * cos_ref[t] + swapped * sin_ref[t]
    for t in range(n_tok):
        q_scr[t] = rope(_tok(q_ref, t, nb), t)
        k_scr[t] = rope(_tok(k_ref, t, nb), t) * HEAD_DIM ** -0.5
        v_scr[t] = _tok(v_ref, t, nb)
    p = pl.program_id(0)
    os = []
    for j in range(2):
        head = jnp.zeros((1, nb), f32) + (2 * p + j).astype(f32)
        gamma = jnp.broadcast_to(1.0 - jnp.exp2(-5.0 - head), (HEAD_DIM, nb))
        os.append(_bl_diag_loop(n_tok, HEAD_DIM, j, j * HEAD_DIM, s_in, s_out, None, k_scr, q_scr, v_scr, gamma))
    _bl_store(o_ref, os, gate_ref, gain_ref, n_tok, nb, True)


def _bl_delta_body(n_tok, q_ref, k_ref, v_ref, gate_ref, small_ref, hq_ref, hk_ref, hv_ref, cwq_ref, cwk_ref, cwv_ref,
                   alog_ref, dtb_ref, gain_ref, s_in, o_prev, s_prev, cq_prev, ck_prev, cv_prev,
                   o_ref, s_out, cq_out, ck_out, cv_out, k_scr, q_scr, kh_scr, u_scr, bg_scr):
    del o_prev, s_prev, cq_prev, ck_prev, cv_prev
    nb = s_in.shape[-1]
    p = pl.program_id(0)

    def conv(x_ref, h_ref, cw_ref):
        seq = [h_ref[0, r].T for r in range(CONV_W - 1)] + [_tok(x_ref, t, nb) for t in range(n_tok)]
        outs = []
        for t in range(n_tok):
            acc = cw_ref[:, 0:1] * seq[t]
            for jj in range(1, CONV_W):
                acc = acc + cw_ref[:, jj:jj + 1] * seq[t + jj]
            outs.append(jax.nn.silu(acc))
        return outs

    def l2norm(x):
        parts = []
        for j in range(2):
            xj = x[j * HEAD_DIM:(j + 1) * HEAD_DIM]
            parts.append(xj * lax.rsqrt(jnp.sum(xj * xj, axis=0, keepdims=True) + NORM_EPS))
        return jnp.concatenate(parts, 0)

    qs, ks, vs = conv(q_ref, hq_ref, cwq_ref), conv(k_ref, hk_ref, cwk_ref), conv(v_ref, hv_ref, cwv_ref)
    for t in range(n_tok):
        q_scr[t] = l2norm(qs[t]) * HEAD_DIM ** -0.5
        k_scr[t] = l2norm(ks[t])
        sm = _tok(small_ref, t, nb)
        both = jnp.where(_iota((LANE, 1), 0) < SM_ALPHA, jax.nn.sigmoid(sm),
                         -jnp.exp(alog_ref[...]) * jax.nn.softplus(sm + dtb_ref[...]))
        bg_scr[t] = both[0:SUBLANE]
    for r in range(CONV_W - 1):
        rows = slice((n_tok - (CONV_W - 1) + r) * nb, (n_tok - (CONV_W - 1) + r + 1) * nb)
        cq_out[0, r] = q_ref[rows, :]
        ck_out[0, r] = k_ref[rows, :]
        cv_out[0, r] = v_ref[rows, :]

    os = []
    for j in range(2):
        hs = slice(j * HEAD_DIM, (j + 1) * HEAD_DIM)
        h = 2 * p + j
        beta = [bg_scr[t, pl.ds(SM_BETA + h, 1), :] for t in range(n_tok)]
        b, acc = [], jnp.zeros((1, nb), f32)
        for t in range(n_tok):
            acc = acc + bg_scr[t, pl.ds(SM_ALPHA + h, 1), :]
            b.append(acc)
        kt = [k_scr[t, hs, :] for t in range(n_tok)]
        qt = [q_scr[t, hs, :] for t in range(n_tok)]
        dot = lambda x, y: jnp.sum(x * y, axis=0, keepdims=True)

        def sweep(scr):
            def body(k, accs):
                s_k = s_in[0, j, k]
                return tuple(a + _bcast_row(scr, t, j * HEAD_DIM + k) * s_k for t, a in enumerate(accs))
            return lax.fori_loop(0, HEAD_DIM, body, (jnp.zeros((HEAD_DIM, nb), f32),) * n_tok)
        k_s, q_s = sweep(k_scr), sweep(q_scr)
        us = []
        for t in range(n_tok):
            u = beta[t] * (vs[t][hs] - jnp.exp(b[t]) * k_s[t])
            for s in range(t):
                u = u - (beta[t] * jnp.exp(b[t] - b[s]) * dot(kt[t], kt[s])) * us[s]
            us.append(u)
        o_h = []
        for t in range(n_tok):
            o = jnp.exp(b[t]) * q_s[t]
            for s in range(t + 1):
                o = o + (jnp.exp(b[t] - b[s]) * dot(qt[t], kt[s])) * us[s]
            o_h.append(o)
        os.append(o_h)
        for t in range(n_tok):
            kh_scr[t, hs, :] = kt[t] * jnp.exp(b[-1] - b[t])
            u_scr[t] = us[t]
        decay = jnp.broadcast_to(jnp.exp(b[-1]), (HEAD_DIM, nb))

        def update(k, carry):
            s_k = decay * s_in[0, j, k]
            for t in range(n_tok):
                s_k = s_k + _bcast_row(kh_scr, t, j * HEAD_DIM + k) * u_scr[t]
            s_out[0, j, k] = s_k
            return carry
        lax.fori_loop(0, HEAD_DIM, update, 0)
    _bl_store(o_ref, os, gate_ref, gain_ref, n_tok, nb, False)


def _bl_call(body, name, mixer, n_tok, nb, blk_row, pt, cols, extra, extra_specs, state_t, kdim, layer,
             o_prev, s_prev, scratch, conv_prev=None):
    def tok_spec(c0, per_pair):
        return pl.BlockSpec((n_tok * nb, LANE),
                            (lambda p: (blk_row, c0 // LANE + p)) if per_pair else (lambda p: (blk_row, c0 // LANE)))
    st_spec = pl.BlockSpec((1, 2, kdim, HEAD_DIM, nb), lambda p: (layer, p, 0, 0, 0))
    args = [pt] * len(cols) + list(extra) + [state_t]
    in_specs = [tok_spec(c0, pp) for c0, pp in cols] + list(extra_specs) + [st_spec]
    aliases = {}

    def donate(arr, out_idx):
        if arr is not None:
            aliases[len(args)] = out_idx
        args.append(arr if arr is not None else jnp.zeros((1,), f32))
        in_specs.append(pl.BlockSpec(memory_space=pl.ANY))
    donate(o_prev, 0)
    donate(s_prev, 1)
    out_shape = [jax.ShapeDtypeStruct(o_prev.shape, o_prev.dtype),
                 jax.ShapeDtypeStruct((DEPTH,) + state_t.shape[1:], f32)]
    out_specs = [pl.BlockSpec((n_tok * nb, LANE), lambda p: (blk_row, 2 * mixer + p)), st_spec]
    if conv_prev is not None:
        for i in range(3):
            donate(conv_prev[i] if conv_prev else None, 2 + i)
            out_shape.append(jax.ShapeDtypeStruct((DEPTH, CONV_W - 1, nb, GROUP), f32))
            out_specs.append(pl.BlockSpec((1, CONV_W - 1, nb, LANE), lambda p: (layer, 0, 0, p)))
    return pl.pallas_call(
        body, out_shape=out_shape, grid=(2,), in_specs=in_specs, out_specs=out_specs,
        scratch_shapes=scratch, input_output_aliases=aliases,
        compiler_params=pltpu.CompilerParams(dimension_semantics=("arbitrary",), vmem_limit_bytes=VMEM_LIMIT),
        name=name,
    )(*args)


def _rope_tables(pos):
    half = HEAD_DIM // 2
    inv = 1.0 / (RET_THETA_BASE ** jnp.linspace(0.0, 1.0, half, dtype=f32))
    ang = pos[:, None] * inv[None, :]
    cos, sin = jnp.cos(ang), jnp.sin(ang)
    zero = jnp.zeros_like(sin)
    tile = lambda a, b: jnp.tile(jnp.concatenate([a, b], -1), (1, N_HEADS))
    return tile(cos, cos), tile(-sin, zero), tile(zero, sin)


def kernel(x_prompt, x_sample, state_delta_conv, state_delta, state_hgrn, state_gla, state_ret, meta_tokens, emb_ln_g, emb_ln_b, w_in, conv_w, delta_a_log, delta_dt_bias, delta_norm_g, hgrn_lb_logits, hgrn_norm_g, gla_w_gate, gla_b_gate, gla_norm_g, ret_norm_g, w_out, ln1_g, ln1_b, w_ffn_gate, w_ffn_up, w_ffn_down, ln2_g, ln2_b):
    bsz, seq, _ = x_prompt.shape
    n_seq, n_tok, _ = x_sample.shape
    pg = PROMPT_GROUP if bsz % PROMPT_GROUP == 0 else 1
    rows_main = bsz * seq
    rows_s = n_tok * n_seq
    assert seq % CHUNK == 0 and rows_main % ROW_TILE == 0 and N_META <= CHUNK
    assert n_tok >= CONV_W - 1 and rows_s % CHUNK == 0 and (n_seq % LANE == 0 or n_seq < LANE)
    lead_pad = CHUNK - N_META
    rows_tail = rows_s + CHUNK
    s_blk = 0
    meta_blk = rows_s // CHUNK

    xs = jnp.swapaxes(x_sample, 0, 1).reshape(rows_s, D_MODEL)
    xt = jnp.concatenate([xs, jnp.zeros((lead_pad, D_MODEL), f32), meta_tokens.astype(f32)], 0)
    xm = x_prompt.reshape(rows_main, D_MODEL)

    tab_m = _rope_tables(jnp.arange(CHUNK, dtype=f32) - lead_pad)
    tab_p = _rope_tables(N_META + jnp.arange(seq, dtype=f32))
    consts_c = tuple(jnp.asarray(a, f32) for a in (_chunk_constants(CHUNK, BASE_BLOCK, 2)
                                                  + _chunk_constants(CHUNK, 0, N_HEADS)))
    row_vec = lambda v: v.reshape(1, -1).astype(f32)
    col_vec = lambda v: v.reshape(-1, 1).astype(f32)
    small_row = lambda v: jnp.zeros((1, LANE), f32).at[0, SM_ALPHA:SM_ALPHA + N_HEADS].set(v)
    n_chunks = seq // CHUNK

    seq_last = lambda s: jnp.transpose(s, (0, 2, 3, 4, 1))
    sd_t, sh_t, sg_t, sr_t = (seq_last(s) for s in (state_delta, state_hgrn, state_gla, state_ret))
    conv_t = jnp.transpose(state_delta_conv, (0, 2, 1, 3))
    half = HEAD_DIM // 2
    ang = ((PAST_LEN + jnp.arange(n_tok, dtype=f32))[:, None]
           * (1.0 / (RET_THETA_BASE ** jnp.linspace(0.0, 1.0, half, dtype=f32)))[None, :])
    cos_s = jnp.tile(jnp.cos(ang), (1, LANE // half))[..., None]
    sin_s = jnp.tile(jnp.concatenate([-jnp.sin(ang), jnp.sin(ang)], -1), (1, 2))[..., None]
    vm = lambda shape: pltpu.VMEM(shape, f32)
    tok_scr = lambda: vm((n_tok, LANE, n_seq))

    w_in_t = jnp.swapaxes(w_in, 1, 2)
    st_p = None
    sa_s = sb_s = sc_s = sd_s = conv_s = None
    for l in range(DEPTH):
        pm, pt = _proj(xm, xt, row_vec(emb_ln_g), row_vec(emb_ln_b), w_in_t, l, with_ln=(l == 0))
        wg = jnp.zeros((LANE, N_HEADS * GLA_DK), f32).at[SM_LR:SM_LR + GLA_RANK].set(gla_w_gate[l])
        gains = jnp.stack([delta_norm_g[l], hgrn_norm_g[l], gla_norm_g[l], ret_norm_g[l]], 0)
        wts = (conv_w[l], small_row(delta_a_log[l]), small_row(delta_dt_bias[l]), hgrn_lb_logits,
               wg, row_vec(gla_b_gate[l]), gains)

        cfg = MixCfg(CHUNK, 1, lead_pad, CHUNK, l, 1, None, False, CHUNK, 0, False, (), 1)
        ot, *st_meta = _mixer_call(
            cfg, "mixers_meta", 1, pt, pl.BlockSpec((CHUNK, D_IN_AL), lambda b, i: (meta_blk, 0)),
            wts, tab_m, pl.BlockSpec((CHUNK, GROUP), lambda b, i: (0, 0)), consts_c, (), (),
            jax.ShapeDtypeStruct((rows_tail, D_MODEL), bf16),
            pl.BlockSpec((CHUNK, D_MODEL), lambda b, i: (meta_blk, 0)), 1)

        n_prev = 0 if st_p is None else len(st_p)
        ffn_w = (w_out, w_ffn_gate, w_ffn_up, w_ffn_down)
        n_steps = (bsz // pg) * n_chunks
        slabs = tuple(max(n for n in range(1, n_steps + 1) if w.shape[1] % n == 0 and (w.shape[1] // n) % (2 * SUBLANE) == 0)
                      for w in ffn_w)
        cfg = MixCfg(CHUNK, pg, 0, CHUNK, l, n_chunks, "shared", True, pg * CHUNK, n_prev, True, slabs, n_steps)
        st1 = lambda k: pl.BlockSpec((1, N_HEADS, k, HEAD_DIM), lambda b, i: (0, 0, 0, 0))
        om, *st_p = _mixer_call(
            cfg, "mixers_prompt", bsz // pg, pm.reshape(bsz // pg, pg, seq, D_IN_AL),
            pl.BlockSpec((1, pg, CHUNK, D_IN_AL), lambda b, i: (b, 0, i, 0)),
            wts, tab_p, pl.BlockSpec((CHUNK, GROUP), lambda b, i: (i, 0)), consts_c, st_meta,
            [pl.BlockSpec((1, CONV_W - 1, 3 * GROUP), lambda b, i: (0, 0, 0)),
             st1(HEAD_DIM), st1(HEAD_DIM), st1(GLA_DK), st1(HEAD_DIM)],
            jax.ShapeDtypeStruct((bsz // pg, pg, seq, D_MODEL), bf16),
            pl.BlockSpec((1, pg, CHUNK, D_MODEL), lambda b, i: (b, 0, i, 0)), bsz, states_prev=st_p, cast=ffn_w)
        om = om.reshape(rows_main, D_MODEL)
        st_p, (wo_b, wg_b, wu_b, wd_b) = st_p[:5], st_p[5:]

        pair_col = lambda v: (col_vec(v), pl.BlockSpec((LANE, 1), lambda p: (p, 0)))
        full = lambda a: (a, pl.BlockSpec(a.shape, lambda p: (0,) * a.ndim))
        split = lambda pairs: ([a for a, _ in pairs], [s for _, s in pairs])
        hist = lambda part: (conv_t, pl.BlockSpec((1, CONV_W - 1, n_seq, LANE), lambda p: (l, 0, 0, 2 * part + p)))
        cw_t = conv_w[l].T
        cw = lambda part: (cw_t, pl.BlockSpec((LANE, CONV_W), lambda p: (2 * part + p, 0)))
        small_col = lambda v: jnp.zeros((LANE, 1), f32).at[SM_ALPHA:SM_ALPHA + N_HEADS, 0].set(v)
        extra, especs = split([hist(0), hist(1), hist(2), cw(0), cw(1), cw(2), full(small_col(delta_a_log[l])),
                               full(small_col(delta_dt_bias[l])), pair_col(delta_norm_g[l])])
        ot, sa_s, *conv_s = _bl_call(
            functools.partial(_bl_delta_body, n_tok), "sample_delta", 0, n_tok, n_seq, s_blk, pt,
            [(A_QKV, True), (A_QKV + GROUP, True), (A_QKV + 2 * GROUP, True), (A_GATE, True), (SMALL, False)],
            extra, especs, sd_t, HEAD_DIM, l, ot, sa_s,
            [tok_scr(), tok_scr(), tok_scr(), vm((n_tok, HEAD_DIM, n_seq)), vm((n_tok, SUBLANE, n_seq))],
            conv_prev=conv_s or [])
        extra, especs = split([(hgrn_lb_logits.T, pl.BlockSpec((LANE, DEPTH), lambda p: (p, 0))),
                               pair_col(hgrn_norm_g[l])])
        ot, sb_s = _bl_call(
            functools.partial(_bl_hgrn_body, l, n_tok), "sample_hgrn", 1, n_tok, n_seq, s_blk, pt,
            [(B_Q, True), (B_F, True), (B_I, True), (B_GATE, True)], extra, especs, sh_t, HEAD_DIM, l, ot, sb_s,
            [tok_scr() for _ in range(4)])
        wg_t = jnp.zeros((N_HEADS * GLA_DK, LANE), f32).at[:, SM_LR:SM_LR + GLA_RANK].set(gla_w_gate[l].T)
        extra, especs = split([full(wg_t), full(col_vec(gla_b_gate[l])), pair_col(gla_norm_g[l])])
        ot, sc_s = _bl_call(
            functools.partial(_bl_gla_body, n_tok), "sample_gla", 2, n_tok, n_seq, s_blk, pt,
            [(C_Q, False), (C_K, False), (C_V, True), (C_GATE, True), (SMALL, False)], extra, especs,
            sg_t, GLA_DK, l, ot, sc_s, [tok_scr() for _ in range(4)])
        extra, especs = split([full(cos_s), full(sin_s), pair_col(ret_norm_g[l])])
        ot, sd_s = _bl_call(
            functools.partial(_bl_ret_body, n_tok), "sample_ret", 3, n_tok, n_seq, s_blk, pt,
            [(D_Q, True), (D_K, True), (D_V, True), (D_GATE, True)], extra, especs, sr_t, HEAD_DIM, l, ot, sd_s,
            [tok_scr() for _ in range(3)])

        xm, xt = _post(xm, xt, om, ot, row_vec(emb_ln_g), row_vec(emb_ln_b),
                       row_vec(ln1_g[l]), row_vec(ln1_b[l]), row_vec(ln2_g[l]), row_vec(ln2_b[l]),
                       wo_b, wg_b, wu_b, wd_b, l)

    y_prompt = xm.reshape(bsz, seq, D_MODEL)
    y_sample = jnp.swapaxes(xt[0:rows_s].reshape(n_tok, n_seq, D_MODEL), 0, 1)
    seq_first = lambda s: jnp.transpose(s, (0, 4, 1, 2, 3))
    st_s = (jnp.transpose(jnp.concatenate(conv_s, -1), (0, 2, 1, 3)),
            seq_first(sa_s), seq_first(sb_s), seq_first(sc_s), seq_first(sd_s))
    return (y_prompt, y_sample) + tuple(v for pair in zip(st_p, st_s) for v in pair)
```

```python
import collections
import functools
import math

import numpy as np
import jax
import jax.numpy as jnp
from jax import lax
from jax.experimental import pallas as pl
from jax.experimental.pallas import tpu as pltpu

f32 = jnp.float32
bf16 = jnp.bfloat16

D_MODEL = 1024
N_META = 16
CHUNK = 64
GROUP = 256
N_HEADS = 4
HEAD_DIM = 64
GLA_DK = 32
GLA_RANK = 16
GLA_GATE_NORM = 16.0
CONV_W = 4
D_FF = 2816
DEPTH = 2
ALPHA = (2 * DEPTH) ** 0.25
PAST_LEN = 16384
RET_THETA_BASE = 10000.0
LN_EPS = 1e-5
NORM_EPS = 1e-6
GATE_CLAMP = 1e-6
IN_SPLITS = (GROUP, GROUP, GROUP, N_HEADS, N_HEADS, GROUP,
             GROUP, GROUP, GROUP, GROUP,
             N_HEADS * GLA_DK, N_HEADS * GLA_DK, GROUP, GLA_RANK, GROUP,
             GROUP, GROUP, GROUP, GROUP)

LANE = 128
SUBLANE = 8
VMEM_LIMIT = 56 * 1024 * 1024

COL_ORDER = (0, 1, 2, 5, 6, 7, 8, 9, 10, 11, 12, 14, 15, 16, 17, 18)
A_QKV, A_GATE = 0, 768
B_Q, B_F, B_I, B_GATE = 1024, 1280, 1536, 1792
C_Q, C_K, C_V, C_GATE = 2048, 2176, 2304, 2560
D_Q, D_K, D_V, D_GATE = 2816, 3072, 3328, 3584
SMALL = 3840
SM_BETA, SM_ALPHA, SM_LR = 0, N_HEADS, 2 * N_HEADS
D_IN_AL = SMALL + LANE
SILU_COLS = (A_GATE, B_Q, B_GATE, C_GATE, D_GATE)
CONV_HIST = SUBLANE
ROW_TILE = 512
PROJ_STAGE_ROWS = 256
PROMPT_GROUP = 8
STATE_ROWS_PER_ITER = 8


def _layer_norm(x, g, b):
    mu = jnp.mean(x, -1, keepdims=True)
    xc = x - mu
    var = jnp.mean(xc * xc, -1, keepdims=True)
    return xc * lax.rsqrt(var + LN_EPS) * g + b


def _row_specs(width, n_main, rows_tail):
    return (pl.BlockSpec((ROW_TILE, width), lambda i: (jnp.minimum(i, n_main - 1), 0)),
            pl.BlockSpec((rows_tail, width), lambda i: (0, 0)))


def _const_spec(shape, single_buffer=False):
    nd = len(shape)
    kw = dict(pipeline_mode=pl.Buffered(1)) if single_buffer else {}
    return pl.BlockSpec(shape, lambda *_: (0,) * nd, **kw)


def _on_region(n_main, main_fn, tail_fn):
    i = pl.program_id(0)
    pl.when(i < n_main)(main_fn)
    pl.when(i >= n_main)(tail_fn)


def _w_in_segments():
    offs = np.concatenate([[0], np.cumsum(IN_SPLITS)])
    runs, dst = [], 0
    for i in COL_ORDER + (3, 4, 13):
        if runs and runs[-1][1] == offs[i]:
            runs[-1][1] = int(offs[i + 1])
        else:
            runs.append([int(offs[i]), int(offs[i + 1]), dst])
        dst += IN_SPLITS[i]
    return [tuple(r) for r in runs]


def _proj_tile(with_ln, x_ref, g_ref, b_ref, w_ref, p_ref):
    x = x_ref[...]
    if with_ln:
        x = _layer_norm(x, g_ref[...], b_ref[...])
    h = x.astype(bf16)
    n = w_ref.shape[1]
    for n0 in range(0, n, GROUP):
        n1 = min(n0 + GROUP, n)
        p = jnp.dot(h, w_ref[:, n0:n1], preferred_element_type=f32)
        p_ref[:, n0:n1] = jax.nn.silu(p) if n0 in SILU_COLS else p


def _stage_w_in(wt_hbm, layer, w_scr, stage, sem):
    rows = PROJ_STAGE_ROWS
    big, small = [], []
    for s0, s1, d0 in _w_in_segments():
        if (s1 - s0) % rows == 0 and d0 % LANE == 0:
            big += [(s0 + j, d0 + j) for j in range(0, s1 - s0, rows)]
        else:
            small.append((s0, s1 - s0, d0 - SMALL))
    assert all(0 <= d and d + n <= LANE and d % SUBLANE == 0 for _, n, d in small) and len(small) <= 2

    def copy(j):
        return pltpu.make_async_copy(wt_hbm.at[layer, pl.ds(big[j][0], rows), :], stage.at[j % 2], sem.at[j % 2])
    copy(0).start()
    for j in range(len(big)):
        if j + 1 < len(big):
            copy(j + 1).start()
        copy(j).wait()
        w_scr[:, big[j][1]:big[j][1] + rows] = stage[j % 2].T.astype(bf16)
    used = max(d + n for _, n, d in small)
    stage[0, used:LANE, :] = jnp.zeros((LANE - used, D_MODEL), f32)
    copies = [pltpu.make_async_copy(wt_hbm.at[layer, pl.ds(s0, n), :], stage.at[0, pl.ds(d, n), :], sem.at[i])
              for i, (s0, n, d) in enumerate(small)]
    for cp in copies:
        cp.start()
    for cp in copies:
        cp.wait()
    w_scr[:, SMALL:SMALL + LANE] = stage[0, 0:LANE, :].T.astype(bf16)


def _proj_body(with_ln, layer, n_main, xm_ref, xt_ref, g_ref, b_ref, wt_hbm, pm_ref, pt_ref, w_scr, stage, sem):
    pl.when(pl.program_id(0) == 0)(lambda: _stage_w_in(wt_hbm, layer, w_scr, stage, sem))
    _on_region(n_main,
               lambda: _proj_tile(with_ln, xm_ref, g_ref, b_ref, w_scr, pm_ref),
               lambda: _proj_tile(with_ln, xt_ref, g_ref, b_ref, w_scr, pt_ref))


def _proj(xm, xt, g, b, w_in_t, layer, with_ln):
    n_main, rows_tail = xm.shape[0] // ROW_TILE, xt.shape[0]
    n = D_IN_AL
    out_shape = [jax.ShapeDtypeStruct((xm.shape[0], n), f32), jax.ShapeDtypeStruct((rows_tail, n), f32)]
    out_specs = list(_row_specs(n, n_main, rows_tail))
    return pl.pallas_call(
        functools.partial(_proj_body, with_ln, layer, n_main),
        out_shape=out_shape,
        grid=(n_main + 1,),
        in_specs=list(_row_specs(D_MODEL, n_main, rows_tail))
                 + [_const_spec((1, D_MODEL)), _const_spec((1, D_MODEL)), pl.BlockSpec(memory_space=pl.ANY)],
        out_specs=out_specs,
        scratch_shapes=[pltpu.VMEM((D_MODEL, n), bf16),
                        pltpu.VMEM((2, PROJ_STAGE_ROWS, D_MODEL), f32),
                        pltpu.SemaphoreType.DMA((2,))],
        compiler_params=pltpu.CompilerParams(dimension_semantics=("arbitrary",),
                                             vmem_limit_bytes=VMEM_LIMIT),
        name="in_proj",
    )(xm, xt, g, b, w_in_t)


def _post_tile(emb_ln, x_ref, o_ref, ge_ref, be_ref, wo_ref, g1_ref, b1_ref, wg_ref, wu_ref, wd_ref, g2_ref, b2_ref,
               y_ref):
    x = x_ref[...]
    if emb_ln:
        x = _layer_norm(x, ge_ref[...], be_ref[...])
    m = jnp.dot(o_ref[...], wo_ref[...], preferred_element_type=f32)
    y1 = _layer_norm(ALPHA * x + m, g1_ref[...], b1_ref[...])
    y1b = y1.astype(bf16)
    dff = wg_ref.shape[1]
    step = 4 * LANE
    acc = jnp.zeros(y1.shape, f32)
    for n0 in range(0, dff, step):
        n1 = min(n0 + step, dff)
        gate = jnp.dot(y1b, wg_ref[:, n0:n1], preferred_element_type=f32)
        up = jnp.dot(y1b, wu_ref[:, n0:n1], preferred_element_type=f32)
        act = (jax.nn.silu(gate) * up).astype(bf16)
        acc = acc + jnp.dot(act, wd_ref[n0:n1, :], preferred_element_type=f32)
    y_ref[...] = _layer_norm(ALPHA * y1 + acc, g2_ref[...], b2_ref[...])


def _post_body(layer, n_main, xm_ref, xt_ref, om_ref, ot_ref, ge_ref, be_ref, g1_ref, b1_ref, g2_ref, b2_ref,
               wo_ref, wg_ref, wu_ref, wd_ref, ym_ref, yt_ref):
    w = (ge_ref, be_ref, wo_ref, g1_ref, b1_ref, wg_ref, wu_ref, wd_ref, g2_ref, b2_ref)
    _on_region(n_main, lambda: _post_tile(layer == 0, xm_ref, om_ref, *w, ym_ref),
               lambda: _post_tile(layer == 0, xt_ref, ot_ref, *w, yt_ref))


def _post(xm, xt, om, ot, ge, be, g1, b1, g2, b2, w_out, w_gate, w_up, w_down, layer):
    n_main = xm.shape[0] // ROW_TILE
    rows = lambda: list(_row_specs(D_MODEL, n_main, xt.shape[0]))
    vec = lambda: _const_spec((1, D_MODEL))
    return pl.pallas_call(
        functools.partial(_post_body, layer, n_main),
        out_shape=[jax.ShapeDtypeStruct(xm.shape, f32), jax.ShapeDtypeStruct(xt.shape, f32)],
        grid=(n_main + 1,),
        in_specs=rows() + rows() + [vec() for _ in range(6)]
                 + [_const_spec(w.shape, True) for w in (w_out, w_gate, w_up, w_down)],
        out_specs=rows(),
        compiler_params=pltpu.CompilerParams(dimension_semantics=("arbitrary",),
                                             vmem_limit_bytes=VMEM_LIMIT),
        name="post_ffn",
    )(xm, xt, om, ot, ge, be, g1, b1, g2, b2, w_out, w_gate, w_up, w_down)


SAFE_EXP = 80.0
BASE_BLOCK = 8
assert (BASE_BLOCK // 2) * -math.log(GATE_CLAMP) < SAFE_EXP


def _levels(c, base=0):
    m, out = c // 2, []
    while m >= max(base, 1):
        out.append(m)
        m //= 2
    return tuple(out)


def _chunk_constants(c, base, nblk):
    t = np.arange(c)
    rows = [(t[None, :] <= t[:, None]).astype(np.float32), (t[None, :] > t[:, None]).astype(np.float32)]
    masks = []
    for m in _levels(c, base):
        blk = t // m
        upper = (blk % 2) == 1
        lo = np.where(upper, blk * m, t + 1)
        hi = np.where(upper, t, (blk + 1) * m - 1)
        rows.append(((t[None, :] >= lo[:, None]) & (t[None, :] <= hi[:, None])).astype(np.float32))
        same = (t[:, None] // (2 * m)) == (t[None, :] // (2 * m))
        masks.append((same & upper[:, None] & (~upper)[None, :]).astype(np.float32))
    if base:
        mid = (t // base) * base + base // 2 - 1
        after = (t[None, :] > mid[:, None]) & (t[None, :] <= t[:, None])
        before = (t[None, :] > t[:, None]) & (t[None, :] <= mid[:, None])
        rows.append(after.astype(np.float32) - before.astype(np.float32))
        masks.append(((t[:, None] // base == t[None, :] // base) & (t[None, :] <= t[:, None])).astype(np.float32))
    else:
        masks.append(np.eye(c, dtype=np.float32))
    return np.concatenate(rows, 0), np.tile(np.stack(masks, 0), (1, 1, nblk))


def _mm(a, b):
    return jnp.einsum('gmk,gkn->gmn', a.astype(bf16), b.astype(bf16), preferred_element_type=f32)


def _mm_nt(a, b):
    return jnp.einsum('gmk,gnk->gmn', a.astype(bf16), b.astype(bf16), preferred_element_type=f32)


def _mm_tn(a, b):
    return lax.dot_general(a.astype(bf16), b.astype(bf16), (((1,), (1,)), ((0,), (0,))),
                           preferred_element_type=f32)


def _iota(shape, dim):
    return lax.broadcasted_iota(jnp.int32, shape, dim)


def _bd(z, nblk, blk=None):
    z = z.astype(bf16)
    if blk is None:
        blk = _iota((1, 1, z.shape[-1]), 2) // (z.shape[-1] // nblk)
    return jnp.concatenate([jnp.where(blk == h, z, jnp.zeros_like(z)) for h in range(nblk)], axis=1)


def _hmm(x, z, nblk):
    return _mm(x, _bd(z, nblk))


def _hmm_nt(x, z, nblk):
    return _mm_nt(x, _bd(z, nblk))


def _hmm_split(x, z, nblk, blk=None):
    m = x.shape[1]
    x_hi, z_hi = x.astype(bf16), z.astype(bf16)
    x_lo = (x - x_hi.astype(f32)).astype(bf16)
    z_lo = (z - z_hi.astype(f32)).astype(bf16)
    both = _mm(jnp.concatenate([x_hi, x_lo], 1), _bd(z_hi, nblk, blk))
    return both[:, 0:m] + (both[:, m:2 * m] + _mm(x_hi, _bd(z_lo, nblk, blk)))


def _block_mask(rows, lanes, dr, dl):
    return (_iota((rows, lanes), 0) // dr) == (_iota((rows, lanes), 1) // dl)


def _dot01(m01, x):
    nb = x.shape[0]
    x2 = jnp.concatenate([x[i] for i in range(nb)], -1) if nb > 1 else x[0]
    hi = x2.astype(bf16)
    lo = (x2 - hi.astype(f32)).astype(bf16)
    m01 = m01.astype(bf16)
    d = lambda y: jnp.dot(m01, y, preferred_element_type=f32)
    out = d(hi) + d(lo)
    w = x.shape[-1]
    return jnp.stack([out[:, i * w:(i + 1) * w] for i in range(nb)], 0)


def _seg_sum(x, ones_bd, pieces=2):
    nb, c, w = x.shape
    x2 = x.reshape(nb * c, w)
    hi = x2.astype(bf16)
    out = jnp.dot(hi, ones_bd, preferred_element_type=f32)
    if pieces == 2:
        lo = (x2 - hi.astype(f32)).astype(bf16)
        out = out + jnp.dot(lo, ones_bd, preferred_element_type=f32)
    return out.reshape(nb, c, w)


def _last_row_as_col(b, c):
    return jnp.stack([b[i, c - SUBLANE:c, :].T[:, SUBLANE - 1:SUBLANE] for i in range(b.shape[0])], 0)


def _channel_decay_mixer(q, k, v, gl, s, nblk, sums_ref, masks_ref, bdmask, c, base):
    lv = _levels(c, base)
    nl = len(lv)
    sums = _dot01(sums_ref[...], gl)
    b = sums[:, 0:c]
    rev = sums[:, c:2 * c]
    row = _iota((1, c, 1), 1)
    if base:
        e0 = sums[:, (nl + 2) * c:(nl + 3) * c]
        att = jnp.where(masks_ref[nl] > 0.0, _hmm_nt(q * jnp.exp(e0), k * jnp.exp(-e0), nblk), 0.0)
    else:
        att = masks_ref[nl] * _hmm_nt(q, k, nblk)
    for li, m in enumerate(lv):
        upper = ((row // m) % 2) == 1
        w = jnp.where(upper, q, k) * jnp.exp(sums[:, (li + 2) * c:(li + 3) * c])
        att = att + masks_ref[li] * _hmm_nt(w, w, nblk)
    o = _mm(jnp.concatenate([q * jnp.exp(b), att], -1), jnp.concatenate([s.astype(bf16), _bd(v, nblk)], 1))
    s_new = jnp.exp(_last_row_as_col(b, c)) * s + jnp.where(bdmask, _mm_tn(k * jnp.exp(rev), v), 0.0)
    return o, s_new


MixCfg = collections.namedtuple("MixCfg",
                                "c g r0 r1 layer n_chunks init p4d o_rows n_alias stacked cast_slabs n_steps")


def _pairs_to_bd(s):
    z = jnp.zeros(s.shape[:1] + (HEAD_DIM, HEAD_DIM), f32)
    pair = lambda p: jnp.concatenate([jnp.concatenate([s[:, 2 * p], z], -1),
                                      jnp.concatenate([z, s[:, 2 * p + 1]], -1)], 1)
    return jnp.concatenate([pair(0), pair(1)], 0)


def _quad_to_bd(s):
    z = lambda n: jnp.zeros(s.shape[:1] + (GLA_DK, n * HEAD_DIM), f32)
    rows = []
    for h in range(N_HEADS):
        parts = ([z(h)] if h else []) + [s[:, h]] + ([z(N_HEADS - 1 - h)] if h < N_HEADS - 1 else [])
        rows.append(jnp.concatenate(parts, -1))
    return jnp.concatenate(rows, 1)


def _mixer_body(cfg, *refs):
    c, g = cfg.c, cfg.g
    n_w, n_t, n_c = 7, 3, 4
    p_ref = refs[0]
    cw_ref, alog_ref, dtb_ref, lbl_ref, wg_ref, bg_ref, gains_ref = refs[1:1 + n_w]
    cos_ref, sina_ref, sinb_ref = refs[1 + n_w:1 + n_w + n_t]
    sumsb_ref, masksb_ref, sumsc_ref, masksc_ref = refs[1 + n_w + n_t:1 + n_w + n_t + n_c]
    pos = 1 + n_w + n_t + n_c
    if cfg.init:
        convi_ref, sai_ref, sbi_ref, sci_ref, sdi_ref = refs[pos:pos + 5]
        pos += 5
    n_cast = len(cfg.cast_slabs)
    cast_in = refs[pos:pos + n_cast]
    pos += n_cast + cfg.n_alias
    o_ref, convo_ref, sao_ref, sbo_ref, sco_ref, sdo_ref = refs[pos:pos + 6]
    if cfg.stacked:
        convo_ref, sao_ref, sbo_ref, sco_ref, sdo_ref = (r.at[0] for r in (convo_ref, sao_ref, sbo_ref, sco_ref, sdo_ref))
    cast_out = refs[pos + 6:pos + 6 + n_cast]
    cbuf, sa_scr, sb_scr, sc_scr, sd_scr = refs[pos + 6 + n_cast:]
    ci = pl.program_id(1)

    @pl.when(ci == 0)
    def _():
        cbuf[:, 0:CONV_HIST, :] = jnp.zeros((g, CONV_HIST, 3 * GROUP), f32)
        if not cfg.init:
            for r in (sa_scr, sb_scr, sc_scr, sd_scr):
                r[...] = jnp.zeros(r.shape, f32)
        else:
            take = lambda r: jnp.broadcast_to(r[...], (g,) + r.shape[1:])
            cbuf[:, CONV_HIST - (CONV_W - 1):CONV_HIST, :] = take(convi_ref)
            sa_scr[...] = _pairs_to_bd(take(sai_ref))
            sb_scr[...] = _pairs_to_bd(take(sbi_ref))
            sc_scr[...] = _quad_to_bd(take(sci_ref))
            sd_scr[...] = _pairs_to_bd(take(sdi_ref))

    step = pl.program_id(0) * cfg.n_chunks + ci
    for w_f32, w_bf16, n_slabs in sorted(zip(cast_in, cast_out, cfg.cast_slabs), key=lambda e: e[2]):
        def narrow(src=w_f32, dst=w_bf16):
            dst[...] = src[0].astype(bf16)
        if n_slabs == cfg.n_steps:
            narrow()
        else:
            pl.when(step < n_slabs)(narrow)

    def load(a, b):
        if cfg.p4d:
            return p_ref[0, :, :, a:b]
        return p_ref[:, a:b].reshape(g, c, b - a)

    r0, r1 = cfg.r0, cfg.r1
    n_valid = float(r1 - r0)
    row = _iota((1, c, 1), 1)
    vm = ((row >= r0) & (row < r1)).astype(f32)
    lane_v = _iota((1, 1, 2 * HEAD_DIM), 2)
    lane_s = _iota((1, 1, 2 * c), 2)
    s_idx = lane_s % c
    incl = s_idx <= row
    strict = s_idx < row
    bd2 = _block_mask(2 * HEAD_DIM, 2 * HEAD_DIM, HEAD_DIM, HEAD_DIM)
    bd4 = _block_mask(N_HEADS * GLA_DK, GROUP, GLA_DK, HEAD_DIM)
    ones2 = bd2.astype(bf16)
    pairs = lambda x: jnp.concatenate([x[..., :LANE], x[..., LANE:]], 0)
    unpairs = lambda x: jnp.concatenate([x[:g], x[g:]], -1)
    n_solve = max(1, math.ceil(math.log2(r1 - r0)))
    sm = load(SMALL, SMALL + LANE)
    valid = (lambda x: x) if (r0 == 0 and r1 == c) else (lambda x: x * vm)

    cbuf[:, CONV_HIST:CONV_HIST + c, :] = valid(load(A_QKV, A_QKV + 3 * GROUP))
    acc = cbuf[:, pl.ds(CONV_HIST - 3, c), :] * cw_ref[0:1, :]
    for j in range(1, CONV_W):
        acc = acc + cbuf[:, pl.ds(CONV_HIST - 3 + j, c), :] * cw_ref[j:j + 1, :]
    conv = jax.nn.silu(acc)
    beta = valid(jax.nn.sigmoid(sm))
    g_a = valid(-jnp.exp(alog_ref[...]) * jax.nn.softplus(sm + dtb_ref[...]))
    b_a = _dot01(sumsc_ref[0:c, :], g_a)

    def head_cols(x, lane0):
        col = lambda h: x[..., lane0 + h:lane0 + h + 1]
        return jnp.concatenate([col(0), col(2)], 0), jnp.concatenate([col(1), col(3)], 0)

    def spread(c0, c1, lane, width):
        return jnp.where(lane < width, c0, c1)

    bc0, bc1 = head_cols(b_a, SM_ALPHA)
    bt0, bt1 = head_cols(beta, SM_BETA)
    bc_v, bet_v = spread(bc0, bc1, lane_v, HEAD_DIM), spread(bt0, bt1, lane_v, HEAD_DIM)
    bc_s, bet_s = spread(bc0, bc1, lane_s, c), spread(bt0, bt1, lane_s, c)
    b_rows = [b_a[i].T for i in range(g)]
    b_row = lambda h: jnp.stack([t[SM_ALPHA + h:SM_ALPHA + h + 1, :] for t in b_rows], 0)
    br_s = jnp.concatenate([jnp.concatenate([b_row(0), b_row(1)], -1),
                            jnp.concatenate([b_row(2), b_row(3)], -1)], 0)
    dec = jnp.where(incl, jnp.exp(jnp.where(incl, bc_s - br_s, 0.0)), 0.0)
    q = pairs(conv[..., 0:GROUP])
    k = pairs(conv[..., GROUP:2 * GROUP])
    v = pairs(valid(conv[..., 2 * GROUP:3 * GROUP]))
    q = q * lax.rsqrt(_seg_sum(q * q, ones2, 1) + NORM_EPS) * HEAD_DIM ** -0.5
    k = valid(k * lax.rsqrt(_seg_sum(k * k, ones2) + NORM_EPS))
    s_a = sa_scr[...]
    kq = jnp.concatenate([k, q], 1)
    kq_k = _hmm_nt(kq, k, 2)
    kq_s = _mm(kq, s_a)
    a_mat = jnp.where(strict, bet_s * dec * kq_k[:, 0:c], 0.0)
    gam = jnp.exp(bc_v)
    u = bet_v * (v - gam * kq_s[:, 0:c])
    p = -a_mat
    lane_pu = _iota((1, 1, 2 * c + 2 * HEAD_DIM), 2)
    head_pu = jnp.where(lane_pu < 2 * c, lane_pu // c, (lane_pu - 2 * c) // HEAD_DIM)
    for i in range(n_solve):
        if i + 1 < n_solve:
            pu = _hmm_split(p, jnp.concatenate([p, u], -1), 2, head_pu)
            p, u = pu[..., 0:2 * c], u + pu[..., 2 * c:]
        else:
            u = u + _hmm_split(p, u, 2)
    o_a = gam * kq_s[:, c:2 * c] + _hmm(dec * kq_k[:, c:2 * c], u, 2)
    bl = bc_v[:, c - 1:c, :]
    sa_scr[...] = jnp.exp(bl) * s_a + jnp.where(bd2, _mm_tn(k * jnp.exp(bl - bc_v), u), 0.0)
    o_a = unpairs(o_a * lax.rsqrt(_seg_sum(o_a * o_a, ones2, 1) * (1.0 / HEAD_DIM) + NORM_EPS))

    lbl = lbl_ref[...]
    e = jnp.exp(lbl - jnp.max(lbl, axis=0, keepdims=True))
    prob = e / jnp.sum(e, axis=0, keepdims=True)
    lb = (jnp.sum(prob[1:cfg.layer + 1], axis=0, keepdims=True) if cfg.layer > 0
          else jnp.zeros((1, GROUP), f32))
    kb = (1.0 - lb) * jax.nn.sigmoid(-load(B_F, B_F + GROUP))
    g_b = valid(jnp.log1p(-jnp.clip(kb, 0.0, 1.0 - GATE_CLAMP)))
    o_b, s_b_new = _channel_decay_mixer(
        pairs(load(B_Q, B_Q + GROUP)), pairs(valid(kb)), pairs(valid(load(B_I, B_I + GROUP))),
        pairs(g_b), sb_scr[...], 2, sumsb_ref, masksb_ref, bd2, c, BASE_BLOCK if c > BASE_BLOCK else 0)
    sb_scr[...] = s_b_new
    o_b = unpairs(o_b * lax.rsqrt(_seg_sum(o_b * o_b, ones2, 1) * (1.0 / HEAD_DIM) + NORM_EPS))

    logit = jnp.dot(sm.reshape(g * c, LANE).astype(bf16), wg_ref[...].astype(bf16),
                    preferred_element_type=f32).reshape(g, c, LANE) + bg_ref[...]
    g_c = valid(jax.nn.log_sigmoid(logit) * (1.0 / GLA_GATE_NORM))
    c_args = (load(C_Q, C_Q + N_HEADS * GLA_DK) * GLA_DK ** -0.5, valid(load(C_K, C_K + N_HEADS * GLA_DK)),
              valid(load(C_V, C_V + GROUP)), g_c, sc_scr[...], N_HEADS)
    o_c, s_c_new = _channel_decay_mixer(*c_args, sumsc_ref, masksc_ref, bd4, c, 0)
    sc_scr[...] = s_c_new
    ones4 = _block_mask(GROUP, GROUP, HEAD_DIM, HEAD_DIM).astype(bf16)
    o_c = o_c * lax.rsqrt(_seg_sum(o_c * o_c, ones4, 1) * (1.0 / HEAD_DIM) + NORM_EPS)

    def rope(x):
        x2 = x.reshape(g * c, GROUP)
        rot = lambda sh: pltpu.roll(x2, sh, 1).reshape(g, c, GROUP)
        return x * cos_ref[...] + rot(GROUP - HEAD_DIM // 2) * sina_ref[...] + rot(HEAD_DIM // 2) * sinb_ref[...]
    q = pairs(rope(load(D_Q, D_Q + GROUP)))
    k = pairs(valid(rope(load(D_K, D_K + GROUP)) * HEAD_DIM ** -0.5))
    v = pairs(valid(load(D_V, D_V + GROUP)))
    pair_idx = _iota((2, 1, 1), 0)
    log_gamma = lambda lane, width: jnp.log1p(-jnp.exp2(-5.0 - (2 * pair_idx + lane // width).astype(f32)))
    per_seq = lambda x: jnp.concatenate([jnp.broadcast_to(x[p:p + 1], (g,) + x.shape[1:]) for p in range(2)], 0)
    lg_v, lg_s = log_gamma(lane_v, HEAD_DIM), log_gamma(lane_s, c)
    cnt_c = jnp.clip(row + 1 - r0, 0, r1 - r0).astype(f32)
    cnt_s = jnp.clip(s_idx + 1 - r0, 0, r1 - r0).astype(f32)
    dec = per_seq(jnp.where(incl, jnp.exp(jnp.where(incl, (cnt_c - cnt_s) * lg_s, 0.0)), 0.0))
    s_d = sd_scr[...]
    o_d = _mm(jnp.concatenate([q * per_seq(jnp.exp(cnt_c * lg_v)), dec * _hmm_nt(q, k, 2)], -1),
              jnp.concatenate([s_d.astype(bf16), _bd(v, 2)], 1))
    sd_scr[...] = (per_seq(jnp.exp(n_valid * lg_v)) * s_d
                   + jnp.where(bd2, _mm_tn(k * per_seq(jnp.exp((n_valid - cnt_c) * lg_v)), v), 0.0))
    oc = o_d - _seg_sum(o_d, ones2) * (1.0 / HEAD_DIM)
    o_d = unpairs(oc * lax.rsqrt(_seg_sum(oc * oc, ones2, 1) * (1.0 / HEAD_DIM) + LN_EPS))

    for i, (o_m, gate_col) in enumerate(zip((o_a, o_b, o_c, o_d), (A_GATE, B_GATE, C_GATE, D_GATE))):
        val = (o_m * gains_ref[i:i + 1, :] * load(gate_col, gate_col + GROUP)).astype(o_ref.dtype)
        if cfg.p4d:
            o_ref[0, :, :, i * GROUP:(i + 1) * GROUP] = val
        else:
            o_ref[0:g * c, i * GROUP:(i + 1) * GROUP] = val.reshape(g * c, GROUP)
    if cfg.o_rows > g * c:
        o_ref[g * c:cfg.o_rows, :] = jnp.zeros((cfg.o_rows - g * c, D_MODEL), o_ref.dtype)
    cbuf[:, 0:CONV_HIST, :] = cbuf[:, c:c + CONV_HIST, :]

    @pl.when(ci == cfg.n_chunks - 1)
    def _():
        convo_ref[...] = cbuf[:, pl.ds(CONV_HIST + r1 - (CONV_W - 1), CONV_W - 1), :]
        for scr, out in ((sa_scr, sao_ref), (sb_scr, sbo_ref), (sd_scr, sdo_ref)):
            s = scr[...]
            for p in range(2):
                out[:, 2 * p] = s[p * g:(p + 1) * g, 0:HEAD_DIM, 0:HEAD_DIM]
                out[:, 2 * p + 1] = s[p * g:(p + 1) * g, HEAD_DIM:2 * HEAD_DIM, HEAD_DIM:2 * HEAD_DIM]
        s = sc_scr[...]
        for h in range(N_HEADS):
            sco_ref[:, h] = s[:, h * GLA_DK:(h + 1) * GLA_DK, h * HEAD_DIM:(h + 1) * HEAD_DIM]


def _state_shapes(n, stacked):
    lead = (DEPTH, n) if stacked else (n,)
    return [jax.ShapeDtypeStruct(lead + (CONV_W - 1, 3 * GROUP), f32),
            jax.ShapeDtypeStruct(lead + (N_HEADS, HEAD_DIM, HEAD_DIM), f32),
            jax.ShapeDtypeStruct(lead + (N_HEADS, HEAD_DIM, HEAD_DIM), f32),
            jax.ShapeDtypeStruct(lead + (N_HEADS, GLA_DK, HEAD_DIM), f32),
            jax.ShapeDtypeStruct(lead + (N_HEADS, HEAD_DIM, HEAD_DIM), f32)]


def _state_out_specs(g, layer, stacked):
    if stacked:
        st = lambda k: pl.BlockSpec((1, g, N_HEADS, k, HEAD_DIM), lambda b, i: (layer, b, 0, 0, 0))
        conv = pl.BlockSpec((1, g, CONV_W - 1, 3 * GROUP), lambda b, i: (layer, b, 0, 0))
    else:
        st = lambda k: pl.BlockSpec((g, N_HEADS, k, HEAD_DIM), lambda b, i: (b, 0, 0, 0))
        conv = pl.BlockSpec((g, CONV_W - 1, 3 * GROUP), lambda b, i: (b, 0, 0))
    return [conv, st(HEAD_DIM), st(HEAD_DIM), st(GLA_DK), st(HEAD_DIM)]


def _mixer_call(cfg, name, n_groups, p, p_spec, wts, tables, table_spec, consts, init, init_specs,
                o_shape, o_spec, n_state_rows, o_prev=None, states_prev=None, cast=()):
    c, g = cfg.c, cfg.g
    n_steps = n_groups * cfg.n_chunks
    wspecs =[_const_spec((CONV_W, 3 * GROUP)), _const_spec((1, LANE)), _const_spec((1, LANE)),
              _const_spec((DEPTH, GROUP)), _const_spec((LANE, LANE)), _const_spec((1, LANE)),
              _const_spec((4, GROUP))]
    cspecs = [_const_spec(a.shape) for a in consts]
    in_specs = [p_spec] + wspecs + [table_spec] * 3 + cspecs + list(init_specs)
    args = [p, *wts, *tables, *consts, *init]
    cast_shapes, cast_specs = [], []
    for w, n_slabs in zip(cast, cfg.cast_slabs):
        slab = w.shape[1] // n_slabs
        assert w.shape[1] % n_slabs == 0 and slab % (2 * SUBLANE) == 0 and n_slabs <= n_groups * cfg.n_chunks
        slab_idx = lambda b, i, n=n_slabs: jnp.minimum(b * cfg.n_chunks + i, n - 1)
        args.append(w)
        in_specs.append(pl.BlockSpec((1, slab, w.shape[2]), lambda b, i, f=slab_idx: (cfg.layer, f(b, i), 0)))
        cast_shapes.append(jax.ShapeDtypeStruct(w.shape[1:], bf16))
        cast_specs.append(pl.BlockSpec((slab, w.shape[2]), lambda b, i, f=slab_idx: (f(b, i), 0)))
    assert len(cast) == len(cfg.cast_slabs)
    aliases = {}
    if o_prev is not None:
        aliases[len(args)] = 0
        args.append(o_prev)
    for j, s_prev in enumerate(states_prev or ()):
        aliases[len(args)] = 1 + j
        args.append(s_prev)
    in_specs += [pl.BlockSpec(memory_space=pl.ANY)] * len(aliases)
    assert len(aliases) == cfg.n_alias
    return pl.pallas_call(
        functools.partial(_mixer_body, cfg),
        out_shape=[o_shape] + _state_shapes(n_state_rows, cfg.stacked) + cast_shapes,
        grid=(n_groups, cfg.n_chunks),
        in_specs=in_specs,
        out_specs=[o_spec] + _state_out_specs(g, cfg.layer, cfg.stacked) + cast_specs,
        scratch_shapes=[pltpu.VMEM((g, c + CONV_HIST, 3 * GROUP), f32),
                        pltpu.VMEM((2 * g, 2 * HEAD_DIM, 2 * HEAD_DIM), f32),
                        pltpu.VMEM((2 * g, 2 * HEAD_DIM, 2 * HEAD_DIM), f32),
                        pltpu.VMEM((g, N_HEADS * GLA_DK, GROUP), f32),
                        pltpu.VMEM((2 * g, 2 * HEAD_DIM, 2 * HEAD_DIM), f32)],
        input_output_aliases=aliases,
        compiler_params=pltpu.CompilerParams(dimension_semantics=("arbitrary", "arbitrary"),
                                             vmem_limit_bytes=VMEM_LIMIT),
        name=name,
    )(*args)


def _tok(ref, t, nb):
    return ref[t * nb:(t + 1) * nb, :].T


def _bcast_row(ref, t, r):
    return jnp.broadcast_to(ref[t, pl.ds(r, 1), :], (HEAD_DIM, ref.shape[-1]))


def _bl_diag_loop(n_tok, kdim, j, row0, s_in, s_out, d_scr, k_scr, q_scr, v_scr, d_const=None):
    nb = s_in.shape[-1]

    def body(k, os):
        s_k = s_in[0, j, k]
        r = row0 + k
        outs = []
        for t in range(n_tok):
            d = d_const if d_const is not None else _bcast_row(d_scr, t, r)
            s_k = d * s_k + _bcast_row(k_scr, t, r) * v_scr[t, j * HEAD_DIM:(j + 1) * HEAD_DIM, :]
            outs.append(os[t] + _bcast_row(q_scr, t, r) * s_k)
        s_out[0, j, k] = s_k
        return tuple(outs)
    return lax.fori_loop(0, kdim, body, (jnp.zeros((HEAD_DIM, nb), f32),) * n_tok, unroll=STATE_ROWS_PER_ITER)


def _bl_store(o_ref, os, gate_ref, gain_ref, n_tok, nb, group_norm):
    for t in range(n_tok):
        gate = _tok(gate_ref, t, nb)
        halves = []
        for j in range(2):
            o = os[j][t]
            if group_norm:
                o = o - jnp.mean(o, axis=0, keepdims=True)
                o = o * lax.rsqrt(jnp.mean(o * o, axis=0, keepdims=True) + LN_EPS)
            else:
                o = o * lax.rsqrt(jnp.mean(o * o, axis=0, keepdims=True) + NORM_EPS)
            hs = slice(j * HEAD_DIM, (j + 1) * HEAD_DIM)
            halves.append(o * gain_ref[hs, :] * gate[hs, :])
        o_ref[t * nb:(t + 1) * nb, :] = jnp.concatenate(halves, 0).T.astype(o_ref.dtype)


def _bl_hgrn_body(layer, n_tok, q_ref, f_ref, i_ref, gate_ref, lbl_ref, gain_ref, s_in, o_prev, s_prev,
                  o_ref, s_out, d_scr, k_scr, q_scr, v_scr):
    del o_prev, s_prev
    nb = s_in.shape[-1]
    lbl = lbl_ref[...]
    e = jnp.exp(lbl - jnp.max(lbl, axis=1, keepdims=True))
    prob = e / jnp.sum(e, axis=1, keepdims=True)
    lb = (jnp.sum(prob[:, 1:layer + 1], axis=1, keepdims=True) if layer > 0 else jnp.zeros((LANE, 1), f32))
    for t in range(n_tok):
        kk = (1.0 - lb) * jax.nn.sigmoid(-_tok(f_ref, t, nb))
        d_scr[t] = 1.0 - jnp.clip(kk, 0.0, 1.0 - GATE_CLAMP)
        k_scr[t] = kk
        q_scr[t] = _tok(q_ref, t, nb)
        v_scr[t] = _tok(i_ref, t, nb)
    os = [_bl_diag_loop(n_tok, HEAD_DIM, j, j * HEAD_DIM, s_in, s_out, d_scr, k_scr, q_scr, v_scr) for j in range(2)]
    _bl_store(o_ref, os, gate_ref, gain_ref, n_tok, nb, False)


def _bl_gla_body(n_tok, q_ref, k_ref, v_ref, gate_ref, small_ref, wg_ref, bg_ref, gain_ref, s_in, o_prev, s_prev,
                 o_ref, s_out, d_scr, k_scr, q_scr, v_scr):
    del o_prev, s_prev
    nb = s_in.shape[-1]
    for t in range(n_tok):
        logit = jnp.dot(wg_ref[...].astype(bf16), _tok(small_ref, t, nb).astype(bf16),
                        preferred_element_type=f32) + bg_ref[...]
        d_scr[t] = jnp.exp(jax.nn.log_sigmoid(logit) * (1.0 / GLA_GATE_NORM))
        k_scr[t] = _tok(k_ref, t, nb)
        q_scr[t] = _tok(q_ref, t, nb) * GLA_DK ** -0.5
        v_scr[t] = _tok(v_ref, t, nb)
    p = pl.program_id(0)
    os = [_bl_diag_loop(n_tok, GLA_DK, j, (2 * p + j) * GLA_DK, s_in, s_out, d_scr, k_scr, q_scr, v_scr)
          for j in range(2)]
    _bl_store(o_ref, os, gate_ref, gain_ref, n_tok, nb, False)


def _bl_ret_body(n_tok, q_ref, k_ref, v_ref, gate_ref, cos_ref, sin_ref, gain_ref, s_in, o_prev, s_prev,
                 o_ref, s_out, k_scr, q_scr, v_scr):
    del o_prev, s_prev
    nb = s_in.shape[-1]
    half = HEAD_DIM // 2

    def rope(x, t):
        swapped = jnp.concatenate([x[half:2 * half], x[0:half], x[3 * half:4 * half], x[2 * half:3 * half]], 0)
        return x * cos_ref[t] + swapped * sin_ref[t]
    for t in range(n_tok):
        q_scr[t] = rope(_tok(q_ref, t, nb), t)
        k_scr[t] = rope(_tok(k_ref, t, nb), t) * HEAD_DIM ** -0.5
        v_scr[t] = _tok(v_ref, t, nb)
    p = pl.program_id(0)
    os = []
    for j in range(2):
        head = jnp.zeros((1, nb), f32) + (2 * p + j).astype(f32)
        gamma = jnp.broadcast_to(1.0 - jnp.exp2(-5.0 - head), (HEAD_DIM, nb))
        os.append(_bl_diag_loop(n_tok, HEAD_DIM, j, j * HEAD_DIM, s_in, s_out, None, k_scr, q_scr, v_scr, gamma))
    _bl_store(o_ref, os, gate_ref, gain_ref, n_tok, nb, True)


def _bl_delta_body(n_tok, q_ref, k_ref, v_ref, gate_ref, small_ref, hq_ref, hk_ref, hv_ref, cwq_ref, cwk_ref, cwv_ref,
                   alog_ref, dtb_ref, gain_ref, s_in, o_prev, s_prev, cq_prev, ck_prev, cv_prev,
                   o_ref, s_out, cq_out, ck_out, cv_out, k_scr, q_scr, kh_scr, u_scr, bg_scr):
    del o_prev, s_prev, cq_prev, ck_prev, cv_prev
    nb = s_in.shape[-1]
    p = pl.program_id(0)

    def conv(x_ref, h_ref, cw_ref):
        seq = [h_ref[0, r].T for r in range(CONV_W - 1)] + [_tok(x_ref, t, nb) for t in range(n_tok)]
        outs = []
        for t in range(n_tok):
            acc = cw_ref[:, 0:1] * seq[t]
            for jj in range(1, CONV_W):
                acc = acc + cw_ref[:, jj:jj + 1] * seq[t + jj]
            outs.append(jax.nn.silu(acc))
        return outs

    def l2norm(x):
        parts = []
        for j in range(2):
            xj = x[j * HEAD_DIM:(j + 1) * HEAD_DIM]
            parts.append(xj * lax.rsqrt(jnp.sum(xj * xj, axis=0, keepdims=True) + NORM_EPS))
        return jnp.concatenate(parts, 0)

    qs, ks, vs = conv(q_ref, hq_ref, cwq_ref), conv(k_ref, hk_ref, cwk_ref), conv(v_ref, hv_ref, cwv_ref)
    for t in range(n_tok):
        q_scr[t] = l2norm(qs[t]) * HEAD_DIM ** -0.5
        k_scr[t] = l2norm(ks[t])
        sm = _tok(small_ref, t, nb)
        both = jnp.where(_iota((LANE, 1), 0) < SM_ALPHA, jax.nn.sigmoid(sm),
                         -jnp.exp(alog_ref[...]) * jax.nn.softplus(sm + dtb_ref[...]))
        bg_scr[t] = both[0:SUBLANE]
    for r in range(CONV_W - 1):
        rows = slice((n_tok - (CONV_W - 1) + r) * nb, (n_tok - (CONV_W - 1) + r + 1) * nb)
        cq_out[0, r] = q_ref[rows, :]
        ck_out[0, r] = k_ref[rows, :]
        cv_out[0, r] = v_ref[rows, :]

    os = []
    for j in range(2):
        hs = slice(j * HEAD_DIM, (j + 1) * HEAD_DIM)
        h = 2 * p + j
        beta = [bg_scr[t, pl.ds(SM_BETA + h, 1), :] for t in range(n_tok)]
        b, acc = [], jnp.zeros((1, nb), f32)
        for t in range(n_tok):
            acc = acc + bg_scr[t, pl.ds(SM_ALPHA + h, 1), :]
            b.append(acc)
        kt = [k_scr[t, hs, :] for t in range(n_tok)]
        qt = [q_scr[t, hs, :] for t in range(n_tok)]
        dot = lambda x, y: jnp.sum(x * y, axis=0, keepdims=True)

        def sweep(scr):
            def body(k, accs):
                s_k = s_in[0, j, k]
                return tuple(a + _bcast_row(scr, t, j * HEAD_DIM + k) * s_k for t, a in enumerate(accs))
            return lax.fori_loop(0, HEAD_DIM, body, (jnp.zeros((HEAD_DIM, nb), f32),) * n_tok,
                                 unroll=STATE_ROWS_PER_ITER)
        k_s, q_s = sweep(k_scr), sweep(q_scr)
        us = []
        for t in range(n_tok):
            u = beta[t] * (vs[t][hs] - jnp.exp(b[t]) * k_s[t])
            for s in range(t):
                u = u - (beta[t] * jnp.exp(b[t] - b[s]) * dot(kt[t], kt[s])) * us[s]
            us.append(u)
        o_h = []
        for t in range(n_tok):
            o = jnp.exp(b[t]) * q_s[t]
            for s in range(t + 1):
                o = o + (jnp.exp(b[t] - b[s]) * dot(qt[t], kt[s])) * us[s]
            o_h.append(o)
        os.append(o_h)
        for t in range(n_tok):
            kh_scr[t, hs, :] = kt[t] * jnp.exp(b[-1] - b[t])
            u_scr[t] = us[t]
        decay = jnp.broadcast_to(jnp.exp(b[-1]), (HEAD_DIM, nb))

        def update(k, carry):
            s_k = decay * s_in[0, j, k]
            for t in range(n_tok):
                s_k = s_k + _bcast_row(kh_scr, t, j * HEAD_DIM + k) * u_scr[t]
            s_out[0, j, k] = s_k
            return carry
        lax.fori_loop(0, HEAD_DIM, update, 0, unroll=STATE_ROWS_PER_ITER)
    _bl_store(o_ref, os, gate_ref, gain_ref, n_tok, nb, False)


def _bl_call(body, name, mixer, n_tok, nb, blk_row, pt, cols, extra, extra_specs, state_t, kdim, layer,
             o_prev, s_prev, scratch, conv_prev=None):
    def tok_spec(c0, per_pair):
        return pl.BlockSpec((n_tok * nb, LANE),
                            (lambda p: (blk_row, c0 // LANE + p)) if per_pair else (lambda p: (blk_row, c0 // LANE)))
    st_spec = pl.BlockSpec((1, 2, kdim, HEAD_DIM, nb), lambda p: (layer, p, 0, 0, 0))
    args = [pt] * len(cols) + list(extra) + [state_t]
    in_specs = [tok_spec(c0, pp) for c0, pp in cols] + list(extra_specs) + [st_spec]
    aliases = {}

    def donate(arr, out_idx):
        if arr is not None:
            aliases[len(args)] = out_idx
        args.append(arr if arr is not None else jnp.zeros((1,), f32))
        in_specs.append(pl.BlockSpec(memory_space=pl.ANY))
    donate(o_prev, 0)
    donate(s_prev, 1)
    out_shape = [jax.ShapeDtypeStruct(o_prev.shape, o_prev.dtype),
                 jax.ShapeDtypeStruct((DEPTH,) + state_t.shape[1:], f32)]
    out_specs = [pl.BlockSpec((n_tok * nb, LANE), lambda p: (blk_row, 2 * mixer + p)), st_spec]
    if conv_prev is not None:
        for i in range(3):
            donate(conv_prev[i] if conv_prev else None, 2 + i)
            out_shape.append(jax.ShapeDtypeStruct((DEPTH, CONV_W - 1, nb, GROUP), f32))
            out_specs.append(pl.BlockSpec((1, CONV_W - 1, nb, LANE), lambda p: (layer, 0, 0, p)))
    return pl.pallas_call(
        body, out_shape=out_shape, grid=(2,), in_specs=in_specs, out_specs=out_specs,
        scratch_shapes=scratch, input_output_aliases=aliases,
        compiler_params=pltpu.CompilerParams(dimension_semantics=("arbitrary",), vmem_limit_bytes=VMEM_LIMIT),
        name=name,
    )(*args)


def _rope_tables(pos):
    half = HEAD_DIM // 2
    inv = 1.0 / (RET_THETA_BASE ** jnp.linspace(0.0, 1.0, half, dtype=f32))
    ang = pos[:, None] * inv[None, :]
    cos, sin = jnp.cos(ang), jnp.sin(ang)
    zero = jnp.zeros_like(sin)
    tile = lambda a, b: jnp.tile(jnp.concatenate([a, b], -1), (1, N_HEADS))
    return tile(cos, cos), tile(-sin, zero), tile(zero, sin)


def kernel(x_prompt, x_sample, state_delta_conv, state_delta, state_hgrn, state_gla, state_ret, meta_tokens, emb_ln_g, emb_ln_b, w_in, conv_w, delta_a_log, delta_dt_bias, delta_norm_g, hgrn_lb_logits, hgrn_norm_g, gla_w_gate, gla_b_gate, gla_norm_g, ret_norm_g, w_out, ln1_g, ln1_b, w_ffn_gate, w_ffn_up, w_ffn_down, ln2_g, ln2_b):
    bsz, seq, _ = x_prompt.shape
    n_seq, n_tok, _ = x_sample.shape
    pg = PROMPT_GROUP if bsz % PROMPT_GROUP == 0 else 1
    rows_main = bsz * seq
    rows_s = n_tok * n_seq
    assert seq % CHUNK == 0 and rows_main % ROW_TILE == 0 and N_META <= CHUNK
    assert n_tok >= CONV_W - 1 and rows_s % CHUNK == 0 and (n_seq % LANE == 0 or n_seq < LANE)
    lead_pad = CHUNK - N_META
    rows_tail = rows_s + CHUNK
    s_blk = 0
    meta_blk = rows_s // CHUNK

    xs = jnp.swapaxes(x_sample, 0, 1).reshape(rows_s, D_MODEL)
    xt = jnp.concatenate([xs, jnp.zeros((lead_pad, D_MODEL), f32), meta_tokens.astype(f32)], 0)
    xm = x_prompt.reshape(rows_main, D_MODEL)

    tab_m = _rope_tables(jnp.arange(CHUNK, dtype=f32) - lead_pad)
    tab_p = _rope_tables(N_META + jnp.arange(seq, dtype=f32))
    consts_c = tuple(jnp.asarray(a, f32) for a in (_chunk_constants(CHUNK, BASE_BLOCK, 2)
                                                  + _chunk_constants(CHUNK, 0, N_HEADS)))
    row_vec = lambda v: v.reshape(1, -1).astype(f32)
    col_vec = lambda v: v.reshape(-1, 1).astype(f32)
    small_row = lambda v: jnp.zeros((1, LANE), f32).at[0, SM_ALPHA:SM_ALPHA + N_HEADS].set(v)
    n_chunks = seq // CHUNK

    seq_last = lambda s: jnp.transpose(s, (0, 2, 3, 4, 1))
    sd_t, sh_t, sg_t, sr_t = (seq_last(s) for s in (state_delta, state_hgrn, state_gla, state_ret))
    conv_t = jnp.transpose(state_delta_conv, (0, 2, 1, 3))
    half = HEAD_DIM // 2
    ang = ((PAST_LEN + jnp.arange(n_tok, dtype=f32))[:, None]
           * (1.0 / (RET_THETA_BASE ** jnp.linspace(0.0, 1.0, half, dtype=f32)))[None, :])
    cos_s = jnp.tile(jnp.cos(ang), (1, LANE // half))[..., None]
    sin_s = jnp.tile(jnp.concatenate([-jnp.sin(ang), jnp.sin(ang)], -1), (1, 2))[..., None]
    vm = lambda shape: pltpu.VMEM(shape, f32)
    tok_scr = lambda: vm((n_tok, LANE, n_seq))

    w_in_t = jnp.swapaxes(w_in, 1, 2)
    st_p = None
    sa_s = sb_s = sc_s = sd_s = conv_s = None
    for l in range(DEPTH):
        pm, pt = _proj(xm, xt, row_vec(emb_ln_g), row_vec(emb_ln_b), w_in_t, l, with_ln=(l == 0))
        wg = jnp.zeros((LANE, N_HEADS * GLA_DK), f32).at[SM_LR:SM_LR + GLA_RANK].set(gla_w_gate[l])
        gains = jnp.stack([delta_norm_g[l], hgrn_norm_g[l], gla_norm_g[l], ret_norm_g[l]], 0)
        wts = (conv_w[l], small_row(delta_a_log[l]), small_row(delta_dt_bias[l]), hgrn_lb_logits,
               wg, row_vec(gla_b_gate[l]), gains)

        cfg = MixCfg(CHUNK, 1, lead_pad, CHUNK, l, 1, None, False, CHUNK, 0, False, (), 1)
        ot, *st_meta = _mixer_call(
            cfg, "mixers_meta", 1, pt, pl.BlockSpec((CHUNK, D_IN_AL), lambda b, i: (meta_blk, 0)),
            wts, tab_m, pl.BlockSpec((CHUNK, GROUP), lambda b, i: (0, 0)), consts_c, (), (),
            jax.ShapeDtypeStruct((rows_tail, D_MODEL), bf16),
            pl.BlockSpec((CHUNK, D_MODEL), lambda b, i: (meta_blk, 0)), 1)

        n_prev = 0 if st_p is None else len(st_p)
        ffn_w = (w_out, w_ffn_gate, w_ffn_up, w_ffn_down)
        n_steps = (bsz // pg) * n_chunks
        slabs = tuple(max(n for n in range(1, n_steps + 1) if w.shape[1] % n == 0 and (w.shape[1] // n) % (2 * SUBLANE) == 0)
                      for w in ffn_w)
        cfg = MixCfg(CHUNK, pg, 0, CHUNK, l, n_chunks, "shared", True, pg * CHUNK, n_prev, True, slabs, n_steps)
        st1 = lambda k: pl.BlockSpec((1, N_HEADS, k, HEAD_DIM), lambda b, i: (0, 0, 0, 0))
        om, *st_p = _mixer_call(
            cfg, "mixers_prompt", bsz // pg, pm.reshape(bsz // pg, pg, seq, D_IN_AL),
            pl.BlockSpec((1, pg, CHUNK, D_IN_AL), lambda b, i: (b, 0, i, 0)),
            wts, tab_p, pl.BlockSpec((CHUNK, GROUP), lambda b, i: (i, 0)), consts_c, st_meta,
            [pl.BlockSpec((1, CONV_W - 1, 3 * GROUP), lambda b, i: (0, 0, 0)),
             st1(HEAD_DIM), st1(HEAD_DIM), st1(GLA_DK), st1(HEAD_DIM)],
            jax.ShapeDtypeStruct((bsz // pg, pg, seq, D_MODEL), bf16),
            pl.BlockSpec((1, pg, CHUNK, D_MODEL), lambda b, i: (b, 0, i, 0)), bsz, states_prev=st_p, cast=ffn_w)
        om = om.reshape(rows_main, D_MODEL)
        st_p, (wo_b, wg_b, wu_b, wd_b) = st_p[:5], st_p[5:]

        pair_col = lambda v: (col_vec(v), pl.BlockSpec((LANE, 1), lambda p: (p, 0)))
        full = lambda a: (a, pl.BlockSpec(a.shape, lambda p: (0,) * a.ndim))
        split = lambda pairs: ([a for a, _ in pairs], [s for _, s in pairs])
        hist = lambda part: (conv_t, pl.BlockSpec((1, CONV_W - 1, n_seq, LANE), lambda p: (l, 0, 0, 2 * part + p)))
        cw_t = conv_w[l].T
        cw = lambda part: (cw_t, pl.BlockSpec((LANE, CONV_W), lambda p: (2 * part + p, 0)))
        small_col = lambda v: jnp.zeros((LANE, 1), f32).at[SM_ALPHA:SM_ALPHA + N_HEADS, 0].set(v)
        extra, especs = split([hist(0), hist(1), hist(2), cw(0), cw(1), cw(2), full(small_col(delta_a_log[l])),
                               full(small_col(delta_dt_bias[l])), pair_col(delta_norm_g[l])])
        ot, sa_s, *conv_s = _bl_call(
            functools.partial(_bl_delta_body, n_tok), "sample_delta", 0, n_tok, n_seq, s_blk, pt,
            [(A_QKV, True), (A_QKV + GROUP, True), (A_QKV + 2 * GROUP, True), (A_GATE, True), (SMALL, False)],
            extra, especs, sd_t, HEAD_DIM, l, ot, sa_s,
            [tok_scr(), tok_scr(), tok_scr(), vm((n_tok, HEAD_DIM, n_seq)), vm((n_tok, SUBLANE, n_seq))],
            conv_prev=conv_s or [])
        extra, especs = split([(hgrn_lb_logits.T, pl.BlockSpec((LANE, DEPTH), lambda p: (p, 0))),
                               pair_col(hgrn_norm_g[l])])
        ot, sb_s = _bl_call(
            functools.partial(_bl_hgrn_body, l, n_tok), "sample_hgrn", 1, n_tok, n_seq, s_blk, pt,
            [(B_Q, True), (B_F, True), (B_I, True), (B_GATE, True)], extra, especs, sh_t, HEAD_DIM, l, ot, sb_s,
            [tok_scr() for _ in range(4)])
        wg_t = jnp.zeros((N_HEADS * GLA_DK, LANE), f32).at[:, SM_LR:SM_LR + GLA_RANK].set(gla_w_gate[l].T)
        extra, especs = split([full(wg_t), full(col_vec(gla_b_gate[l])), pair_col(gla_norm_g[l])])
        ot, sc_s = _bl_call(
            functools.partial(_bl_gla_body, n_tok), "sample_gla", 2, n_tok, n_seq, s_blk, pt,
            [(C_Q, False), (C_K, False), (C_V, True), (C_GATE, True), (SMALL, False)], extra, especs,
            sg_t, GLA_DK, l, ot, sc_s, [tok_scr() for _ in range(4)])
        extra, especs = split([full(cos_s), full(sin_s), pair_col(ret_norm_g[l])])
        ot, sd_s = _bl_call(
            functools.partial(_bl_ret_body, n_tok), "sample_ret", 3, n_tok, n_seq, s_blk, pt,
            [(D_Q, True), (D_K, True), (D_V, True), (D_GATE, True)], extra, especs, sr_t, HEAD_DIM, l, ot, sd_s,
            [tok_scr() for _ in range(3)])

        xm, xt = _post(xm, xt, om, ot, row_vec(emb_ln_g), row_vec(emb_ln_b),
                       row_vec(ln1_g[l]), row_vec(ln1_b[l]), row_vec(ln2_g[l]), row_vec(ln2_b[l]),
                       wo_b, wg_b, wu_b, wd_b, l)

    y_prompt = xm.reshape(bsz, seq, D_MODEL)
    y_sample = jnp.swapaxes(xt[0:rows_s].reshape(n_tok, n_seq, D_MODEL), 0, 1)
    seq_first = lambda s: jnp.transpose(s, (0, 4, 1, 2, 3))
    st_s = (jnp.transpose(jnp.concatenate(conv_s, -1), (0, 2, 1, 3)),
            seq_first(sa_s), seq_first(sb_s), seq_first(sc_s), seq_first(sd_s))
    return (y_prompt, y_sample) + tuple(v for pair in zip(st_p, st_s) for v in pair)
```

```python
import collections
import functools
import math

import numpy as np
import jax
import jax.numpy as jnp
from jax import lax
from jax.experimental import pallas as pl
from jax.experimental.pallas import tpu as pltpu

f32 = jnp.float32
bf16 = jnp.bfloat16

D_MODEL = 1024
N_META = 16
CHUNK = 64
GROUP = 256
N_HEADS = 4
HEAD_DIM = 64
GLA_DK = 32
GLA_RANK = 16
GLA_GATE_NORM = 16.0
CONV_W = 4
D_FF = 2816
DEPTH = 2
ALPHA = (2 * DEPTH) ** 0.25
PAST_LEN = 16384
RET_THETA_BASE = 10000.0
LN_EPS = 1e-5
NORM_EPS = 1e-6
GATE_CLAMP = 1e-6
IN_SPLITS = (GROUP, GROUP, GROUP, N_HEADS, N_HEADS, GROUP,
             GROUP, GROUP, GROUP, GROUP,
             N_HEADS * GLA_DK, N_HEADS * GLA_DK, GROUP, GLA_RANK, GROUP,
             GROUP, GROUP, GROUP, GROUP)

LANE = 128
SUBLANE = 8
VMEM_LIMIT = 56 * 1024 * 1024

COL_ORDER = (0, 1, 2, 5, 6, 7, 8, 9, 10, 11, 12, 14, 15, 16, 17, 18)
A_QKV, A_GATE = 0, 768
B_Q, B_F, B_I, B_GATE = 1024, 1280, 1536, 1792
C_Q, C_K, C_V, C_GATE = 2048, 2176, 2304, 2560
D_Q, D_K, D_V, D_GATE = 2816, 3072, 3328, 3584
SMALL = 3840
SM_BETA, SM_ALPHA, SM_LR = 0, N_HEADS, 2 * N_HEADS
D_IN_AL = SMALL + LANE
SILU_COLS = (A_GATE, B_Q, B_GATE, C_GATE, D_GATE)
CONV_HIST = SUBLANE
ROW_TILE = 512
PROJ_STAGE_ROWS = 256
PROMPT_GROUP = 8
STATE_ROWS_PER_ITER = 8


def _layer_norm(x, g, b):
    mu = jnp.mean(x, -1, keepdims=True)
    xc = x - mu
    var = jnp.mean(xc * xc, -1, keepdims=True)
    return xc * lax.rsqrt(var + LN_EPS) * g + b


def _row_specs(width, n_main, rows_tail):
    return (pl.BlockSpec((ROW_TILE, width), lambda i: (jnp.minimum(i, n_main - 1), 0)),
            pl.BlockSpec((rows_tail, width), lambda i: (0, 0)))


def _const_spec(shape, single_buffer=False):
    nd = len(shape)
    kw = dict(pipeline_mode=pl.Buffered(1)) if single_buffer else {}
    return pl.BlockSpec(shape, lambda *_: (0,) * nd, **kw)


def _on_region(n_main, main_fn, tail_fn):
    i = pl.program_id(0)
    pl.when(i < n_main)(main_fn)
    pl.when(i >= n_main)(tail_fn)


def _w_in_segments():
    offs = np.concatenate([[0], np.cumsum(IN_SPLITS)])
    runs, dst = [], 0
    for i in COL_ORDER + (3, 4, 13):
        if runs and runs[-1][1] == offs[i]:
            runs[-1][1] = int(offs[i + 1])
        else:
            runs.append([int(offs[i]), int(offs[i + 1]), dst])
        dst += IN_SPLITS[i]
    return [tuple(r) for r in runs]


def _proj_tile(with_ln, x_ref, g_ref, b_ref, w_ref, p_ref):
    x = x_ref[...]
    if with_ln:
        x = _layer_norm(x, g_ref[...], b_ref[...])
    h = x.astype(bf16)
    n = w_ref.shape[1]
    for n0 in range(0, n, GROUP):
        n1 = min(n0 + GROUP, n)
        p = jnp.dot(h, w_ref[:, n0:n1], preferred_element_type=f32)
        p_ref[:, n0:n1] = jax.nn.silu(p) if n0 in SILU_COLS else p


def _stage_w_in(wt_hbm, layer, w_scr, stage, sem):
    rows = PROJ_STAGE_ROWS
    big, small = [], []
    for s0, s1, d0 in _w_in_segments():
        if (s1 - s0) % rows == 0 and d0 % LANE == 0:
            big += [(s0 + j, d0 + j) for j in range(0, s1 - s0, rows)]
        else:
            small.append((s0, s1 - s0, d0 - SMALL))
    assert all(0 <= d and d + n <= LANE and d % SUBLANE == 0 for _, n, d in small) and len(small) <= 2

    def copy(j):
        return pltpu.make_async_copy(wt_hbm.at[layer, pl.ds(big[j][0], rows), :], stage.at[j % 2], sem.at[j % 2])
    copy(0).start()
    for j in range(len(big)):
        if j + 1 < len(big):
            copy(j + 1).start()
        copy(j).wait()
        w_scr[:, big[j][1]:big[j][1] + rows] = stage[j % 2].T.astype(bf16)
    used = max(d + n for _, n, d in small)
    stage[0, used:LANE, :] = jnp.zeros((LANE - used, D_MODEL), f32)
    copies = [pltpu.make_async_copy(wt_hbm.at[layer, pl.ds(s0, n), :], stage.at[0, pl.ds(d, n), :], sem.at[i])
              for i, (s0, n, d) in enumerate(small)]
    for cp in copies:
        cp.start()
    for cp in copies:
        cp.wait()
    w_scr[:, SMALL:SMALL + LANE] = stage[0, 0:LANE, :].T.astype(bf16)


def _proj_body(with_ln, layer, n_main, xm_ref, xt_ref, g_ref, b_ref, wt_hbm, pm_ref, pt_ref, w_scr, stage, sem):
    pl.when(pl.program_id(0) == 0)(lambda: _stage_w_in(wt_hbm, layer, w_scr, stage, sem))
    _on_region(n_main,
               lambda: _proj_tile(with_ln, xm_ref, g_ref, b_ref, w_scr, pm_ref),
               lambda: _proj_tile(with_ln, xt_ref, g_ref, b_ref, w_scr, pt_ref))


def _proj(xm, xt, g, b, w_in_t, layer, with_ln):
    n_main, rows_tail = xm.shape[0] // ROW_TILE, xt.shape[0]
    n = D_IN_AL
    out_shape = [jax.ShapeDtypeStruct((xm.shape[0], n), f32), jax.ShapeDtypeStruct((rows_tail, n), f32)]
    out_specs = list(_row_specs(n, n_main, rows_tail))
    return pl.pallas_call(
        functools.partial(_proj_body, with_ln, layer, n_main),
        out_shape=out_shape,
        grid=(n_main + 1,),
        in_specs=list(_row_specs(D_MODEL, n_main, rows_tail))
                 + [_const_spec((1, D_MODEL)), _const_spec((1, D_MODEL)), pl.BlockSpec(memory_space=pl.ANY)],
        out_specs=out_specs,
        scratch_shapes=[pltpu.VMEM((D_MODEL, n), bf16),
                        pltpu.VMEM((2, PROJ_STAGE_ROWS, D_MODEL), f32),
                        pltpu.SemaphoreType.DMA((2,))],
        compiler_params=pltpu.CompilerParams(dimension_semantics=("arbitrary",),
                                             vmem_limit_bytes=VMEM_LIMIT),
        name="in_proj",
    )(xm, xt, g, b, w_in_t)


def _post_tile(emb_ln, x_ref, o_ref, ge_ref, be_ref, wo_ref, g1_ref, b1_ref, wg_ref, wu_ref, wd_ref, g2_ref, b2_ref,
               y_ref):
    x = x_ref[...]
    if emb_ln:
        x = _layer_norm(x, ge_ref[...], be_ref[...])
    m = jnp.dot(o_ref[...], wo_ref[...], preferred_element_type=f32)
    y1 = _layer_norm(ALPHA * x + m, g1_ref[...], b1_ref[...])
    y1b = y1.astype(bf16)
    dff = wg_ref.shape[1]
    step = 4 * LANE
    acc = jnp.zeros(y1.shape, f32)
    for n0 in range(0, dff, step):
        n1 = min(n0 + step, dff)
        gate = jnp.dot(y1b, wg_ref[:, n0:n1], preferred_element_type=f32)
        up = jnp.dot(y1b, wu_ref[:, n0:n1], preferred_element_type=f32)
        act = (jax.nn.silu(gate) * up).astype(bf16)
        acc = acc + jnp.dot(act, wd_ref[n0:n1, :], preferred_element_type=f32)
    y_ref[...] = _layer_norm(ALPHA * y1 + acc, g2_ref[...], b2_ref[...])


def _post_body(layer, n_main, xm_ref, xt_ref, om_ref, ot_ref, ge_ref, be_ref, g1_ref, b1_ref, g2_ref, b2_ref,
               wo_ref, wg_ref, wu_ref, wd_ref, ym_ref, yt_ref):
    w = (ge_ref, be_ref, wo_ref, g1_ref, b1_ref, wg_ref, wu_ref, wd_ref, g2_ref, b2_ref)
    _on_region(n_main, lambda: _post_tile(layer == 0, xm_ref, om_ref, *w, ym_ref),
               lambda: _post_tile(layer == 0, xt_ref, ot_ref, *w, yt_ref))


def _post(xm, xt, om, ot, ge, be, g1, b1, g2, b2, w_out, w_gate, w_up, w_down, layer):
    n_main = xm.shape[0] // ROW_TILE
    rows = lambda: list(_row_specs(D_MODEL, n_main, xt.shape[0]))
    vec = lambda: _const_spec((1, D_MODEL))
    return pl.pallas_call(
        functools.partial(_post_body, layer, n_main),
        out_shape=[jax.ShapeDtypeStruct(xm.shape, f32), jax.ShapeDtypeStruct(xt.shape, f32)],
        grid=(n_main + 1,),
        in_specs=rows() + rows() + [vec() for _ in range(6)]
                 + [_const_spec(w.shape, True) for w in (w_out, w_gate, w_up, w_down)],
        out_specs=rows(),
        compiler_params=pltpu.CompilerParams(dimension_semantics=("arbitrary",),
                                             vmem_limit_bytes=VMEM_LIMIT),
        name="post_ffn",
    )(xm, xt, om, ot, ge, be, g1, b1, g2, b2, w_out, w_gate, w_up, w_down)


SAFE_EXP = 80.0
BASE_BLOCK = 8
assert (BASE_BLOCK // 2) * -math.log(GATE_CLAMP) < SAFE_EXP


def _levels(c, base=0):
    m, out = c // 2, []
    while m >= max(base, 1):
        out.append(m)
        m //= 2
    return tuple(out)


def _chunk_constants(c, base, nblk):
    t = np.arange(c)
    rows = [(t[None, :] <= t[:, None]).astype(np.float32), (t[None, :] > t[:, None]).astype(np.float32)]
    masks = []
    for m in _levels(c, base):
        blk = t // m
        upper = (blk % 2) == 1
        lo = np.where(upper, blk * m, t + 1)
        hi = np.where(upper, t, (blk + 1) * m - 1)
        rows.append(((t[None, :] >= lo[:, None]) & (t[None, :] <= hi[:, None])).astype(np.float32))
        same = (t[:, None] // (2 * m)) == (t[None, :] // (2 * m))
        masks.append((same & upper[:, None] & (~upper)[None, :]).astype(np.float32))
    if base:
        mid = (t // base) * base + base // 2 - 1
        after = (t[None, :] > mid[:, None]) & (t[None, :] <= t[:, None])
        before = (t[None, :] > t[:, None]) & (t[None, :] <= mid[:, None])
        rows.append(after.astype(np.float32) - before.astype(np.float32))
        masks.append(((t[:, None] // base == t[None, :] // base) & (t[None, :] <= t[:, None])).astype(np.float32))
    else:
        masks.append(np.eye(c, dtype=np.float32))
    return np.concatenate(rows, 0), np.tile(np.stack(masks, 0), (1, 1, nblk))


def _mm(a, b):
    return jnp.einsum('gmk,gkn->gmn', a.astype(bf16), b.astype(bf16), preferred_element_type=f32)


def _mm_nt(a, b):
    return jnp.einsum('gmk,gnk->gmn', a.astype(bf16), b.astype(bf16), preferred_element_type=f32)


def _mm_tn(a, b):
    return lax.dot_general(a.astype(bf16), b.astype(bf16), (((1,), (1,)), ((0,), (0,))),
                           preferred_element_type=f32)


def _iota(shape, dim):
    return lax.broadcasted_iota(jnp.int32, shape, dim)


def _bd(z, nblk, blk=None):
    z = z.astype(bf16)
    if blk is None:
        blk = _iota((1, 1, z.shape[-1]), 2) // (z.shape[-1] // nblk)
    return jnp.concatenate([jnp.where(blk == h, z, jnp.zeros_like(z)) for h in range(nblk)], axis=1)


def _hmm(x, z, nblk):
    return _mm(x, _bd(z, nblk))


def _hmm_nt(x, z, nblk):
    return _mm_nt(x, _bd(z, nblk))


def _hmm_split(x, z, nblk, blk=None):
    m = x.shape[1]
    x_hi, z_hi = x.astype(bf16), z.astype(bf16)
    x_lo = (x - x_hi.astype(f32)).astype(bf16)
    z_lo = (z - z_hi.astype(f32)).astype(bf16)
    both = _mm(jnp.concatenate([x_hi, x_lo], 1), _bd(z_hi, nblk, blk))
    return both[:, 0:m] + (both[:, m:2 * m] + _mm(x_hi, _bd(z_lo, nblk, blk)))


def _block_mask(rows, lanes, dr, dl):
    return (_iota((rows, lanes), 0) // dr) == (_iota((rows, lanes), 1) // dl)


def _dot01(m01, x):
    nb = x.shape[0]
    x2 = jnp.concatenate([x[i] for i in range(nb)], -1) if nb > 1 else x[0]
    hi = x2.astype(bf16)
    lo = (x2 - hi.astype(f32)).astype(bf16)
    m01 = m01.astype(bf16)
    d = lambda y: jnp.dot(m01, y, preferred_element_type=f32)
    out = d(hi) + d(lo)
    w = x.shape[-1]
    return jnp.stack([out[:, i * w:(i + 1) * w] for i in range(nb)], 0)


def _seg_sum(x, ones_bd, pieces=2):
    nb, c, w = x.shape
    x2 = x.reshape(nb * c, w)
    hi = x2.astype(bf16)
    out = jnp.dot(hi, ones_bd, preferred_element_type=f32)
    if pieces == 2:
        lo = (x2 - hi.astype(f32)).astype(bf16)
        out = out + jnp.dot(lo, ones_bd, preferred_element_type=f32)
    return out.reshape(nb, c, w)


def _last_row_as_col(b, c):
    return jnp.stack([b[i, c - SUBLANE:c, :].T[:, SUBLANE - 1:SUBLANE] for i in range(b.shape[0])], 0)


def _channel_decay_mixer(q, k, v, gl, s, nblk, sums_ref, masks_ref, bdmask, c, base):
    lv = _levels(c, base)
    nl = len(lv)
    sums = _dot01(sums_ref[...], gl)
    b = sums[:, 0:c]
    rev = sums[:, c:2 * c]
    row = _iota((1, c, 1), 1)
    if base:
        e0 = sums[:, (nl + 2) * c:(nl + 3) * c]
        att = jnp.where(masks_ref[nl] > 0.0, _hmm_nt(q * jnp.exp(e0), k * jnp.exp(-e0), nblk), 0.0)
    else:
        att = masks_ref[nl] * _hmm_nt(q, k, nblk)
    for li, m in enumerate(lv):
        upper = ((row // m) % 2) == 1
        w = jnp.where(upper, q, k) * jnp.exp(sums[:, (li + 2) * c:(li + 3) * c])
        att = att + masks_ref[li] * _hmm_nt(w, w, nblk)
    o = _mm(jnp.concatenate([q * jnp.exp(b), att], -1), jnp.concatenate([s.astype(bf16), _bd(v, nblk)], 1))
    s_new = jnp.exp(_last_row_as_col(b, c)) * s + jnp.where(bdmask, _mm_tn(k * jnp.exp(rev), v), 0.0)
    return o, s_new


MixCfg = collections.namedtuple("MixCfg",
                                "c g r0 r1 layer n_chunks init p4d o_rows n_alias stacked cast_slabs n_steps")


def _pairs_to_bd(s):
    z = jnp.zeros(s.shape[:1] + (HEAD_DIM, HEAD_DIM), f32)
    pair = lambda p: jnp.concatenate([jnp.concatenate([s[:, 2 * p], z], -1),
                                      jnp.concatenate([z, s[:, 2 * p + 1]], -1)], 1)
    return jnp.concatenate([pair(0), pair(1)], 0)


def _quad_to_bd(s):
    z = lambda n: jnp.zeros(s.shape[:1] + (GLA_DK, n * HEAD_DIM), f32)
    rows = []
    for h in range(N_HEADS):
        parts = ([z(h)] if h else []) + [s[:, h]] + ([z(N_HEADS - 1 - h)] if h < N_HEADS - 1 else [])
        rows.append(jnp.concatenate(parts, -1))
    return jnp.concatenate(rows, 1)


def _mixer_body(cfg, *refs):
    c, g = cfg.c, cfg.g
    n_w, n_t, n_c = 7, 3, 4
    p_ref = refs[0]
    cw_ref, alog_ref, dtb_ref, lbl_ref, wg_ref, bg_ref, gains_ref = refs[1:1 + n_w]
    cos_ref, sina_ref, sinb_ref = refs[1 + n_w:1 + n_w + n_t]
    sumsb_ref, masksb_ref, sumsc_ref, masksc_ref = refs[1 + n_w + n_t:1 + n_w + n_t + n_c]
    pos = 1 + n_w + n_t + n_c
    if cfg.init:
        convi_ref, sai_ref, sbi_ref, sci_ref, sdi_ref = refs[pos:pos + 5]
        pos += 5
    n_cast = len(cfg.cast_slabs)
    cast_in = refs[pos:pos + n_cast]
    pos += n_cast + cfg.n_alias
    o_ref, convo_ref, sao_ref, sbo_ref, sco_ref, sdo_ref = refs[pos:pos + 6]
    if cfg.stacked:
        convo_ref, sao_ref, sbo_ref, sco_ref, sdo_ref = (r.at[0] for r in (convo_ref, sao_ref, sbo_ref, sco_ref, sdo_ref))
    cast_out = refs[pos + 6:pos + 6 + n_cast]
    cbuf, sa_scr, sb_scr, sc_scr, sd_scr = refs[pos + 6 + n_cast:]
    ci = pl.program_id(1)

    @pl.when(ci == 0)
    def _():
        cbuf[:, 0:CONV_HIST, :] = jnp.zeros((g, CONV_HIST, 3 * GROUP), f32)
        if not cfg.init:
            for r in (sa_scr, sb_scr, sc_scr, sd_scr):
                r[...] = jnp.zeros(r.shape, f32)
        else:
            take = lambda r: jnp.broadcast_to(r[...], (g,) + r.shape[1:])
            cbuf[:, CONV_HIST - (CONV_W - 1):CONV_HIST, :] = take(convi_ref)
            sa_scr[...] = _pairs_to_bd(take(sai_ref))
            sb_scr[...] = _pairs_to_bd(take(sbi_ref))
            sc_scr[...] = _quad_to_bd(take(sci_ref))
            sd_scr[...] = _pairs_to_bd(take(sdi_ref))

    step = pl.program_id(0) * cfg.n_chunks + ci
    for w_f32, w_bf16, n_slabs in sorted(zip(cast_in, cast_out, cfg.cast_slabs), key=lambda e: e[2]):
        def narrow(src=w_f32, dst=w_bf16):
            dst[...] = src[0].astype(bf16)
        if n_slabs == cfg.n_steps:
            narrow()
        else:
            pl.when(step < n_slabs)(narrow)

    def load(a, b):
        if cfg.p4d:
            return p_ref[0, :, :, a:b]
        return p_ref[:, a:b].reshape(g, c, b - a)

    r0, r1 = cfg.r0, cfg.r1
    n_valid = float(r1 - r0)
    row = _iota((1, c, 1), 1)
    vm = ((row >= r0) & (row < r1)).astype(f32)
    lane_v = _iota((1, 1, 2 * HEAD_DIM), 2)
    lane_s = _iota((1, 1, 2 * c), 2)
    s_idx = lane_s % c
    incl = s_idx <= row
    strict = s_idx < row
    bd2 = _block_mask(2 * HEAD_DIM, 2 * HEAD_DIM, HEAD_DIM, HEAD_DIM)
    bd4 = _block_mask(N_HEADS * GLA_DK, GROUP, GLA_DK, HEAD_DIM)
    ones2 = bd2.astype(bf16)
    pairs = lambda x: jnp.concatenate([x[..., :LANE], x[..., LANE:]], 0)
    unpairs = lambda x: jnp.concatenate([x[:g], x[g:]], -1)
    n_solve = max(1, math.ceil(math.log2(r1 - r0)))
    sm = load(SMALL, SMALL + LANE)
    valid = (lambda x: x) if (r0 == 0 and r1 == c) else (lambda x: x * vm)

    cbuf[:, CONV_HIST:CONV_HIST + c, :] = valid(load(A_QKV, A_QKV + 3 * GROUP))
    buf = cbuf[...].reshape(g * (c + CONV_HIST), 3 * GROUP)
    acc = cbuf[:, CONV_HIST:CONV_HIST + c, :] * cw_ref[CONV_W - 1:CONV_W, :]
    for j in range(CONV_W - 1):
        back = pltpu.roll(buf, CONV_W - 1 - j, 0).reshape(g, c + CONV_HIST, 3 * GROUP)
        acc = acc + back[:, CONV_HIST:CONV_HIST + c, :] * cw_ref[j:j + 1, :]
    conv = jax.nn.silu(acc)
    beta = valid(jax.nn.sigmoid(sm))
    g_a = valid(-jnp.exp(alog_ref[...]) * jax.nn.softplus(sm + dtb_ref[...]))
    b_a = _dot01(sumsc_ref[0:c, :], g_a)

    def head_cols(x, lane0):
        col = lambda h: x[..., lane0 + h:lane0 + h + 1]
        return jnp.concatenate([col(0), col(2)], 0), jnp.concatenate([col(1), col(3)], 0)

    def spread(c0, c1, lane, width):
        return jnp.where(lane < width, c0, c1)

    bc0, bc1 = head_cols(b_a, SM_ALPHA)
    bt0, bt1 = head_cols(beta, SM_BETA)
    bc_v, bet_v = spread(bc0, bc1, lane_v, HEAD_DIM), spread(bt0, bt1, lane_v, HEAD_DIM)
    bc_s, bet_s = spread(bc0, bc1, lane_s, c), spread(bt0, bt1, lane_s, c)
    b_rows = [b_a[i].T for i in range(g)]
    b_row = lambda h: jnp.stack([t[SM_ALPHA + h:SM_ALPHA + h + 1, :] for t in b_rows], 0)
    br_s = jnp.concatenate([jnp.concatenate([b_row(0), b_row(1)], -1),
                            jnp.concatenate([b_row(2), b_row(3)], -1)], 0)
    dec = jnp.where(incl, jnp.exp(jnp.where(incl, bc_s - br_s, 0.0)), 0.0)
    q = pairs(conv[..., 0:GROUP])
    k = pairs(conv[..., GROUP:2 * GROUP])
    v = pairs(valid(conv[..., 2 * GROUP:3 * GROUP]))
    q = q * lax.rsqrt(_seg_sum(q * q, ones2, 1) + NORM_EPS) * HEAD_DIM ** -0.5
    k = valid(k * lax.rsqrt(_seg_sum(k * k, ones2) + NORM_EPS))
    s_a = sa_scr[...]
    kq = jnp.concatenate([k, q], 1)
    kq_k = _hmm_nt(kq, k, 2)
    kq_s = _mm(kq, s_a)
    a_mat = jnp.where(strict, bet_s * dec * kq_k[:, 0:c], 0.0)
    gam = jnp.exp(bc_v)
    u = bet_v * (v - gam * kq_s[:, 0:c])
    p = -a_mat
    lane_pu = _iota((1, 1, 2 * c + 2 * HEAD_DIM), 2)
    head_pu = jnp.where(lane_pu < 2 * c, lane_pu // c, (lane_pu - 2 * c) // HEAD_DIM)
    for i in range(n_solve):
        if i + 1 < n_solve:
            pu = _hmm_split(p, jnp.concatenate([p, u], -1), 2, head_pu)
            p, u = pu[..., 0:2 * c], u + pu[..., 2 * c:]
        else:
            u = u + _hmm_split(p, u, 2)
    o_a = gam * kq_s[:, c:2 * c] + _hmm(dec * kq_k[:, c:2 * c], u, 2)
    bl = bc_v[:, c - 1:c, :]
    sa_scr[...] = jnp.exp(bl) * s_a + jnp.where(bd2, _mm_tn(k * jnp.exp(bl - bc_v), u), 0.0)
    o_a = unpairs(o_a * lax.rsqrt(_seg_sum(o_a * o_a, ones2, 1) * (1.0 / HEAD_DIM) + NORM_EPS))

    lbl = lbl_ref[...]
    e = jnp.exp(lbl - jnp.max(lbl, axis=0, keepdims=True))
    prob = e / jnp.sum(e, axis=0, keepdims=True)
    lb = (jnp.sum(prob[1:cfg.layer + 1], axis=0, keepdims=True) if cfg.layer > 0
          else jnp.zeros((1, GROUP), f32))
    kb = (1.0 - lb) * jax.nn.sigmoid(-load(B_F, B_F + GROUP))
    g_b = valid(jnp.log1p(-jnp.clip(kb, 0.0, 1.0 - GATE_CLAMP)))
    o_b, s_b_new = _channel_decay_mixer(
        pairs(load(B_Q, B_Q + GROUP)), pairs(valid(kb)), pairs(valid(load(B_I, B_I + GROUP))),
        pairs(g_b), sb_scr[...], 2, sumsb_ref, masksb_ref, bd2, c, BASE_BLOCK if c > BASE_BLOCK else 0)
    sb_scr[...] = s_b_new
    o_b = unpairs(o_b * lax.rsqrt(_seg_sum(o_b * o_b, ones2, 1) * (1.0 / HEAD_DIM) + NORM_EPS))

    logit = jnp.dot(sm.reshape(g * c, LANE).astype(bf16), wg_ref[...].astype(bf16),
                    preferred_element_type=f32).reshape(g, c, LANE) + bg_ref[...]
    g_c = valid(jax.nn.log_sigmoid(logit) * (1.0 / GLA_GATE_NORM))
    c_args = (load(C_Q, C_Q + N_HEADS * GLA_DK) * GLA_DK ** -0.5, valid(load(C_K, C_K + N_HEADS * GLA_DK)),
              valid(load(C_V, C_V + GROUP)), g_c, sc_scr[...], N_HEADS)
    o_c, s_c_new = _channel_decay_mixer(*c_args, sumsc_ref, masksc_ref, bd4, c, 0)
    sc_scr[...] = s_c_new
    ones4 = _block_mask(GROUP, GROUP, HEAD_DIM, HEAD_DIM).astype(bf16)
    o_c = o_c * lax.rsqrt(_seg_sum(o_c * o_c, ones4, 1) * (1.0 / HEAD_DIM) + NORM_EPS)

    def rope(x):
        x2 = x.reshape(g * c, GROUP)
        rot = lambda sh: pltpu.roll(x2, sh, 1).reshape(g, c, GROUP)
        return x * cos_ref[...] + rot(GROUP - HEAD_DIM // 2) * sina_ref[...] + rot(HEAD_DIM // 2) * sinb_ref[...]
    q = pairs(rope(load(D_Q, D_Q + GROUP)))
    k = pairs(valid(rope(load(D_K, D_K + GROUP)) * HEAD_DIM ** -0.5))
    v = pairs(valid(load(D_V, D_V + GROUP)))
    pair_idx = _iota((2, 1, 1), 0)
    log_gamma = lambda lane, width: jnp.log1p(-jnp.exp2(-5.0 - (2 * pair_idx + lane // width).astype(f32)))
    per_seq = lambda x: jnp.concatenate([jnp.broadcast_to(x[p:p + 1], (g,) + x.shape[1:]) for p in range(2)], 0)
    lg_v, lg_s = log_gamma(lane_v, HEAD_DIM), log_gamma(lane_s, c)
    cnt_c = jnp.clip(row + 1 - r0, 0, r1 - r0).astype(f32)
    cnt_s = jnp.clip(s_idx + 1 - r0, 0, r1 - r0).astype(f32)
    dec = per_seq(jnp.where(incl, jnp.exp(jnp.where(incl, (cnt_c - cnt_s) * lg_s, 0.0)), 0.0))
    s_d = sd_scr[...]
    o_d = _mm(jnp.concatenate([q * per_seq(jnp.exp(cnt_c * lg_v)), dec * _hmm_nt(q, k, 2)], -1),
              jnp.concatenate([s_d.astype(bf16), _bd(v, 2)], 1))
    sd_scr[...] = (per_seq(jnp.exp(n_valid * lg_v)) * s_d
                   + jnp.where(bd2, _mm_tn(k * per_seq(jnp.exp((n_valid - cnt_c) * lg_v)), v), 0.0))
    oc = o_d - _seg_sum(o_d, ones2) * (1.0 / HEAD_DIM)
    o_d = unpairs(oc * lax.rsqrt(_seg_sum(oc * oc, ones2, 1) * (1.0 / HEAD_DIM) + LN_EPS))

    for i, (o_m, gate_col) in enumerate(zip((o_a, o_b, o_c, o_d), (A_GATE, B_GATE, C_GATE, D_GATE))):
        val = (o_m * gains_ref[i:i + 1, :] * load(gate_col, gate_col + GROUP)).astype(o_ref.dtype)
        if cfg.p4d:
            o_ref[0, :, :, i * GROUP:(i + 1) * GROUP] = val
        else:
            o_ref[0:g * c, i * GROUP:(i + 1) * GROUP] = val.reshape(g * c, GROUP)
    if cfg.o_rows > g * c:
        o_ref[g * c:cfg.o_rows, :] = jnp.zeros((cfg.o_rows - g * c, D_MODEL), o_ref.dtype)
    cbuf[:, 0:CONV_HIST, :] = cbuf[:, c:c + CONV_HIST, :]

    @pl.when(ci == cfg.n_chunks - 1)
    def _():
        convo_ref[...] = cbuf[:, pl.ds(CONV_HIST + r1 - (CONV_W - 1), CONV_W - 1), :]
        for scr, out in ((sa_scr, sao_ref), (sb_scr, sbo_ref), (sd_scr, sdo_ref)):
            s = scr[...]
            for p in range(2):
                out[:, 2 * p] = s[p * g:(p + 1) * g, 0:HEAD_DIM, 0:HEAD_DIM]
                out[:, 2 * p + 1] = s[p * g:(p + 1) * g, HEAD_DIM:2 * HEAD_DIM, HEAD_DIM:2 * HEAD_DIM]
        s = sc_scr[...]
        for h in range(N_HEADS):
            sco_ref[:, h] = s[:, h * GLA_DK:(h + 1) * GLA_DK, h * HEAD_DIM:(h + 1) * HEAD_DIM]


def _state_shapes(n, stacked):
    lead = (DEPTH, n) if stacked else (n,)
    return [jax.ShapeDtypeStruct(lead + (CONV_W - 1, 3 * GROUP), f32),
            jax.ShapeDtypeStruct(lead + (N_HEADS, HEAD_DIM, HEAD_DIM), f32),
            jax.ShapeDtypeStruct(lead + (N_HEADS, HEAD_DIM, HEAD_DIM), f32),
            jax.ShapeDtypeStruct(lead + (N_HEADS, GLA_DK, HEAD_DIM), f32),
            jax.ShapeDtypeStruct(lead + (N_HEADS, HEAD_DIM, HEAD_DIM), f32)]


def _state_out_specs(g, layer, stacked):
    if stacked:
        st = lambda k: pl.BlockSpec((1, g, N_HEADS, k, HEAD_DIM), lambda b, i: (layer, b, 0, 0, 0))
        conv = pl.BlockSpec((1, g, CONV_W - 1, 3 * GROUP), lambda b, i: (layer, b, 0, 0))
    else:
        st = lambda k: pl.BlockSpec((g, N_HEADS, k, HEAD_DIM), lambda b, i: (b, 0, 0, 0))
        conv = pl.BlockSpec((g, CONV_W - 1, 3 * GROUP), lambda b, i: (b, 0, 0))
    return [conv, st(HEAD_DIM), st(HEAD_DIM), st(GLA_DK), st(HEAD_DIM)]


def _mixer_call(cfg, name, n_groups, p, p_spec, wts, tables, table_spec, consts, init, init_specs,
                o_shape, o_spec, n_state_rows, o_prev=None, states_prev=None, cast=()):
    c, g = cfg.c, cfg.g
    n_steps = n_groups * cfg.n_chunks
    wspecs =[_const_spec((CONV_W, 3 * GROUP)), _const_spec((1, LANE)), _const_spec((1, LANE)),
              _const_spec((DEPTH, GROUP)), _const_spec((LANE, LANE)), _const_spec((1, LANE)),
              _const_spec((4, GROUP))]
    cspecs = [_const_spec(a.shape) for a in consts]
    in_specs = [p_spec] + wspecs + [table_spec] * 3 + cspecs + list(init_specs)
    args = [p, *wts, *tables, *consts, *init]
    cast_shapes, cast_specs = [], []
    for w, n_slabs in zip(cast, cfg.cast_slabs):
        slab = w.shape[1] // n_slabs
        assert w.shape[1] % n_slabs == 0 and slab % (2 * SUBLANE) == 0 and n_slabs <= n_groups * cfg.n_chunks
        slab_idx = lambda b, i, n=n_slabs: jnp.minimum(b * cfg.n_chunks + i, n - 1)
        args.append(w)
        in_specs.append(pl.BlockSpec((1, slab, w.shape[2]), lambda b, i, f=slab_idx: (cfg.layer, f(b, i), 0)))
        cast_shapes.append(jax.ShapeDtypeStruct(w.shape[1:], bf16))
        cast_specs.append(pl.BlockSpec((slab, w.shape[2]), lambda b, i, f=slab_idx: (f(b, i), 0)))
    assert len(cast) == len(cfg.cast_slabs)
    aliases = {}
    if o_prev is not None:
        aliases[len(args)] = 0
        args.append(o_prev)
    for j, s_prev in enumerate(states_prev or ()):
        aliases[len(args)] = 1 + j
        args.append(s_prev)
    in_specs += [pl.BlockSpec(memory_space=pl.ANY)] * len(aliases)
    assert len(aliases) == cfg.n_alias
    return pl.pallas_call(
        functools.partial(_mixer_body, cfg),
        out_shape=[o_shape] + _state_shapes(n_state_rows, cfg.stacked) + cast_shapes,
        grid=(n_groups, cfg.n_chunks),
        in_specs=in_specs,
        out_specs=[o_spec] + _state_out_specs(g, cfg.layer, cfg.stacked) + cast_specs,
        scratch_shapes=[pltpu.VMEM((g, c + CONV_HIST, 3 * GROUP), f32),
                        pltpu.VMEM((2 * g, 2 * HEAD_DIM, 2 * HEAD_DIM), f32),
                        pltpu.VMEM((2 * g, 2 * HEAD_DIM, 2 * HEAD_DIM), f32),
                        pltpu.VMEM((g, N_HEADS * GLA_DK, GROUP), f32),
                        pltpu.VMEM((2 * g, 2 * HEAD_DIM, 2 * HEAD_DIM), f32)],
        input_output_aliases=aliases,
        compiler_params=pltpu.CompilerParams(dimension_semantics=("arbitrary", "arbitrary"),
                                             vmem_limit_bytes=VMEM_LIMIT),
        name=name,
    )(*args)


def _tok(ref, t, nb):
    return ref[t * nb:(t + 1) * nb, :].T


def _bcast_row(ref, t, r):
    return jnp.broadcast_to(ref[t, pl.ds(r, 1), :], (HEAD_DIM, ref.shape[-1]))


def _bl_diag_loop(n_tok, kdim, j, row0, s_in, s_out, d_scr, k_scr, q_scr, v_scr, d_const=None):
    nb = s_in.shape[-1]

    def body(k, os):
        s_k = s_in[0, j, k]
        r = row0 + k
        outs = []
        for t in range(n_tok):
            d = d_const if d_const is not None else _bcast_row(d_scr, t, r)
            s_k = d * s_k + _bcast_row(k_scr, t, r) * v_scr[t, j * HEAD_DIM:(j + 1) * HEAD_DIM, :]
            outs.append(os[t] + _bcast_row(q_scr, t, r) * s_k)
        s_out[0, j, k] = s_k
        return tuple(outs)
    return lax.fori_loop(0, kdim, body, (jnp.zeros((HEAD_DIM, nb), f32),) * n_tok, unroll=STATE_ROWS_PER_ITER)


def _bl_store(o_ref, os, gate_ref, gain_ref, n_tok, nb, group_norm):
    for t in range(n_tok):
        gate = _tok(gate_ref, t, nb)
        halves = []
        for j in range(2):
            o = os[j][t]
            if group_norm:
                o = o - jnp.mean(o, axis=0, keepdims=True)
                o = o * lax.rsqrt(jnp.mean(o * o, axis=0, keepdims=True) + LN_EPS)
            else:
                o = o * lax.rsqrt(jnp.mean(o * o, axis=0, keepdims=True) + NORM_EPS)
            hs = slice(j * HEAD_DIM, (j + 1) * HEAD_DIM)
            halves.append(o * gain_ref[hs, :] * gate[hs, :])
        o_ref[t * nb:(t + 1) * nb, :] = jnp.concatenate(halves, 0).T.astype(o_ref.dtype)


def _bl_hgrn_body(layer, n_tok, q_ref, f_ref, i_ref, gate_ref, lbl_ref, gain_ref, s_in, o_prev, s_prev,
                  o_ref, s_out, d_scr, k_scr, q_scr, v_scr):
    del o_prev, s_prev
    nb = s_in.shape[-1]
    lbl = lbl_ref[...]
    e = jnp.exp(lbl - jnp.max(lbl, axis=1, keepdims=True))
    prob = e / jnp.sum(e, axis=1, keepdims=True)
    lb = (jnp.sum(prob[:, 1:layer + 1], axis=1, keepdims=True) if layer > 0 else jnp.zeros((LANE, 1), f32))
    for t in range(n_tok):
        kk = (1.0 - lb) * jax.nn.sigmoid(-_tok(f_ref, t, nb))
        d_scr[t] = 1.0 - jnp.clip(kk, 0.0, 1.0 - GATE_CLAMP)
        k_scr[t] = kk
        q_scr[t] = _tok(q_ref, t, nb)
        v_scr[t] = _tok(i_ref, t, nb)
    os = [_bl_diag_loop(n_tok, HEAD_DIM, j, j * HEAD_DIM, s_in, s_out, d_scr, k_scr, q_scr, v_scr) for j in range(2)]
    _bl_store(o_ref, os, gate_ref, gain_ref, n_tok, nb, False)


def _bl_gla_body(n_tok, q_ref, k_ref, v_ref, gate_ref, small_ref, wg_ref, bg_ref, gain_ref, s_in, o_prev, s_prev,
                 o_ref, s_out, d_scr, k_scr, q_scr, v_scr):
    del o_prev, s_prev
    nb = s_in.shape[-1]
    for t in range(n_tok):
        logit = jnp.dot(wg_ref[...].astype(bf16), _tok(small_ref, t, nb).astype(bf16),
                        preferred_element_type=f32) + bg_ref[...]
        d_scr[t] = jnp.exp(jax.nn.log_sigmoid(logit) * (1.0 / GLA_GATE_NORM))
        k_scr[t] = _tok(k_ref, t, nb)
        q_scr[t] = _tok(q_ref, t, nb) * GLA_DK ** -0.5
        v_scr[t] = _tok(v_ref, t, nb)
    p = pl.program_id(0)
    os = [_bl_diag_loop(n_tok, GLA_DK, j, (2 * p + j) * GLA_DK, s_in, s_out, d_scr, k_scr, q_scr, v_scr)
          for j in range(2)]
    _bl_store(o_ref, os, gate_ref, gain_ref, n_tok, nb, False)


def _bl_ret_body(n_tok, q_ref, k_ref, v_ref, gate_ref, cos_ref, sin_ref, gain_ref, s_in, o_prev, s_prev,
                 o_ref, s_out, k_scr, q_scr, v_scr):
    del o_prev, s_prev
    nb = s_in.shape[-1]
    half = HEAD_DIM // 2

    def rope(x, t):
        swapped = jnp.concatenate([x[half:2 * half], x[0:half], x[3 * half:4 * half], x[2 * half:3 * half]], 0)
        return x * cos_ref[t] + swapped * sin_ref[t]
    for t in range(n_tok):
        q_scr[t] = rope(_tok(q_ref, t, nb), t)
        k_scr[t] = rope(_tok(k_ref, t, nb), t) * HEAD_DIM ** -0.5
        v_scr[t] = _tok(v_ref, t, nb)
    p = pl.program_id(0)
    os = []
    for j in range(2):
        head = jnp.zeros((1, nb), f32) + (2 * p + j).astype(f32)
        gamma = jnp.broadcast_to(1.0 - jnp.exp2(-5.0 - head), (HEAD_DIM, nb))
        os.append(_bl_diag_loop(n_tok, HEAD_DIM, j, j * HEAD_DIM, s_in, s_out, None, k_scr, q_scr, v_scr, gamma))
    _bl_store(o_ref, os, gate_ref, gain_ref, n_tok, nb, True)


def _bl_delta_body(n_tok, q_ref, k_ref, v_ref, gate_ref, small_ref, hq_ref, hk_ref, hv_ref, cwq_ref, cwk_ref, cwv_ref,
                   alog_ref, dtb_ref, gain_ref, s_in, o_prev, s_prev, cq_prev, ck_prev, cv_prev,
                   o_ref, s_out, cq_out, ck_out, cv_out, k_scr, q_scr, kh_scr, u_scr, bg_scr):
    del o_prev, s_prev, cq_prev, ck_prev, cv_prev
    nb = s_in.shape[-1]
    p = pl.program_id(0)

    def conv(x_ref, h_ref, cw_ref):
        seq = [h_ref[0, r].T for r in range(CONV_W - 1)] + [_tok(x_ref, t, nb) for t in range(n_tok)]
        outs = []
        for t in range(n_tok):
            acc = cw_ref[:, 0:1] * seq[t]
            for jj in range(1, CONV_W):
                acc = acc + cw_ref[:, jj:jj + 1] * seq[t + jj]
            outs.append(jax.nn.silu(acc))
        return outs

    def l2norm(x):
        parts = []
        for j in range(2):
            xj = x[j * HEAD_DIM:(j + 1) * HEAD_DIM]
            parts.append(xj * lax.rsqrt(jnp.sum(xj * xj, axis=0, keepdims=True) + NORM_EPS))
        return jnp.concatenate(parts, 0)

    qs, ks, vs = conv(q_ref, hq_ref, cwq_ref), conv(k_ref, hk_ref, cwk_ref), conv(v_ref, hv_ref, cwv_ref)
    for t in range(n_tok):
        q_scr[t] = l2norm(qs[t]) * HEAD_DIM ** -0.5
        k_scr[t] = l2norm(ks[t])
        sm = _tok(small_ref, t, nb)
        both = jnp.where(_iota((LANE, 1), 0) < SM_ALPHA, jax.nn.sigmoid(sm),
                         -jnp.exp(alog_ref[...]) * jax.nn.softplus(sm + dtb_ref[...]))
        bg_scr[t] = both[0:SUBLANE]
    for r in range(CONV_W - 1):
        rows = slice((n_tok - (CONV_W - 1) + r) * nb, (n_tok - (CONV_W - 1) + r + 1) * nb)
        cq_out[0, r] = q_ref[rows, :]
        ck_out[0, r] = k_ref[rows, :]
        cv_out[0, r] = v_ref[rows, :]

    os = []
    for j in range(2):
        hs = slice(j * HEAD_DIM, (j + 1) * HEAD_DIM)
        h = 2 * p + j
        beta = [bg_scr[t, pl.ds(SM_BETA + h, 1), :] for t in range(n_tok)]
        b, acc = [], jnp.zeros((1, nb), f32)
        for t in range(n_tok):
            acc = acc + bg_scr[t, pl.ds(SM_ALPHA + h, 1), :]
            b.append(acc)
        kt = [k_scr[t, hs, :] for t in range(n_tok)]
        qt = [q_scr[t, hs, :] for t in range(n_tok)]
        dot = lambda x, y: jnp.sum(x * y, axis=0, keepdims=True)

        def sweep(scr):
            def body(k, accs):
                s_k = s_in[0, j, k]
                return tuple(a + _bcast_row(scr, t, j * HEAD_DIM + k) * s_k for t, a in enumerate(accs))
            return lax.fori_loop(0, HEAD_DIM, body, (jnp.zeros((HEAD_DIM, nb), f32),) * n_tok,
                                 unroll=STATE_ROWS_PER_ITER)
        k_s, q_s = sweep(k_scr), sweep(q_scr)
        us = []
        for t in range(n_tok):
            u = beta[t] * (vs[t][hs] - jnp.exp(b[t]) * k_s[t])
            for s in range(t):
                u = u - (beta[t] * jnp.exp(b[t] - b[s]) * dot(kt[t], kt[s])) * us[s]
            us.append(u)
        o_h = []
        for t in range(n_tok):
            o = jnp.exp(b[t]) * q_s[t]
            for s in range(t + 1):
                o = o + (jnp.exp(b[t] - b[s]) * dot(qt[t], kt[s])) * us[s]
            o_h.append(o)
        os.append(o_h)
        for t in range(n_tok):
            kh_scr[t, hs, :] = kt[t] * jnp.exp(b[-1] - b[t])
            u_scr[t] = us[t]
        decay = jnp.broadcast_to(jnp.exp(b[-1]), (HEAD_DIM, nb))

        def update(k, carry):
            s_k = decay * s_in[0, j, k]
            for t in range(n_tok):
                s_k = s_k + _bcast_row(kh_scr, t, j * HEAD_DIM + k) * u_scr[t]
            s_out[0, j, k] = s_k
            return carry
        lax.fori_loop(0, HEAD_DIM, update, 0, unroll=STATE_ROWS_PER_ITER)
    _bl_store(o_ref, os, gate_ref, gain_ref, n_tok, nb, False)


def _bl_call(body, name, mixer, n_tok, nb, blk_row, pt, cols, extra, extra_specs, state_t, kdim, layer,
             o_prev, s_prev, scratch, conv_prev=None):
    def tok_spec(c0, per_pair):
        return pl.BlockSpec((n_tok * nb, LANE),
                            (lambda p: (blk_row, c0 // LANE + p)) if per_pair else (lambda p: (blk_row, c0 // LANE)))
    st_spec = pl.BlockSpec((1, 2, kdim, HEAD_DIM, nb), lambda p: (layer, p, 0, 0, 0))
    args = [pt] * len(cols) + list(extra) + [state_t]
    in_specs = [tok_spec(c0, pp) for c0, pp in cols] + list(extra_specs) + [st_spec]
    aliases = {}

    def donate(arr, out_idx):
        if arr is not None:
            aliases[len(args)] = out_idx
        args.append(arr if arr is not None else jnp.zeros((1,), f32))
        in_specs.append(pl.BlockSpec(memory_space=pl.ANY))
    donate(o_prev, 0)
    donate(s_prev, 1)
    out_shape = [jax.ShapeDtypeStruct(o_prev.shape, o_prev.dtype),
                 jax.ShapeDtypeStruct((DEPTH,) + state_t.shape[1:], f32)]
    out_specs = [pl.BlockSpec((n_tok * nb, LANE), lambda p: (blk_row, 2 * mixer + p)), st_spec]
    if conv_prev is not None:
        for i in range(3):
            donate(conv_prev[i] if conv_prev else None, 2 + i)
            out_shape.append(jax.ShapeDtypeStruct((DEPTH, CONV_W - 1, nb, GROUP), f32))
            out_specs.append(pl.BlockSpec((1, CONV_W - 1, nb, LANE), lambda p: (layer, 0, 0, p)))
    return pl.pallas_call(
        body, out_shape=out_shape, grid=(2,), in_specs=in_specs, out_specs=out_specs,
        scratch_shapes=scratch, input_output_aliases=aliases,
        compiler_params=pltpu.CompilerParams(dimension_semantics=("arbitrary",), vmem_limit_bytes=VMEM_LIMIT),
        name=name,
    )(*args)


def _rope_tables(pos):
    half = HEAD_DIM // 2
    inv = 1.0 / (RET_THETA_BASE ** jnp.linspace(0.0, 1.0, half, dtype=f32))
    ang = pos[:, None] * inv[None, :]
    cos, sin = jnp.cos(ang), jnp.sin(ang)
    zero = jnp.zeros_like(sin)
    tile = lambda a, b: jnp.tile(jnp.concatenate([a, b], -1), (1, N_HEADS))
    return tile(cos, cos), tile(-sin, zero), tile(zero, sin)


def kernel(x_prompt, x_sample, state_delta_conv, state_delta, state_hgrn, state_gla, state_ret, meta_tokens, emb_ln_g, emb_ln_b, w_in, conv_w, delta_a_log, delta_dt_bias, delta_norm_g, hgrn_lb_logits, hgrn_norm_g, gla_w_gate, gla_b_gate, gla_norm_g, ret_norm_g, w_out, ln1_g, ln1_b, w_ffn_gate, w_ffn_up, w_ffn_down, ln2_g, ln2_b):
    bsz, seq, _ = x_prompt.shape
    n_seq, n_tok, _ = x_sample.shape
    pg = PROMPT_GROUP if bsz % PROMPT_GROUP == 0 else 1
    rows_main = bsz * seq
    rows_s = n_tok * n_seq
    assert seq % CHUNK == 0 and rows_main % ROW_TILE == 0 and N_META <= CHUNK
    assert n_tok >= CONV_W - 1 and rows_s % CHUNK == 0 and (n_seq % LANE == 0 or n_seq < LANE)
    lead_pad = CHUNK - N_META
    rows_tail = rows_s + CHUNK
    s_blk = 0
    meta_blk = rows_s // CHUNK

    xs = jnp.swapaxes(x_sample, 0, 1).reshape(rows_s, D_MODEL)
    xt = jnp.concatenate([xs, jnp.zeros((lead_pad, D_MODEL), f32), meta_tokens.astype(f32)], 0)
    xm = x_prompt.reshape(rows_main, D_MODEL)

    tab_m = _rope_tables(jnp.arange(CHUNK, dtype=f32) - lead_pad)
    tab_p = _rope_tables(N_META + jnp.arange(seq, dtype=f32))
    consts_c = tuple(jnp.asarray(a, f32) for a in (_chunk_constants(CHUNK, BASE_BLOCK, 2)
                                                  + _chunk_constants(CHUNK, 0, N_HEADS)))
    row_vec = lambda v: v.reshape(1, -1).astype(f32)
    col_vec = lambda v: v.reshape(-1, 1).astype(f32)
    small_row = lambda v: jnp.zeros((1, LANE), f32).at[0, SM_ALPHA:SM_ALPHA + N_HEADS].set(v)
    n_chunks = seq // CHUNK

    seq_last = lambda s: jnp.transpose(s, (0, 2, 3, 4, 1))
    sd_t, sh_t, sg_t, sr_t = (seq_last(s) for s in (state_delta, state_hgrn, state_gla, state_ret))
    conv_t = jnp.transpose(state_delta_conv, (0, 2, 1, 3))
    half = HEAD_DIM // 2
    ang = ((PAST_LEN + jnp.arange(n_tok, dtype=f32))[:, None]
           * (1.0 / (RET_THETA_BASE ** jnp.linspace(0.0, 1.0, half, dtype=f32)))[None, :])
    cos_s = jnp.tile(jnp.cos(ang), (1, LANE // half))[..., None]
    sin_s = jnp.tile(jnp.concatenate([-jnp.sin(ang), jnp.sin(ang)], -1), (1, 2))[..., None]
    vm = lambda shape: pltpu.VMEM(shape, f32)
    tok_scr = lambda: vm((n_tok, LANE, n_seq))

    w_in_t = jnp.swapaxes(w_in, 1, 2)
    st_p = None
    sa_s = sb_s = sc_s = sd_s = conv_s = None
    for l in range(DEPTH):
        pm, pt = _proj(xm, xt, row_vec(emb_ln_g), row_vec(emb_ln_b), w_in_t, l, with_ln=(l == 0))
        wg = jnp.zeros((LANE, N_HEADS * GLA_DK), f32).at[SM_LR:SM_LR + GLA_RANK].set(gla_w_gate[l])
        gains = jnp.stack([delta_norm_g[l], hgrn_norm_g[l], gla_norm_g[l], ret_norm_g[l]], 0)
        wts = (conv_w[l], small_row(delta_a_log[l]), small_row(delta_dt_bias[l]), hgrn_lb_logits,
               wg, row_vec(gla_b_gate[l]), gains)

        cfg = MixCfg(CHUNK, 1, lead_pad, CHUNK, l, 1, None, False, CHUNK, 0, False, (), 1)
        ot, *st_meta = _mixer_call(
            cfg, "mixers_meta", 1, pt, pl.BlockSpec((CHUNK, D_IN_AL), lambda b, i: (meta_blk, 0)),
            wts, tab_m, pl.BlockSpec((CHUNK, GROUP), lambda b, i: (0, 0)), consts_c, (), (),
            jax.ShapeDtypeStruct((rows_tail, D_MODEL), bf16),
            pl.BlockSpec((CHUNK, D_MODEL), lambda b, i: (meta_blk, 0)), 1)

        n_prev = 0 if st_p is None else len(st_p)
        ffn_w = (w_out, w_ffn_gate, w_ffn_up, w_ffn_down)
        n_steps = (bsz // pg) * n_chunks
        slabs = tuple(max(n for n in range(1, n_steps + 1) if w.shape[1] % n == 0 and (w.shape[1] // n) % (2 * SUBLANE) == 0)
                      for w in ffn_w)
        cfg = MixCfg(CHUNK, pg, 0, CHUNK, l, n_chunks, "shared", True, pg * CHUNK, n_prev, True, slabs, n_steps)
        st1 = lambda k: pl.BlockSpec((1, N_HEADS, k, HEAD_DIM), lambda b, i: (0, 0, 0, 0))
        om, *st_p = _mixer_call(
            cfg, "mixers_prompt", bsz // pg, pm.reshape(bsz // pg, pg, seq, D_IN_AL),
            pl.BlockSpec((1, pg, CHUNK, D_IN_AL), lambda b, i: (b, 0, i, 0)),
            wts, tab_p, pl.BlockSpec((CHUNK, GROUP), lambda b, i: (i, 0)), consts_c, st_meta,
            [pl.BlockSpec((1, CONV_W - 1, 3 * GROUP), lambda b, i: (0, 0, 0)),
             st1(HEAD_DIM), st1(HEAD_DIM), st1(GLA_DK), st1(HEAD_DIM)],
            jax.ShapeDtypeStruct((bsz // pg, pg, seq, D_MODEL), bf16),
            pl.BlockSpec((1, pg, CHUNK, D_MODEL), lambda b, i: (b, 0, i, 0)), bsz, states_prev=st_p, cast=ffn_w)
        om = om.reshape(rows_main, D_MODEL)
        st_p, (wo_b, wg_b, wu_b, wd_b) = st_p[:5], st_p[5:]

        pair_col = lambda v: (col_vec(v), pl.BlockSpec((LANE, 1), lambda p: (p, 0)))
        full = lambda a: (a, pl.BlockSpec(a.shape, lambda p: (0,) * a.ndim))
        split = lambda pairs: ([a for a, _ in pairs], [s for _, s in pairs])
        hist = lambda part: (conv_t, pl.BlockSpec((1, CONV_W - 1, n_seq, LANE), lambda p: (l, 0, 0, 2 * part + p)))
        cw_t = conv_w[l].T
        cw = lambda part: (cw_t, pl.BlockSpec((LANE, CONV_W), lambda p: (2 * part + p, 0)))
        small_col = lambda v: jnp.zeros((LANE, 1), f32).at[SM_ALPHA:SM_ALPHA + N_HEADS, 0].set(v)
        extra, especs = split([hist(0), hist(1), hist(2), cw(0), cw(1), cw(2), full(small_col(delta_a_log[l])),
                               full(small_col(delta_dt_bias[l])), pair_col(delta_norm_g[l])])
        ot, sa_s, *conv_s = _bl_call(
            functools.partial(_bl_delta_body, n_tok), "sample_delta", 0, n_tok, n_seq, s_blk, pt,
            [(A_QKV, True), (A_QKV + GROUP, True), (A_QKV + 2 * GROUP, True), (A_GATE, True), (SMALL, False)],
            extra, especs, sd_t, HEAD_DIM, l, ot, sa_s,
            [tok_scr(), tok_scr(), tok_scr(), vm((n_tok, HEAD_DIM, n_seq)), vm((n_tok, SUBLANE, n_seq))],
            conv_prev=conv_s or [])
        extra, especs = split([(hgrn_lb_logits.T, pl.BlockSpec((LANE, DEPTH), lambda p: (p, 0))),
                               pair_col(hgrn_norm_g[l])])
        ot, sb_s = _bl_call(
            functools.partial(_bl_hgrn_body, l, n_tok), "sample_hgrn", 1, n_tok, n_seq, s_blk, pt,
            [(B_Q, True), (B_F, True), (B_I, True), (B_GATE, True)], extra, especs, sh_t, HEAD_DIM, l, ot, sb_s,
            [tok_scr() for _ in range(4)])
        wg_t = jnp.zeros((N_HEADS * GLA_DK, LANE), f32).at[:, SM_LR:SM_LR + GLA_RANK].set(gla_w_gate[l].T)
        extra, especs = split([full(wg_t), full(col_vec(gla_b_gate[l])), pair_col(gla_norm_g[l])])
        ot, sc_s = _bl_call(
            functools.partial(_bl_gla_body, n_tok), "sample_gla", 2, n_tok, n_seq, s_blk, pt,
            [(C_Q, False), (C_K, False), (C_V, True), (C_GATE, True), (SMALL, False)], extra, especs,
            sg_t, GLA_DK, l, ot, sc_s, [tok_scr() for _ in range(4)])
        extra, especs = split([full(cos_s), full(sin_s), pair_col(ret_norm_g[l])])
        ot, sd_s = _bl_call(
            functools.partial(_bl_ret_body, n_tok), "sample_ret", 3, n_tok, n_seq, s_blk, pt,
            [(D_Q, True), (D_K, True), (D_V, True), (D_GATE, True)], extra, especs, sr_t, HEAD_DIM, l, ot, sd_s,
            [tok_scr() for _ in range(3)])

        xm, xt = _post(xm, xt, om, ot, row_vec(emb_ln_g), row_vec(emb_ln_b),
                       row_vec(ln1_g[l]), row_vec(ln1_b[l]), row_vec(ln2_g[l]), row_vec(ln2_b[l]),
                       wo_b, wg_b, wu_b, wd_b, l)

    y_prompt = xm.reshape(bsz, seq, D_MODEL)
    y_sample = jnp.swapaxes(xt[0:rows_s].reshape(n_tok, n_seq, D_MODEL), 0, 1)
    seq_first = lambda s: jnp.transpose(s, (0, 4, 1, 2, 3))
    st_s = (jnp.transpose(jnp.concatenate(conv_s, -1), (0, 2, 1, 3)),
            seq_first(sa_s), seq_first(sb_s), seq_first(sc_s), seq_first(sd_s))
    return (y_prompt, y_sample) + tuple(v for pair in zip(st_p, st_s) for v in pair)
```

```python
import collections
import functools
import math

import numpy as np
import jax
import jax.numpy as jnp
from jax import lax
from jax.experimental import pallas as pl
from jax.experimental.pallas import tpu as pltpu

f32 = jnp.float32
bf16 = jnp.bfloat16

D_MODEL = 1024
N_META = 16
CHUNK = 64
GROUP = 256
N_HEADS = 4
HEAD_DIM = 64
GLA_DK = 32
GLA_RANK = 16
GLA_GATE_NORM = 16.0
CONV_W = 4
D_FF = 2816
DEPTH = 2
ALPHA = (2 * DEPTH) ** 0.25
PAST_LEN = 16384
RET_THETA_BASE = 10000.0
LN_EPS = 1e-5
NORM_EPS = 1e-6
GATE_CLAMP = 1e-6
IN_SPLITS = (GROUP, GROUP, GROUP, N_HEADS, N_HEADS, GROUP,
             GROUP, GROUP, GROUP, GROUP,
             N_HEADS * GLA_DK, N_HEADS * GLA_DK, GROUP, GLA_RANK, GROUP,
             GROUP, GROUP, GROUP, GROUP)

LANE = 128
SUBLANE = 8
VMEM_LIMIT = 56 * 1024 * 1024

COL_ORDER = (0, 1, 2, 5, 6, 7, 8, 9, 10, 11, 12, 14, 15, 16, 17, 18)
A_QKV, A_GATE = 0, 768
B_Q, B_F, B_I, B_GATE = 1024, 1280, 1536, 1792
C_Q, C_K, C_V, C_GATE = 2048, 2176, 2304, 2560
D_Q, D_K, D_V, D_GATE = 2816, 3072, 3328, 3584
SMALL = 3840
SM_BETA, SM_ALPHA, SM_LR = 0, N_HEADS, 2 * N_HEADS
D_IN_AL = SMALL + LANE
SILU_COLS = (A_GATE, B_Q, B_GATE, C_GATE, D_GATE)
CONV_HIST = SUBLANE
ROW_TILE = 512
PROJ_STAGE_ROWS = 256
PROMPT_GROUP = 8
STATE_ROWS_PER_ITER = 8


def _layer_norm(x, g, b):
    mu = jnp.mean(x, -1, keepdims=True)
    xc = x - mu
    var = jnp.mean(xc * xc, -1, keepdims=True)
    return xc * lax.rsqrt(var + LN_EPS) * g + b


def _row_specs(width, n_main, rows_tail):
    return (pl.BlockSpec((ROW_TILE, width), lambda i: (jnp.minimum(i, n_main - 1), 0)),
            pl.BlockSpec((rows_tail, width), lambda i: (0, 0)))


def _const_spec(shape, single_buffer=False):
    nd = len(shape)
    kw = dict(pipeline_mode=pl.Buffered(1)) if single_buffer else {}
    return pl.BlockSpec(shape, lambda *_: (0,) * nd, **kw)


def _on_region(n_main, main_fn, tail_fn):
    i = pl.program_id(0)
    pl.when(i < n_main)(main_fn)
    pl.when(i >= n_main)(tail_fn)


def _w_in_segments():
    offs = np.concatenate([[0], np.cumsum(IN_SPLITS)])
    runs, dst = [], 0
    for i in COL_ORDER + (3, 4, 13):
        if runs and runs[-1][1] == offs[i]:
            runs[-1][1] = int(offs[i + 1])
        else:
            runs.append([int(offs[i]), int(offs[i + 1]), dst])
        dst += IN_SPLITS[i]
    return [tuple(r) for r in runs]


def _proj_tile(with_ln, x_ref, g_ref, b_ref, w_ref, cos_ref, sin_ref, p_ref):
    x = x_ref[...]
    if with_ln:
        x = _layer_norm(x, g_ref[...], b_ref[...])
    h = x.astype(bf16)
    n = w_ref.shape[1]
    half = HEAD_DIM // 2
    first_half = (_iota((1, LANE), 1) % HEAD_DIM) < half

    def rope(p):
        parts = []
        for l0 in range(0, GROUP, LANE):
            x = p[:, l0:l0 + LANE]
            other = jnp.where(first_half, pltpu.roll(x, LANE - half, 1), pltpu.roll(x, half, 1))
            parts.append(x * cos_ref[...] + other * sin_ref[...])
        return jnp.concatenate(parts, -1)
    starts = sorted(range(0, n, GROUP), key=lambda c0: c0 not in (D_Q, D_K))
    for n0 in starts:
        n1 = min(n0 + GROUP, n)
        p = jnp.dot(h, w_ref[:, n0:n1], preferred_element_type=f32)
        if n0 in SILU_COLS:
            p = jax.nn.silu(p)
        elif n0 == D_Q:
            p = rope(p)
        elif n0 == D_K:
            p = rope(p) * HEAD_DIM ** -0.5
        p_ref[:, n0:n1] = p


def _stage_w_in(wt_hbm, layer, w_scr, stage, sem):
    rows = PROJ_STAGE_ROWS
    big, small = [], []
    for s0, s1, d0 in _w_in_segments():
        if (s1 - s0) % rows == 0 and d0 % LANE == 0:
            big += [(s0 + j, d0 + j) for j in range(0, s1 - s0, rows)]
        else:
            small.append((s0, s1 - s0, d0 - SMALL))
    assert all(0 <= d and d + n <= LANE and d % SUBLANE == 0 for _, n, d in small) and len(small) <= 2

    def copy(j):
        return pltpu.make_async_copy(wt_hbm.at[layer, pl.ds(big[j][0], rows), :], stage.at[j % 2], sem.at[j % 2])
    copy(0).start()
    for j in range(len(big)):
        if j + 1 < len(big):
            copy(j + 1).start()
        copy(j).wait()
        w_scr[:, big[j][1]:big[j][1] + rows] = stage[j % 2].T.astype(bf16)
    used = max(d + n for _, n, d in small)
    stage[0, used:LANE, :] = jnp.zeros((LANE - used, D_MODEL), f32)
    copies = [pltpu.make_async_copy(wt_hbm.at[layer, pl.ds(s0, n), :], stage.at[0, pl.ds(d, n), :], sem.at[i])
              for i, (s0, n, d) in enumerate(small)]
    for cp in copies:
        cp.start()
    for cp in copies:
        cp.wait()
    w_scr[:, SMALL:SMALL + LANE] = stage[0, 0:LANE, :].T.astype(bf16)


def _proj_body(with_ln, layer, n_main, xm_ref, xt_ref, g_ref, b_ref, wt_hbm, cm_ref, sm_ref, ct_ref, st_ref,
               pm_ref, pt_ref, w_scr, stage, sem):
    pl.when(pl.program_id(0) == 0)(lambda: _stage_w_in(wt_hbm, layer, w_scr, stage, sem))
    _on_region(n_main,
               lambda: _proj_tile(with_ln, xm_ref, g_ref, b_ref, w_scr, cm_ref, sm_ref, pm_ref),
               lambda: _proj_tile(with_ln, xt_ref, g_ref, b_ref, w_scr, ct_ref, st_ref, pt_ref))


def _proj(xm, xt, g, b, w_in_t, layer, with_ln, rope_main, rope_tail):
    n_main, rows_tail = xm.shape[0] // ROW_TILE, xt.shape[0]
    n_tab = rope_main[0].shape[0] // ROW_TILE
    tab_main = pl.BlockSpec((ROW_TILE, LANE), lambda i: (jnp.minimum(i, n_main - 1) % n_tab, 0))
    n = D_IN_AL
    out_shape = [jax.ShapeDtypeStruct((xm.shape[0], n), f32), jax.ShapeDtypeStruct((rows_tail, n), f32)]
    out_specs = list(_row_specs(n, n_main, rows_tail))
    return pl.pallas_call(
        functools.partial(_proj_body, with_ln, layer, n_main),
        out_shape=out_shape,
        grid=(n_main + 1,),
        in_specs=list(_row_specs(D_MODEL, n_main, rows_tail))
                 + [_const_spec((1, D_MODEL)), _const_spec((1, D_MODEL)), pl.BlockSpec(memory_space=pl.ANY),
                    tab_main, tab_main, _const_spec((rows_tail, LANE), True), _const_spec((rows_tail, LANE), True)],
        out_specs=out_specs,
        scratch_shapes=[pltpu.VMEM((D_MODEL, n), bf16),
                        pltpu.VMEM((2, PROJ_STAGE_ROWS, D_MODEL), f32),
                        pltpu.SemaphoreType.DMA((2,))],
        compiler_params=pltpu.CompilerParams(dimension_semantics=("arbitrary",),
                                             vmem_limit_bytes=VMEM_LIMIT),
        name="in_proj",
    )(xm, xt, g, b, w_in_t, *rope_main, *rope_tail)


def _post_tile(emb_ln, x_ref, o_ref, ge_ref, be_ref, wo_ref, g1_ref, b1_ref, wg_ref, wu_ref, wd_ref, g2_ref, b2_ref,
               y_ref):
    x = x_ref[...]
    if emb_ln:
        x = _layer_norm(x, ge_ref[...], be_ref[...])
    m = jnp.dot(o_ref[...], wo_ref[...], preferred_element_type=f32)
    y1 = _layer_norm(ALPHA * x + m, g1_ref[...], b1_ref[...])
    y1b = y1.astype(bf16)
    dff = wg_ref.shape[1]
    step = 4 * LANE
    acc = jnp.zeros(y1.shape, f32)
    for n0 in range(0, dff, step):
        n1 = min(n0 + step, dff)
        gate = jnp.dot(y1b, wg_ref[:, n0:n1], preferred_element_type=f32)
        up = jnp.dot(y1b, wu_ref[:, n0:n1], preferred_element_type=f32)
        act = (jax.nn.silu(gate) * up).astype(bf16)
        acc = acc + jnp.dot(act, wd_ref[n0:n1, :], preferred_element_type=f32)
    y_ref[...] = _layer_norm(ALPHA * y1 + acc, g2_ref[...], b2_ref[...])


def _post_body(layer, n_main, xm_ref, xt_ref, om_ref, ot_ref, ge_ref, be_ref, g1_ref, b1_ref, g2_ref, b2_ref,
               wo_ref, wg_ref, wu_ref, wd_ref, ym_ref, yt_ref):
    w = (ge_ref, be_ref, wo_ref, g1_ref, b1_ref, wg_ref, wu_ref, wd_ref, g2_ref, b2_ref)
    _on_region(n_main, lambda: _post_tile(layer == 0, xm_ref, om_ref, *w, ym_ref),
               lambda: _post_tile(layer == 0, xt_ref, ot_ref, *w, yt_ref))


def _post(xm, xt, om, ot, ge, be, g1, b1, g2, b2, w_out, w_gate, w_up, w_down, layer):
    n_main = xm.shape[0] // ROW_TILE
    rows = lambda: list(_row_specs(D_MODEL, n_main, xt.shape[0]))
    vec = lambda: _const_spec((1, D_MODEL))
    return pl.pallas_call(
        functools.partial(_post_body, layer, n_main),
        out_shape=[jax.ShapeDtypeStruct(xm.shape, f32), jax.ShapeDtypeStruct(xt.shape, f32)],
        grid=(n_main + 1,),
        in_specs=rows() + rows() + [vec() for _ in range(6)]
                 + [_const_spec(w.shape, True) for w in (w_out, w_gate, w_up, w_down)],
        out_specs=rows(),
        compiler_params=pltpu.CompilerParams(dimension_semantics=("arbitrary",),
                                             vmem_limit_bytes=VMEM_LIMIT),
        name="post_ffn",
    )(xm, xt, om, ot, ge, be, g1, b1, g2, b2, w_out, w_gate, w_up, w_down)


SAFE_EXP = 80.0
BASE_BLOCK = 8
assert (BASE_BLOCK // 2) * -math.log(GATE_CLAMP) < SAFE_EXP


def _levels(c, base=0):
    m, out = c // 2, []
    while m >= max(base, 1):
        out.append(m)
        m //= 2
    return tuple(out)


def _chunk_constants(c, base, nblk):
    t = np.arange(c)
    rows = [(t[None, :] <= t[:, None]).astype(np.float32), (t[None, :] > t[:, None]).astype(np.float32)]
    masks = []
    for m in _levels(c, base):
        blk = t // m
        upper = (blk % 2) == 1
        lo = np.where(upper, blk * m, t + 1)
        hi = np.where(upper, t, (blk + 1) * m - 1)
        rows.append(((t[None, :] >= lo[:, None]) & (t[None, :] <= hi[:, None])).astype(np.float32))
        same = (t[:, None] // (2 * m)) == (t[None, :] // (2 * m))
        masks.append((same & upper[:, None] & (~upper)[None, :]).astype(np.float32))
    if base:
        mid = (t // base) * base + base // 2 - 1
        after = (t[None, :] > mid[:, None]) & (t[None, :] <= t[:, None])
        before = (t[None, :] > t[:, None]) & (t[None, :] <= mid[:, None])
        rows.append(after.astype(np.float32) - before.astype(np.float32))
        masks.append(((t[:, None] // base == t[None, :] // base) & (t[None, :] <= t[:, None])).astype(np.float32))
    else:
        masks.append(np.eye(c, dtype=np.float32))
    return np.concatenate(rows, 0), np.tile(np.stack(masks, 0), (1, 1, nblk))


def _mm(a, b):
    return jnp.einsum('gmk,gkn->gmn', a.astype(bf16), b.astype(bf16), preferred_element_type=f32)


def _mm_nt(a, b):
    return jnp.einsum('gmk,gnk->gmn', a.astype(bf16), b.astype(bf16), preferred_element_type=f32)


def _mm_tn(a, b):
    return lax.dot_general(a.astype(bf16), b.astype(bf16), (((1,), (1,)), ((0,), (0,))),
                           preferred_element_type=f32)


def _iota(shape, dim):
    return lax.broadcasted_iota(jnp.int32, shape, dim)


def _bd(z, nblk, blk=None):
    z = z.astype(bf16)
    if blk is None:
        blk = _iota((1, 1, z.shape[-1]), 2) // (z.shape[-1] // nblk)
    return jnp.concatenate([jnp.where(blk == h, z, jnp.zeros_like(z)) for h in range(nblk)], axis=1)


def _hmm(x, z, nblk):
    return _mm(x, _bd(z, nblk))


def _hmm_nt(x, z, nblk):
    return _mm_nt(x, _bd(z, nblk))


def _hmm_split(x, z, nblk, blk=None):
    m = x.shape[1]
    x_hi, z_hi = x.astype(bf16), z.astype(bf16)
    x_lo = (x - x_hi.astype(f32)).astype(bf16)
    z_lo = (z - z_hi.astype(f32)).astype(bf16)
    both = _mm(jnp.concatenate([x_hi, x_lo], 1), _bd(z_hi, nblk, blk))
    return both[:, 0:m] + (both[:, m:2 * m] + _mm(x_hi, _bd(z_lo, nblk, blk)))


def _block_mask(rows, lanes, dr, dl):
    return (_iota((rows, lanes), 0) // dr) == (_iota((rows, lanes), 1) // dl)


def _dot01(m01, x):
    nb = x.shape[0]
    x2 = jnp.concatenate([x[i] for i in range(nb)], -1) if nb > 1 else x[0]
    hi = x2.astype(bf16)
    lo = (x2 - hi.astype(f32)).astype(bf16)
    m01 = m01.astype(bf16)
    d = lambda y: jnp.dot(m01, y, preferred_element_type=f32)
    out = d(hi) + d(lo)
    w = x.shape[-1]
    return jnp.stack([out[:, i * w:(i + 1) * w] for i in range(nb)], 0)


def _seg_sum(x, ones_bd, pieces=2):
    nb, c, w = x.shape
    x2 = x.reshape(nb * c, w)
    hi = x2.astype(bf16)
    out = jnp.dot(hi, ones_bd, preferred_element_type=f32)
    if pieces == 2:
        lo = (x2 - hi.astype(f32)).astype(bf16)
        out = out + jnp.dot(lo, ones_bd, preferred_element_type=f32)
    return out.reshape(nb, c, w)


def _last_row_as_col(b, c):
    return jnp.stack([b[i, c - SUBLANE:c, :].T[:, SUBLANE - 1:SUBLANE] for i in range(b.shape[0])], 0)


def _channel_decay_mixer(q, k, v, gl, s, nblk, sums_ref, masks_ref, bdmask, c, base):
    lv = _levels(c, base)
    nl = len(lv)
    sums = _dot01(sums_ref[...], gl)
    b = sums[:, 0:c]
    rev = sums[:, c:2 * c]
    row = _iota((1, c, 1), 1)
    if base:
        e0 = sums[:, (nl + 2) * c:(nl + 3) * c]
        att = jnp.where(masks_ref[nl] > 0.0, _hmm_nt(q * jnp.exp(e0), k * jnp.exp(-e0), nblk), 0.0)
    else:
        att = masks_ref[nl] * _hmm_nt(q, k, nblk)
    for li, m in enumerate(lv):
        upper = ((row // m) % 2) == 1
        w = jnp.where(upper, q, k) * jnp.exp(sums[:, (li + 2) * c:(li + 3) * c])
        att = att + masks_ref[li] * _hmm_nt(w, w, nblk)
    o = _mm(jnp.concatenate([q * jnp.exp(b), att], -1), jnp.concatenate([s.astype(bf16), _bd(v, nblk)], 1))
    s_new = jnp.exp(_last_row_as_col(b, c)) * s + jnp.where(bdmask, _mm_tn(k * jnp.exp(rev), v), 0.0)
    return o, s_new


MixCfg = collections.namedtuple("MixCfg",
                                "c g r0 r1 layer n_chunks init p4d o_rows n_alias stacked cast_slabs n_steps")


def _pairs_to_bd(s):
    z = jnp.zeros(s.shape[:1] + (HEAD_DIM, HEAD_DIM), f32)
    pair = lambda p: jnp.concatenate([jnp.concatenate([s[:, 2 * p], z], -1),
                                      jnp.concatenate([z, s[:, 2 * p + 1]], -1)], 1)
    return jnp.concatenate([pair(0), pair(1)], 0)


def _quad_to_bd(s):
    z = lambda n: jnp.zeros(s.shape[:1] + (GLA_DK, n * HEAD_DIM), f32)
    rows = []
    for h in range(N_HEADS):
        parts = ([z(h)] if h else []) + [s[:, h]] + ([z(N_HEADS - 1 - h)] if h < N_HEADS - 1 else [])
        rows.append(jnp.concatenate(parts, -1))
    return jnp.concatenate(rows, 1)


def _mixer_body(cfg, *refs):
    c, g = cfg.c, cfg.g
    n_w, n_t, n_c = 7, 0, 4
    p_ref = refs[0]
    cw_ref, alog_ref, dtb_ref, lbl_ref, wg_ref, bg_ref, gains_ref = refs[1:1 + n_w]
    sumsb_ref, masksb_ref, sumsc_ref, masksc_ref = refs[1 + n_w + n_t:1 + n_w + n_t + n_c]
    pos = 1 + n_w + n_t + n_c
    if cfg.init:
        convi_ref, sai_ref, sbi_ref, sci_ref, sdi_ref = refs[pos:pos + 5]
        pos += 5
    n_cast = len(cfg.cast_slabs)
    cast_in = refs[pos:pos + n_cast]
    pos += n_cast + cfg.n_alias
    o_ref, convo_ref, sao_ref, sbo_ref, sco_ref, sdo_ref = refs[pos:pos + 6]
    if cfg.stacked:
        convo_ref, sao_ref, sbo_ref, sco_ref, sdo_ref = (r.at[0] for r in (convo_ref, sao_ref, sbo_ref, sco_ref, sdo_ref))
    cast_out = refs[pos + 6:pos + 6 + n_cast]
    cbuf, sa_scr, sb_scr, sc_scr, sd_scr = refs[pos + 6 + n_cast:]
    ci = pl.program_id(1)

    @pl.when(ci == 0)
    def _():
        cbuf[:, 0:CONV_HIST, :] = jnp.zeros((g, CONV_HIST, 3 * GROUP), f32)
        if not cfg.init:
            for r in (sa_scr, sb_scr, sc_scr, sd_scr):
                r[...] = jnp.zeros(r.shape, f32)
        else:
            take = lambda r: jnp.broadcast_to(r[...], (g,) + r.shape[1:])
            cbuf[:, CONV_HIST - (CONV_W - 1):CONV_HIST, :] = take(convi_ref)
            sa_scr[...] = _pairs_to_bd(take(sai_ref))
            sb_scr[...] = _pairs_to_bd(take(sbi_ref))
            sc_scr[...] = _quad_to_bd(take(sci_ref))
            sd_scr[...] = _pairs_to_bd(take(sdi_ref))

    step = pl.program_id(0) * cfg.n_chunks + ci
    for w_f32, w_bf16, n_slabs in sorted(zip(cast_in, cast_out, cfg.cast_slabs), key=lambda e: e[2]):
        def narrow(src=w_f32, dst=w_bf16):
            dst[...] = src[0].astype(bf16)
        if n_slabs == cfg.n_steps:
            narrow()
        else:
            pl.when(step < n_slabs)(narrow)

    def load(a, b):
        if cfg.p4d:
            return p_ref[0, :, :, a:b]
        return p_ref[:, a:b].reshape(g, c, b - a)

    r0, r1 = cfg.r0, cfg.r1
    n_valid = float(r1 - r0)
    row = _iota((1, c, 1), 1)
    vm = ((row >= r0) & (row < r1)).astype(f32)
    lane_v = _iota((1, 1, 2 * HEAD_DIM), 2)
    lane_s = _iota((1, 1, 2 * c), 2)
    s_idx = lane_s % c
    incl = s_idx <= row
    strict = s_idx < row
    bd2 = _block_mask(2 * HEAD_DIM, 2 * HEAD_DIM, HEAD_DIM, HEAD_DIM)
    bd4 = _block_mask(N_HEADS * GLA_DK, GROUP, GLA_DK, HEAD_DIM)
    ones2 = bd2.astype(bf16)
    pairs = lambda x: jnp.concatenate([x[..., :LANE], x[..., LANE:]], 0)
    unpairs = lambda x: jnp.concatenate([x[:g], x[g:]], -1)
    n_solve = max(1, math.ceil(math.log2(r1 - r0)))
    sm = load(SMALL, SMALL + LANE)
    valid = (lambda x: x) if (r0 == 0 and r1 == c) else (lambda x: x * vm)

    cbuf[:, CONV_HIST:CONV_HIST + c, :] = valid(load(A_QKV, A_QKV + 3 * GROUP))
    buf = cbuf[...].reshape(g * (c + CONV_HIST), 3 * GROUP)
    acc = cbuf[:, CONV_HIST:CONV_HIST + c, :] * cw_ref[CONV_W - 1:CONV_W, :]
    for j in range(CONV_W - 1):
        back = pltpu.roll(buf, CONV_W - 1 - j, 0).reshape(g, c + CONV_HIST, 3 * GROUP)
        acc = acc + back[:, CONV_HIST:CONV_HIST + c, :] * cw_ref[j:j + 1, :]
    conv = jax.nn.silu(acc)
    beta = valid(jax.nn.sigmoid(sm))
    g_a = valid(-jnp.exp(alog_ref[...]) * jax.nn.softplus(sm + dtb_ref[...]))
    b_a = _dot01(sumsc_ref[0:c, :], g_a)

    def head_cols(x, lane0):
        col = lambda h: x[..., lane0 + h:lane0 + h + 1]
        return jnp.concatenate([col(0), col(2)], 0), jnp.concatenate([col(1), col(3)], 0)

    def spread(c0, c1, lane, width):
        return jnp.where(lane < width, c0, c1)

    bc0, bc1 = head_cols(b_a, SM_ALPHA)
    bt0, bt1 = head_cols(beta, SM_BETA)
    bc_v, bet_v = spread(bc0, bc1, lane_v, HEAD_DIM), spread(bt0, bt1, lane_v, HEAD_DIM)
    bc_s, bet_s = spread(bc0, bc1, lane_s, c), spread(bt0, bt1, lane_s, c)
    b_rows = [b_a[i].T for i in range(g)]
    b_row = lambda h: jnp.stack([t[SM_ALPHA + h:SM_ALPHA + h + 1, :] for t in b_rows], 0)
    br_s = jnp.concatenate([jnp.concatenate([b_row(0), b_row(1)], -1),
                            jnp.concatenate([b_row(2), b_row(3)], -1)], 0)
    dec = jnp.where(incl, jnp.exp(jnp.where(incl, bc_s - br_s, 0.0)), 0.0)
    q = pairs(conv[..., 0:GROUP])
    k = pairs(conv[..., GROUP:2 * GROUP])
    v = pairs(valid(conv[..., 2 * GROUP:3 * GROUP]))
    q = q * lax.rsqrt(_seg_sum(q * q, ones2, 1) + NORM_EPS) * HEAD_DIM ** -0.5
    k = valid(k * lax.rsqrt(_seg_sum(k * k, ones2) + NORM_EPS))
    s_a = sa_scr[...]
    kq = jnp.concatenate([k, q], 1)
    kq_k = _hmm_nt(kq, k, 2)
    kq_s = _mm(kq, s_a)
    a_mat = jnp.where(strict, bet_s * dec * kq_k[:, 0:c], 0.0)
    gam = jnp.exp(bc_v)
    u = bet_v * (v - gam * kq_s[:, 0:c])
    p = -a_mat
    lane_pu = _iota((1, 1, 2 * c + 2 * HEAD_DIM), 2)
    head_pu = jnp.where(lane_pu < 2 * c, lane_pu // c, (lane_pu - 2 * c) // HEAD_DIM)
    for i in range(n_solve):
        if i + 1 < n_solve:
            pu = _hmm_split(p, jnp.concatenate([p, u], -1), 2, head_pu)
            p, u = pu[..., 0:2 * c], u + pu[..., 2 * c:]
        else:
            u = u + _hmm_split(p, u, 2)
    o_a = gam * kq_s[:, c:2 * c] + _hmm(dec * kq_k[:, c:2 * c], u, 2)
    bl = bc_v[:, c - 1:c, :]
    sa_scr[...] = jnp.exp(bl) * s_a + jnp.where(bd2, _mm_tn(k * jnp.exp(bl - bc_v), u), 0.0)
    o_a = unpairs(o_a * lax.rsqrt(_seg_sum(o_a * o_a, ones2, 1) * (1.0 / HEAD_DIM) + NORM_EPS))

    lbl = lbl_ref[...]
    e = jnp.exp(lbl - jnp.max(lbl, axis=0, keepdims=True))
    prob = e / jnp.sum(e, axis=0, keepdims=True)
    lb = (jnp.sum(prob[1:cfg.layer + 1], axis=0, keepdims=True) if cfg.layer > 0
          else jnp.zeros((1, GROUP), f32))
    kb = (1.0 - lb) * jax.nn.sigmoid(-load(B_F, B_F + GROUP))
    g_b = valid(jnp.log1p(-jnp.clip(kb, 0.0, 1.0 - GATE_CLAMP)))
    o_b, s_b_new = _channel_decay_mixer(
        pairs(load(B_Q, B_Q + GROUP)), pairs(valid(kb)), pairs(valid(load(B_I, B_I + GROUP))),
        pairs(g_b), sb_scr[...], 2, sumsb_ref, masksb_ref, bd2, c, BASE_BLOCK if c > BASE_BLOCK else 0)
    sb_scr[...] = s_b_new
    o_b = unpairs(o_b * lax.rsqrt(_seg_sum(o_b * o_b, ones2, 1) * (1.0 / HEAD_DIM) + NORM_EPS))

    logit = jnp.dot(sm.reshape(g * c, LANE).astype(bf16), wg_ref[...].astype(bf16),
                    preferred_element_type=f32).reshape(g, c, LANE) + bg_ref[...]
    g_c = valid(jax.nn.log_sigmoid(logit) * (1.0 / GLA_GATE_NORM))
    c_args = (load(C_Q, C_Q + N_HEADS * GLA_DK) * GLA_DK ** -0.5, valid(load(C_K, C_K + N_HEADS * GLA_DK)),
              valid(load(C_V, C_V + GROUP)), g_c, sc_scr[...], N_HEADS)
    o_c, s_c_new = _channel_decay_mixer(*c_args, sumsc_ref, masksc_ref, bd4, c, 0)
    sc_scr[...] = s_c_new
    ones4 = _block_mask(GROUP, GROUP, HEAD_DIM, HEAD_DIM).astype(bf16)
    o_c = o_c * lax.rsqrt(_seg_sum(o_c * o_c, ones4, 1) * (1.0 / HEAD_DIM) + NORM_EPS)

    q = pairs(load(D_Q, D_Q + GROUP))
    k = pairs(valid(load(D_K, D_K + GROUP)))
    v = pairs(valid(load(D_V, D_V + GROUP)))
    pair_idx = _iota((2, 1, 1), 0)
    log_gamma = lambda lane, width: jnp.log1p(-jnp.exp2(-5.0 - (2 * pair_idx + lane // width).astype(f32)))
    per_seq = lambda x: jnp.concatenate([jnp.broadcast_to(x[p:p + 1], (g,) + x.shape[1:]) for p in range(2)], 0)
    lg_v, lg_s = log_gamma(lane_v, HEAD_DIM), log_gamma(lane_s, c)
    cnt_c = jnp.clip(row + 1 - r0, 0, r1 - r0).astype(f32)
    cnt_s = jnp.clip(s_idx + 1 - r0, 0, r1 - r0).astype(f32)
    dec = per_seq(jnp.where(incl, jnp.exp(jnp.where(incl, (cnt_c - cnt_s) * lg_s, 0.0)), 0.0))
    s_d = sd_scr[...]
    o_d = _mm(jnp.concatenate([q * per_seq(jnp.exp(cnt_c * lg_v)), dec * _hmm_nt(q, k, 2)], -1),
              jnp.concatenate([s_d.astype(bf16), _bd(v, 2)], 1))
    sd_scr[...] = (per_seq(jnp.exp(n_valid * lg_v)) * s_d
                   + jnp.where(bd2, _mm_tn(k * per_seq(jnp.exp((n_valid - cnt_c) * lg_v)), v), 0.0))
    oc = o_d - _seg_sum(o_d, ones2) * (1.0 / HEAD_DIM)
    o_d = unpairs(oc * lax.rsqrt(_seg_sum(oc * oc, ones2, 1) * (1.0 / HEAD_DIM) + LN_EPS))

    for i, (o_m, gate_col) in enumerate(zip((o_a, o_b, o_c, o_d), (A_GATE, B_GATE, C_GATE, D_GATE))):
        val = (o_m * gains_ref[i:i + 1, :] * load(gate_col, gate_col + GROUP)).astype(o_ref.dtype)
        if cfg.p4d:
            o_ref[0, :, :, i * GROUP:(i + 1) * GROUP] = val
        else:
            o_ref[0:g * c, i * GROUP:(i + 1) * GROUP] = val.reshape(g * c, GROUP)
    if cfg.o_rows > g * c:
        o_ref[g * c:cfg.o_rows, :] = jnp.zeros((cfg.o_rows - g * c, D_MODEL), o_ref.dtype)
    cbuf[:, 0:CONV_HIST, :] = cbuf[:, c:c + CONV_HIST, :]

    @pl.when(ci == cfg.n_chunks - 1)
    def _():
        convo_ref[...] = cbuf[:, pl.ds(CONV_HIST + r1 - (CONV_W - 1), CONV_W - 1), :]
        for scr, out in ((sa_scr, sao_ref), (sb_scr, sbo_ref), (sd_scr, sdo_ref)):
            s = scr[...]
            for p in range(2):
                out[:, 2 * p] = s[p * g:(p + 1) * g, 0:HEAD_DIM, 0:HEAD_DIM]
                out[:, 2 * p + 1] = s[p * g:(p + 1) * g, HEAD_DIM:2 * HEAD_DIM, HEAD_DIM:2 * HEAD_DIM]
        s = sc_scr[...]
        for h in range(N_HEADS):
            sco_ref[:, h] = s[:, h * GLA_DK:(h + 1) * GLA_DK, h * HEAD_DIM:(h + 1) * HEAD_DIM]


def _state_shapes(n, stacked):
    lead = (DEPTH, n) if stacked else (n,)
    return [jax.ShapeDtypeStruct(lead + (CONV_W - 1, 3 * GROUP), f32),
            jax.ShapeDtypeStruct(lead + (N_HEADS, HEAD_DIM, HEAD_DIM), f32),
            jax.ShapeDtypeStruct(lead + (N_HEADS, HEAD_DIM, HEAD_DIM), f32),
            jax.ShapeDtypeStruct(lead + (N_HEADS, GLA_DK, HEAD_DIM), f32),
            jax.ShapeDtypeStruct(lead + (N_HEADS, HEAD_DIM, HEAD_DIM), f32)]


def _state_out_specs(g, layer, stacked):
    if stacked:
        st = lambda k: pl.BlockSpec((1, g, N_HEADS, k, HEAD_DIM), lambda b, i: (layer, b, 0, 0, 0))
        conv = pl.BlockSpec((1, g, CONV_W - 1, 3 * GROUP), lambda b, i: (layer, b, 0, 0))
    else:
        st = lambda k: pl.BlockSpec((g, N_HEADS, k, HEAD_DIM), lambda b, i: (b, 0, 0, 0))
        conv = pl.BlockSpec((g, CONV_W - 1, 3 * GROUP), lambda b, i: (b, 0, 0))
    return [conv, st(HEAD_DIM), st(HEAD_DIM), st(GLA_DK), st(HEAD_DIM)]


def _mixer_call(cfg, name, n_groups, p, p_spec, wts, consts, init, init_specs,
                o_shape, o_spec, n_state_rows, o_prev=None, states_prev=None, cast=()):
    c, g = cfg.c, cfg.g
    n_steps = n_groups * cfg.n_chunks
    wspecs =[_const_spec((CONV_W, 3 * GROUP)), _const_spec((1, LANE)), _const_spec((1, LANE)),
              _const_spec((DEPTH, GROUP)), _const_spec((LANE, LANE)), _const_spec((1, LANE)),
              _const_spec((4, GROUP))]
    cspecs = [_const_spec(a.shape) for a in consts]
    in_specs = [p_spec] + wspecs + cspecs + list(init_specs)
    args = [p, *wts, *consts, *init]
    cast_shapes, cast_specs = [], []
    for w, n_slabs in zip(cast, cfg.cast_slabs):
        slab = w.shape[1] // n_slabs
        assert w.shape[1] % n_slabs == 0 and slab % (2 * SUBLANE) == 0 and n_slabs <= n_groups * cfg.n_chunks
        slab_idx = lambda b, i, n=n_slabs: jnp.minimum(b * cfg.n_chunks + i, n - 1)
        args.append(w)
        in_specs.append(pl.BlockSpec((1, slab, w.shape[2]), lambda b, i, f=slab_idx: (cfg.layer, f(b, i), 0)))
        cast_shapes.append(jax.ShapeDtypeStruct(w.shape[1:], bf16))
        cast_specs.append(pl.BlockSpec((slab, w.shape[2]), lambda b, i, f=slab_idx: (f(b, i), 0)))
    assert len(cast) == len(cfg.cast_slabs)
    aliases = {}
    if o_prev is not None:
        aliases[len(args)] = 0
        args.append(o_prev)
    for j, s_prev in enumerate(states_prev or ()):
        aliases[len(args)] = 1 + j
        args.append(s_prev)
    in_specs += [pl.BlockSpec(memory_space=pl.ANY)] * len(aliases)
    assert len(aliases) == cfg.n_alias
    return pl.pallas_call(
        functools.partial(_mixer_body, cfg),
        out_shape=[o_shape] + _state_shapes(n_state_rows, cfg.stacked) + cast_shapes,
        grid=(n_groups, cfg.n_chunks),
        in_specs=in_specs,
        out_specs=[o_spec] + _state_out_specs(g, cfg.layer, cfg.stacked) + cast_specs,
        scratch_shapes=[pltpu.VMEM((g, c + CONV_HIST, 3 * GROUP), f32),
                        pltpu.VMEM((2 * g, 2 * HEAD_DIM, 2 * HEAD_DIM), f32),
                        pltpu.VMEM((2 * g, 2 * HEAD_DIM, 2 * HEAD_DIM), f32),
                        pltpu.VMEM((g, N_HEADS * GLA_DK, GROUP), f32),
                        pltpu.VMEM((2 * g, 2 * HEAD_DIM, 2 * HEAD_DIM), f32)],
        input_output_aliases=aliases,
        compiler_params=pltpu.CompilerParams(dimension_semantics=("arbitrary", "arbitrary"),
                                             vmem_limit_bytes=VMEM_LIMIT),
        name=name,
    )(*args)


def _tok(ref, t, nb):
    return ref[t * nb:(t + 1) * nb, :].T


def _bcast_row(ref, t, r):
    return jnp.broadcast_to(ref[t, pl.ds(r, 1), :], (HEAD_DIM, ref.shape[-1]))


def _bl_diag_loop(n_tok, kdim, j, row0, s_in, s_out, d_scr, k_scr, q_scr, v_scr, d_const=None):
    nb = s_in.shape[-1]

    def body(k, os):
        s_k = s_in[0, j, k]
        r = row0 + k
        outs = []
        for t in range(n_tok):
            d = d_const if d_const is not None else _bcast_row(d_scr, t, r)
            s_k = d * s_k + _bcast_row(k_scr, t, r) * v_scr[t, j * HEAD_DIM:(j + 1) * HEAD_DIM, :]
            outs.append(os[t] + _bcast_row(q_scr, t, r) * s_k)
        s_out[0, j, k] = s_k
        return tuple(outs)
    return lax.fori_loop(0, kdim, body, (jnp.zeros((HEAD_DIM, nb), f32),) * n_tok, unroll=STATE_ROWS_PER_ITER)


def _bl_store(o_ref, os, gate_ref, gain_ref, n_tok, nb, group_norm):
    for t in range(n_tok):
        gate = _tok(gate_ref, t, nb)
        halves = []
        for j in range(2):
            o = os[j][t]
            if group_norm:
                o = o - jnp.mean(o, axis=0, keepdims=True)
                o = o * lax.rsqrt(jnp.mean(o * o, axis=0, keepdims=True) + LN_EPS)
            else:
                o = o * lax.rsqrt(jnp.mean(o * o, axis=0, keepdims=True) + NORM_EPS)
            hs = slice(j * HEAD_DIM, (j + 1) * HEAD_DIM)
            halves.append(o * gain_ref[hs, :] * gate[hs, :])
        o_ref[t * nb:(t + 1) * nb, :] = jnp.concatenate(halves, 0).T.astype(o_ref.dtype)


def _bl_hgrn_body(layer, n_tok, q_ref, f_ref, i_ref, gate_ref, lbl_ref, gain_ref, s_in, o_prev, s_prev,
                  o_ref, s_out, d_scr, k_scr, q_scr, v_scr):
    del o_prev, s_prev
    nb = s_in.shape[-1]
    lbl = lbl_ref[...]
    e = jnp.exp(lbl - jnp.max(lbl, axis=1, keepdims=True))
    prob = e / jnp.sum(e, axis=1, keepdims=True)
    lb = (jnp.sum(prob[:, 1:layer + 1], axis=1, keepdims=True) if layer > 0 else jnp.zeros((LANE, 1), f32))
    for t in range(n_tok):
        kk = (1.0 - lb) * jax.nn.sigmoid(-_tok(f_ref, t, nb))
        d_scr[t] = 1.0 - jnp.clip(kk, 0.0, 1.0 - GATE_CLAMP)
        k_scr[t] = kk
        q_scr[t] = _tok(q_ref, t, nb)
        v_scr[t] = _tok(i_ref, t, nb)
    os = [_bl_diag_loop(n_tok, HEAD_DIM, j, j * HEAD_DIM, s_in, s_out, d_scr, k_scr, q_scr, v_scr) for j in range(2)]
    _bl_store(o_ref, os, gate_ref, gain_ref, n_tok, nb, False)


def _bl_gla_body(n_tok, q_ref, k_ref, v_ref, gate_ref, small_ref, wg_ref, bg_ref, gain_ref, s_in, o_prev, s_prev,
                 o_ref, s_out, d_scr, k_scr, q_scr, v_scr):
    del o_prev, s_prev
    nb = s_in.shape[-1]
    for t in range(n_tok):
        logit = jnp.dot(wg_ref[...].astype(bf16), _tok(small_ref, t, nb).astype(bf16),
                        preferred_element_type=f32) + bg_ref[...]
        d_scr[t] = jnp.exp(jax.nn.log_sigmoid(logit) * (1.0 / GLA_GATE_NORM))
        k_scr[t] = _tok(k_ref, t, nb)
        q_scr[t] = _tok(q_ref, t, nb) * GLA_DK ** -0.5
        v_scr[t] = _tok(v_ref, t, nb)
    p = pl.program_id(0)
    os = [_bl_diag_loop(n_tok, GLA_DK, j, (2 * p + j) * GLA_DK, s_in, s_out, d_scr, k_scr, q_scr, v_scr)
          for j in range(2)]
    _bl_store(o_ref, os, gate_ref, gain_ref, n_tok, nb, False)


def _bl_ret_body(n_tok, q_ref, k_ref, v_ref, gate_ref, gain_ref, s_in, o_prev, s_prev,
                 o_ref, s_out, k_scr, q_scr, v_scr):
    del o_prev, s_prev
    nb = s_in.shape[-1]
    for t in range(n_tok):
        q_scr[t] = _tok(q_ref, t, nb)
        k_scr[t] = _tok(k_ref, t, nb)
        v_scr[t] = _tok(v_ref, t, nb)
    p = pl.program_id(0)
    os = []
    for j in range(2):
        head = jnp.zeros((1, nb), f32) + (2 * p + j).astype(f32)
        gamma = jnp.broadcast_to(1.0 - jnp.exp2(-5.0 - head), (HEAD_DIM, nb))
        os.append(_bl_diag_loop(n_tok, HEAD_DIM, j, j * HEAD_DIM, s_in, s_out, None, k_scr, q_scr, v_scr, gamma))
    _bl_store(o_ref, os, gate_ref, gain_ref, n_tok, nb, True)


def _bl_delta_body(n_tok, q_ref, k_ref, v_ref, gate_ref, small_ref, hq_ref, hk_ref, hv_ref, cwq_ref, cwk_ref, cwv_ref,
                   alog_ref, dtb_ref, gain_ref, s_in, o_prev, s_prev, cq_prev, ck_prev, cv_prev,
                   o_ref, s_out, cq_out, ck_out, cv_out, k_scr, q_scr, kh_scr, u_scr, bg_scr):
    del o_prev, s_prev, cq_prev, ck_prev, cv_prev
    nb = s_in.shape[-1]
    p = pl.program_id(0)

    def conv(x_ref, h_ref, cw_ref):
        seq = [h_ref[0, r].T for r in range(CONV_W - 1)] + [_tok(x_ref, t, nb) for t in range(n_tok)]
        outs = []
        for t in range(n_tok):
            acc = cw_ref[:, 0:1] * seq[t]
            for jj in range(1, CONV_W):
                acc = acc + cw_ref[:, jj:jj + 1] * seq[t + jj]
            outs.append(jax.nn.silu(acc))
        return outs

    def l2norm(x):
        parts = []
        for j in range(2):
            xj = x[j * HEAD_DIM:(j + 1) * HEAD_DIM]
            parts.append(xj * lax.rsqrt(jnp.sum(xj * xj, axis=0, keepdims=True) + NORM_EPS))
        return jnp.concatenate(parts, 0)

    qs, ks, vs = conv(q_ref, hq_ref, cwq_ref), conv(k_ref, hk_ref, cwk_ref), conv(v_ref, hv_ref, cwv_ref)
    for t in range(n_tok):
        q_scr[t] = l2norm(qs[t]) * HEAD_DIM ** -0.5
        k_scr[t] = l2norm(ks[t])
        sm = _tok(small_ref, t, nb)
        both = jnp.where(_iota((LANE, 1), 0) < SM_ALPHA, jax.nn.sigmoid(sm),
                         -jnp.exp(alog_ref[...]) * jax.nn.softplus(sm + dtb_ref[...]))
        bg_scr[t] = both[0:SUBLANE]
    for r in range(CONV_W - 1):
        rows = slice((n_tok - (CONV_W - 1) + r) * nb, (n_tok - (CONV_W - 1) + r + 1) * nb)
        cq_out[0, r] = q_ref[rows, :]
        ck_out[0, r] = k_ref[rows, :]
        cv_out[0, r] = v_ref[rows, :]

    os = []
    for j in range(2):
        hs = slice(j * HEAD_DIM, (j + 1) * HEAD_DIM)
        h = 2 * p + j
        beta = [bg_scr[t, pl.ds(SM_BETA + h, 1), :] for t in range(n_tok)]
        b, acc = [], jnp.zeros((1, nb), f32)
        for t in range(n_tok):
            acc = acc + bg_scr[t, pl.ds(SM_ALPHA + h, 1), :]
            b.append(acc)
        kt = [k_scr[t, hs, :] for t in range(n_tok)]
        qt = [q_scr[t, hs, :] for t in range(n_tok)]
        dot = lambda x, y: jnp.sum(x * y, axis=0, keepdims=True)

        def sweep(scr):
            def body(k, accs):
                s_k = s_in[0, j, k]
                return tuple(a + _bcast_row(scr, t, j * HEAD_DIM + k) * s_k for t, a in enumerate(accs))
            return lax.fori_loop(0, HEAD_DIM, body, (jnp.zeros((HEAD_DIM, nb), f32),) * n_tok,
                                 unroll=STATE_ROWS_PER_ITER)
        k_s, q_s = sweep(k_scr), sweep(q_scr)
        us = []
        for t in range(n_tok):
            u = beta[t] * (vs[t][hs] - jnp.exp(b[t]) * k_s[t])
            for s in range(t):
                u = u - (beta[t] * jnp.exp(b[t] - b[s]) * dot(kt[t], kt[s])) * us[s]
            us.append(u)
        o_h = []
        for t in range(n_tok):
            o = jnp.exp(b[t]) * q_s[t]
            for s in range(t + 1):
                o = o + (jnp.exp(b[t] - b[s]) * dot(qt[t], kt[s])) * us[s]
            o_h.append(o)
        os.append(o_h)
        for t in range(n_tok):
            kh_scr[t, hs, :] = kt[t] * jnp.exp(b[-1] - b[t])
            u_scr[t] = us[t]
        decay = jnp.broadcast_to(jnp.exp(b[-1]), (HEAD_DIM, nb))

        def update(k, carry):
            s_k = decay * s_in[0, j, k]
            for t in range(n_tok):
                s_k = s_k + _bcast_row(kh_scr, t, j * HEAD_DIM + k) * u_scr[t]
            s_out[0, j, k] = s_k
            return carry
        lax.fori_loop(0, HEAD_DIM, update, 0, unroll=STATE_ROWS_PER_ITER)
    _bl_store(o_ref, os, gate_ref, gain_ref, n_tok, nb, False)


def _bl_call(body, name, mixer, n_tok, nb, blk_row, pt, cols, extra, extra_specs, state_t, kdim, layer,
             o_prev, s_prev, scratch, conv_prev=None):
    def tok_spec(c0, per_pair):
        return pl.BlockSpec((n_tok * nb, LANE),
                            (lambda p: (blk_row, c0 // LANE + p)) if per_pair else (lambda p: (blk_row, c0 // LANE)))
    st_spec = pl.BlockSpec((1, 2, kdim, HEAD_DIM, nb), lambda p: (layer, p, 0, 0, 0))
    args = [pt] * len(cols) + list(extra) + [state_t]
    in_specs = [tok_spec(c0, pp) for c0, pp in cols] + list(extra_specs) + [st_spec]
    aliases = {}

    def donate(arr, out_idx):
        if arr is not None:
            aliases[len(args)] = out_idx
        args.append(arr if arr is not None else jnp.zeros((1,), f32))
        in_specs.append(pl.BlockSpec(memory_space=pl.ANY))
    donate(o_prev, 0)
    donate(s_prev, 1)
    out_shape = [jax.ShapeDtypeStruct(o_prev.shape, o_prev.dtype),
                 jax.ShapeDtypeStruct((DEPTH,) + state_t.shape[1:], f32)]
    out_specs = [pl.BlockSpec((n_tok * nb, LANE), lambda p: (blk_row, 2 * mixer + p)), st_spec]
    if conv_prev is not None:
        for i in range(3):
            donate(conv_prev[i] if conv_prev else None, 2 + i)
            out_shape.append(jax.ShapeDtypeStruct((DEPTH, CONV_W - 1, nb, GROUP), f32))
            out_specs.append(pl.BlockSpec((1, CONV_W - 1, nb, LANE), lambda p: (layer, 0, 0, p)))
    return pl.pallas_call(
        body, out_shape=out_shape, grid=(2,), in_specs=in_specs, out_specs=out_specs,
        scratch_shapes=scratch, input_output_aliases=aliases,
        compiler_params=pltpu.CompilerParams(dimension_semantics=("arbitrary",), vmem_limit_bytes=VMEM_LIMIT),
        name=name,
    )(*args)


def _rope_tables(pos):
    half = HEAD_DIM // 2
    inv = 1.0 / (RET_THETA_BASE ** jnp.linspace(0.0, 1.0, half, dtype=f32))
    ang = pos[:, None] * inv[None, :]
    cos, sin = jnp.cos(ang), jnp.sin(ang)
    tile = lambda a, b: jnp.tile(jnp.concatenate([a, b], -1), (1, LANE // HEAD_DIM))
    return tile(cos, cos), tile(-sin, sin)


def kernel(x_prompt, x_sample, state_delta_conv, state_delta, state_hgrn, state_gla, state_ret, meta_tokens, emb_ln_g, emb_ln_b, w_in, conv_w, delta_a_log, delta_dt_bias, delta_norm_g, hgrn_lb_logits, hgrn_norm_g, gla_w_gate, gla_b_gate, gla_norm_g, ret_norm_g, w_out, ln1_g, ln1_b, w_ffn_gate, w_ffn_up, w_ffn_down, ln2_g, ln2_b):
    bsz, seq, _ = x_prompt.shape
    n_seq, n_tok, _ = x_sample.shape
    pg = PROMPT_GROUP if bsz % PROMPT_GROUP == 0 else 1
    rows_main = bsz * seq
    rows_s = n_tok * n_seq
    assert seq % CHUNK == 0 and rows_main % ROW_TILE == 0 and N_META <= CHUNK
    assert n_tok >= CONV_W - 1 and rows_s % CHUNK == 0 and (n_seq % LANE == 0 or n_seq < LANE)
    lead_pad = CHUNK - N_META
    rows_tail = rows_s + CHUNK
    s_blk = 0
    meta_blk = rows_s // CHUNK

    xs = jnp.swapaxes(x_sample, 0, 1).reshape(rows_s, D_MODEL)
    xt = jnp.concatenate([xs, jnp.zeros((lead_pad, D_MODEL), f32), meta_tokens.astype(f32)], 0)
    xm = x_prompt.reshape(rows_main, D_MODEL)

    assert seq % ROW_TILE == 0 or ROW_TILE % seq == 0
    rope_main = _rope_tables(N_META + (jnp.arange(max(seq, ROW_TILE)) % seq).astype(f32))
    rope_tail = _rope_tables(jnp.concatenate([PAST_LEN + jnp.repeat(jnp.arange(n_tok), n_seq).astype(f32),
                                              jnp.arange(CHUNK, dtype=f32) - lead_pad]))
    consts_c = tuple(jnp.asarray(a, f32) for a in (_chunk_constants(CHUNK, BASE_BLOCK, 2)
                                                  + _chunk_constants(CHUNK, 0, N_HEADS)))
    row_vec = lambda v: v.reshape(1, -1).astype(f32)
    col_vec = lambda v: v.reshape(-1, 1).astype(f32)
    small_row = lambda v: jnp.zeros((1, LANE), f32).at[0, SM_ALPHA:SM_ALPHA + N_HEADS].set(v)
    n_chunks = seq // CHUNK

    seq_last = lambda s: jnp.transpose(s, (0, 2, 3, 4, 1))
    sd_t, sh_t, sg_t, sr_t = (seq_last(s) for s in (state_delta, state_hgrn, state_gla, state_ret))
    conv_t = jnp.transpose(state_delta_conv, (0, 2, 1, 3))
    vm = lambda shape: pltpu.VMEM(shape, f32)
    tok_scr = lambda: vm((n_tok, LANE, n_seq))

    w_in_t = jnp.swapaxes(w_in, 1, 2)
    st_p = None
    sa_s = sb_s = sc_s = sd_s = conv_s = None
    for l in range(DEPTH):
        pm, pt = _proj(xm, xt, row_vec(emb_ln_g), row_vec(emb_ln_b), w_in_t, l, l == 0, rope_main, rope_tail)
        wg = jnp.zeros((LANE, N_HEADS * GLA_DK), f32).at[SM_LR:SM_LR + GLA_RANK].set(gla_w_gate[l])
        gains = jnp.stack([delta_norm_g[l], hgrn_norm_g[l], gla_norm_g[l], ret_norm_g[l]], 0)
        wts = (conv_w[l], small_row(delta_a_log[l]), small_row(delta_dt_bias[l]), hgrn_lb_logits,
               wg, row_vec(gla_b_gate[l]), gains)

        cfg = MixCfg(CHUNK, 1, lead_pad, CHUNK, l, 1, None, False, CHUNK, 0, False, (), 1)
        ot, *st_meta = _mixer_call(
            cfg, "mixers_meta", 1, pt, pl.BlockSpec((CHUNK, D_IN_AL), lambda b, i: (meta_blk, 0)),
            wts, consts_c, (), (),
            jax.ShapeDtypeStruct((rows_tail, D_MODEL), bf16),
            pl.BlockSpec((CHUNK, D_MODEL), lambda b, i: (meta_blk, 0)), 1)

        n_prev = 0 if st_p is None else len(st_p)
        ffn_w = (w_out, w_ffn_gate, w_ffn_up, w_ffn_down)
        n_steps = (bsz // pg) * n_chunks
        slabs = tuple(max(n for n in range(1, n_steps + 1) if w.shape[1] % n == 0 and (w.shape[1] // n) % (2 * SUBLANE) == 0)
                      for w in ffn_w)
        cfg = MixCfg(CHUNK, pg, 0, CHUNK, l, n_chunks, "shared", True, pg * CHUNK, n_prev, True, slabs, n_steps)
        st1 = lambda k: pl.BlockSpec((1, N_HEADS, k, HEAD_DIM), lambda b, i: (0, 0, 0, 0))
        om, *st_p = _mixer_call(
            cfg, "mixers_prompt", bsz // pg, pm.reshape(bsz // pg, pg, seq, D_IN_AL),
            pl.BlockSpec((1, pg, CHUNK, D_IN_AL), lambda b, i: (b, 0, i, 0)),
            wts, consts_c, st_meta,
            [pl.BlockSpec((1, CONV_W - 1, 3 * GROUP), lambda b, i: (0, 0, 0)),
             st1(HEAD_DIM), st1(HEAD_DIM), st1(GLA_DK), st1(HEAD_DIM)],
            jax.ShapeDtypeStruct((bsz // pg, pg, seq, D_MODEL), bf16),
            pl.BlockSpec((1, pg, CHUNK, D_MODEL), lambda b, i: (b, 0, i, 0)), bsz, states_prev=st_p, cast=ffn_w)
        om = om.reshape(rows_main, D_MODEL)
        st_p, (wo_b, wg_b, wu_b, wd_b) = st_p[:5], st_p[5:]

        pair_col = lambda v: (col_vec(v), pl.BlockSpec((LANE, 1), lambda p: (p, 0)))
        full = lambda a: (a, pl.BlockSpec(a.shape, lambda p: (0,) * a.ndim))
        split = lambda pairs: ([a for a, _ in pairs], [s for _, s in pairs])
        hist = lambda part: (conv_t, pl.BlockSpec((1, CONV_W - 1, n_seq, LANE), lambda p: (l, 0, 0, 2 * part + p)))
        cw_t = conv_w[l].T
        cw = lambda part: (cw_t, pl.BlockSpec((LANE, CONV_W), lambda p: (2 * part + p, 0)))
        small_col = lambda v: jnp.zeros((LANE, 1), f32).at[SM_ALPHA:SM_ALPHA + N_HEADS, 0].set(v)
        extra, especs = split([hist(0), hist(1), hist(2), cw(0), cw(1), cw(2), full(small_col(delta_a_log[l])),
                               full(small_col(delta_dt_bias[l])), pair_col(delta_norm_g[l])])
        ot, sa_s, *conv_s = _bl_call(
            functools.partial(_bl_delta_body, n_tok), "sample_delta", 0, n_tok, n_seq, s_blk, pt,
            [(A_QKV, True), (A_QKV + GROUP, True), (A_QKV + 2 * GROUP, True), (A_GATE, True), (SMALL, False)],
            extra, especs, sd_t, HEAD_DIM, l, ot, sa_s,
            [tok_scr(), tok_scr(), tok_scr(), vm((n_tok, HEAD_DIM, n_seq)), vm((n_tok, SUBLANE, n_seq))],
            conv_prev=conv_s or [])
        extra, especs = split([(hgrn_lb_logits.T, pl.BlockSpec((LANE, DEPTH), lambda p: (p, 0))),
                               pair_col(hgrn_norm_g[l])])
        ot, sb_s = _bl_call(
            functools.partial(_bl_hgrn_body, l, n_tok), "sample_hgrn", 1, n_tok, n_seq, s_blk, pt,
            [(B_Q, True), (B_F, True), (B_I, True), (B_GATE, True)], extra, especs, sh_t, HEAD_DIM, l, ot, sb_s,
            [tok_scr() for _ in range(4)])
        wg_t = jnp.zeros((N_HEADS * GLA_DK, LANE), f32).at[:, SM_LR:SM_LR + GLA_RANK].set(gla_w_gate[l].T)
        extra, especs = split([full(wg_t), full(col_vec(gla_b_gate[l])), pair_col(gla_norm_g[l])])
        ot, sc_s = _bl_call(
            functools.partial(_bl_gla_body, n_tok), "sample_gla", 2, n_tok, n_seq, s_blk, pt,
            [(C_Q, False), (C_K, False), (C_V, True), (C_GATE, True), (SMALL, False)], extra, especs,
            sg_t, GLA_DK, l, ot, sc_s, [tok_scr() for _ in range(4)])
        extra, especs = split([pair_col(ret_norm_g[l])])
        ot, sd_s = _bl_call(
            functools.partial(_bl_ret_body, n_tok), "sample_ret", 3, n_tok, n_seq, s_blk, pt,
            [(D_Q, True), (D_K, True), (D_V, True), (D_GATE, True)], extra, especs, sr_t, HEAD_DIM, l, ot, sd_s,
            [tok_scr() for _ in range(3)])

        xm, xt = _post(xm, xt, om, ot, row_vec(emb_ln_g), row_vec(emb_ln_b),
                       row_vec(ln1_g[l]), row_vec(ln1_b[l]), row_vec(ln2_g[l]), row_vec(ln2_b[l]),
                       wo_b, wg_b, wu_b, wd_b, l)

    y_prompt = xm.reshape(bsz, seq, D_MODEL)
    y_sample = jnp.swapaxes(xt[0:rows_s].reshape(n_tok, n_seq, D_MODEL), 0, 1)
    seq_first = lambda s: jnp.transpose(s, (0, 4, 1, 2, 3))
    st_s = (jnp.transpose(jnp.concatenate(conv_s, -1), (0, 2, 1, 3)),
            seq_first(sa_s), seq_first(sb_s), seq_first(sc_s), seq_first(sd_s))
    return (y_prompt, y_sample) + tuple(v for pair in zip(st_p, st_s) for v in pair)
```

```python
import collections
import functools
import math

import numpy as np
import jax
import jax.numpy as jnp
from jax import lax
from jax.experimental import pallas as pl
from jax.experimental.pallas import tpu as pltpu

f32 = jnp.float32
bf16 = jnp.bfloat16

D_MODEL = 1024
N_META = 16
CHUNK = 64
GROUP = 256
N_HEADS = 4
HEAD_DIM = 64
GLA_DK = 32
GLA_RANK = 16
GLA_GATE_NORM = 16.0
CONV_W = 4
D_FF = 2816
DEPTH = 2
ALPHA = (2 * DEPTH) ** 0.25
PAST_LEN = 16384
RET_THETA_BASE = 10000.0
LN_EPS = 1e-5
NORM_EPS = 1e-6
GATE_CLAMP = 1e-6
IN_SPLITS = (GROUP, GROUP, GROUP, N_HEADS, N_HEADS, GROUP,
             GROUP, GROUP, GROUP, GROUP,
             N_HEADS * GLA_DK, N_HEADS * GLA_DK, GROUP, GLA_RANK, GROUP,
             GROUP, GROUP, GROUP, GROUP)

LANE = 128
SUBLANE = 8
VMEM_LIMIT = 56 * 1024 * 1024

COL_ORDER = (0, 1, 2, 5, 6, 7, 8, 9, 10, 11, 12, 14, 15, 16, 17, 18)
A_QKV, A_GATE = 0, 768
B_Q, B_F, B_I, B_GATE = 1024, 1280, 1536, 1792
C_Q, C_K, C_V, C_GATE = 2048, 2176, 2304, 2560
D_Q, D_K, D_V, D_GATE = 2816, 3072, 3328, 3584
SMALL = 3840
SM_BETA, SM_ALPHA, SM_LR = 0, N_HEADS, 2 * N_HEADS
D_IN_AL = SMALL + LANE
SILU_COLS = (A_GATE, B_Q, B_GATE, C_GATE, D_GATE)
CONV_HIST = SUBLANE
ROW_TILE = 512
PROJ_STAGE_ROWS = 256
PROMPT_GROUP = 8
STATE_ROWS_PER_ITER = 8


def _layer_norm(x, g, b):
    mu = jnp.mean(x, -1, keepdims=True)
    xc = x - mu
    var = jnp.mean(xc * xc, -1, keepdims=True)
    return xc * lax.rsqrt(var + LN_EPS) * g + b


def _row_specs(width, n_main, rows_tail):
    return (pl.BlockSpec((ROW_TILE, width), lambda i: (jnp.minimum(i, n_main - 1), 0)),
            pl.BlockSpec((rows_tail, width), lambda i: (0, 0)))


def _const_spec(shape, single_buffer=False):
    nd = len(shape)
    kw = dict(pipeline_mode=pl.Buffered(1)) if single_buffer else {}
    return pl.BlockSpec(shape, lambda *_: (0,) * nd, **kw)


def _on_region(n_main, main_fn, tail_fn):
    i = pl.program_id(0)
    pl.when(i < n_main)(main_fn)
    pl.when(i >= n_main)(tail_fn)


def _w_in_segments():
    offs = np.concatenate([[0], np.cumsum(IN_SPLITS)])
    runs, dst = [], 0
    for i in COL_ORDER + (3, 4, 13):
        if runs and runs[-1][1] == offs[i]:
            runs[-1][1] = int(offs[i + 1])
        else:
            runs.append([int(offs[i]), int(offs[i + 1]), dst])
        dst += IN_SPLITS[i]
    return [tuple(r) for r in runs]


def _proj_tile(with_ln, layer, x_ref, g_ref, b_ref, lbl_ref, w_ref, cos_ref, sin_ref, p_ref):
    x = x_ref[...]
    if with_ln:
        x = _layer_norm(x, g_ref[...], b_ref[...])
    h = x.astype(bf16)
    n = w_ref.shape[1]
    half = HEAD_DIM // 2
    first_half = (_iota((1, LANE), 1) % HEAD_DIM) < half

    def rope(p):
        parts = []
        for l0 in range(0, GROUP, LANE):
            x = p[:, l0:l0 + LANE]
            other = jnp.where(first_half, pltpu.roll(x, LANE - half, 1), pltpu.roll(x, half, 1))
            parts.append(x * cos_ref[...] + other * sin_ref[...])
        return jnp.concatenate(parts, -1)
    starts = sorted(range(0, n, GROUP), key=lambda c0: c0 not in (D_Q, D_K))
    for n0 in starts:
        n1 = min(n0 + GROUP, n)
        p = jnp.dot(h, w_ref[:, n0:n1], preferred_element_type=f32)
        if n0 in SILU_COLS:
            p = jax.nn.silu(p)
        elif n0 == B_F:
            lbl = lbl_ref[...]
            e = jnp.exp(lbl - jnp.max(lbl, axis=0, keepdims=True))
            prob = e / jnp.sum(e, axis=0, keepdims=True)
            lb = jnp.sum(prob[1:layer + 1], axis=0, keepdims=True) if layer > 0 else jnp.zeros((1, GROUP), f32)
            p = (1.0 - lb) * jax.nn.sigmoid(-p)
        elif n0 == D_Q:
            p = rope(p)
        elif n0 == D_K:
            p = rope(p) * HEAD_DIM ** -0.5
        p_ref[:, n0:n1] = p


def _stage_w_in(wt_hbm, layer, w_scr, stage, sem):
    rows = PROJ_STAGE_ROWS
    big, small = [], []
    for s0, s1, d0 in _w_in_segments():
        if (s1 - s0) % rows == 0 and d0 % LANE == 0:
            big += [(s0 + j, d0 + j) for j in range(0, s1 - s0, rows)]
        else:
            small.append((s0, s1 - s0, d0 - SMALL))
    assert all(0 <= d and d + n <= LANE and d % SUBLANE == 0 for _, n, d in small) and len(small) <= 2

    def copy(j):
        return pltpu.make_async_copy(wt_hbm.at[layer, pl.ds(big[j][0], rows), :], stage.at[j % 2], sem.at[j % 2])
    copy(0).start()
    for j in range(len(big)):
        if j + 1 < len(big):
            copy(j + 1).start()
        copy(j).wait()
        w_scr[:, big[j][1]:big[j][1] + rows] = stage[j % 2].T.astype(bf16)
    used = max(d + n for _, n, d in small)
    stage[0, used:LANE, :] = jnp.zeros((LANE - used, D_MODEL), f32)
    copies = [pltpu.make_async_copy(wt_hbm.at[layer, pl.ds(s0, n), :], stage.at[0, pl.ds(d, n), :], sem.at[i])
              for i, (s0, n, d) in enumerate(small)]
    for cp in copies:
        cp.start()
    for cp in copies:
        cp.wait()
    w_scr[:, SMALL:SMALL + LANE] = stage[0, 0:LANE, :].T.astype(bf16)


def _proj_body(with_ln, layer, n_main, xm_ref, xt_ref, g_ref, b_ref, lbl_ref, wt_hbm, cm_ref, sm_ref, ct_ref, st_ref,
               pm_ref, pt_ref, w_scr, stage, sem):
    pl.when(pl.program_id(0) == 0)(lambda: _stage_w_in(wt_hbm, layer, w_scr, stage, sem))
    _on_region(n_main,
               lambda: _proj_tile(with_ln, layer, xm_ref, g_ref, b_ref, lbl_ref, w_scr, cm_ref, sm_ref, pm_ref),
               lambda: _proj_tile(with_ln, layer, xt_ref, g_ref, b_ref, lbl_ref, w_scr, ct_ref, st_ref, pt_ref))


def _proj(xm, xt, g, b, lb_logits, w_in_t, layer, with_ln, rope_main, rope_tail):
    n_main, rows_tail = xm.shape[0] // ROW_TILE, xt.shape[0]
    n_tab = rope_main[0].shape[0] // ROW_TILE
    tab_main = pl.BlockSpec((ROW_TILE, LANE), lambda i: (jnp.minimum(i, n_main - 1) % n_tab, 0))
    n = D_IN_AL
    out_shape = [jax.ShapeDtypeStruct((xm.shape[0], n), f32), jax.ShapeDtypeStruct((rows_tail, n), f32)]
    out_specs = list(_row_specs(n, n_main, rows_tail))
    return pl.pallas_call(
        functools.partial(_proj_body, with_ln, layer, n_main),
        out_shape=out_shape,
        grid=(n_main + 1,),
        in_specs=list(_row_specs(D_MODEL, n_main, rows_tail))
                 + [_const_spec((1, D_MODEL)), _const_spec((1, D_MODEL)), _const_spec((DEPTH, GROUP)),
                    pl.BlockSpec(memory_space=pl.ANY), tab_main, tab_main, _const_spec((rows_tail, LANE), True), _const_spec((rows_tail, LANE), True)],
        out_specs=out_specs,
        scratch_shapes=[pltpu.VMEM((D_MODEL, n), bf16),
                        pltpu.VMEM((2, PROJ_STAGE_ROWS, D_MODEL), f32),
                        pltpu.SemaphoreType.DMA((2,))],
        compiler_params=pltpu.CompilerParams(dimension_semantics=("arbitrary",),
                                             vmem_limit_bytes=VMEM_LIMIT),
        name="in_proj",
    )(xm, xt, g, b, lb_logits, w_in_t, *rope_main, *rope_tail)


def _post_tile(emb_ln, x_ref, o_ref, ge_ref, be_ref, wo_ref, g1_ref, b1_ref, wg_ref, wu_ref, wd_ref, g2_ref, b2_ref,
               y_ref):
    x = x_ref[...]
    if emb_ln:
        x = _layer_norm(x, ge_ref[...], be_ref[...])
    m = jnp.dot(o_ref[...], wo_ref[...], preferred_element_type=f32)
    y1 = _layer_norm(ALPHA * x + m, g1_ref[...], b1_ref[...])
    y1b = y1.astype(bf16)
    dff = wg_ref.shape[1]
    step = 4 * LANE
    acc = jnp.zeros(y1.shape, f32)
    for n0 in range(0, dff, step):
        n1 = min(n0 + step, dff)
        gate = jnp.dot(y1b, wg_ref[:, n0:n1], preferred_element_type=f32)
        up = jnp.dot(y1b, wu_ref[:, n0:n1], preferred_element_type=f32)
        act = (jax.nn.silu(gate) * up).astype(bf16)
        acc = acc + jnp.dot(act, wd_ref[n0:n1, :], preferred_element_type=f32)
    y_ref[...] = _layer_norm(ALPHA * y1 + acc, g2_ref[...], b2_ref[...])


def _post_body(layer, n_main, xm_ref, xt_ref, om_ref, ot_ref, ge_ref, be_ref, g1_ref, b1_ref, g2_ref, b2_ref,
               wo_ref, wg_ref, wu_ref, wd_ref, ym_ref, yt_ref):
    w = (ge_ref, be_ref, wo_ref, g1_ref, b1_ref, wg_ref, wu_ref, wd_ref, g2_ref, b2_ref)
    _on_region(n_main, lambda: _post_tile(layer == 0, xm_ref, om_ref, *w, ym_ref),
               lambda: _post_tile(layer == 0, xt_ref, ot_ref, *w, yt_ref))


def _post(xm, xt, om, ot, ge, be, g1, b1, g2, b2, w_out, w_gate, w_up, w_down, layer):
    n_main = xm.shape[0] // ROW_TILE
    rows = lambda: list(_row_specs(D_MODEL, n_main, xt.shape[0]))
    vec = lambda: _const_spec((1, D_MODEL))
    return pl.pallas_call(
        functools.partial(_post_body, layer, n_main),
        out_shape=[jax.ShapeDtypeStruct(xm.shape, f32), jax.ShapeDtypeStruct(xt.shape, f32)],
        grid=(n_main + 1,),
        in_specs=rows() + rows() + [vec() for _ in range(6)]
                 + [_const_spec(w.shape, True) for w in (w_out, w_gate, w_up, w_down)],
        out_specs=rows(),
        compiler_params=pltpu.CompilerParams(dimension_semantics=("arbitrary",),
                                             vmem_limit_bytes=VMEM_LIMIT),
        name="post_ffn",
    )(xm, xt, om, ot, ge, be, g1, b1, g2, b2, w_out, w_gate, w_up, w_down)


SAFE_EXP = 80.0
BASE_BLOCK = 8
assert (BASE_BLOCK // 2) * -math.log(GATE_CLAMP) < SAFE_EXP


def _levels(c, base=0):
    m, out = c // 2, []
    while m >= max(base, 1):
        out.append(m)
        m //= 2
    return tuple(out)


def _chunk_constants(c, base, nblk):
    t = np.arange(c)
    rows = [(t[None, :] <= t[:, None]).astype(np.float32), (t[None, :] > t[:, None]).astype(np.float32)]
    masks = []
    for m in _levels(c, base):
        blk = t // m
        upper = (blk % 2) == 1
        lo = np.where(upper, blk * m, t + 1)
        hi = np.where(upper, t, (blk + 1) * m - 1)
        rows.append(((t[None, :] >= lo[:, None]) & (t[None, :] <= hi[:, None])).astype(np.float32))
        same = (t[:, None] // (2 * m)) == (t[None, :] // (2 * m))
        masks.append((same & upper[:, None] & (~upper)[None, :]).astype(np.float32))
    if base:
        mid = (t // base) * base + base // 2 - 1
        after = (t[None, :] > mid[:, None]) & (t[None, :] <= t[:, None])
        before = (t[None, :] > t[:, None]) & (t[None, :] <= mid[:, None])
        rows.append(after.astype(np.float32) - before.astype(np.float32))
        masks.append(((t[:, None] // base == t[None, :] // base) & (t[None, :] <= t[:, None])).astype(np.float32))
    else:
        masks.append(np.eye(c, dtype=np.float32))
    return np.concatenate(rows, 0), np.tile(np.stack(masks, 0), (1, 1, nblk))


def _mm(a, b):
    return jnp.einsum('gmk,gkn->gmn', a.astype(bf16), b.astype(bf16), preferred_element_type=f32)


def _mm_nt(a, b):
    return jnp.einsum('gmk,gnk->gmn', a.astype(bf16), b.astype(bf16), preferred_element_type=f32)


def _mm_tn(a, b):
    return lax.dot_general(a.astype(bf16), b.astype(bf16), (((1,), (1,)), ((0,), (0,))),
                           preferred_element_type=f32)


def _iota(shape, dim):
    return lax.broadcasted_iota(jnp.int32, shape, dim)


def _bd(z, nblk, blk=None):
    z = z.astype(bf16)
    if blk is None:
        blk = _iota((1, 1, z.shape[-1]), 2) // (z.shape[-1] // nblk)
    return jnp.concatenate([jnp.where(blk == h, z, jnp.zeros_like(z)) for h in range(nblk)], axis=1)


def _hmm(x, z, nblk):
    return _mm(x, _bd(z, nblk))


def _hmm_nt(x, z, nblk):
    return _mm_nt(x, _bd(z, nblk))


def _hmm_split(x, z, nblk, blk=None):
    m = x.shape[1]
    x_hi, z_hi = x.astype(bf16), z.astype(bf16)
    x_lo = (x - x_hi.astype(f32)).astype(bf16)
    z_lo = (z - z_hi.astype(f32)).astype(bf16)
    both = _mm(jnp.concatenate([x_hi, x_lo], 1), _bd(z_hi, nblk, blk))
    return both[:, 0:m] + (both[:, m:2 * m] + _mm(x_hi, _bd(z_lo, nblk, blk)))


def _block_mask(rows, lanes, dr, dl):
    return (_iota((rows, lanes), 0) // dr) == (_iota((rows, lanes), 1) // dl)


def _dot01(m01, x):
    nb = x.shape[0]
    x2 = jnp.concatenate([x[i] for i in range(nb)], -1) if nb > 1 else x[0]
    hi = x2.astype(bf16)
    lo = (x2 - hi.astype(f32)).astype(bf16)
    m01 = m01.astype(bf16)
    d = lambda y: jnp.dot(m01, y, preferred_element_type=f32)
    out = d(hi) + d(lo)
    w = x.shape[-1]
    return jnp.stack([out[:, i * w:(i + 1) * w] for i in range(nb)], 0)


def _seg_sum(x, ones_bd, pieces=2):
    nb, c, w = x.shape
    x2 = x.reshape(nb * c, w)
    hi = x2.astype(bf16)
    out = jnp.dot(hi, ones_bd, preferred_element_type=f32)
    if pieces == 2:
        lo = (x2 - hi.astype(f32)).astype(bf16)
        out = out + jnp.dot(lo, ones_bd, preferred_element_type=f32)
    return out.reshape(nb, c, w)


def _last_row_as_col(b, c):
    return jnp.stack([b[i, c - SUBLANE:c, :].T[:, SUBLANE - 1:SUBLANE] for i in range(b.shape[0])], 0)


def _channel_decay_mixer(q, k, v, gl, s, nblk, sums_ref, masks_ref, bdmask, c, base):
    lv = _levels(c, base)
    nl = len(lv)
    sums = _dot01(sums_ref[...], gl)
    b = sums[:, 0:c]
    rev = sums[:, c:2 * c]
    row = _iota((1, c, 1), 1)
    if base:
        e0 = sums[:, (nl + 2) * c:(nl + 3) * c]
        att = jnp.where(masks_ref[nl] > 0.0, _hmm_nt(q * jnp.exp(e0), k * jnp.exp(-e0), nblk), 0.0)
    else:
        att = masks_ref[nl] * _hmm_nt(q, k, nblk)
    for li, m in enumerate(lv):
        upper = ((row // m) % 2) == 1
        w = jnp.where(upper, q, k) * jnp.exp(sums[:, (li + 2) * c:(li + 3) * c])
        att = att + masks_ref[li] * _hmm_nt(w, w, nblk)
    o = _mm(jnp.concatenate([q * jnp.exp(b), att], -1), jnp.concatenate([s.astype(bf16), _bd(v, nblk)], 1))
    s_new = jnp.exp(_last_row_as_col(b, c)) * s + jnp.where(bdmask, _mm_tn(k * jnp.exp(rev), v), 0.0)
    return o, s_new


MixCfg = collections.namedtuple("MixCfg",
                                "c g r0 r1 layer n_chunks init p4d o_rows n_alias stacked cast_slabs n_steps")


def _pairs_to_bd(s):
    z = jnp.zeros(s.shape[:1] + (HEAD_DIM, HEAD_DIM), f32)
    pair = lambda p: jnp.concatenate([jnp.concatenate([s[:, 2 * p], z], -1),
                                      jnp.concatenate([z, s[:, 2 * p + 1]], -1)], 1)
    return jnp.concatenate([pair(0), pair(1)], 0)


def _quad_to_bd(s):
    z = lambda n: jnp.zeros(s.shape[:1] + (GLA_DK, n * HEAD_DIM), f32)
    rows = []
    for h in range(N_HEADS):
        parts = ([z(h)] if h else []) + [s[:, h]] + ([z(N_HEADS - 1 - h)] if h < N_HEADS - 1 else [])
        rows.append(jnp.concatenate(parts, -1))
    return jnp.concatenate(rows, 1)


def _mixer_body(cfg, *refs):
    c, g = cfg.c, cfg.g
    n_w, n_t, n_c = 7, 0, 4
    p_ref = refs[0]
    cw_ref, alog_ref, dtb_ref, lbl_ref, wg_ref, bg_ref, gains_ref = refs[1:1 + n_w]
    sumsb_ref, masksb_ref, sumsc_ref, masksc_ref = refs[1 + n_w + n_t:1 + n_w + n_t + n_c]
    pos = 1 + n_w + n_t + n_c
    if cfg.init:
        convi_ref, sai_ref, sbi_ref, sci_ref, sdi_ref = refs[pos:pos + 5]
        pos += 5
    n_cast = len(cfg.cast_slabs)
    cast_in = refs[pos:pos + n_cast]
    pos += n_cast + cfg.n_alias
    o_ref, convo_ref, sao_ref, sbo_ref, sco_ref, sdo_ref = refs[pos:pos + 6]
    if cfg.stacked:
        convo_ref, sao_ref, sbo_ref, sco_ref, sdo_ref = (r.at[0] for r in (convo_ref, sao_ref, sbo_ref, sco_ref, sdo_ref))
    cast_out = refs[pos + 6:pos + 6 + n_cast]
    cbuf, sa_scr, sb_scr, sc_scr, sd_scr = refs[pos + 6 + n_cast:]
    ci = pl.program_id(1)

    @pl.when(ci == 0)
    def _():
        cbuf[:, 0:CONV_HIST, :] = jnp.zeros((g, CONV_HIST, 3 * GROUP), f32)
        if not cfg.init:
            for r in (sa_scr, sb_scr, sc_scr, sd_scr):
                r[...] = jnp.zeros(r.shape, f32)
        else:
            take = lambda r: jnp.broadcast_to(r[...], (g,) + r.shape[1:])
            cbuf[:, CONV_HIST - (CONV_W - 1):CONV_HIST, :] = take(convi_ref)
            sa_scr[...] = _pairs_to_bd(take(sai_ref))
            sb_scr[...] = _pairs_to_bd(take(sbi_ref))
            sc_scr[...] = _quad_to_bd(take(sci_ref))
            sd_scr[...] = _pairs_to_bd(take(sdi_ref))

    step = pl.program_id(0) * cfg.n_chunks + ci
    for w_f32, w_bf16, n_slabs in sorted(zip(cast_in, cast_out, cfg.cast_slabs), key=lambda e: e[2]):
        def narrow(src=w_f32, dst=w_bf16):
            dst[...] = src[0].astype(bf16)
        if n_slabs == cfg.n_steps:
            narrow()
        else:
            pl.when(step < n_slabs)(narrow)

    def load(a, b):
        if cfg.p4d:
            return p_ref[0, :, :, a:b]
        return p_ref[:, a:b].reshape(g, c, b - a)

    r0, r1 = cfg.r0, cfg.r1
    n_valid = float(r1 - r0)
    row = _iota((1, c, 1), 1)
    vm = ((row >= r0) & (row < r1)).astype(f32)
    lane_v = _iota((1, 1, 2 * HEAD_DIM), 2)
    lane_s = _iota((1, 1, 2 * c), 2)
    s_idx = lane_s % c
    incl = s_idx <= row
    strict = s_idx < row
    bd2 = _block_mask(2 * HEAD_DIM, 2 * HEAD_DIM, HEAD_DIM, HEAD_DIM)
    bd4 = _block_mask(N_HEADS * GLA_DK, GROUP, GLA_DK, HEAD_DIM)
    ones2 = bd2.astype(bf16)
    pairs = lambda x: jnp.concatenate([x[..., :LANE], x[..., LANE:]], 0)
    unpairs = lambda x: jnp.concatenate([x[:g], x[g:]], -1)
    n_solve = max(1, math.ceil(math.log2(r1 - r0)))
    sm = load(SMALL, SMALL + LANE)
    valid = (lambda x: x) if (r0 == 0 and r1 == c) else (lambda x: x * vm)

    cbuf[:, CONV_HIST:CONV_HIST + c, :] = valid(load(A_QKV, A_QKV + 3 * GROUP))
    buf = cbuf[...].reshape(g * (c + CONV_HIST), 3 * GROUP)
    acc = cbuf[:, CONV_HIST:CONV_HIST + c, :] * cw_ref[CONV_W - 1:CONV_W, :]
    for j in range(CONV_W - 1):
        back = pltpu.roll(buf, CONV_W - 1 - j, 0).reshape(g, c + CONV_HIST, 3 * GROUP)
        acc = acc + back[:, CONV_HIST:CONV_HIST + c, :] * cw_ref[j:j + 1, :]
    conv = jax.nn.silu(acc)
    beta = valid(jax.nn.sigmoid(sm))
    g_a = valid(-jnp.exp(alog_ref[...]) * jax.nn.softplus(sm + dtb_ref[...]))
    b_a = _dot01(sumsc_ref[0:c, :], g_a)

    def head_cols(x, lane0):
        col = lambda h: x[..., lane0 + h:lane0 + h + 1]
        return jnp.concatenate([col(0), col(2)], 0), jnp.concatenate([col(1), col(3)], 0)

    def spread(c0, c1, lane, width):
        return jnp.where(lane < width, c0, c1)

    bc0, bc1 = head_cols(b_a, SM_ALPHA)
    bt0, bt1 = head_cols(beta, SM_BETA)
    bc_v, bet_v = spread(bc0, bc1, lane_v, HEAD_DIM), spread(bt0, bt1, lane_v, HEAD_DIM)
    bc_s, bet_s = spread(bc0, bc1, lane_s, c), spread(bt0, bt1, lane_s, c)
    b_rows = [b_a[i].T for i in range(g)]
    b_row = lambda h: jnp.stack([t[SM_ALPHA + h:SM_ALPHA + h + 1, :] for t in b_rows], 0)
    br_s = jnp.concatenate([jnp.concatenate([b_row(0), b_row(1)], -1),
                            jnp.concatenate([b_row(2), b_row(3)], -1)], 0)
    dec = jnp.where(incl, jnp.exp(jnp.where(incl, bc_s - br_s, 0.0)), 0.0)
    q = pairs(conv[..., 0:GROUP])
    k = pairs(conv[..., GROUP:2 * GROUP])
    v = pairs(valid(conv[..., 2 * GROUP:3 * GROUP]))
    q = q * lax.rsqrt(_seg_sum(q * q, ones2, 1) + NORM_EPS) * HEAD_DIM ** -0.5
    k = valid(k * lax.rsqrt(_seg_sum(k * k, ones2) + NORM_EPS))
    s_a = sa_scr[...]
    kq = jnp.concatenate([k, q], 1)
    kq_k = _hmm_nt(kq, k, 2)
    kq_s = _mm(kq, s_a)
    a_mat = jnp.where(strict, bet_s * dec * kq_k[:, 0:c], 0.0)
    gam = jnp.exp(bc_v)
    u = bet_v * (v - gam * kq_s[:, 0:c])
    p = -a_mat
    lane_pu = _iota((1, 1, 2 * c + 2 * HEAD_DIM), 2)
    head_pu = jnp.where(lane_pu < 2 * c, lane_pu // c, (lane_pu - 2 * c) // HEAD_DIM)
    for i in range(n_solve):
        if i + 1 < n_solve:
            pu = _hmm_split(p, jnp.concatenate([p, u], -1), 2, head_pu)
            p, u = pu[..., 0:2 * c], u + pu[..., 2 * c:]
        else:
            u = u + _hmm_split(p, u, 2)
    o_a = gam * kq_s[:, c:2 * c] + _hmm(dec * kq_k[:, c:2 * c], u, 2)
    bl = bc_v[:, c - 1:c, :]
    sa_scr[...] = jnp.exp(bl) * s_a + jnp.where(bd2, _mm_tn(k * jnp.exp(bl - bc_v), u), 0.0)
    o_a = unpairs(o_a * lax.rsqrt(_seg_sum(o_a * o_a, ones2, 1) * (1.0 / HEAD_DIM) + NORM_EPS))

    kb = load(B_F, B_F + GROUP)
    g_b = valid(jnp.log1p(-jnp.clip(kb, 0.0, 1.0 - GATE_CLAMP)))
    o_b, s_b_new = _channel_decay_mixer(
        pairs(load(B_Q, B_Q + GROUP)), pairs(valid(kb)), pairs(valid(load(B_I, B_I + GROUP))),
        pairs(g_b), sb_scr[...], 2, sumsb_ref, masksb_ref, bd2, c, BASE_BLOCK if c > BASE_BLOCK else 0)
    sb_scr[...] = s_b_new
    o_b = unpairs(o_b * lax.rsqrt(_seg_sum(o_b * o_b, ones2, 1) * (1.0 / HEAD_DIM) + NORM_EPS))

    logit = jnp.dot(sm.reshape(g * c, LANE).astype(bf16), wg_ref[...].astype(bf16),
                    preferred_element_type=f32).reshape(g, c, LANE) + bg_ref[...]
    g_c = valid(jax.nn.log_sigmoid(logit) * (1.0 / GLA_GATE_NORM))
    c_args = (load(C_Q, C_Q + N_HEADS * GLA_DK) * GLA_DK ** -0.5, valid(load(C_K, C_K + N_HEADS * GLA_DK)),
              valid(load(C_V, C_V + GROUP)), g_c, sc_scr[...], N_HEADS)
    o_c, s_c_new = _channel_decay_mixer(*c_args, sumsc_ref, masksc_ref, bd4, c, 0)
    sc_scr[...] = s_c_new
    ones4 = _block_mask(GROUP, GROUP, HEAD_DIM, HEAD_DIM).astype(bf16)
    o_c = o_c * lax.rsqrt(_seg_sum(o_c * o_c, ones4, 1) * (1.0 / HEAD_DIM) + NORM_EPS)

    q = pairs(load(D_Q, D_Q + GROUP))
    k = pairs(valid(load(D_K, D_K + GROUP)))
    v = pairs(valid(load(D_V, D_V + GROUP)))
    pair_idx = _iota((2, 1, 1), 0)
    log_gamma = lambda lane, width: jnp.log1p(-jnp.exp2(-5.0 - (2 * pair_idx + lane // width).astype(f32)))
    per_seq = lambda x: jnp.concatenate([jnp.broadcast_to(x[p:p + 1], (g,) + x.shape[1:]) for p in range(2)], 0)
    lg_v, lg_s = log_gamma(lane_v, HEAD_DIM), log_gamma(lane_s, c)
    cnt_c = jnp.clip(row + 1 - r0, 0, r1 - r0).astype(f32)
    cnt_s = jnp.clip(s_idx + 1 - r0, 0, r1 - r0).astype(f32)
    dec = per_seq(jnp.where(incl, jnp.exp(jnp.where(incl, (cnt_c - cnt_s) * lg_s, 0.0)), 0.0))
    s_d = sd_scr[...]
    o_d = _mm(jnp.concatenate([q * per_seq(jnp.exp(cnt_c * lg_v)), dec * _hmm_nt(q, k, 2)], -1),
              jnp.concatenate([s_d.astype(bf16), _bd(v, 2)], 1))
    sd_scr[...] = (per_seq(jnp.exp(n_valid * lg_v)) * s_d
                   + jnp.where(bd2, _mm_tn(k * per_seq(jnp.exp((n_valid - cnt_c) * lg_v)), v), 0.0))
    oc = o_d - _seg_sum(o_d, ones2) * (1.0 / HEAD_DIM)
    o_d = unpairs(oc * lax.rsqrt(_seg_sum(oc * oc, ones2, 1) * (1.0 / HEAD_DIM) + LN_EPS))

    for i, (o_m, gate_col) in enumerate(zip((o_a, o_b, o_c, o_d), (A_GATE, B_GATE, C_GATE, D_GATE))):
        val = (o_m * gains_ref[i:i + 1, :] * load(gate_col, gate_col + GROUP)).astype(o_ref.dtype)
        if cfg.p4d:
            o_ref[0, :, :, i * GROUP:(i + 1) * GROUP] = val
        else:
            o_ref[0:g * c, i * GROUP:(i + 1) * GROUP] = val.reshape(g * c, GROUP)
    if cfg.o_rows > g * c:
        o_ref[g * c:cfg.o_rows, :] = jnp.zeros((cfg.o_rows - g * c, D_MODEL), o_ref.dtype)
    cbuf[:, 0:CONV_HIST, :] = cbuf[:, c:c + CONV_HIST, :]

    @pl.when(ci == cfg.n_chunks - 1)
    def _():
        convo_ref[...] = cbuf[:, pl.ds(CONV_HIST + r1 - (CONV_W - 1), CONV_W - 1), :]
        for scr, out in ((sa_scr, sao_ref), (sb_scr, sbo_ref), (sd_scr, sdo_ref)):
            s = scr[...]
            for p in range(2):
                out[:, 2 * p] = s[p * g:(p + 1) * g, 0:HEAD_DIM, 0:HEAD_DIM]
                out[:, 2 * p + 1] = s[p * g:(p + 1) * g, HEAD_DIM:2 * HEAD_DIM, HEAD_DIM:2 * HEAD_DIM]
        s = sc_scr[...]
        for h in range(N_HEADS):
            sco_ref[:, h] = s[:, h * GLA_DK:(h + 1) * GLA_DK, h * HEAD_DIM:(h + 1) * HEAD_DIM]


def _state_shapes(n, stacked):
    lead = (DEPTH, n) if stacked else (n,)
    return [jax.ShapeDtypeStruct(lead + (CONV_W - 1, 3 * GROUP), f32),
            jax.ShapeDtypeStruct(lead + (N_HEADS, HEAD_DIM, HEAD_DIM), f32),
            jax.ShapeDtypeStruct(lead + (N_HEADS, HEAD_DIM, HEAD_DIM), f32),
            jax.ShapeDtypeStruct(lead + (N_HEADS, GLA_DK, HEAD_DIM), f32),
            jax.ShapeDtypeStruct(lead + (N_HEADS, HEAD_DIM, HEAD_DIM), f32)]


def _state_out_specs(g, layer, stacked):
    if stacked:
        st = lambda k: pl.BlockSpec((1, g, N_HEADS, k, HEAD_DIM), lambda b, i: (layer, b, 0, 0, 0))
        conv = pl.BlockSpec((1, g, CONV_W - 1, 3 * GROUP), lambda b, i: (layer, b, 0, 0))
    else:
        st = lambda k: pl.BlockSpec((g, N_HEADS, k, HEAD_DIM), lambda b, i: (b, 0, 0, 0))
        conv = pl.BlockSpec((g, CONV_W - 1, 3 * GROUP), lambda b, i: (b, 0, 0))
    return [conv, st(HEAD_DIM), st(HEAD_DIM), st(GLA_DK), st(HEAD_DIM)]


def _mixer_call(cfg, name, n_groups, p, p_spec, wts, consts, init, init_specs,
                o_shape, o_spec, n_state_rows, o_prev=None, states_prev=None, cast=()):
    c, g = cfg.c, cfg.g
    n_steps = n_groups * cfg.n_chunks
    wspecs =[_const_spec((CONV_W, 3 * GROUP)), _const_spec((1, LANE)), _const_spec((1, LANE)),
              _const_spec((DEPTH, GROUP)), _const_spec((LANE, LANE)), _const_spec((1, LANE)),
              _const_spec((4, GROUP))]
    cspecs = [_const_spec(a.shape) for a in consts]
    in_specs = [p_spec] + wspecs + cspecs + list(init_specs)
    args = [p, *wts, *consts, *init]
    cast_shapes, cast_specs = [], []
    for w, n_slabs in zip(cast, cfg.cast_slabs):
        slab = w.shape[1] // n_slabs
        assert w.shape[1] % n_slabs == 0 and slab % (2 * SUBLANE) == 0 and n_slabs <= n_groups * cfg.n_chunks
        slab_idx = lambda b, i, n=n_slabs: jnp.minimum(b * cfg.n_chunks + i, n - 1)
        args.append(w)
        in_specs.append(pl.BlockSpec((1, slab, w.shape[2]), lambda b, i, f=slab_idx: (cfg.layer, f(b, i), 0)))
        cast_shapes.append(jax.ShapeDtypeStruct(w.shape[1:], bf16))
        cast_specs.append(pl.BlockSpec((slab, w.shape[2]), lambda b, i, f=slab_idx: (f(b, i), 0)))
    assert len(cast) == len(cfg.cast_slabs)
    aliases = {}
    if o_prev is not None:
        aliases[len(args)] = 0
        args.append(o_prev)
    for j, s_prev in enumerate(states_prev or ()):
        aliases[len(args)] = 1 + j
        args.append(s_prev)
    in_specs += [pl.BlockSpec(memory_space=pl.ANY)] * len(aliases)
    assert len(aliases) == cfg.n_alias
    return pl.pallas_call(
        functools.partial(_mixer_body, cfg),
        out_shape=[o_shape] + _state_shapes(n_state_rows, cfg.stacked) + cast_shapes,
        grid=(n_groups, cfg.n_chunks),
        in_specs=in_specs,
        out_specs=[o_spec] + _state_out_specs(g, cfg.layer, cfg.stacked) + cast_specs,
        scratch_shapes=[pltpu.VMEM((g, c + CONV_HIST, 3 * GROUP), f32),
                        pltpu.VMEM((2 * g, 2 * HEAD_DIM, 2 * HEAD_DIM), f32),
                        pltpu.VMEM((2 * g, 2 * HEAD_DIM, 2 * HEAD_DIM), f32),
                        pltpu.VMEM((g, N_HEADS * GLA_DK, GROUP), f32),
                        pltpu.VMEM((2 * g, 2 * HEAD_DIM, 2 * HEAD_DIM), f32)],
        input_output_aliases=aliases,
        compiler_params=pltpu.CompilerParams(dimension_semantics=("arbitrary", "arbitrary"),
                                             vmem_limit_bytes=VMEM_LIMIT),
        name=name,
    )(*args)


def _tok(ref, t, nb):
    return ref[t * nb:(t + 1) * nb, :].T


def _bcast_row(ref, t, r):
    return jnp.broadcast_to(ref[t, pl.ds(r, 1), :], (HEAD_DIM, ref.shape[-1]))


def _bl_diag_loop(n_tok, kdim, j, row0, s_in, s_out, d_scr, k_scr, q_scr, v_scr, d_const=None):
    nb = s_in.shape[-1]

    def body(k, os):
        s_k = s_in[0, j, k]
        r = row0 + k
        outs = []
        for t in range(n_tok):
            d = d_const if d_const is not None else _bcast_row(d_scr, t, r)
            s_k = d * s_k + _bcast_row(k_scr, t, r) * v_scr[t, j * HEAD_DIM:(j + 1) * HEAD_DIM, :]
            outs.append(os[t] + _bcast_row(q_scr, t, r) * s_k)
        s_out[0, j, k] = s_k
        return tuple(outs)
    return lax.fori_loop(0, kdim, body, (jnp.zeros((HEAD_DIM, nb), f32),) * n_tok, unroll=STATE_ROWS_PER_ITER)


def _bl_store(o_ref, os, gate_ref, gain_ref, n_tok, nb, group_norm):
    for t in range(n_tok):
        gate = _tok(gate_ref, t, nb)
        halves = []
        for j in range(2):
            o = os[j][t]
            if group_norm:
                o = o - jnp.mean(o, axis=0, keepdims=True)
                o = o * lax.rsqrt(jnp.mean(o * o, axis=0, keepdims=True) + LN_EPS)
            else:
                o = o * lax.rsqrt(jnp.mean(o * o, axis=0, keepdims=True) + NORM_EPS)
            hs = slice(j * HEAD_DIM, (j + 1) * HEAD_DIM)
            halves.append(o * gain_ref[hs, :] * gate[hs, :])
        o_ref[t * nb:(t + 1) * nb, :] = jnp.concatenate(halves, 0).T.astype(o_ref.dtype)


def _bl_hgrn_body(layer, n_tok, q_ref, f_ref, i_ref, gate_ref, lbl_ref, gain_ref, s_in, o_prev, s_prev,
                  o_ref, s_out, d_scr, k_scr, q_scr, v_scr):
    del o_prev, s_prev
    nb = s_in.shape[-1]
    del layer, lbl_ref
    for t in range(n_tok):
        kk = _tok(f_ref, t, nb)
        d_scr[t] = 1.0 - jnp.clip(kk, 0.0, 1.0 - GATE_CLAMP)
        k_scr[t] = kk
        q_scr[t] = _tok(q_ref, t, nb)
        v_scr[t] = _tok(i_ref, t, nb)
    os = [_bl_diag_loop(n_tok, HEAD_DIM, j, j * HEAD_DIM, s_in, s_out, d_scr, k_scr, q_scr, v_scr) for j in range(2)]
    _bl_store(o_ref, os, gate_ref, gain_ref, n_tok, nb, False)


def _bl_gla_body(n_tok, q_ref, k_ref, v_ref, gate_ref, small_ref, wg_ref, bg_ref, gain_ref, s_in, o_prev, s_prev,
                 o_ref, s_out, d_scr, k_scr, q_scr, v_scr):
    del o_prev, s_prev
    nb = s_in.shape[-1]
    for t in range(n_tok):
        logit = jnp.dot(wg_ref[...].astype(bf16), _tok(small_ref, t, nb).astype(bf16),
                        preferred_element_type=f32) + bg_ref[...]
        d_scr[t] = jnp.exp(jax.nn.log_sigmoid(logit) * (1.0 / GLA_GATE_NORM))
        k_scr[t] = _tok(k_ref, t, nb)
        q_scr[t] = _tok(q_ref, t, nb) * GLA_DK ** -0.5
        v_scr[t] = _tok(v_ref, t, nb)
    p = pl.program_id(0)
    os = [_bl_diag_loop(n_tok, GLA_DK, j, (2 * p + j) * GLA_DK, s_in, s_out, d_scr, k_scr, q_scr, v_scr)
          for j in range(2)]
    _bl_store(o_ref, os, gate_ref, gain_ref, n_tok, nb, False)


def _bl_ret_body(n_tok, q_ref, k_ref, v_ref, gate_ref, gain_ref, s_in, o_prev, s_prev,
                 o_ref, s_out, k_scr, q_scr, v_scr):
    del o_prev, s_prev
    nb = s_in.shape[-1]
    for t in range(n_tok):
        q_scr[t] = _tok(q_ref, t, nb)
        k_scr[t] = _tok(k_ref, t, nb)
        v_scr[t] = _tok(v_ref, t, nb)
    p = pl.program_id(0)
    os = []
    for j in range(2):
        head = jnp.zeros((1, nb), f32) + (2 * p + j).astype(f32)
        gamma = jnp.broadcast_to(1.0 - jnp.exp2(-5.0 - head), (HEAD_DIM, nb))
        os.append(_bl_diag_loop(n_tok, HEAD_DIM, j, j * HEAD_DIM, s_in, s_out, None, k_scr, q_scr, v_scr, gamma))
    _bl_store(o_ref, os, gate_ref, gain_ref, n_tok, nb, True)


def _bl_delta_body(n_tok, q_ref, k_ref, v_ref, gate_ref, small_ref, hq_ref, hk_ref, hv_ref, cwq_ref, cwk_ref, cwv_ref,
                   alog_ref, dtb_ref, gain_ref, s_in, o_prev, s_prev, cq_prev, ck_prev, cv_prev,
                   o_ref, s_out, cq_out, ck_out, cv_out, k_scr, q_scr, kh_scr, u_scr, bg_scr):
    del o_prev, s_prev, cq_prev, ck_prev, cv_prev
    nb = s_in.shape[-1]
    p = pl.program_id(0)

    def conv(x_ref, h_ref, cw_ref):
        seq = [h_ref[0, r].T for r in range(CONV_W - 1)] + [_tok(x_ref, t, nb) for t in range(n_tok)]
        outs = []
        for t in range(n_tok):
            acc = cw_ref[:, 0:1] * seq[t]
            for jj in range(1, CONV_W):
                acc = acc + cw_ref[:, jj:jj + 1] * seq[t + jj]
            outs.append(jax.nn.silu(acc))
        return outs

    def l2norm(x):
        parts = []
        for j in range(2):
            xj = x[j * HEAD_DIM:(j + 1) * HEAD_DIM]
            parts.append(xj * lax.rsqrt(jnp.sum(xj * xj, axis=0, keepdims=True) + NORM_EPS))
        return jnp.concatenate(parts, 0)

    qs, ks, vs = conv(q_ref, hq_ref, cwq_ref), conv(k_ref, hk_ref, cwk_ref), conv(v_ref, hv_ref, cwv_ref)
    for t in range(n_tok):
        q_scr[t] = l2norm(qs[t]) * HEAD_DIM ** -0.5
        k_scr[t] = l2norm(ks[t])
        sm = _tok(small_ref, t, nb)
        both = jnp.where(_iota((LANE, 1), 0) < SM_ALPHA, jax.nn.sigmoid(sm),
                         -jnp.exp(alog_ref[...]) * jax.nn.softplus(sm + dtb_ref[...]))
        bg_scr[t] = both[0:SUBLANE]
    for r in range(CONV_W - 1):
        rows = slice((n_tok - (CONV_W - 1) + r) * nb, (n_tok - (CONV_W - 1) + r + 1) * nb)
        cq_out[0, r] = q_ref[rows, :]
        ck_out[0, r] = k_ref[rows, :]
        cv_out[0, r] = v_ref[rows, :]

    os = []
    for j in range(2):
        hs = slice(j * HEAD_DIM, (j + 1) * HEAD_DIM)
        h = 2 * p + j
        beta = [bg_scr[t, pl.ds(SM_BETA + h, 1), :] for t in range(n_tok)]
        b, acc = [], jnp.zeros((1, nb), f32)
        for t in range(n_tok):
            acc = acc + bg_scr[t, pl.ds(SM_ALPHA + h, 1), :]
            b.append(acc)
        kt = [k_scr[t, hs, :] for t in range(n_tok)]
        qt = [q_scr[t, hs, :] for t in range(n_tok)]
        dot = lambda x, y: jnp.sum(x * y, axis=0, keepdims=True)

        def sweep(scr):
            def body(k, accs):
                s_k = s_in[0, j, k]
                return tuple(a + _bcast_row(scr, t, j * HEAD_DIM + k) * s_k for t, a in enumerate(accs))
            return lax.fori_loop(0, HEAD_DIM, body, (jnp.zeros((HEAD_DIM, nb), f32),) * n_tok,
                                 unroll=STATE_ROWS_PER_ITER)
        k_s, q_s = sweep(k_scr), sweep(q_scr)
        us = []
        for t in range(n_tok):
            u = beta[t] * (vs[t][hs] - jnp.exp(b[t]) * k_s[t])
            for s in range(t):
                u = u - (beta[t] * jnp.exp(b[t] - b[s]) * dot(kt[t], kt[s])) * us[s]
            us.append(u)
        o_h = []
        for t in range(n_tok):
            o = jnp.exp(b[t]) * q_s[t]
            for s in range(t + 1):
                o = o + (jnp.exp(b[t] - b[s]) * dot(qt[t], kt[s])) * us[s]
            o_h.append(o)
        os.append(o_h)
        for t in range(n_tok):
            kh_scr[t, hs, :] = kt[t] * jnp.exp(b[-1] - b[t])
            u_scr[t] = us[t]
        decay = jnp.broadcast_to(jnp.exp(b[-1]), (HEAD_DIM, nb))

        def update(k, carry):
            s_k = decay * s_in[0, j, k]
            for t in range(n_tok):
                s_k = s_k + _bcast_row(kh_scr, t, j * HEAD_DIM + k) * u_scr[t]
            s_out[0, j, k] = s_k
            return carry
        lax.fori_loop(0, HEAD_DIM, update, 0, unroll=STATE_ROWS_PER_ITER)
    _bl_store(o_ref, os, gate_ref, gain_ref, n_tok, nb, False)


def _bl_call(body, name, mixer, n_tok, nb, blk_row, pt, cols, extra, extra_specs, state_t, kdim, layer,
             o_prev, s_prev, scratch, conv_prev=None):
    def tok_spec(c0, per_pair):
        return pl.BlockSpec((n_tok * nb, LANE),
                            (lambda p: (blk_row, c0 // LANE + p)) if per_pair else (lambda p: (blk_row, c0 // LANE)))
    st_spec = pl.BlockSpec((1, 2, kdim, HEAD_DIM, nb), lambda p: (layer, p, 0, 0, 0))
    args = [pt] * len(cols) + list(extra) + [state_t]
    in_specs = [tok_spec(c0, pp) for c0, pp in cols] + list(extra_specs) + [st_spec]
    aliases = {}

    def donate(arr, out_idx):
        if arr is not None:
            aliases[len(args)] = out_idx
        args.append(arr if arr is not None else jnp.zeros((1,), f32))
        in_specs.append(pl.BlockSpec(memory_space=pl.ANY))
    donate(o_prev, 0)
    donate(s_prev, 1)
    out_shape = [jax.ShapeDtypeStruct(o_prev.shape, o_prev.dtype),
                 jax.ShapeDtypeStruct((DEPTH,) + state_t.shape[1:], f32)]
    out_specs = [pl.BlockSpec((n_tok * nb, LANE), lambda p: (blk_row, 2 * mixer + p)), st_spec]
    if conv_prev is not None:
        for i in range(3):
            donate(conv_prev[i] if conv_prev else None, 2 + i)
            out_shape.append(jax.ShapeDtypeStruct((DEPTH, CONV_W - 1, nb, GROUP), f32))
            out_specs.append(pl.BlockSpec((1, CONV_W - 1, nb, LANE), lambda p: (layer, 0, 0, p)))
    return pl.pallas_call(
        body, out_shape=out_shape, grid=(2,), in_specs=in_specs, out_specs=out_specs,
        scratch_shapes=scratch, input_output_aliases=aliases,
        compiler_params=pltpu.CompilerParams(dimension_semantics=("arbitrary",), vmem_limit_bytes=VMEM_LIMIT),
        name=name,
    )(*args)


def _rope_tables(pos):
    half = HEAD_DIM // 2
    inv = 1.0 / (RET_THETA_BASE ** jnp.linspace(0.0, 1.0, half, dtype=f32))
    ang = pos[:, None] * inv[None, :]
    cos, sin = jnp.cos(ang), jnp.sin(ang)
    tile = lambda a, b: jnp.tile(jnp.concatenate([a, b], -1), (1, LANE // HEAD_DIM))
    return tile(cos, cos), tile(-sin, sin)


def kernel(x_prompt, x_sample, state_delta_conv, state_delta, state_hgrn, state_gla, state_ret, meta_tokens, emb_ln_g, emb_ln_b, w_in, conv_w, delta_a_log, delta_dt_bias, delta_norm_g, hgrn_lb_logits, hgrn_norm_g, gla_w_gate, gla_b_gate, gla_norm_g, ret_norm_g, w_out, ln1_g, ln1_b, w_ffn_gate, w_ffn_up, w_ffn_down, ln2_g, ln2_b):
    bsz, seq, _ = x_prompt.shape
    n_seq, n_tok, _ = x_sample.shape
    pg = PROMPT_GROUP if bsz % PROMPT_GROUP == 0 else 1
    rows_main = bsz * seq
    rows_s = n_tok * n_seq
    assert seq % CHUNK == 0 and rows_main % ROW_TILE == 0 and N_META <= CHUNK
    assert n_tok >= CONV_W - 1 and rows_s % CHUNK == 0 and (n_seq % LANE == 0 or n_seq < LANE)
    lead_pad = CHUNK - N_META
    rows_tail = rows_s + CHUNK
    s_blk = 0
    meta_blk = rows_s // CHUNK

    xs = jnp.swapaxes(x_sample, 0, 1).reshape(rows_s, D_MODEL)
    xt = jnp.concatenate([xs, jnp.zeros((lead_pad, D_MODEL), f32), meta_tokens.astype(f32)], 0)
    xm = x_prompt.reshape(rows_main, D_MODEL)

    assert seq % ROW_TILE == 0 or ROW_TILE % seq == 0
    rope_main = _rope_tables(N_META + (jnp.arange(max(seq, ROW_TILE)) % seq).astype(f32))
    rope_tail = _rope_tables(jnp.concatenate([PAST_LEN + jnp.repeat(jnp.arange(n_tok), n_seq).astype(f32),
                                              jnp.arange(CHUNK, dtype=f32) - lead_pad]))
    consts_c = tuple(jnp.asarray(a, f32) for a in (_chunk_constants(CHUNK, BASE_BLOCK, 2)
                                                  + _chunk_constants(CHUNK, 0, N_HEADS)))
    row_vec = lambda v: v.reshape(1, -1).astype(f32)
    col_vec = lambda v: v.reshape(-1, 1).astype(f32)
    small_row = lambda v: jnp.zeros((1, LANE), f32).at[0, SM_ALPHA:SM_ALPHA + N_HEADS].set(v)
    n_chunks = seq // CHUNK

    seq_last = lambda s: jnp.transpose(s, (0, 2, 3, 4, 1))
    sd_t, sh_t, sg_t, sr_t = (seq_last(s) for s in (state_delta, state_hgrn, state_gla, state_ret))
    conv_t = jnp.transpose(state_delta_conv, (0, 2, 1, 3))
    vm = lambda shape: pltpu.VMEM(shape, f32)
    tok_scr = lambda: vm((n_tok, LANE, n_seq))

    w_in_t = jnp.swapaxes(w_in, 1, 2)
    st_p = None
    sa_s = sb_s = sc_s = sd_s = conv_s = None
    for l in range(DEPTH):
        pm, pt = _proj(xm, xt, row_vec(emb_ln_g), row_vec(emb_ln_b), hgrn_lb_logits, w_in_t, l, l == 0,
                       rope_main, rope_tail)
        wg = jnp.zeros((LANE, N_HEADS * GLA_DK), f32).at[SM_LR:SM_LR + GLA_RANK].set(gla_w_gate[l])
        gains = jnp.stack([delta_norm_g[l], hgrn_norm_g[l], gla_norm_g[l], ret_norm_g[l]], 0)
        wts = (conv_w[l], small_row(delta_a_log[l]), small_row(delta_dt_bias[l]), hgrn_lb_logits,
               wg, row_vec(gla_b_gate[l]), gains)

        cfg = MixCfg(CHUNK, 1, lead_pad, CHUNK, l, 1, None, False, CHUNK, 0, False, (), 1)
        ot, *st_meta = _mixer_call(
            cfg, "mixers_meta", 1, pt, pl.BlockSpec((CHUNK, D_IN_AL), lambda b, i: (meta_blk, 0)),
            wts, consts_c, (), (),
            jax.ShapeDtypeStruct((rows_tail, D_MODEL), bf16),
            pl.BlockSpec((CHUNK, D_MODEL), lambda b, i: (meta_blk, 0)), 1)

        n_prev = 0 if st_p is None else len(st_p)
        ffn_w = (w_out, w_ffn_gate, w_ffn_up, w_ffn_down)
        n_steps = (bsz // pg) * n_chunks
        slabs = tuple(max(n for n in range(1, n_steps + 1) if w.shape[1] % n == 0 and (w.shape[1] // n) % (2 * SUBLANE) == 0)
                      for w in ffn_w)
        cfg = MixCfg(CHUNK, pg, 0, CHUNK, l, n_chunks, "shared", True, pg * CHUNK, n_prev, True, slabs, n_steps)
        st1 = lambda k: pl.BlockSpec((1, N_HEADS, k, HEAD_DIM), lambda b, i: (0, 0, 0, 0))
        om, *st_p = _mixer_call(
            cfg, "mixers_prompt", bsz // pg, pm.reshape(bsz // pg, pg, seq, D_IN_AL),
            pl.BlockSpec((1, pg, CHUNK, D_IN_AL), lambda b, i: (b, 0, i, 0)),
            wts, consts_c, st_meta,
            [pl.BlockSpec((1, CONV_W - 1, 3 * GROUP), lambda b, i: (0, 0, 0)),
             st1(HEAD_DIM), st1(HEAD_DIM), st1(GLA_DK), st1(HEAD_DIM)],
            jax.ShapeDtypeStruct((bsz // pg, pg, seq, D_MODEL), bf16),
            pl.BlockSpec((1, pg, CHUNK, D_MODEL), lambda b, i: (b, 0, i, 0)), bsz, states_prev=st_p, cast=ffn_w)
        om = om.reshape(rows_main, D_MODEL)
        st_p, (wo_b, wg_b, wu_b, wd_b) = st_p[:5], st_p[5:]

        pair_col = lambda v: (col_vec(v), pl.BlockSpec((LANE, 1), lambda p: (p, 0)))
        full = lambda a: (a, pl.BlockSpec(a.shape, lambda p: (0,) * a.ndim))
        split = lambda pairs: ([a for a, _ in pairs], [s for _, s in pairs])
        hist = lambda part: (conv_t, pl.BlockSpec((1, CONV_W - 1, n_seq, LANE), lambda p: (l, 0, 0, 2 * part + p)))
        cw_t = conv_w[l].T
        cw = lambda part: (cw_t, pl.BlockSpec((LANE, CONV_W), lambda p: (2 * part + p, 0)))
        small_col = lambda v: jnp.zeros((LANE, 1), f32).at[SM_ALPHA:SM_ALPHA + N_HEADS, 0].set(v)
        extra, especs = split([hist(0), hist(1), hist(2), cw(0), cw(1), cw(2), full(small_col(delta_a_log[l])),
                               full(small_col(delta_dt_bias[l])), pair_col(delta_norm_g[l])])
        ot, sa_s, *conv_s = _bl_call(
            functools.partial(_bl_delta_body, n_tok), "sample_delta", 0, n_tok, n_seq, s_blk, pt,
            [(A_QKV, True), (A_QKV + GROUP, True), (A_QKV + 2 * GROUP, True), (A_GATE, True), (SMALL, False)],
            extra, especs, sd_t, HEAD_DIM, l, ot, sa_s,
            [tok_scr(), tok_scr(), tok_scr(), vm((n_tok, HEAD_DIM, n_seq)), vm((n_tok, SUBLANE, n_seq))],
            conv_prev=conv_s or [])
        extra, especs = split([(hgrn_lb_logits.T, pl.BlockSpec((LANE, DEPTH), lambda p: (p, 0))),
                               pair_col(hgrn_norm_g[l])])
        ot, sb_s = _bl_call(
            functools.partial(_bl_hgrn_body, l, n_tok), "sample_hgrn", 1, n_tok, n_seq, s_blk, pt,
            [(B_Q, True), (B_F, True), (B_I, True), (B_GATE, True)], extra, especs, sh_t, HEAD_DIM, l, ot, sb_s,
            [tok_scr() for _ in range(4)])
        wg_t = jnp.zeros((N_HEADS * GLA_DK, LANE), f32).at[:, SM_LR:SM_LR + GLA_RANK].set(gla_w_gate[l].T)
        extra, especs = split([full(wg_t), full(col_vec(gla_b_gate[l])), pair_col(gla_norm_g[l])])
        ot, sc_s = _bl_call(
            functools.partial(_bl_gla_body, n_tok), "sample_gla", 2, n_tok, n_seq, s_blk, pt,
            [(C_Q, False), (C_K, False), (C_V, True), (C_GATE, True), (SMALL, False)], extra, especs,
            sg_t, GLA_DK, l, ot, sc_s, [tok_scr() for _ in range(4)])
        extra, especs = split([pair_col(ret_norm_g[l])])
        ot, sd_s = _bl_call(
            functools.partial(_bl_ret_body, n_tok), "sample_ret", 3, n_tok, n_seq, s_blk, pt,
            [(D_Q, True), (D_K, True), (D_V, True), (D_GATE, True)], extra, especs, sr_t, HEAD_DIM, l, ot, sd_s,
            [tok_scr() for _ in range(3)])

        xm, xt = _post(xm, xt, om, ot, row_vec(emb_ln_g), row_vec(emb_ln_b),
                       row_vec(ln1_g[l]), row_vec(ln1_b[l]), row_vec(ln2_g[l]), row_vec(ln2_b[l]),
                       wo_b, wg_b, wu_b, wd_b, l)

    y_prompt = xm.reshape(bsz, seq, D_MODEL)
    y_sample = jnp.swapaxes(xt[0:rows_s].reshape(n_tok, n_seq, D_MODEL), 0, 1)
    seq_first = lambda s: jnp.transpose(s, (0, 4, 1, 2, 3))
    st_s = (jnp.transpose(jnp.concatenate(conv_s, -1), (0, 2, 1, 3)),
            seq_first(sa_s), seq_first(sb_s), seq_first(sc_s), seq_first(sd_s))
    return (y_prompt, y_sample) + tuple(v for pair in zip(st_p, st_s) for v in pair)
```

```python
import collections
import functools
import math

import numpy as np
import jax
import jax.numpy as jnp
from jax import lax
from jax.experimental import pallas as pl
from jax.experimental.pallas import tpu as pltpu

f32 = jnp.float32
bf16 = jnp.bfloat16

D_MODEL = 1024
N_META = 16
CHUNK = 64
GROUP = 256
N_HEADS = 4
HEAD_DIM = 64
GLA_DK = 32
GLA_RANK = 16
GLA_GATE_NORM = 16.0
CONV_W = 4
D_FF = 2816
DEPTH = 2
ALPHA = (2 * DEPTH) ** 0.25
PAST_LEN = 16384
RET_THETA_BASE = 10000.0
LN_EPS = 1e-5
NORM_EPS = 1e-6
GATE_CLAMP = 1e-6
IN_SPLITS = (GROUP, GROUP, GROUP, N_HEADS, N_HEADS, GROUP,
             GROUP, GROUP, GROUP, GROUP,
             N_HEADS * GLA_DK, N_HEADS * GLA_DK, GROUP, GLA_RANK, GROUP,
             GROUP, GROUP, GROUP, GROUP)

LANE = 128
SUBLANE = 8
VMEM_LIMIT = 56 * 1024 * 1024

COL_ORDER = (0, 1, 2, 5, 6, 7, 8, 9, 10, 11, 12, 14, 15, 16, 17, 18)
A_QKV, A_GATE = 0, 768
B_Q, B_F, B_I, B_GATE = 1024, 1280, 1536, 1792
C_Q, C_K, C_V, C_GATE = 2048, 2176, 2304, 2560
D_Q, D_K, D_V, D_GATE = 2816, 3072, 3328, 3584
SMALL = 3840
SM_BETA, SM_ALPHA, SM_LR = 0, N_HEADS, 2 * N_HEADS
D_IN_AL = SMALL + LANE
SILU_COLS = (A_GATE, B_Q, B_GATE, C_GATE, D_GATE)
CONV_HIST = SUBLANE
ROW_TILE = 512
PROJ_STAGE_ROWS = 256
PROMPT_GROUP = 8
STATE_ROWS_PER_ITER = 16


def _layer_norm(x, g, b):
    mu = jnp.mean(x, -1, keepdims=True)
    xc = x - mu
    var = jnp.mean(xc * xc, -1, keepdims=True)
    return xc * lax.rsqrt(var + LN_EPS) * g + b


def _row_specs(width, n_main, rows_tail):
    return (pl.BlockSpec((ROW_TILE, width), lambda i: (jnp.minimum(i, n_main - 1), 0)),
            pl.BlockSpec((rows_tail, width), lambda i: (0, 0)))


def _const_spec(shape, single_buffer=False):
    nd = len(shape)
    kw = dict(pipeline_mode=pl.Buffered(1)) if single_buffer else {}
    return pl.BlockSpec(shape, lambda *_: (0,) * nd, **kw)


def _on_region(n_main, main_fn, tail_fn):
    i = pl.program_id(0)
    pl.when(i < n_main)(main_fn)
    pl.when(i >= n_main)(tail_fn)


def _w_in_segments():
    offs = np.concatenate([[0], np.cumsum(IN_SPLITS)])
    runs, dst = [], 0
    for i in COL_ORDER + (3, 4, 13):
        if runs and runs[-1][1] == offs[i]:
            runs[-1][1] = int(offs[i + 1])
        else:
            runs.append([int(offs[i]), int(offs[i + 1]), dst])
        dst += IN_SPLITS[i]
    return [tuple(r) for r in runs]


def _proj_tile(with_ln, x_ref, g_ref, b_ref, w_ref, cos_ref, sin_ref, p_ref):
    x = x_ref[...]
    if with_ln:
        x = _layer_norm(x, g_ref[...], b_ref[...])
    h = x.astype(bf16)
    n = w_ref.shape[1]
    half = HEAD_DIM // 2
    first_half = (_iota((1, LANE), 1) % HEAD_DIM) < half

    def rope(p):
        parts = []
        for l0 in range(0, GROUP, LANE):
            x = p[:, l0:l0 + LANE]
            other = jnp.where(first_half, pltpu.roll(x, LANE - half, 1), pltpu.roll(x, half, 1))
            parts.append(x * cos_ref[...] + other * sin_ref[...])
        return jnp.concatenate(parts, -1)
    starts = sorted(range(0, n, GROUP), key=lambda c0: c0 not in (D_Q, D_K))
    for n0 in starts:
        n1 = min(n0 + GROUP, n)
        p = jnp.dot(h, w_ref[:, n0:n1], preferred_element_type=f32)
        if n0 in SILU_COLS:
            p = jax.nn.silu(p)
        elif n0 == D_Q:
            p = rope(p)
        elif n0 == D_K:
            p = rope(p) * HEAD_DIM ** -0.5
        p_ref[:, n0:n1] = p


def _stage_w_in(wt_hbm, layer, w_scr, stage, sem):
    rows = PROJ_STAGE_ROWS
    big, small = [], []
    for s0, s1, d0 in _w_in_segments():
        if (s1 - s0) % rows == 0 and d0 % LANE == 0:
            big += [(s0 + j, d0 + j) for j in range(0, s1 - s0, rows)]
        else:
            small.append((s0, s1 - s0, d0 - SMALL))
    assert all(0 <= d and d + n <= LANE and d % SUBLANE == 0 for _, n, d in small) and len(small) <= 2

    def copy(j):
        return pltpu.make_async_copy(wt_hbm.at[layer, pl.ds(big[j][0], rows), :], stage.at[j % 2], sem.at[j % 2])
    copy(0).start()
    for j in range(len(big)):
        if j + 1 < len(big):
            copy(j + 1).start()
        copy(j).wait()
        w_scr[:, big[j][1]:big[j][1] + rows] = stage[j % 2].T.astype(bf16)
    used = max(d + n for _, n, d in small)
    stage[0, used:LANE, :] = jnp.zeros((LANE - used, D_MODEL), f32)
    copies = [pltpu.make_async_copy(wt_hbm.at[layer, pl.ds(s0, n), :], stage.at[0, pl.ds(d, n), :], sem.at[i])
              for i, (s0, n, d) in enumerate(small)]
    for cp in copies:
        cp.start()
    for cp in copies:
        cp.wait()
    w_scr[:, SMALL:SMALL + LANE] = stage[0, 0:LANE, :].T.astype(bf16)


def _proj_body(with_ln, layer, n_main, xm_ref, xt_ref, g_ref, b_ref, wt_hbm, cm_ref, sm_ref, ct_ref, st_ref,
               pm_ref, pt_ref, w_scr, stage, sem):
    pl.when(pl.program_id(0) == 0)(lambda: _stage_w_in(wt_hbm, layer, w_scr, stage, sem))
    _on_region(n_main,
               lambda: _proj_tile(with_ln, xm_ref, g_ref, b_ref, w_scr, cm_ref, sm_ref, pm_ref),
               lambda: _proj_tile(with_ln, xt_ref, g_ref, b_ref, w_scr, ct_ref, st_ref, pt_ref))


def _proj(xm, xt, g, b, w_in_t, layer, with_ln, rope_main, rope_tail):
    n_main, rows_tail = xm.shape[0] // ROW_TILE, xt.shape[0]
    n_tab = rope_main[0].shape[0] // ROW_TILE
    tab_main = pl.BlockSpec((ROW_TILE, LANE), lambda i: (jnp.minimum(i, n_main - 1) % n_tab, 0))
    n = D_IN_AL
    out_shape = [jax.ShapeDtypeStruct((xm.shape[0], n), f32), jax.ShapeDtypeStruct((rows_tail, n), f32)]
    out_specs = list(_row_specs(n, n_main, rows_tail))
    return pl.pallas_call(
        functools.partial(_proj_body, with_ln, layer, n_main),
        out_shape=out_shape,
        grid=(n_main + 1,),
        in_specs=list(_row_specs(D_MODEL, n_main, rows_tail))
                 + [_const_spec((1, D_MODEL)), _const_spec((1, D_MODEL)), pl.BlockSpec(memory_space=pl.ANY),
                    tab_main, tab_main, _const_spec((rows_tail, LANE), True), _const_spec((rows_tail, LANE), True)],
        out_specs=out_specs,
        scratch_shapes=[pltpu.VMEM((D_MODEL, n), bf16),
                        pltpu.VMEM((2, PROJ_STAGE_ROWS, D_MODEL), f32),
                        pltpu.SemaphoreType.DMA((2,))],
        compiler_params=pltpu.CompilerParams(dimension_semantics=("arbitrary",),
                                             vmem_limit_bytes=VMEM_LIMIT),
        name="in_proj",
    )(xm, xt, g, b, w_in_t, *rope_main, *rope_tail)


def _post_tile(emb_ln, x_ref, o_ref, ge_ref, be_ref, wo_ref, g1_ref, b1_ref, wg_ref, wu_ref, wd_ref, g2_ref, b2_ref,
               y_ref):
    x = x_ref[...]
    if emb_ln:
        x = _layer_norm(x, ge_ref[...], be_ref[...])
    m = jnp.dot(o_ref[...], wo_ref[...], preferred_element_type=f32)
    y1 = _layer_norm(ALPHA * x + m, g1_ref[...], b1_ref[...])
    y1b = y1.astype(bf16)
    dff = wg_ref.shape[1]
    step = 4 * LANE
    acc = jnp.zeros(y1.shape, f32)
    for n0 in range(0, dff, step):
        n1 = min(n0 + step, dff)
        gate = jnp.dot(y1b, wg_ref[:, n0:n1], preferred_element_type=f32)
        up = jnp.dot(y1b, wu_ref[:, n0:n1], preferred_element_type=f32)
        act = (jax.nn.silu(gate) * up).astype(bf16)
        acc = acc + jnp.dot(act, wd_ref[n0:n1, :], preferred_element_type=f32)
    y_ref[...] = _layer_norm(ALPHA * y1 + acc, g2_ref[...], b2_ref[...])


def _post_body(layer, n_main, xm_ref, xt_ref, om_ref, ot_ref, ge_ref, be_ref, g1_ref, b1_ref, g2_ref, b2_ref,
               wo_ref, wg_ref, wu_ref, wd_ref, ym_ref, yt_ref):
    w = (ge_ref, be_ref, wo_ref, g1_ref, b1_ref, wg_ref, wu_ref, wd_ref, g2_ref, b2_ref)
    _on_region(n_main, lambda: _post_tile(layer == 0, xm_ref, om_ref, *w, ym_ref),
               lambda: _post_tile(layer == 0, xt_ref, ot_ref, *w, yt_ref))


def _post(xm, xt, om, ot, ge, be, g1, b1, g2, b2, w_out, w_gate, w_up, w_down, layer):
    n_main = xm.shape[0] // ROW_TILE
    rows = lambda: list(_row_specs(D_MODEL, n_main, xt.shape[0]))
    vec = lambda: _const_spec((1, D_MODEL))
    return pl.pallas_call(
        functools.partial(_post_body, layer, n_main),
        out_shape=[jax.ShapeDtypeStruct(xm.shape, f32), jax.ShapeDtypeStruct(xt.shape, f32)],
        grid=(n_main + 1,),
        in_specs=rows() + rows() + [vec() for _ in range(6)]
                 + [_const_spec(w.shape, True) for w in (w_out, w_gate, w_up, w_down)],
        out_specs=rows(),
        compiler_params=pltpu.CompilerParams(dimension_semantics=("arbitrary",),
                                             vmem_limit_bytes=VMEM_LIMIT),
        name="post_ffn",
    )(xm, xt, om, ot, ge, be, g1, b1, g2, b2, w_out, w_gate, w_up, w_down)


SAFE_EXP = 80.0
BASE_BLOCK = 8
assert (BASE_BLOCK // 2) * -math.log(GATE_CLAMP) < SAFE_EXP


def _levels(c, base=0):
    m, out = c // 2, []
    while m >= max(base, 1):
        out.append(m)
        m //= 2
    return tuple(out)


def _chunk_constants(c, base, nblk):
    t = np.arange(c)
    rows = [(t[None, :] <= t[:, None]).astype(np.float32), (t[None, :] > t[:, None]).astype(np.float32)]
    masks = []
    for m in _levels(c, base):
        blk = t // m
        upper = (blk % 2) == 1
        lo = np.where(upper, blk * m, t + 1)
        hi = np.where(upper, t, (blk + 1) * m - 1)
        rows.append(((t[None, :] >= lo[:, None]) & (t[None, :] <= hi[:, None])).astype(np.float32))
        same = (t[:, None] // (2 * m)) == (t[None, :] // (2 * m))
        masks.append((same & upper[:, None] & (~upper)[None, :]).astype(np.float32))
    if base:
        mid = (t // base) * base + base // 2 - 1
        after = (t[None, :] > mid[:, None]) & (t[None, :] <= t[:, None])
        before = (t[None, :] > t[:, None]) & (t[None, :] <= mid[:, None])
        rows.append(after.astype(np.float32) - before.astype(np.float32))
        masks.append(((t[:, None] // base == t[None, :] // base) & (t[None, :] <= t[:, None])).astype(np.float32))
    else:
        masks.append(np.eye(c, dtype=np.float32))
    return np.concatenate(rows, 0), np.tile(np.stack(masks, 0), (1, 1, nblk))


def _mm(a, b):
    return jnp.einsum('gmk,gkn->gmn', a.astype(bf16), b.astype(bf16), preferred_element_type=f32)


def _mm_nt(a, b):
    return jnp.einsum('gmk,gnk->gmn', a.astype(bf16), b.astype(bf16), preferred_element_type=f32)


def _mm_tn(a, b):
    return lax.dot_general(a.astype(bf16), b.astype(bf16), (((1,), (1,)), ((0,), (0,))),
                           preferred_element_type=f32)


def _iota(shape, dim):
    return lax.broadcasted_iota(jnp.int32, shape, dim)


def _bd(z, nblk, blk=None):
    z = z.astype(bf16)
    if blk is None:
        blk = _iota((1, 1, z.shape[-1]), 2) // (z.shape[-1] // nblk)
    return jnp.concatenate([jnp.where(blk == h, z, jnp.zeros_like(z)) for h in range(nblk)], axis=1)


def _hmm(x, z, nblk):
    return _mm(x, _bd(z, nblk))


def _hmm_nt(x, z, nblk):
    return _mm_nt(x, _bd(z, nblk))


def _hmm_split(x, z, nblk, blk=None):
    m = x.shape[1]
    x_hi, z_hi = x.astype(bf16), z.astype(bf16)
    x_lo = (x - x_hi.astype(f32)).astype(bf16)
    z_lo = (z - z_hi.astype(f32)).astype(bf16)
    both = _mm(jnp.concatenate([x_hi, x_lo], 1), _bd(z_hi, nblk, blk))
    return both[:, 0:m] + (both[:, m:2 * m] + _mm(x_hi, _bd(z_lo, nblk, blk)))


def _block_mask(rows, lanes, dr, dl):
    return (_iota((rows, lanes), 0) // dr) == (_iota((rows, lanes), 1) // dl)


def _dot01(m01, x):
    nb = x.shape[0]
    x2 = jnp.concatenate([x[i] for i in range(nb)], -1) if nb > 1 else x[0]
    hi = x2.astype(bf16)
    lo = (x2 - hi.astype(f32)).astype(bf16)
    m01 = m01.astype(bf16)
    d = lambda y: jnp.dot(m01, y, preferred_element_type=f32)
    out = d(hi) + d(lo)
    w = x.shape[-1]
    return jnp.stack([out[:, i * w:(i + 1) * w] for i in range(nb)], 0)


def _seg_sum(x, ones_bd, pieces=2):
    nb, c, w = x.shape
    x2 = x.reshape(nb * c, w)
    hi = x2.astype(bf16)
    out = jnp.dot(hi, ones_bd, preferred_element_type=f32)
    if pieces == 2:
        lo = (x2 - hi.astype(f32)).astype(bf16)
        out = out + jnp.dot(lo, ones_bd, preferred_element_type=f32)
    return out.reshape(nb, c, w)


def _last_row_as_col(b, c):
    return jnp.stack([b[i, c - SUBLANE:c, :].T[:, SUBLANE - 1:SUBLANE] for i in range(b.shape[0])], 0)


def _channel_decay_mixer(q, k, v, gl, s, nblk, sums_ref, masks_ref, bdmask, c, base):
    lv = _levels(c, base)
    nl = len(lv)
    sums = _dot01(sums_ref[...], gl)
    b = sums[:, 0:c]
    rev = sums[:, c:2 * c]
    row = _iota((1, c, 1), 1)
    if base:
        e0 = sums[:, (nl + 2) * c:(nl + 3) * c]
        att = jnp.where(masks_ref[nl] > 0.0, _hmm_nt(q * jnp.exp(e0), k * jnp.exp(-e0), nblk), 0.0)
    else:
        att = masks_ref[nl] * _hmm_nt(q, k, nblk)
    for li, m in enumerate(lv):
        upper = ((row // m) % 2) == 1
        w = jnp.where(upper, q, k) * jnp.exp(sums[:, (li + 2) * c:(li + 3) * c])
        att = att + masks_ref[li] * _hmm_nt(w, w, nblk)
    o = _mm(jnp.concatenate([q * jnp.exp(b), att], -1), jnp.concatenate([s.astype(bf16), _bd(v, nblk)], 1))
    s_new = jnp.exp(_last_row_as_col(b, c)) * s + jnp.where(bdmask, _mm_tn(k * jnp.exp(rev), v), 0.0)
    return o, s_new


MixCfg = collections.namedtuple("MixCfg",
                                "c g r0 r1 layer n_chunks init p4d o_rows n_alias stacked cast_slabs n_steps")


def _pairs_to_bd(s):
    z = jnp.zeros(s.shape[:1] + (HEAD_DIM, HEAD_DIM), f32)
    pair = lambda p: jnp.concatenate([jnp.concatenate([s[:, 2 * p], z], -1),
                                      jnp.concatenate([z, s[:, 2 * p + 1]], -1)], 1)
    return jnp.concatenate([pair(0), pair(1)], 0)


def _quad_to_bd(s):
    z = lambda n: jnp.zeros(s.shape[:1] + (GLA_DK, n * HEAD_DIM), f32)
    rows = []
    for h in range(N_HEADS):
        parts = ([z(h)] if h else []) + [s[:, h]] + ([z(N_HEADS - 1 - h)] if h < N_HEADS - 1 else [])
        rows.append(jnp.concatenate(parts, -1))
    return jnp.concatenate(rows, 1)


def _mixer_body(cfg, *refs):
    c, g = cfg.c, cfg.g
    n_w, n_t, n_c = 7, 0, 4
    p_ref = refs[0]
    cw_ref, alog_ref, dtb_ref, lbl_ref, wg_ref, bg_ref, gains_ref = refs[1:1 + n_w]
    sumsb_ref, masksb_ref, sumsc_ref, masksc_ref = refs[1 + n_w + n_t:1 + n_w + n_t + n_c]
    pos = 1 + n_w + n_t + n_c
    if cfg.init:
        convi_ref, sai_ref, sbi_ref, sci_ref, sdi_ref = refs[pos:pos + 5]
        pos += 5
    n_cast = len(cfg.cast_slabs)
    cast_in = refs[pos:pos + n_cast]
    pos += n_cast + cfg.n_alias
    o_ref, convo_ref, sao_ref, sbo_ref, sco_ref, sdo_ref = refs[pos:pos + 6]
    if cfg.stacked:
        convo_ref, sao_ref, sbo_ref, sco_ref, sdo_ref = (r.at[0] for r in (convo_ref, sao_ref, sbo_ref, sco_ref, sdo_ref))
    cast_out = refs[pos + 6:pos + 6 + n_cast]
    cbuf, sa_scr, sb_scr, sc_scr, sd_scr = refs[pos + 6 + n_cast:]
    ci = pl.program_id(1)

    @pl.when(ci == 0)
    def _():
        cbuf[:, 0:CONV_HIST, :] = jnp.zeros((g, CONV_HIST, 3 * GROUP), f32)
        if not cfg.init:
            for r in (sa_scr, sb_scr, sc_scr, sd_scr):
                r[...] = jnp.zeros(r.shape, f32)
        else:
            take = lambda r: jnp.broadcast_to(r[...], (g,) + r.shape[1:])
            cbuf[:, CONV_HIST - (CONV_W - 1):CONV_HIST, :] = take(convi_ref)
            sa_scr[...] = _pairs_to_bd(take(sai_ref))
            sb_scr[...] = _pairs_to_bd(take(sbi_ref))
            sc_scr[...] = _quad_to_bd(take(sci_ref))
            sd_scr[...] = _pairs_to_bd(take(sdi_ref))

    step = pl.program_id(0) * cfg.n_chunks + ci
    for w_f32, w_bf16, n_slabs in sorted(zip(cast_in, cast_out, cfg.cast_slabs), key=lambda e: e[2]):
        def narrow(src=w_f32, dst=w_bf16):
            dst[...] = src[0].astype(bf16)
        if n_slabs == cfg.n_steps:
            narrow()
        else:
            pl.when(step < n_slabs)(narrow)

    def load(a, b):
        if cfg.p4d:
            return p_ref[0, :, :, a:b]
        return p_ref[:, a:b].reshape(g, c, b - a)

    r0, r1 = cfg.r0, cfg.r1
    n_valid = float(r1 - r0)
    row = _iota((1, c, 1), 1)
    vm = ((row >= r0) & (row < r1)).astype(f32)
    lane_v = _iota((1, 1, 2 * HEAD_DIM), 2)
    lane_s = _iota((1, 1, 2 * c), 2)
    s_idx = lane_s % c
    incl = s_idx <= row
    strict = s_idx < row
    bd2 = _block_mask(2 * HEAD_DIM, 2 * HEAD_DIM, HEAD_DIM, HEAD_DIM)
    bd4 = _block_mask(N_HEADS * GLA_DK, GROUP, GLA_DK, HEAD_DIM)
    ones2 = bd2.astype(bf16)
    pairs = lambda x: jnp.concatenate([x[..., :LANE], x[..., LANE:]], 0)
    unpairs = lambda x: jnp.concatenate([x[:g], x[g:]], -1)
    n_solve = max(1, math.ceil(math.log2(r1 - r0)))
    sm = load(SMALL, SMALL + LANE)
    valid = (lambda x: x) if (r0 == 0 and r1 == c) else (lambda x: x * vm)

    cbuf[:, CONV_HIST:CONV_HIST + c, :] = valid(load(A_QKV, A_QKV + 3 * GROUP))
    buf = cbuf[...].reshape(g * (c + CONV_HIST), 3 * GROUP)
    acc = cbuf[:, CONV_HIST:CONV_HIST + c, :] * cw_ref[CONV_W - 1:CONV_W, :]
    for j in range(CONV_W - 1):
        back = pltpu.roll(buf, CONV_W - 1 - j, 0).reshape(g, c + CONV_HIST, 3 * GROUP)
        acc = acc + back[:, CONV_HIST:CONV_HIST + c, :] * cw_ref[j:j + 1, :]
    conv = jax.nn.silu(acc)
    beta = valid(jax.nn.sigmoid(sm))
    g_a = valid(-jnp.exp(alog_ref[...]) * jax.nn.softplus(sm + dtb_ref[...]))
    b_a = _dot01(sumsc_ref[0:c, :], g_a)

    def head_cols(x, lane0):
        col = lambda h: x[..., lane0 + h:lane0 + h + 1]
        return jnp.concatenate([col(0), col(2)], 0), jnp.concatenate([col(1), col(3)], 0)

    def spread(c0, c1, lane, width):
        return jnp.where(lane < width, c0, c1)

    bc0, bc1 = head_cols(b_a, SM_ALPHA)
    bt0, bt1 = head_cols(beta, SM_BETA)
    bc_v, bet_v = spread(bc0, bc1, lane_v, HEAD_DIM), spread(bt0, bt1, lane_v, HEAD_DIM)
    bc_s, bet_s = spread(bc0, bc1, lane_s, c), spread(bt0, bt1, lane_s, c)
    b_rows = [b_a[i].T for i in range(g)]
    b_row = lambda h: jnp.stack([t[SM_ALPHA + h:SM_ALPHA + h + 1, :] for t in b_rows], 0)
    br_s = jnp.concatenate([jnp.concatenate([b_row(0), b_row(1)], -1),
                            jnp.concatenate([b_row(2), b_row(3)], -1)], 0)
    dec = jnp.where(incl, jnp.exp(jnp.where(incl, bc_s - br_s, 0.0)), 0.0)
    q = pairs(conv[..., 0:GROUP])
    k = pairs(conv[..., GROUP:2 * GROUP])
    v = pairs(valid(conv[..., 2 * GROUP:3 * GROUP]))
    q = q * lax.rsqrt(_seg_sum(q * q, ones2, 1) + NORM_EPS) * HEAD_DIM ** -0.5
    k = valid(k * lax.rsqrt(_seg_sum(k * k, ones2) + NORM_EPS))
    s_a = sa_scr[...]
    kq = jnp.concatenate([k, q], 1)
    kq_k = _hmm_nt(kq, k, 2)
    kq_s = _mm(kq, s_a)
    a_mat = jnp.where(strict, bet_s * dec * kq_k[:, 0:c], 0.0)
    gam = jnp.exp(bc_v)
    u = bet_v * (v - gam * kq_s[:, 0:c])
    p = -a_mat
    lane_pu = _iota((1, 1, 2 * c + 2 * HEAD_DIM), 2)
    head_pu = jnp.where(lane_pu < 2 * c, lane_pu // c, (lane_pu - 2 * c) // HEAD_DIM)
    for i in range(n_solve):
        if i + 1 < n_solve:
            pu = _hmm_split(p, jnp.concatenate([p, u], -1), 2, head_pu)
            p, u = pu[..., 0:2 * c], u + pu[..., 2 * c:]
        else:
            u = u + _hmm_split(p, u, 2)
    o_a = gam * kq_s[:, c:2 * c] + _hmm(dec * kq_k[:, c:2 * c], u, 2)
    bl = bc_v[:, c - 1:c, :]
    sa_scr[...] = jnp.exp(bl) * s_a + jnp.where(bd2, _mm_tn(k * jnp.exp(bl - bc_v), u), 0.0)
    o_a = unpairs(o_a * lax.rsqrt(_seg_sum(o_a * o_a, ones2, 1) * (1.0 / HEAD_DIM) + NORM_EPS))

    lbl = lbl_ref[...]
    e = jnp.exp(lbl - jnp.max(lbl, axis=0, keepdims=True))
    prob = e / jnp.sum(e, axis=0, keepdims=True)
    lb = (jnp.sum(prob[1:cfg.layer + 1], axis=0, keepdims=True) if cfg.layer > 0
          else jnp.zeros((1, GROUP), f32))
    kb = (1.0 - lb) * jax.nn.sigmoid(-load(B_F, B_F + GROUP))
    g_b = valid(jnp.log1p(-jnp.clip(kb, 0.0, 1.0 - GATE_CLAMP)))
    o_b, s_b_new = _channel_decay_mixer(
        pairs(load(B_Q, B_Q + GROUP)), pairs(valid(kb)), pairs(valid(load(B_I, B_I + GROUP))),
        pairs(g_b), sb_scr[...], 2, sumsb_ref, masksb_ref, bd2, c, BASE_BLOCK if c > BASE_BLOCK else 0)
    sb_scr[...] = s_b_new
    o_b = unpairs(o_b * lax.rsqrt(_seg_sum(o_b * o_b, ones2, 1) * (1.0 / HEAD_DIM) + NORM_EPS))

    logit = jnp.dot(sm.reshape(g * c, LANE).astype(bf16), wg_ref[...].astype(bf16),
                    preferred_element_type=f32).reshape(g, c, LANE) + bg_ref[...]
    g_c = valid(jax.nn.log_sigmoid(logit) * (1.0 / GLA_GATE_NORM))
    c_args = (load(C_Q, C_Q + N_HEADS * GLA_DK) * GLA_DK ** -0.5, valid(load(C_K, C_K + N_HEADS * GLA_DK)),
              valid(load(C_V, C_V + GROUP)), g_c, sc_scr[...], N_HEADS)
    o_c, s_c_new = _channel_decay_mixer(*c_args, sumsc_ref, masksc_ref, bd4, c, 0)
    sc_scr[...] = s_c_new
    ones4 = _block_mask(GROUP, GROUP, HEAD_DIM, HEAD_DIM).astype(bf16)
    o_c = o_c * lax.rsqrt(_seg_sum(o_c * o_c, ones4, 1) * (1.0 / HEAD_DIM) + NORM_EPS)

    q = pairs(load(D_Q, D_Q + GROUP))
    k = pairs(valid(load(D_K, D_K + GROUP)))
    v = pairs(valid(load(D_V, D_V + GROUP)))
    pair_idx = _iota((2, 1, 1), 0)
    log_gamma = lambda lane, width: jnp.log1p(-jnp.exp2(-5.0 - (2 * pair_idx + lane // width).astype(f32)))
    per_seq = lambda x: jnp.concatenate([jnp.broadcast_to(x[p:p + 1], (g,) + x.shape[1:]) for p in range(2)], 0)
    lg_v, lg_s = log_gamma(lane_v, HEAD_DIM), log_gamma(lane_s, c)
    cnt_c = jnp.clip(row + 1 - r0, 0, r1 - r0).astype(f32)
    cnt_s = jnp.clip(s_idx + 1 - r0, 0, r1 - r0).astype(f32)
    dec = per_seq(jnp.where(incl, jnp.exp(jnp.where(incl, (cnt_c - cnt_s) * lg_s, 0.0)), 0.0))
    s_d = sd_scr[...]
    o_d = _mm(jnp.concatenate([q * per_seq(jnp.exp(cnt_c * lg_v)), dec * _hmm_nt(q, k, 2)], -1),
              jnp.concatenate([s_d.astype(bf16), _bd(v, 2)], 1))
    sd_scr[...] = (per_seq(jnp.exp(n_valid * lg_v)) * s_d
                   + jnp.where(bd2, _mm_tn(k * per_seq(jnp.exp((n_valid - cnt_c) * lg_v)), v), 0.0))
    oc = o_d - _seg_sum(o_d, ones2) * (1.0 / HEAD_DIM)
    o_d = unpairs(oc * lax.rsqrt(_seg_sum(oc * oc, ones2, 1) * (1.0 / HEAD_DIM) + LN_EPS))

    for i, (o_m, gate_col) in enumerate(zip((o_a, o_b, o_c, o_d), (A_GATE, B_GATE, C_GATE, D_GATE))):
        val = (o_m * gains_ref[i:i + 1, :] * load(gate_col, gate_col + GROUP)).astype(o_ref.dtype)
        if cfg.p4d:
            o_ref[0, :, :, i * GROUP:(i + 1) * GROUP] = val
        else:
            o_ref[0:g * c, i * GROUP:(i + 1) * GROUP] = val.reshape(g * c, GROUP)
    if cfg.o_rows > g * c:
        o_ref[g * c:cfg.o_rows, :] = jnp.zeros((cfg.o_rows - g * c, D_MODEL), o_ref.dtype)
    cbuf[:, 0:CONV_HIST, :] = cbuf[:, c:c + CONV_HIST, :]

    @pl.when(ci == cfg.n_chunks - 1)
    def _():
        convo_ref[...] = cbuf[:, pl.ds(CONV_HIST + r1 - (CONV_W - 1), CONV_W - 1), :]
        for scr, out in ((sa_scr, sao_ref), (sb_scr, sbo_ref), (sd_scr, sdo_ref)):
            s = scr[...]
            for p in range(2):
                out[:, 2 * p] = s[p * g:(p + 1) * g, 0:HEAD_DIM, 0:HEAD_DIM]
                out[:, 2 * p + 1] = s[p * g:(p + 1) * g, HEAD_DIM:2 * HEAD_DIM, HEAD_DIM:2 * HEAD_DIM]
        s = sc_scr[...]
        for h in range(N_HEADS):
            sco_ref[:, h] = s[:, h * GLA_DK:(h + 1) * GLA_DK, h * HEAD_DIM:(h + 1) * HEAD_DIM]


def _state_shapes(n, stacked):
    lead = (DEPTH, n) if stacked else (n,)
    return [jax.ShapeDtypeStruct(lead + (CONV_W - 1, 3 * GROUP), f32),
            jax.ShapeDtypeStruct(lead + (N_HEADS, HEAD_DIM, HEAD_DIM), f32),
            jax.ShapeDtypeStruct(lead + (N_HEADS, HEAD_DIM, HEAD_DIM), f32),
            jax.ShapeDtypeStruct(lead + (N_HEADS, GLA_DK, HEAD_DIM), f32),
            jax.ShapeDtypeStruct(lead + (N_HEADS, HEAD_DIM, HEAD_DIM), f32)]


def _state_out_specs(g, layer, stacked):
    if stacked:
        st = lambda k: pl.BlockSpec((1, g, N_HEADS, k, HEAD_DIM), lambda b, i: (layer, b, 0, 0, 0))
        conv = pl.BlockSpec((1, g, CONV_W - 1, 3 * GROUP), lambda b, i: (layer, b, 0, 0))
    else:
        st = lambda k: pl.BlockSpec((g, N_HEADS, k, HEAD_DIM), lambda b, i: (b, 0, 0, 0))
        conv = pl.BlockSpec((g, CONV_W - 1, 3 * GROUP), lambda b, i: (b, 0, 0))
    return [conv, st(HEAD_DIM), st(HEAD_DIM), st(GLA_DK), st(HEAD_DIM)]


def _mixer_call(cfg, name, n_groups, p, p_spec, wts, consts, init, init_specs,
                o_shape, o_spec, n_state_rows, o_prev=None, states_prev=None, cast=()):
    c, g = cfg.c, cfg.g
    n_steps = n_groups * cfg.n_chunks
    wspecs =[_const_spec((CONV_W, 3 * GROUP)), _const_spec((1, LANE)), _const_spec((1, LANE)),
              _const_spec((DEPTH, GROUP)), _const_spec((LANE, LANE)), _const_spec((1, LANE)),
              _const_spec((4, GROUP))]
    cspecs = [_const_spec(a.shape) for a in consts]
    in_specs = [p_spec] + wspecs + cspecs + list(init_specs)
    args = [p, *wts, *consts, *init]
    cast_shapes, cast_specs = [], []
    for w, n_slabs in zip(cast, cfg.cast_slabs):
        slab = w.shape[1] // n_slabs
        assert w.shape[1] % n_slabs == 0 and slab % (2 * SUBLANE) == 0 and n_slabs <= n_groups * cfg.n_chunks
        slab_idx = lambda b, i, n=n_slabs: jnp.minimum(b * cfg.n_chunks + i, n - 1)
        args.append(w)
        in_specs.append(pl.BlockSpec((1, slab, w.shape[2]), lambda b, i, f=slab_idx: (cfg.layer, f(b, i), 0)))
        cast_shapes.append(jax.ShapeDtypeStruct(w.shape[1:], bf16))
        cast_specs.append(pl.BlockSpec((slab, w.shape[2]), lambda b, i, f=slab_idx: (f(b, i), 0)))
    assert len(cast) == len(cfg.cast_slabs)
    aliases = {}
    if o_prev is not None:
        aliases[len(args)] = 0
        args.append(o_prev)
    for j, s_prev in enumerate(states_prev or ()):
        aliases[len(args)] = 1 + j
        args.append(s_prev)
    in_specs += [pl.BlockSpec(memory_space=pl.ANY)] * len(aliases)
    assert len(aliases) == cfg.n_alias
    return pl.pallas_call(
        functools.partial(_mixer_body, cfg),
        out_shape=[o_shape] + _state_shapes(n_state_rows, cfg.stacked) + cast_shapes,
        grid=(n_groups, cfg.n_chunks),
        in_specs=in_specs,
        out_specs=[o_spec] + _state_out_specs(g, cfg.layer, cfg.stacked) + cast_specs,
        scratch_shapes=[pltpu.VMEM((g, c + CONV_HIST, 3 * GROUP), f32),
                        pltpu.VMEM((2 * g, 2 * HEAD_DIM, 2 * HEAD_DIM), f32),
                        pltpu.VMEM((2 * g, 2 * HEAD_DIM, 2 * HEAD_DIM), f32),
                        pltpu.VMEM((g, N_HEADS * GLA_DK, GROUP), f32),
                        pltpu.VMEM((2 * g, 2 * HEAD_DIM, 2 * HEAD_DIM), f32)],
        input_output_aliases=aliases,
        compiler_params=pltpu.CompilerParams(dimension_semantics=("arbitrary", "arbitrary"),
                                             vmem_limit_bytes=VMEM_LIMIT),
        name=name,
    )(*args)


def _tok(ref, t, nb):
    return ref[t * nb:(t + 1) * nb, :].T


def _bcast_row(ref, t, r):
    return jnp.broadcast_to(ref[t, pl.ds(r, 1), :], (HEAD_DIM, ref.shape[-1]))


def _bl_diag_loop(n_tok, kdim, j, row0, s_in, s_out, d_scr, k_scr, q_scr, v_scr, d_const=None):
    nb = s_in.shape[-1]

    def body(k, os):
        s_k = s_in[0, j, k]
        r = row0 + k
        outs = []
        for t in range(n_tok):
            d = d_const if d_const is not None else _bcast_row(d_scr, t, r)
            s_k = d * s_k + _bcast_row(k_scr, t, r) * v_scr[t, j * HEAD_DIM:(j + 1) * HEAD_DIM, :]
            outs.append(os[t] + _bcast_row(q_scr, t, r) * s_k)
        s_out[0, j, k] = s_k
        return tuple(outs)
    return lax.fori_loop(0, kdim, body, (jnp.zeros((HEAD_DIM, nb), f32),) * n_tok, unroll=STATE_ROWS_PER_ITER)


def _bl_store(o_ref, os, gate_ref, gain_ref, n_tok, nb, group_norm):
    for t in range(n_tok):
        gate = _tok(gate_ref, t, nb)
        halves = []
        for j in range(2):
            o = os[j][t]
            if group_norm:
                o = o - jnp.mean(o, axis=0, keepdims=True)
                o = o * lax.rsqrt(jnp.mean(o * o, axis=0, keepdims=True) + LN_EPS)
            else:
                o = o * lax.rsqrt(jnp.mean(o * o, axis=0, keepdims=True) + NORM_EPS)
            hs = slice(j * HEAD_DIM, (j + 1) * HEAD_DIM)
            halves.append(o * gain_ref[hs, :] * gate[hs, :])
        o_ref[t * nb:(t + 1) * nb, :] = jnp.concatenate(halves, 0).T.astype(o_ref.dtype)


def _bl_hgrn_body(layer, n_tok, q_ref, f_ref, i_ref, gate_ref, lbl_ref, gain_ref, s_in, o_prev, s_prev,
                  o_ref, s_out, d_scr, k_scr, q_scr, v_scr):
    del o_prev, s_prev
    nb = s_in.shape[-1]
    lbl = lbl_ref[...]
    e = jnp.exp(lbl - jnp.max(lbl, axis=1, keepdims=True))
    prob = e / jnp.sum(e, axis=1, keepdims=True)
    lb = (jnp.sum(prob[:, 1:layer + 1], axis=1, keepdims=True) if layer > 0 else jnp.zeros((LANE, 1), f32))
    for t in range(n_tok):
        kk = (1.0 - lb) * jax.nn.sigmoid(-_tok(f_ref, t, nb))
        d_scr[t] = 1.0 - jnp.clip(kk, 0.0, 1.0 - GATE_CLAMP)
        k_scr[t] = kk
        q_scr[t] = _tok(q_ref, t, nb)
        v_scr[t] = _tok(i_ref, t, nb)
    os = [_bl_diag_loop(n_tok, HEAD_DIM, j, j * HEAD_DIM, s_in, s_out, d_scr, k_scr, q_scr, v_scr) for j in range(2)]
    _bl_store(o_ref, os, gate_ref, gain_ref, n_tok, nb, False)


def _bl_gla_body(n_tok, q_ref, k_ref, v_ref, gate_ref, small_ref, wg_ref, bg_ref, gain_ref, s_in, o_prev, s_prev,
                 o_ref, s_out, d_scr, k_scr, q_scr, v_scr):
    del o_prev, s_prev
    nb = s_in.shape[-1]
    for t in range(n_tok):
        logit = jnp.dot(wg_ref[...].astype(bf16), _tok(small_ref, t, nb).astype(bf16),
                        preferred_element_type=f32) + bg_ref[...]
        d_scr[t] = jnp.exp(jax.nn.log_sigmoid(logit) * (1.0 / GLA_GATE_NORM))
        k_scr[t] = _tok(k_ref, t, nb)
        q_scr[t] = _tok(q_ref, t, nb) * GLA_DK ** -0.5
        v_scr[t] = _tok(v_ref, t, nb)
    p = pl.program_id(0)
    os = [_bl_diag_loop(n_tok, GLA_DK, j, (2 * p + j) * GLA_DK, s_in, s_out, d_scr, k_scr, q_scr, v_scr)
          for j in range(2)]
    _bl_store(o_ref, os, gate_ref, gain_ref, n_tok, nb, False)


def _bl_ret_body(n_tok, q_ref, k_ref, v_ref, gate_ref, gain_ref, s_in, o_prev, s_prev,
                 o_ref, s_out, k_scr, q_scr, v_scr):
    del o_prev, s_prev
    nb = s_in.shape[-1]
    for t in range(n_tok):
        q_scr[t] = _tok(q_ref, t, nb)
        k_scr[t] = _tok(k_ref, t, nb)
        v_scr[t] = _tok(v_ref, t, nb)
    p = pl.program_id(0)
    os = []
    for j in range(2):
        head = jnp.zeros((1, nb), f32) + (2 * p + j).astype(f32)
        gamma = jnp.broadcast_to(1.0 - jnp.exp2(-5.0 - head), (HEAD_DIM, nb))
        os.append(_bl_diag_loop(n_tok, HEAD_DIM, j, j * HEAD_DIM, s_in, s_out, None, k_scr, q_scr, v_scr, gamma))
    _bl_store(o_ref, os, gate_ref, gain_ref, n_tok, nb, True)


def _bl_delta_body(n_tok, q_ref, k_ref, v_ref, gate_ref, small_ref, hq_ref, hk_ref, hv_ref, cwq_ref, cwk_ref, cwv_ref,
                   alog_ref, dtb_ref, gain_ref, s_in, o_prev, s_prev, cq_prev, ck_prev, cv_prev,
                   o_ref, s_out, cq_out, ck_out, cv_out, k_scr, q_scr, kh_scr, u_scr, bg_scr):
    del o_prev, s_prev, cq_prev, ck_prev, cv_prev
    nb = s_in.shape[-1]
    p = pl.program_id(0)

    def conv(x_ref, h_ref, cw_ref):
        seq = [h_ref[0, r].T for r in range(CONV_W - 1)] + [_tok(x_ref, t, nb) for t in range(n_tok)]
        outs = []
        for t in range(n_tok):
            acc = cw_ref[:, 0:1] * seq[t]
            for jj in range(1, CONV_W):
                acc = acc + cw_ref[:, jj:jj + 1] * seq[t + jj]
            outs.append(jax.nn.silu(acc))
        return outs

    def l2norm(x):
        parts = []
        for j in range(2):
            xj = x[j * HEAD_DIM:(j + 1) * HEAD_DIM]
            parts.append(xj * lax.rsqrt(jnp.sum(xj * xj, axis=0, keepdims=True) + NORM_EPS))
        return jnp.concatenate(parts, 0)

    qs, ks, vs = conv(q_ref, hq_ref, cwq_ref), conv(k_ref, hk_ref, cwk_ref), conv(v_ref, hv_ref, cwv_ref)
    for t in range(n_tok):
        q_scr[t] = l2norm(qs[t]) * HEAD_DIM ** -0.5
        k_scr[t] = l2norm(ks[t])
        sm = _tok(small_ref, t, nb)
        both = jnp.where(_iota((LANE, 1), 0) < SM_ALPHA, jax.nn.sigmoid(sm),
                         -jnp.exp(alog_ref[...]) * jax.nn.softplus(sm + dtb_ref[...]))
        bg_scr[t] = both[0:SUBLANE]
    for r in range(CONV_W - 1):
        rows = slice((n_tok - (CONV_W - 1) + r) * nb, (n_tok - (CONV_W - 1) + r + 1) * nb)
        cq_out[0, r] = q_ref[rows, :]
        ck_out[0, r] = k_ref[rows, :]
        cv_out[0, r] = v_ref[rows, :]

    os = []
    for j in range(2):
        hs = slice(j * HEAD_DIM, (j + 1) * HEAD_DIM)
        h = 2 * p + j
        beta = [bg_scr[t, pl.ds(SM_BETA + h, 1), :] for t in range(n_tok)]
        b, acc = [], jnp.zeros((1, nb), f32)
        for t in range(n_tok):
            acc = acc + bg_scr[t, pl.ds(SM_ALPHA + h, 1), :]
            b.append(acc)
        kt = [k_scr[t, hs, :] for t in range(n_tok)]
        qt = [q_scr[t, hs, :] for t in range(n_tok)]
        dot = lambda x, y: jnp.sum(x * y, axis=0, keepdims=True)

        def sweep(scr):
            def body(k, accs):
                s_k = s_in[0, j, k]
                return tuple(a + _bcast_row(scr, t, j * HEAD_DIM + k) * s_k for t, a in enumerate(accs))
            return lax.fori_loop(0, HEAD_DIM, body, (jnp.zeros((HEAD_DIM, nb), f32),) * n_tok,
                                 unroll=STATE_ROWS_PER_ITER)
        k_s, q_s = sweep(k_scr), sweep(q_scr)
        us = []
        for t in range(n_tok):
            u = beta[t] * (vs[t][hs] - jnp.exp(b[t]) * k_s[t])
            for s in range(t):
                u = u - (beta[t] * jnp.exp(b[t] - b[s]) * dot(kt[t], kt[s])) * us[s]
            us.append(u)
        o_h = []
        for t in range(n_tok):
            o = jnp.exp(b[t]) * q_s[t]
            for s in range(t + 1):
                o = o + (jnp.exp(b[t] - b[s]) * dot(qt[t], kt[s])) * us[s]
            o_h.append(o)
        os.append(o_h)
        for t in range(n_tok):
            kh_scr[t, hs, :] = kt[t] * jnp.exp(b[-1] - b[t])
            u_scr[t] = us[t]
        decay = jnp.broadcast_to(jnp.exp(b[-1]), (HEAD_DIM, nb))

        def update(k, carry):
            s_k = decay * s_in[0, j, k]
            for t in range(n_tok):
                s_k = s_k + _bcast_row(kh_scr, t, j * HEAD_DIM + k) * u_scr[t]
            s_out[0, j, k] = s_k
            return carry
        lax.fori_loop(0, HEAD_DIM, update, 0, unroll=STATE_ROWS_PER_ITER)
    _bl_store(o_ref, os, gate_ref, gain_ref, n_tok, nb, False)


def _bl_call(body, name, mixer, n_tok, nb, blk_row, pt, cols, extra, extra_specs, state_t, kdim, layer,
             o_prev, s_prev, scratch, conv_prev=None):
    def tok_spec(c0, per_pair):
        return pl.BlockSpec((n_tok * nb, LANE),
                            (lambda p: (blk_row, c0 // LANE + p)) if per_pair else (lambda p: (blk_row, c0 // LANE)))
    st_spec = pl.BlockSpec((1, 2, kdim, HEAD_DIM, nb), lambda p: (layer, p, 0, 0, 0))
    args = [pt] * len(cols) + list(extra) + [state_t]
    in_specs = [tok_spec(c0, pp) for c0, pp in cols] + list(extra_specs) + [st_spec]
    aliases = {}

    def donate(arr, out_idx):
        if arr is not None:
            aliases[len(args)] = out_idx
        args.append(arr if arr is not None else jnp.zeros((1,), f32))
        in_specs.append(pl.BlockSpec(memory_space=pl.ANY))
    donate(o_prev, 0)
    donate(s_prev, 1)
    out_shape = [jax.ShapeDtypeStruct(o_prev.shape, o_prev.dtype),
                 jax.ShapeDtypeStruct((DEPTH,) + state_t.shape[1:], f32)]
    out_specs = [pl.BlockSpec((n_tok * nb, LANE), lambda p: (blk_row, 2 * mixer + p)), st_spec]
    if conv_prev is not None:
        for i in range(3):
            donate(conv_prev[i] if conv_prev else None, 2 + i)
            out_shape.append(jax.ShapeDtypeStruct((DEPTH, CONV_W - 1, nb, GROUP), f32))
            out_specs.append(pl.BlockSpec((1, CONV_W - 1, nb, LANE), lambda p: (layer, 0, 0, p)))
    return pl.pallas_call(
        body, out_shape=out_shape, grid=(2,), in_specs=in_specs, out_specs=out_specs,
        scratch_shapes=scratch, input_output_aliases=aliases,
        compiler_params=pltpu.CompilerParams(dimension_semantics=("arbitrary",), vmem_limit_bytes=VMEM_LIMIT),
        name=name,
    )(*args)


def _rope_tables(pos):
    half = HEAD_DIM // 2
    inv = 1.0 / (RET_THETA_BASE ** jnp.linspace(0.0, 1.0, half, dtype=f32))
    ang = pos[:, None] * inv[None, :]
    cos, sin = jnp.cos(ang), jnp.sin(ang)
    tile = lambda a, b: jnp.tile(jnp.concatenate([a, b], -1), (1, LANE // HEAD_DIM))
    return tile(cos, cos), tile(-sin, sin)


def kernel(x_prompt, x_sample, state_delta_conv, state_delta, state_hgrn, state_gla, state_ret, meta_tokens, emb_ln_g, emb_ln_b, w_in, conv_w, delta_a_log, delta_dt_bias, delta_norm_g, hgrn_lb_logits, hgrn_norm_g, gla_w_gate, gla_b_gate, gla_norm_g, ret_norm_g, w_out, ln1_g, ln1_b, w_ffn_gate, w_ffn_up, w_ffn_down, ln2_g, ln2_b):
    bsz, seq, _ = x_prompt.shape
    n_seq, n_tok, _ = x_sample.shape
    pg = PROMPT_GROUP if bsz % PROMPT_GROUP == 0 else 1
    rows_main = bsz * seq
    rows_s = n_tok * n_seq
    assert seq % CHUNK == 0 and rows_main % ROW_TILE == 0 and N_META <= CHUNK
    assert n_tok >= CONV_W - 1 and rows_s % CHUNK == 0 and (n_seq % LANE == 0 or n_seq < LANE)
    lead_pad = CHUNK - N_META
    rows_tail = rows_s + CHUNK
    s_blk = 0
    meta_blk = rows_s // CHUNK

    xs = jnp.swapaxes(x_sample, 0, 1).reshape(rows_s, D_MODEL)
    xt = jnp.concatenate([xs, jnp.zeros((lead_pad, D_MODEL), f32), meta_tokens.astype(f32)], 0)
    xm = x_prompt.reshape(rows_main, D_MODEL)

    assert seq % ROW_TILE == 0 or ROW_TILE % seq == 0
    rope_main = _rope_tables(N_META + (jnp.arange(max(seq, ROW_TILE)) % seq).astype(f32))
    rope_tail = _rope_tables(jnp.concatenate([PAST_LEN + jnp.repeat(jnp.arange(n_tok), n_seq).astype(f32),
                                              jnp.arange(CHUNK, dtype=f32) - lead_pad]))
    consts_c = tuple(jnp.asarray(a, f32) for a in (_chunk_constants(CHUNK, BASE_BLOCK, 2)
                                                  + _chunk_constants(CHUNK, 0, N_HEADS)))
    row_vec = lambda v: v.reshape(1, -1).astype(f32)
    col_vec = lambda v: v.reshape(-1, 1).astype(f32)
    small_row = lambda v: jnp.zeros((1, LANE), f32).at[0, SM_ALPHA:SM_ALPHA + N_HEADS].set(v)
    n_chunks = seq // CHUNK

    seq_last = lambda s: jnp.transpose(s, (0, 2, 3, 4, 1))
    sd_t, sh_t, sg_t, sr_t = (seq_last(s) for s in (state_delta, state_hgrn, state_gla, state_ret))
    conv_t = jnp.transpose(state_delta_conv, (0, 2, 1, 3))
    vm = lambda shape: pltpu.VMEM(shape, f32)
    tok_scr = lambda: vm((n_tok, LANE, n_seq))

    w_in_t = jnp.swapaxes(w_in, 1, 2)
    st_p = None
    sa_s = sb_s = sc_s = sd_s = conv_s = None
    for l in range(DEPTH):
        pm, pt = _proj(xm, xt, row_vec(emb_ln_g), row_vec(emb_ln_b), w_in_t, l, l == 0, rope_main, rope_tail)
        wg = jnp.zeros((LANE, N_HEADS * GLA_DK), f32).at[SM_LR:SM_LR + GLA_RANK].set(gla_w_gate[l])
        gains = jnp.stack([delta_norm_g[l], hgrn_norm_g[l], gla_norm_g[l], ret_norm_g[l]], 0)
        wts = (conv_w[l], small_row(delta_a_log[l]), small_row(delta_dt_bias[l]), hgrn_lb_logits,
               wg, row_vec(gla_b_gate[l]), gains)

        cfg = MixCfg(CHUNK, 1, lead_pad, CHUNK, l, 1, None, False, CHUNK, 0, False, (), 1)
        ot, *st_meta = _mixer_call(
            cfg, "mixers_meta", 1, pt, pl.BlockSpec((CHUNK, D_IN_AL), lambda b, i: (meta_blk, 0)),
            wts, consts_c, (), (),
            jax.ShapeDtypeStruct((rows_tail, D_MODEL), bf16),
            pl.BlockSpec((CHUNK, D_MODEL), lambda b, i: (meta_blk, 0)), 1)

        n_prev = 0 if st_p is None else len(st_p)
        ffn_w = (w_out, w_ffn_gate, w_ffn_up, w_ffn_down)
        n_steps = (bsz // pg) * n_chunks
        slabs = tuple(max(n for n in range(1, n_steps + 1) if w.shape[1] % n == 0 and (w.shape[1] // n) % (2 * SUBLANE) == 0)
                      for w in ffn_w)
        cfg = MixCfg(CHUNK, pg, 0, CHUNK, l, n_chunks, "shared", True, pg * CHUNK, n_prev, True, slabs, n_steps)
        st1 = lambda k: pl.BlockSpec((1, N_HEADS, k, HEAD_DIM), lambda b, i: (0, 0, 0, 0))
        om, *st_p = _mixer_call(
            cfg, "mixers_prompt", bsz // pg, pm.reshape(bsz // pg, pg, seq, D_IN_AL),
            pl.BlockSpec((1, pg, CHUNK, D_IN_AL), lambda b, i: (b, 0, i, 0)),
            wts, consts_c, st_meta,
            [pl.BlockSpec((1, CONV_W - 1, 3 * GROUP), lambda b, i: (0, 0, 0)),
             st1(HEAD_DIM), st1(HEAD_DIM), st1(GLA_DK), st1(HEAD_DIM)],
            jax.ShapeDtypeStruct((bsz // pg, pg, seq, D_MODEL), bf16),
            pl.BlockSpec((1, pg, CHUNK, D_MODEL), lambda b, i: (b, 0, i, 0)), bsz, states_prev=st_p, cast=ffn_w)
        om = om.reshape(rows_main, D_MODEL)
        st_p, (wo_b, wg_b, wu_b, wd_b) = st_p[:5], st_p[5:]

        pair_col = lambda v: (col_vec(v), pl.BlockSpec((LANE, 1), lambda p: (p, 0)))
        full = lambda a: (a, pl.BlockSpec(a.shape, lambda p: (0,) * a.ndim))
        split = lambda pairs: ([a for a, _ in pairs], [s for _, s in pairs])
        hist = lambda part: (conv_t, pl.BlockSpec((1, CONV_W - 1, n_seq, LANE), lambda p: (l, 0, 0, 2 * part + p)))
        cw_t = conv_w[l].T
        cw = lambda part: (cw_t, pl.BlockSpec((LANE, CONV_W), lambda p: (2 * part + p, 0)))
        small_col = lambda v: jnp.zeros((LANE, 1), f32).at[SM_ALPHA:SM_ALPHA + N_HEADS, 0].set(v)
        extra, especs = split([hist(0), hist(1), hist(2), cw(0), cw(1), cw(2), full(small_col(delta_a_log[l])),
                               full(small_col(delta_dt_bias[l])), pair_col(delta_norm_g[l])])
        ot, sa_s, *conv_s = _bl_call(
            functools.partial(_bl_delta_body, n_tok), "sample_delta", 0, n_tok, n_seq, s_blk, pt,
            [(A_QKV, True), (A_QKV + GROUP, True), (A_QKV + 2 * GROUP, True), (A_GATE, True), (SMALL, False)],
            extra, especs, sd_t, HEAD_DIM, l, ot, sa_s,
            [tok_scr(), tok_scr(), tok_scr(), vm((n_tok, HEAD_DIM, n_seq)), vm((n_tok, SUBLANE, n_seq))],
            conv_prev=conv_s or [])
        extra, especs = split([(hgrn_lb_logits.T, pl.BlockSpec((LANE, DEPTH), lambda p: (p, 0))),
                               pair_col(hgrn_norm_g[l])])
        ot, sb_s = _bl_call(
            functools.partial(_bl_hgrn_body, l, n_tok), "sample_hgrn", 1, n_tok, n_seq, s_blk, pt,
            [(B_Q, True), (B_F, True), (B_I, True), (B_GATE, True)], extra, especs, sh_t, HEAD_DIM, l, ot, sb_s,
            [tok_scr() for _ in range(4)])
        wg_t = jnp.zeros((N_HEADS * GLA_DK, LANE), f32).at[:, SM_LR:SM_LR + GLA_RANK].set(gla_w_gate[l].T)
        extra, especs = split([full(wg_t), full(col_vec(gla_b_gate[l])), pair_col(gla_norm_g[l])])
        ot, sc_s = _bl_call(
            functools.partial(_bl_gla_body, n_tok), "sample_gla", 2, n_tok, n_seq, s_blk, pt,
            [(C_Q, False), (C_K, False), (C_V, True), (C_GATE, True), (SMALL, False)], extra, especs,
            sg_t, GLA_DK, l, ot, sc_s, [tok_scr() for _ in range(4)])
        extra, especs = split([pair_col(ret_norm_g[l])])
        ot, sd_s = _bl_call(
            functools.partial(_bl_ret_body, n_tok), "sample_ret", 3, n_tok, n_seq, s_blk, pt,
            [(D_Q, True), (D_K, True), (D_V, True), (D_GATE, True)], extra, especs, sr_t, HEAD_DIM, l, ot, sd_s,
            [tok_scr() for _ in range(3)])

        xm, xt = _post(xm, xt, om, ot, row_vec(emb_ln_g), row_vec(emb_ln_b),
                       row_vec(ln1_g[l]), row_vec(ln1_b[l]), row_vec(ln2_g[l]), row_vec(ln2_b[l]),
                       wo_b, wg_b, wu_b, wd_b, l)

    y_prompt = xm.reshape(bsz, seq, D_MODEL)
    y_sample = jnp.swapaxes(xt[0:rows_s].reshape(n_tok, n_seq, D_MODEL), 0, 1)
    seq_first = lambda s: jnp.transpose(s, (0, 4, 1, 2, 3))
    st_s = (jnp.transpose(jnp.concatenate(conv_s, -1), (0, 2, 1, 3)),
            seq_first(sa_s), seq_first(sb_s), seq_first(sc_s), seq_first(sd_s))
    return (y_prompt, y_sample) + tuple(v for pair in zip(st_p, st_s) for v in pair)
```
